```python
import jax
import jax.numpy as jnp
from jax import lax
import numpy as np

D_MODEL = 2048
BATCH = 4
SEQ = 2048
DEPTH = 1
DEC_BATCH = 128
DEC_SEQ = 4
PAST_LEN = 16384
PAGE_SIZE = 128

CONV_WIDTH = 1024
CONV_K = 3
RWKV_WIDTH = 2048
HEAD_SIZE = 64
N_HEADS = RWKV_WIDTH // HEAD_SIZE
DECAY_LORA = 96
A_LORA = 96
GATE_LORA = 256
RWKV_PROJ = 3 * RWKV_WIDTH + DECAY_LORA + A_LORA + GATE_LORA
IN_PROJ = 3 * CONV_WIDTH + RWKV_PROJ + 2 * D_MODEL
N_GROUPS = 8
EXPERTS_PER_GROUP = 8
N_EXPERTS = N_GROUPS * EXPERTS_PER_GROUP
TOP_K = 2
D_EXPERT = 512
MOE_BLOCK = 128
PLE_DIM = 256
RMS_EPS = 1e-6
GN_EPS = 64e-5

kernel_name = 'hybrid_conv_rwkv7_hmoe_decode_step'


def rmsnorm(x, g):
    xf = x.astype(jnp.float32)
    y = xf * lax.rsqrt(jnp.mean(xf * xf, axis=-1, keepdims=True) + RMS_EPS)
    return (y * g.astype(jnp.float32)).astype(x.dtype)


def short_conv_branch(zc, conv_s, conv_w, w_out):
    T = zc.shape[1]
    gate_b, gate_c, h = jnp.split(zc, 3, axis=-1)
    u = gate_c * h
    u_ext = jnp.concatenate([conv_s.astype(u.dtype), u], axis=1)
    conv = sum(conv_w[j] * u_ext[:, j:j + T] for j in range(CONV_K))
    return (gate_b * conv) @ w_out, u_ext[:, T:]


def wkv7_scan(r, decay, k, v, a_vec, b_vec, s0):
    def step(S, inp):
        r_t, w_t, k_t, v_t, a_t, b_t = inp
        sa = jnp.einsum('bhvk,bhk->bhv', S, a_t)
        S = S * w_t[:, :, None, :] + sa[..., None] * b_t[:, :, None, :] + v_t[..., None] * k_t[:, :, None, :]
        o = jnp.einsum('bhvk,bhk->bhv', S, r_t)
        return S, o
    xs = tuple(jnp.swapaxes(t, 0, 1) for t in (r, decay, k, v, a_vec, b_vec))
    S, o = lax.scan(step, s0.astype(jnp.float32), xs)
    return jnp.swapaxes(o, 0, 1), S.astype(s0.dtype)


def rwkv7_branch(zr, shift_s, wkv_s, lp):
    Bn, T, _ = zr.shape
    f32 = jnp.float32
    C = RWKV_WIDTH
    z_prev = jnp.concatenate([shift_s[:, None].astype(zr.dtype), zr[:, :-1]], axis=1)
    zs = zr + lp['shift_mu'] * (z_prev - zr)
    r, k, v, xw, xa, xg = jnp.split(zs, [C, 2 * C, 3 * C, 3 * C + DECAY_LORA, 3 * C + DECAY_LORA + A_LORA], axis=-1)
    w_logit = (lp['w0'] + jnp.tanh(xw) @ lp['w2']).astype(f32)
    decay = jnp.exp(-jnp.exp(-jax.nn.softplus(-w_logit) - 0.5))
    a = jax.nn.sigmoid((lp['a0'] + xa @ lp['a2']).astype(f32))
    g = (jax.nn.sigmoid(xg) @ lp['g2']).astype(f32)
    hs = (Bn, T, N_HEADS, HEAD_SIZE)
    rf = r.astype(f32).reshape(hs)
    vf = v.astype(f32).reshape(hs)
    kf = k.astype(f32)
    kk = (kf * lp['k_k'].astype(f32)).reshape(hs)
    kk = kk / jnp.maximum(jnp.linalg.norm(kk, axis=-1, keepdims=True), 1e-12)
    kf = (kf * (1.0 + (a - 1.0) * lp['k_a'].astype(f32))).reshape(hs)
    a = a.reshape(hs)
    o, s_new = wkv7_scan(rf, decay.reshape(hs), kf, vf, -kk, kk * a, wkv_s)
    mu = jnp.mean(o, axis=-1, keepdims=True)
    var = jnp.mean(jnp.square(o - mu), axis=-1, keepdims=True)
    o = ((o - mu) * lax.rsqrt(var + GN_EPS)).reshape(Bn, T, C)
    o = o * lp['lnx_w'].astype(f32) + lp['lnx_b'].astype(f32)
    bonus = jnp.sum(rf * kf * lp['r_k'].astype(f32), axis=-1, keepdims=True) * vf
    o = (o + bonus.reshape(Bn, T, C)) * g
    return o.astype(zr.dtype) @ lp['w_rwkv_out'], zr[:, -1], s_new


def hier_route(x, w_rg, b_rg, w_re, b_re):
    T = x.shape[0]
    gl = (x @ w_rg).astype(jnp.float32) + b_rg.astype(jnp.float32)
    gp = jax.nn.softmax(gl, axis=-1)
    g_idx = jnp.argmax(gl, axis=-1).astype(jnp.int32)
    g_w = jnp.take_along_axis(gp, g_idx[:, None], axis=-1)
    el = ((x @ w_re).astype(jnp.float32) + b_re.astype(jnp.float32)).reshape(T, N_GROUPS, EXPERTS_PER_GROUP)
    el_sel = jnp.take_along_axis(el, g_idx[:, None, None], axis=1)[:, 0]
    top_v, top_i = lax.top_k(el_sel, TOP_K)
    ew = jax.nn.softmax(top_v, axis=-1) * g_w
    eidx = g_idx[:, None] * EXPERTS_PER_GROUP + top_i.astype(jnp.int32)
    return eidx, ew


def moe_ffn(x, eidx, ew, w_gate, w_up, w_down):
    T, D = x.shape
    A = T * TOP_K
    e_flat = eidx.reshape(A)
    tok = jnp.repeat(jnp.arange(T, dtype=jnp.int32), TOP_K)
    wts = ew.reshape(A)
    order = jnp.argsort(e_flat)
    e_sorted = e_flat[order]
    counts = jnp.zeros((N_EXPERTS,), jnp.int32).at[e_flat].add(1)
    padded = (counts + MOE_BLOCK - 1) // MOE_BLOCK * MOE_BLOCK
    start = jnp.cumsum(counts) - counts
    pad_end = jnp.cumsum(padded)
    pad_start = pad_end - padded
    dest = pad_start[e_sorted] + jnp.arange(A, dtype=jnp.int32) - start[e_sorted]
    n_blocks = -(-A // MOE_BLOCK) + N_EXPERTS
    P = n_blocks * MOE_BLOCK
    slot_tok = jnp.full((P,), T, jnp.int32).at[dest].set(tok[order])
    slot_w = jnp.zeros((P,), wts.dtype).at[dest].set(wts[order])
    blk_e = jnp.minimum(jnp.searchsorted(pad_end, jnp.arange(n_blocks, dtype=jnp.int32) * MOE_BLOCK, side='right'), N_EXPERTS - 1)
    x_pad = jnp.concatenate([x, jnp.zeros((1, D), x.dtype)], axis=0)
    xb = x_pad[slot_tok].reshape(n_blocks, MOE_BLOCK, D)

    def expert_block(args):
        xe, e = args
        hdn = jax.nn.silu(xe @ w_gate[e]) * (xe @ w_up[e])
        return hdn @ w_down[e]

    yb = lax.map(expert_block, (xb, blk_e))
    y = yb.reshape(P, D) * slot_w[:, None].astype(x.dtype)
    return jnp.zeros((T + 1, D), x.dtype).at[slot_tok].add(y)[:T]


def decoder_layer(x, p, conv_s, shift_s, wkv_s, lp):
    Bn, T, D = x.shape
    xn = rmsnorm(x, lp['norm_mix'])
    z = xn @ lp['w_in']
    zc, zr, zg = jnp.split(z, [3 * CONV_WIDTH, 3 * CONV_WIDTH + RWKV_PROJ], axis=-1)
    conv_out, conv_new = short_conv_branch(zc, conv_s, lp['conv_w'], lp['w_conv_out'])
    rwkv_out, shift_new, wkv_new = rwkv7_branch(zr, shift_s, wkv_s, lp)
    g_conv, g_rwkv = jnp.split(zg, 2, axis=-1)
    mixed = jax.nn.sigmoid(g_conv) * conv_out + jax.nn.sigmoid(g_rwkv) * rwkv_out
    h = x + mixed @ lp['w_mix_out']
    hn = rmsnorm(h, lp['norm_ffn']).reshape(Bn * T, D)
    eidx, ew = hier_route(hn, lp['w_route_group'], lp['b_route_group'], lp['w_route_expert'], lp['b_route_expert'])
    h = h + moe_ffn(hn, eidx, ew, lp['w_exp_gate'], lp['w_exp_up'], lp['w_exp_down']).reshape(Bn, T, D)
    h = h + jax.nn.sigmoid(h @ lp['w_ple_gate']) * (p @ lp['w_ple_proj'])
    return h, conv_new, shift_new, wkv_new


def setup_inputs(seed: int = 0) -> dict:
    key = jax.random.key(seed)
    ks = list(jax.random.split(key, 40))
    f32 = jnp.float32
    C = RWKV_WIDTH

    def nrm(i, shape, scale):
        return jax.random.normal(ks[i], shape, f32) * scale

    return {
        'x_prompt': nrm(0, (BATCH, SEQ, D_MODEL), 1.0),
        'x_sample': nrm(1, (DEC_BATCH, DEC_SEQ, D_MODEL), 1.0),
        'state_conv': nrm(2, (DEPTH, DEC_BATCH, CONV_K - 1, CONV_WIDTH), 1.0),
        'state_shift': nrm(3, (DEPTH, DEC_BATCH, RWKV_PROJ), 1.0),
        'state_wkv': nrm(4, (DEPTH, DEC_BATCH, N_HEADS, HEAD_SIZE, HEAD_SIZE), 0.1),
        'p_prompt': nrm(5, (DEPTH, BATCH, SEQ, PLE_DIM), 1.0),
        'p_sample': nrm(6, (DEPTH, DEC_BATCH, DEC_SEQ, PLE_DIM), 1.0),
        'norm_mix': 1.0 + nrm(7, (DEPTH, D_MODEL), 0.02),
        'w_in': nrm(8, (DEPTH, D_MODEL, IN_PROJ), D_MODEL ** -0.5),
        'conv_w': nrm(9, (DEPTH, CONV_K, CONV_WIDTH), CONV_K ** -0.5),
        'w_conv_out': nrm(10, (DEPTH, CONV_WIDTH, D_MODEL), CONV_WIDTH ** -0.5),
        'shift_mu': jax.random.uniform(ks[11], (DEPTH, RWKV_PROJ), f32),
        'w0': -2.0 + nrm(12, (DEPTH, C), 0.5),
        'w2': nrm(13, (DEPTH, DECAY_LORA, C), 0.1 * DECAY_LORA ** -0.5),
        'a0': nrm(14, (DEPTH, C), 0.1),
        'a2': nrm(15, (DEPTH, A_LORA, C), 0.1 * A_LORA ** -0.5),
        'g2': nrm(16, (DEPTH, GATE_LORA, C), GATE_LORA ** -0.5),
        'k_k': 0.85 + nrm(17, (DEPTH, C), 0.02),
        'k_a': 1.0 + nrm(18, (DEPTH, C), 0.02),
        'r_k': nrm(19, (DEPTH, N_HEADS, HEAD_SIZE), 0.1),
        'lnx_w': 1.0 + nrm(20, (DEPTH, C), 0.02),
        'lnx_b': nrm(21, (DEPTH, C), 0.02),
        'w_rwkv_out': nrm(22, (DEPTH, C, D_MODEL), C ** -0.5),
        'w_mix_out': nrm(23, (DEPTH, D_MODEL, D_MODEL), D_MODEL ** -0.5),
        'norm_ffn': 1.0 + nrm(24, (DEPTH, D_MODEL), 0.02),
        'w_route_group': nrm(25, (DEPTH, D_MODEL, N_GROUPS), D_MODEL ** -0.5),
        'b_route_group': nrm(26, (DEPTH, N_GROUPS), 0.01),
        'w_route_expert': nrm(27, (DEPTH, D_MODEL, N_EXPERTS), D_MODEL ** -0.5),
        'b_route_expert': nrm(28, (DEPTH, N_EXPERTS), 0.01),
        'w_exp_gate': nrm(29, (DEPTH, N_EXPERTS, D_MODEL, D_EXPERT), D_MODEL ** -0.5),
        'w_exp_up': nrm(30, (DEPTH, N_EXPERTS, D_MODEL, D_EXPERT), D_MODEL ** -0.5),
        'w_exp_down': nrm(31, (DEPTH, N_EXPERTS, D_EXPERT, D_MODEL), D_EXPERT ** -0.5),
        'w_ple_proj': nrm(32, (DEPTH, PLE_DIM, D_MODEL), PLE_DIM ** -0.5),
        'w_ple_gate': nrm(33, (DEPTH, D_MODEL, D_MODEL), D_MODEL ** -0.5),
        'norm_final': 1.0 + nrm(34, (D_MODEL,), 0.02),
    }


def reference(x_prompt, x_sample, state_conv, state_shift, state_wkv, p_prompt, p_sample,
              norm_mix, w_in, conv_w, w_conv_out, shift_mu, w0, w2, a0, a2, g2, k_k, k_a, r_k,
              lnx_w, lnx_b, w_rwkv_out, w_mix_out, norm_ffn, w_route_group, b_route_group,
              w_route_expert, b_route_expert, w_exp_gate, w_exp_up, w_exp_down,
              w_ple_proj, w_ple_gate, norm_final):
    dt = x_prompt.dtype
    nb = x_prompt.shape[0]
    conv0 = jnp.zeros((nb, CONV_K - 1, CONV_WIDTH), dt)
    shift0 = jnp.zeros((nb, RWKV_PROJ), dt)
    wkv0 = jnp.zeros((nb, N_HEADS, HEAD_SIZE, HEAD_SIZE), dt)
    hp, hs = x_prompt, x_sample
    conv_p, shift_p, wkv_p, conv_s, shift_s, wkv_s = [], [], [], [], [], []
    for i in range(DEPTH):
        lp = dict(norm_mix=norm_mix[i], w_in=w_in[i], conv_w=conv_w[i], w_conv_out=w_conv_out[i],
                  shift_mu=shift_mu[i], w0=w0[i], w2=w2[i], a0=a0[i], a2=a2[i], g2=g2[i],
                  k_k=k_k[i], k_a=k_a[i], r_k=r_k[i], lnx_w=lnx_w[i], lnx_b=lnx_b[i],
                  w_rwkv_out=w_rwkv_out[i], w_mix_out=w_mix_out[i], norm_ffn=norm_ffn[i],
                  w_route_group=w_route_group[i], b_route_group=b_route_group[i],
                  w_route_expert=w_route_expert[i], b_route_expert=b_route_expert[i],
                  w_exp_gate=w_exp_gate[i], w_exp_up=w_exp_up[i], w_exp_down=w_exp_down[i],
                  w_ple_proj=w_ple_proj[i], w_ple_gate=w_ple_gate[i])
        hp, c1, s1, k1 = decoder_layer(hp, p_prompt[i], conv0, shift0, wkv0, lp)
        hs, c2, s2, k2 = decoder_layer(hs, p_sample[i], state_conv[i], state_shift[i], state_wkv[i], lp)
        conv_p.append(c1)
        shift_p.append(s1)
        wkv_p.append(k1)
        conv_s.append(c2)
        shift_s.append(s2)
        wkv_s.append(k2)
    y_prompt = rmsnorm(hp, norm_final)
    y_sample = rmsnorm(hs, norm_final)
    return (y_prompt, y_sample, jnp.stack(conv_p), jnp.stack(shift_p), jnp.stack(wkv_p),
            jnp.stack(conv_s), jnp.stack(shift_s), jnp.stack(wkv_s))
```

```python
import functools

import jax
import jax.numpy as jnp
from jax import lax
from jax.experimental import pallas as pl
from jax.experimental.pallas import tpu as pltpu

D_MODEL = 2048
BATCH = 4
SEQ = 2048
DEC_BATCH = 128
DEC_SEQ = 4
CONV_WIDTH = 1024
RWKV_WIDTH = 2048
HEAD_SIZE = 64
N_HEADS = RWKV_WIDTH // HEAD_SIZE
DECAY_LORA = 96
A_LORA = 96
GATE_LORA = 256
RWKV_PROJ = 3 * RWKV_WIDTH + DECAY_LORA + A_LORA + GATE_LORA
N_GROUPS = 8
EXPERTS_PER_GROUP = 8
N_EXPERTS = N_GROUPS * EXPERTS_PER_GROUP
TOP_K = 2
D_EXPERT = 512
MOE_BLOCK = 128
PLE_DIM = 256
RMS_EPS = 1e-6
GN_EPS = 64e-5

N_PROMPT = BATCH * SEQ
N_SAMPLE = DEC_BATCH * DEC_SEQ
N_TOK = N_PROMPT + N_SAMPLE
LANES = 128
SUBLANES = 8
TM = 512
TM_S = 256
CHAIN_B = LANES // N_HEADS
SCAN_TT = 64
LORA_W = 512
N_TAIL = 2 * D_MODEL + LORA_W
N_MAIN = 3 * CONV_WIDTH + 3 * RWKV_WIDTH
EXP_M05 = 0.6065306597126334
F32 = jnp.float32
BF16 = jnp.bfloat16


def _sigmoid(x):
    return 1.0 / (1.0 + jnp.exp(-x))


def _cparams(*sem, vmem_mb=None):
    kw = dict(dimension_semantics=sem)
    if vmem_mb is not None:
        kw["vmem_limit_bytes"] = vmem_mb * 1024 * 1024
    return pltpu.CompilerParams(**kw)


def _resident(shape):
    nd = len(shape)
    return pl.BlockSpec(shape, lambda *_: (0,) * nd, pipeline_mode=pl.Buffered(1))


def _norm_body(xp_ref, xs_ref, g_ref, o_ref, *, n_prompt_tiles):
    i = pl.program_id(0)

    def f(x):
        ms = jnp.mean(x * x, axis=-1, keepdims=True)
        return (x * lax.rsqrt(ms + RMS_EPS) * g_ref[...]).astype(o_ref.dtype)

    @pl.when(i < n_prompt_tiles)
    def _():
        o_ref[...] = f(xp_ref[...])

    @pl.when(i >= n_prompt_tiles)
    def _():
        o_ref[...] = f(xs_ref[...])


def _norm_cast(xp, xs, g):
    npt = N_PROMPT // TM
    return pl.pallas_call(
        functools.partial(_norm_body, n_prompt_tiles=npt),
        grid=(N_TOK // TM,),
        in_specs=[pl.BlockSpec((TM, D_MODEL), lambda i: (jnp.minimum(i, npt - 1), 0)),
                  pl.BlockSpec((TM, D_MODEL), lambda i: (jnp.maximum(i - npt, 0), 0)),
                  pl.BlockSpec((1, D_MODEL), lambda i: (0, 0))],
        out_specs=pl.BlockSpec((TM, D_MODEL), lambda i: (i, 0)),
        out_shape=jax.ShapeDtypeStruct((N_TOK, D_MODEL), BF16),
        compiler_params=_cparams("arbitrary"),
        name="norm_cast",
    )(xp, xs, g)


def _inproj_body(xn_ref, w_ref, mu_ref, st_ref, z_ref, last_ref, sraw_ref, wb_ref, carry_ref, *,
                 shift_lo, shift_hi, n_prompt_tiles, tiles_per_seq):
    j = pl.program_id(0)
    i = pl.program_id(1)

    @pl.when(i == 0)
    def _():
        wb_ref[...] = w_ref[...].astype(BF16)

    z = jnp.dot(xn_ref[...], wb_ref[...], preferred_element_type=F32)
    tm = z.shape[0]
    last_ref[...] = z[tm - SUBLANES:tm]
    shifted = jnp.logical_and(j >= shift_lo, j < shift_hi)
    is_prompt = i < n_prompt_tiles

    @pl.when(jnp.logical_not(shifted))
    def _():
        z_ref[...] = z

    @pl.when(jnp.logical_and(shifted, is_prompt))
    def _():
        @pl.when(i % tiles_per_seq == 0)
        def _():
            carry_ref[...] = jnp.zeros_like(carry_ref)

        prev = pltpu.roll(z, 1, 0)
        row = lax.broadcasted_iota(jnp.int32, (tm, 1), 0)
        prev = jnp.where(row == 0, carry_ref[SUBLANES - 1:SUBLANES, :], prev)
        z_ref[...] = z + mu_ref[...] * (prev - z)
        carry_ref[...] = z[tm - SUBLANES:tm]

    @pl.when(jnp.logical_and(shifted, jnp.logical_not(is_prompt)))
    def _():
        prev = jnp.concatenate([st_ref[...], z[:tm - DEC_BATCH]], axis=0)
        z_ref[...] = z + mu_ref[...] * (prev - z)

    @pl.when(jnp.logical_not(is_prompt))
    def _():
        sraw_ref[...] = z[tm - DEC_BATCH:tm]


def _inproj(xn, w, mu, st, *, tn, n_col_blocks, shift_lo, shift_hi):
    n_out = tn * n_col_blocks
    n_tiles = N_TOK // TM
    body = functools.partial(_inproj_body, shift_lo=shift_lo, shift_hi=shift_hi,
                             n_prompt_tiles=N_PROMPT // TM, tiles_per_seq=SEQ // TM)
    return pl.pallas_call(
        body,
        grid=(n_col_blocks, n_tiles),
        in_specs=[pl.BlockSpec((TM, D_MODEL), lambda j, i: (i, 0)),
                  pl.BlockSpec((D_MODEL, tn), lambda j, i: (0, j)),
                  pl.BlockSpec((1, tn), lambda j, i: (0, j)),
                  pl.BlockSpec((DEC_BATCH, tn), lambda j, i: (0, j))],
        out_specs=[pl.BlockSpec((TM, tn), lambda j, i: (i, j)),
                   pl.BlockSpec((SUBLANES, tn), lambda j, i: (i, j)),
                   pl.BlockSpec((DEC_BATCH, tn), lambda j, i: (0, j))],
        out_shape=[jax.ShapeDtypeStruct((N_TOK, n_out), F32),
                   jax.ShapeDtypeStruct((n_tiles * SUBLANES, n_out), F32),
                   jax.ShapeDtypeStruct((DEC_BATCH, n_out), F32)],
        scratch_shapes=[pltpu.VMEM((D_MODEL, tn), BF16), pltpu.VMEM((SUBLANES, tn), F32)],
        compiler_params=_cparams("arbitrary", "arbitrary", vmem_mb=48),
        name="inproj_%d" % tn,
    )(xn, w, mu, st)


def _conv_body(zc_ref, cw_ref, sc_ref, wco_ref, o_ref, ulast_ref, us_ref, carry_ref, *,
               n_prompt_tiles, tiles_per_seq):
    i = pl.program_id(0)
    gate_b = zc_ref[:, 0:CONV_WIDTH]
    u = zc_ref[:, CONV_WIDTH:2 * CONV_WIDTH] * zc_ref[:, 2 * CONV_WIDTH:3 * CONV_WIDTH]
    tm = u.shape[0]
    ulast_ref[...] = u[tm - SUBLANES:tm]
    w0 = cw_ref[0:1, :]
    w1 = cw_ref[1:2, :]
    w2 = cw_ref[2:3, :]

    def finish(p1, p2):
        conv = w0 * p2 + w1 * p1 + w2 * u
        y = (gate_b * conv).astype(BF16)
        o_ref[...] = jnp.dot(y, wco_ref[...], preferred_element_type=F32)

    @pl.when(i < n_prompt_tiles)
    def _():
        @pl.when(i % tiles_per_seq == 0)
        def _():
            carry_ref[...] = jnp.zeros_like(carry_ref)

        row = lax.broadcasted_iota(jnp.int32, (tm, 1), 0)
        c1 = carry_ref[SUBLANES - 1:SUBLANES, :]
        c2 = carry_ref[SUBLANES - 2:SUBLANES - 1, :]
        p1 = jnp.where(row == 0, c1, pltpu.roll(u, 1, 0))
        p2 = jnp.where(row == 0, c2, jnp.where(row == 1, c1, pltpu.roll(u, 2, 0)))
        carry_ref[...] = u[tm - SUBLANES:tm]
        finish(p1, p2)

    @pl.when(i >= n_prompt_tiles)
    def _():
        p1 = jnp.concatenate([sc_ref[DEC_BATCH:2 * DEC_BATCH, :], u[:tm - DEC_BATCH]], axis=0)
        p2 = jnp.concatenate([sc_ref[...], u[:tm - 2 * DEC_BATCH]], axis=0)
        us_ref[...] = u[tm - 2 * DEC_BATCH:tm]
        finish(p1, p2)


def _conv_branch(z_main, conv_w, sc, wco):
    n_tiles = N_TOK // TM
    body = functools.partial(_conv_body, n_prompt_tiles=N_PROMPT // TM, tiles_per_seq=SEQ // TM)
    return pl.pallas_call(
        body,
        grid=(n_tiles,),
        in_specs=[pl.BlockSpec((TM, 3 * CONV_WIDTH), lambda i: (i, 0)),
                  pl.BlockSpec((3, CONV_WIDTH), lambda i: (0, 0)),
                  pl.BlockSpec((2 * DEC_BATCH, CONV_WIDTH), lambda i: (0, 0)),
                  _resident((CONV_WIDTH, D_MODEL))],
        out_specs=[pl.BlockSpec((TM, D_MODEL), lambda i: (i, 0)),
                   pl.BlockSpec((SUBLANES, CONV_WIDTH), lambda i: (i, 0)),
                   pl.BlockSpec((2 * DEC_BATCH, CONV_WIDTH), lambda i: (0, 0))],
        out_shape=[jax.ShapeDtypeStruct((N_TOK, D_MODEL), F32),
                   jax.ShapeDtypeStruct((n_tiles * SUBLANES, CONV_WIDTH), F32),
                   jax.ShapeDtypeStruct((2 * DEC_BATCH, CONV_WIDTH), F32)],
        scratch_shapes=[pltpu.VMEM((SUBLANES, CONV_WIDTH), F32)],
        compiler_params=_cparams("arbitrary", vmem_mb=48),
        name="conv_branch",
    )(z_main, conv_w, sc, wco)


def _lora_body(zl_ref, w2_ref, a2_ref, g2_ref, w0_ref, a0_ref, wl_ref, al_ref, g_ref):
    xw = zl_ref[:, 0:LANES]
    xa = zl_ref[:, LANES:2 * LANES]
    xg = zl_ref[:, 2 * LANES:LORA_W]
    wl_ref[...] = w0_ref[...] + jnp.dot(jnp.tanh(xw).astype(BF16), w2_ref[...], preferred_element_type=F32)
    al_ref[...] = a0_ref[...] + jnp.dot(xa.astype(BF16), a2_ref[...], preferred_element_type=F32)
    g_ref[...] = jnp.dot(_sigmoid(xg).astype(BF16), g2_ref[...], preferred_element_type=F32)


def _lora(z_tail, w2p, a2p, g2b, w0, a0):
    lora_blk = 2 * D_MODEL // LORA_W
    out = jax.ShapeDtypeStruct((N_TOK, RWKV_WIDTH), F32)
    tok_spec = pl.BlockSpec((TM, RWKV_WIDTH), lambda i: (i, 0))
    return pl.pallas_call(
        _lora_body,
        grid=(N_TOK // TM,),
        in_specs=[pl.BlockSpec((TM, LORA_W), lambda i: (i, lora_blk)),
                  _resident((LANES, RWKV_WIDTH)), _resident((LANES, RWKV_WIDTH)),
                  _resident((GATE_LORA, RWKV_WIDTH)),
                  pl.BlockSpec((1, RWKV_WIDTH), lambda i: (0, 0)),
                  pl.BlockSpec((1, RWKV_WIDTH), lambda i: (0, 0))],
        out_specs=[tok_spec, tok_spec, tok_spec],
        out_shape=[out, out, out],
        compiler_params=_cparams("arbitrary"),
        name="lora",
    )(z_tail, w2p, a2p, g2b, w0, a0)


def _scan_body(r_ref, k_ref, v_ref, wl_ref, al_ref, kk_ref, ka_ref, rk_ref, lw_ref, lb_ref, s0_ref,
               y_ref, s_ref, vec_ref, *, tt):
    @pl.when(pl.program_id(1) == 0)
    def _():
        s_ref[...] = s0_ref[...]

    def step(t, carry):
        r = r_ref[t]
        k = k_ref[t]
        v = v_ref[t]
        decay = jnp.exp(-EXP_M05 * _sigmoid(wl_ref[t]))
        a = _sigmoid(al_ref[t])
        kk = k * kk_ref[...]
        nrm = jnp.sqrt(jnp.sum(kk * kk, axis=0, keepdims=True))
        kk = kk / jnp.maximum(nrm, 1e-12)
        kf = k * (1.0 + (a - 1.0) * ka_ref[...])
        vec_ref[0] = -kk
        vec_ref[1] = decay
        vec_ref[2] = kk * a
        vec_ref[3] = kf
        vec_ref[4] = r

        def row(j, kx):
            return vec_ref[j, pl.ds(kx, HEAD_SIZE, stride=0), :]

        parts = [jnp.zeros((HEAD_SIZE, LANES), F32) for _ in range(4)]
        for kx in range(HEAD_SIZE):
            parts[kx % 4] = parts[kx % 4] + s_ref[kx] * row(0, kx)
        sa = (parts[0] + parts[1]) + (parts[2] + parts[3])

        parts = [jnp.zeros((HEAD_SIZE, LANES), F32) for _ in range(4)]
        for kx in range(HEAD_SIZE):
            sn = s_ref[kx] * row(1, kx) + sa * row(2, kx) + v * row(3, kx)
            s_ref[kx] = sn
            parts[kx % 4] = parts[kx % 4] + sn * row(4, kx)
        o = (parts[0] + parts[1]) + (parts[2] + parts[3])

        mu = jnp.mean(o, axis=0, keepdims=True)
        dlt = o - mu
        var = jnp.mean(dlt * dlt, axis=0, keepdims=True)
        on = dlt * lax.rsqrt(var + GN_EPS) * lw_ref[...] + lb_ref[...]
        bonus = jnp.sum(r * kf * rk_ref[...], axis=0, keepdims=True) * v
        y_ref[t] = on + bonus
        return carry

    lax.fori_loop(0, tt, step, 0)


def _scan(r, k, v, wl, al, params, s0, tt):
    g, t = r.shape[0], r.shape[1]
    seq_spec = pl.BlockSpec((None, tt, HEAD_SIZE, LANES), lambda gi, ti: (gi, ti, 0, 0))
    par_spec = pl.BlockSpec((HEAD_SIZE, LANES), lambda gi, ti: (0, 0))
    st_spec = pl.BlockSpec((None, HEAD_SIZE, HEAD_SIZE, LANES), lambda gi, ti: (gi, 0, 0, 0))
    return pl.pallas_call(
        functools.partial(_scan_body, tt=tt),
        grid=(g, t // tt),
        in_specs=[seq_spec] * 5 + [par_spec] * 5 + [st_spec],
        out_specs=[seq_spec, st_spec],
        out_shape=[jax.ShapeDtypeStruct((g, t, HEAD_SIZE, LANES), F32),
                   jax.ShapeDtypeStruct((g, HEAD_SIZE, HEAD_SIZE, LANES), F32)],
        scratch_shapes=[pltpu.VMEM((5, HEAD_SIZE, LANES), F32)],
        compiler_params=_cparams("arbitrary", "arbitrary", vmem_mb=48),
        name="wkv_scan_%d" % t,
    )(r, k, v, wl, al, *params, s0)


def _to_chain_prompt(x):
    return x.reshape(BATCH, SEQ, N_HEADS, HEAD_SIZE).transpose(1, 3, 0, 2).reshape(1, SEQ, HEAD_SIZE, LANES)


def _from_chain_prompt(y):
    return y.reshape(SEQ, HEAD_SIZE, BATCH, N_HEADS).transpose(2, 0, 3, 1).reshape(N_PROMPT, RWKV_WIDTH)


def _to_chain_sample(x):
    ng = DEC_BATCH // CHAIN_B
    x = x.reshape(DEC_SEQ, ng, CHAIN_B, N_HEADS, HEAD_SIZE).transpose(1, 0, 4, 2, 3)
    return x.reshape(ng, DEC_SEQ, HEAD_SIZE, LANES)


def _from_chain_sample(y):
    ng = DEC_BATCH // CHAIN_B
    y = y.reshape(ng, DEC_SEQ, HEAD_SIZE, CHAIN_B, N_HEADS).transpose(1, 0, 3, 4, 2)
    return y.reshape(N_SAMPLE, RWKV_WIDTH)


def _param_chain(p):
    return jnp.tile(p.reshape(N_HEADS, HEAD_SIZE).T, (1, CHAIN_B))


def _rwkv_out_body(y_ref, g_ref, w_ref, o_ref):
    o_ref[...] = jnp.dot((y_ref[...] * g_ref[...]).astype(BF16), w_ref[...], preferred_element_type=F32)


def _rwkv_out(y, g, w):
    tok_spec = pl.BlockSpec((TM, D_MODEL), lambda i: (i, 0))
    return pl.pallas_call(
        _rwkv_out_body,
        grid=(N_TOK // TM,),
        in_specs=[tok_spec, tok_spec, _resident((RWKV_WIDTH, D_MODEL))],
        out_specs=tok_spec,
        out_shape=jax.ShapeDtypeStruct((N_TOK, D_MODEL), F32),
        compiler_params=_cparams("arbitrary"),
        name="rwkv_out",
    )(y, g, w)


def _mix_body(co_ref, ro_ref, gc_ref, gr_ref, xp_ref, xs_ref, wm_ref, nf_ref, wr_ref, br_ref,
              h_ref, hn_ref, ridx_ref, rw_ref, *, n_prompt_tiles):
    i = pl.program_id(0)
    mixed = _sigmoid(gc_ref[...]) * co_ref[...] + _sigmoid(gr_ref[...]) * ro_ref[...]
    mo = jnp.dot(mixed.astype(BF16), wm_ref[...], preferred_element_type=F32)

    def finish(x):
        h = x + mo
        h_ref[...] = h
        ms = jnp.mean(h * h, axis=-1, keepdims=True)
        hn = h * lax.rsqrt(ms + RMS_EPS) * nf_ref[...]
        hn_ref[...] = hn
        logits = jnp.dot(hn, wr_ref[...], preferred_element_type=F32,
                         precision=lax.Precision.HIGHEST) + br_ref[...]
        tm = logits.shape[0]
        lane = lax.broadcasted_iota(jnp.int32, (tm, LANES), 1)
        neg = jnp.float32(-jnp.inf)
        gl = jnp.where(lane < N_GROUPS, logits, neg)
        gmax = jnp.max(gl, axis=-1, keepdims=True)
        g_idx = jnp.min(jnp.where(gl == gmax, lane, LANES), axis=-1, keepdims=True)
        g_w = 1.0 / jnp.sum(jnp.exp(gl - gmax), axis=-1, keepdims=True)
        lo = N_GROUPS + g_idx * EXPERTS_PER_GROUP
        el = jnp.where(jnp.logical_and(lane >= lo, lane < lo + EXPERTS_PER_GROUP), logits, neg)
        m1 = jnp.max(el, axis=-1, keepdims=True)
        i1 = jnp.min(jnp.where(el == m1, lane, LANES), axis=-1, keepdims=True)
        el2 = jnp.where(lane == i1, neg, el)
        m2 = jnp.max(el2, axis=-1, keepdims=True)
        i2 = jnp.min(jnp.where(el2 == m2, lane, LANES), axis=-1, keepdims=True)
        t2 = jnp.exp(m2 - m1)
        den = 1.0 + t2
        ridx_ref[...] = jnp.where(lane == 0, i1 - N_GROUPS, jnp.where(lane == 1, i2 - N_GROUPS, 0))
        rw_ref[...] = jnp.where(lane == 0, (1.0 / den) * g_w, jnp.where(lane == 1, (t2 / den) * g_w, 0.0))

    @pl.when(i < n_prompt_tiles)
    def _():
        finish(xp_ref[...])

    @pl.when(i >= n_prompt_tiles)
    def _():
        finish(xs_ref[...])


def _mix_route(conv_out, rwkv_out, z_tail, xp, xs, wm, nf, wr, br):
    npt = N_PROMPT // TM_S
    tok_spec = pl.BlockSpec((TM_S, D_MODEL), lambda i: (i, 0))
    small_spec = pl.BlockSpec((TM_S, LANES), lambda i: (i, 0))
    return pl.pallas_call(
        functools.partial(_mix_body, n_prompt_tiles=npt),
        grid=(N_TOK // TM_S,),
        in_specs=[tok_spec, tok_spec,
                  pl.BlockSpec((TM_S, D_MODEL), lambda i: (i, 0)),
                  pl.BlockSpec((TM_S, D_MODEL), lambda i: (i, 1)),
                  pl.BlockSpec((TM_S, D_MODEL), lambda i: (jnp.minimum(i, npt - 1), 0)),
                  pl.BlockSpec((TM_S, D_MODEL), lambda i: (jnp.maximum(i - npt, 0), 0)),
                  _resident((D_MODEL, D_MODEL)),
                  pl.BlockSpec((1, D_MODEL), lambda i: (0, 0)),
                  _resident((D_MODEL, LANES)),
                  pl.BlockSpec((1, LANES), lambda i: (0, 0))],
        out_specs=[tok_spec, tok_spec, small_spec, small_spec],
        out_shape=[jax.ShapeDtypeStruct((N_TOK, D_MODEL), F32),
                   jax.ShapeDtypeStruct((N_TOK, D_MODEL), F32),
                   jax.ShapeDtypeStruct((N_TOK, LANES), jnp.int32),
                   jax.ShapeDtypeStruct((N_TOK, LANES), F32)],
        compiler_params=_cparams("arbitrary", vmem_mb=56),
        name="mix_route",
    )(conv_out, rwkv_out, z_tail, z_tail, xp, xs, wm, nf, wr, br)


def _expert_body(blk_e_ref, slot_tok_ref, nused_ref, hn_ref, wg_ref, wu_ref, wd_ref, yb_ref,
                 xbuf, sem, wgb, wub, wdb):
    i = pl.program_id(0)
    nused = nused_ref[0]
    slot = i % 2

    def row_copy(blk, r, s):
        tok = slot_tok_ref[blk * MOE_BLOCK + r]
        return pltpu.make_async_copy(hn_ref.at[pl.ds(tok, 1), :], xbuf.at[s, pl.ds(r, 1), :], sem.at[s])

    def issue(blk, s):
        for r in range(MOE_BLOCK):
            row_copy(blk, r, s).start()

    @pl.when(jnp.logical_and(i == 0, nused > 0))
    def _():
        issue(0, 0)

    @pl.when(i + 1 < nused)
    def _():
        issue(i + 1, 1 - slot)

    @pl.when(i < nused)
    def _():
        prev_e = blk_e_ref[jnp.maximum(i - 1, 0)]

        @pl.when(jnp.logical_or(i == 0, blk_e_ref[i] != prev_e))
        def _():
            wgb[...] = wg_ref[...].astype(BF16)
            wub[...] = wu_ref[...].astype(BF16)
            wdb[...] = wd_ref[...].astype(BF16)

        for r in range(MOE_BLOCK):
            row_copy(i, r, slot).wait()
        xe = xbuf[slot].astype(BF16)
        gate = jnp.dot(xe, wgb[...], preferred_element_type=F32)
        up = jnp.dot(xe, wub[...], preferred_element_type=F32)
        hdn = (gate * _sigmoid(gate)) * up
        yb_ref[...] = jnp.dot(hdn.astype(BF16), wdb[...], preferred_element_type=F32)

    @pl.when(i >= nused)
    def _():
        yb_ref[...] = jnp.zeros_like(yb_ref)


def _experts(blk_e, slot_tok, nused, hn, wg, wu, wd):
    n_blocks = blk_e.shape[0]
    return pl.pallas_call(
        _expert_body,
        grid_spec=pltpu.PrefetchScalarGridSpec(
            num_scalar_prefetch=3,
            grid=(n_blocks,),
            in_specs=[pl.BlockSpec(memory_space=pl.ANY),
                      pl.BlockSpec((None, D_MODEL, D_EXPERT), lambda i, be, st, nu: (be[i], 0, 0)),
                      pl.BlockSpec((None, D_MODEL, D_EXPERT), lambda i, be, st, nu: (be[i], 0, 0)),
                      pl.BlockSpec((None, D_EXPERT, D_MODEL), lambda i, be, st, nu: (be[i], 0, 0))],
            out_specs=pl.BlockSpec((MOE_BLOCK, D_MODEL), lambda i, be, st, nu: (i, 0)),
            scratch_shapes=[pltpu.VMEM((2, MOE_BLOCK, D_MODEL), F32),
                            pltpu.SemaphoreType.DMA((2,)),
                            pltpu.VMEM((D_MODEL, D_EXPERT), BF16),
                            pltpu.VMEM((D_MODEL, D_EXPERT), BF16),
                            pltpu.VMEM((D_EXPERT, D_MODEL), BF16)]),
        out_shape=jax.ShapeDtypeStruct((n_blocks * MOE_BLOCK, D_MODEL), F32),
        compiler_params=_cparams("arbitrary", vmem_mb=48),
        name="experts",
    )(blk_e, slot_tok, nused, hn, wg, wu, wd)


def _combine_body(dest_ref, yb_ref, h_ref, rw_ref, p_ref, wpg_ref, wpp_ref, nf_ref, y_ref, ybuf, sem):
    i = pl.program_id(0)
    tm = h_ref.shape[0]

    def row_copy(r, s):
        d = dest_ref[(i * tm + r) * TOP_K + s]
        return pltpu.make_async_copy(yb_ref.at[pl.ds(d, 1), :], ybuf.at[s, pl.ds(r, 1), :], sem.at[0])

    def issue(r, c):
        row_copy(r, 0).start()
        row_copy(r, 1).start()
        return c

    lax.fori_loop(0, tm, issue, 0)

    def drain(r, c):
        row_copy(r, 0).wait()
        row_copy(r, 1).wait()
        return c

    lax.fori_loop(0, tm, drain, 0)

    rw = rw_ref[...]
    h2 = h_ref[...] + (ybuf[0] * rw[:, 0:1] + ybuf[1] * rw[:, 1:2])
    gate = _sigmoid(jnp.dot(h2.astype(BF16), wpg_ref[...], preferred_element_type=F32))
    pp = jnp.dot(p_ref[...].astype(BF16), wpp_ref[...], preferred_element_type=F32)
    h3 = h2 + gate * pp
    ms = jnp.mean(h3 * h3, axis=-1, keepdims=True)
    y_ref[...] = h3 * lax.rsqrt(ms + RMS_EPS) * nf_ref[...]


def _combine(dest, yb, h, rw, p_all, wpg, wpp, nf):
    return pl.pallas_call(
        _combine_body,
        grid_spec=pltpu.PrefetchScalarGridSpec(
            num_scalar_prefetch=1,
            grid=(N_TOK // TM_S,),
            in_specs=[pl.BlockSpec(memory_space=pl.ANY),
                      pl.BlockSpec((TM_S, D_MODEL), lambda i, d: (i, 0)),
                      pl.BlockSpec((TM_S, LANES), lambda i, d: (i, 0)),
                      pl.BlockSpec((TM_S, PLE_DIM), lambda i, d: (i, 0)),
                      pl.BlockSpec((D_MODEL, D_MODEL), lambda i, d: (0, 0), pipeline_mode=pl.Buffered(1)),
                      pl.BlockSpec((PLE_DIM, D_MODEL), lambda i, d: (0, 0), pipeline_mode=pl.Buffered(1)),
                      pl.BlockSpec((1, D_MODEL), lambda i, d: (0, 0))],
            out_specs=pl.BlockSpec((TM_S, D_MODEL), lambda i, d: (i, 0)),
            scratch_shapes=[pltpu.VMEM((TOP_K, TM_S, D_MODEL), F32),
                            pltpu.SemaphoreType.DMA((1,))]),
        out_shape=jax.ShapeDtypeStruct((N_TOK, D_MODEL), F32),
        compiler_params=_cparams("arbitrary", vmem_mb=48),
        name="combine_ple",
    )(dest, yb, h, rw, p_all, wpg, wpp, nf)


def _dispatch_plan(eidx):
    n_assign = N_TOK * TOP_K
    e_flat = eidx.reshape(n_assign)
    onehot = (e_flat[:, None] == jnp.arange(N_EXPERTS, dtype=jnp.int32)[None, :]).astype(jnp.int32)
    csum = jnp.cumsum(onehot, axis=0)
    counts = csum[-1]
    rank = jnp.sum(csum * onehot, axis=1) - 1
    padded = (counts + MOE_BLOCK - 1) // MOE_BLOCK * MOE_BLOCK
    pad_end = jnp.cumsum(padded)
    pad_start = pad_end - padded
    dest = pad_start[e_flat] + rank
    n_blocks = -(-n_assign // MOE_BLOCK) + N_EXPERTS
    tok = jnp.arange(n_assign, dtype=jnp.int32) // TOP_K
    slot_tok = jnp.zeros((n_blocks * MOE_BLOCK,), jnp.int32).at[dest].set(tok)
    blk_e = jnp.minimum(
        jnp.searchsorted(pad_end, jnp.arange(n_blocks, dtype=jnp.int32) * MOE_BLOCK, side="right"),
        N_EXPERTS - 1).astype(jnp.int32)
    nused = (pad_end[-1] // MOE_BLOCK).astype(jnp.int32).reshape(1)
    return dest.astype(jnp.int32), slot_tok, blk_e, nused


def kernel(x_prompt, x_sample, state_conv, state_shift, state_wkv, p_prompt, p_sample, norm_mix, w_in, conv_w, w_conv_out, shift_mu, w0, w2, a0, a2, g2, k_k, k_a, r_k, lnx_w, lnx_b, w_rwkv_out, w_mix_out, norm_ffn, w_route_group, b_route_group, w_route_expert, b_route_expert, w_exp_gate, w_exp_up, w_exp_down, w_ple_proj, w_ple_gate, norm_final):
    c3 = 3 * CONV_WIDTH
    rw3 = 3 * RWKV_WIDTH
    xp = x_prompt.reshape(N_PROMPT, D_MODEL)
    xs = x_sample.transpose(1, 0, 2).reshape(N_SAMPLE, D_MODEL)
    win = w_in[0]
    mu = shift_mu[0]
    st = state_shift[0]

    def lora_cols(a, rows):
        z32 = jnp.zeros((rows, LANES - DECAY_LORA), a.dtype)
        return jnp.concatenate([a[:, rw3:rw3 + DECAY_LORA], z32,
                                a[:, rw3 + DECAY_LORA:rw3 + DECAY_LORA + A_LORA], z32,
                                a[:, rw3 + DECAY_LORA + A_LORA:]], axis=1)

    xn = _norm_cast(xp, xs, norm_mix)
    mu_main = jnp.concatenate([jnp.zeros((1, c3), F32), mu[None, :rw3]], axis=1)
    st_main = jnp.concatenate([jnp.zeros((DEC_BATCH, c3), F32), st[:, :rw3]], axis=1)
    z_main, last_main, sraw_main = _inproj(xn, win, mu_main, st_main, tn=1024, n_col_blocks=N_MAIN // 1024,
                                           shift_lo=c3 // 1024, shift_hi=N_MAIN // 1024)
    w_tail = jnp.concatenate([win[:, c3 + RWKV_PROJ:], lora_cols(win[:, c3:c3 + RWKV_PROJ], D_MODEL)], axis=1)
    mu_tail = jnp.concatenate([jnp.zeros((1, 2 * D_MODEL), F32), lora_cols(mu[None, :], 1)], axis=1)
    st_tail = jnp.concatenate([jnp.zeros((DEC_BATCH, 2 * D_MODEL), F32), lora_cols(st, DEC_BATCH)], axis=1)
    nb_tail = N_TAIL // LORA_W
    z_tail, last_tail, sraw_tail = _inproj(xn, w_tail, mu_tail, st_tail, tn=LORA_W, n_col_blocks=nb_tail,
                                           shift_lo=nb_tail - 1, shift_hi=nb_tail)

    sc = state_conv[0].transpose(1, 0, 2).reshape(2 * DEC_BATCH, CONV_WIDTH)
    conv_out, ulast, us = _conv_branch(z_main, conv_w[0], sc, w_conv_out[0].astype(BF16))

    def pad_rows(w):
        return jnp.concatenate([w, jnp.zeros((LANES - w.shape[0], w.shape[1]), w.dtype)], axis=0).astype(BF16)

    wl, al, g = _lora(z_tail, pad_rows(w2[0]), pad_rows(a2[0]), g2[0].astype(BF16), w0, a0)
    params = [_param_chain(p) for p in (k_k[0], k_a[0], r_k[0].reshape(RWKV_WIDTH), lnx_w[0], lnx_b[0])]
    rkv = [z_main[:, c3 + n * RWKV_WIDTH:c3 + (n + 1) * RWKV_WIDTH] for n in range(3)]
    seqs = rkv + [wl, al]
    s0_p = jnp.zeros((1, HEAD_SIZE, HEAD_SIZE, LANES), F32)
    y_p, sf_p = _scan(*[_to_chain_prompt(a[:N_PROMPT]) for a in seqs], params, s0_p, SCAN_TT)
    ng = DEC_BATCH // CHAIN_B
    s0_s = state_wkv[0].reshape(ng, CHAIN_B, N_HEADS, HEAD_SIZE, HEAD_SIZE).transpose(0, 4, 3, 1, 2)
    s0_s = s0_s.reshape(ng, HEAD_SIZE, HEAD_SIZE, LANES)
    y_s, sf_s = _scan(*[_to_chain_sample(a[N_PROMPT:]) for a in seqs], params, s0_s, DEC_SEQ)
    y_nat = jnp.concatenate([_from_chain_prompt(y_p), _from_chain_sample(y_s)], axis=0)
    rwkv_out = _rwkv_out(y_nat, g, w_rwkv_out[0].astype(BF16))

    wr = jnp.concatenate([w_route_group[0], w_route_expert[0],
                          jnp.zeros((D_MODEL, LANES - N_GROUPS - N_EXPERTS), F32)], axis=1)
    br = jnp.concatenate([b_route_group[0], b_route_expert[0],
                          jnp.zeros((LANES - N_GROUPS - N_EXPERTS,), F32)])[None, :]
    h, hn, ridx, rw = _mix_route(conv_out, rwkv_out, z_tail, xp, xs, w_mix_out[0].astype(BF16),
                                 norm_ffn, wr, br)

    dest, slot_tok, blk_e, nused = _dispatch_plan(ridx[:, :TOP_K])
    yb = _experts(blk_e, slot_tok, nused, hn, w_exp_gate[0], w_exp_up[0], w_exp_down[0])
    p_all = jnp.concatenate([p_prompt[0].reshape(N_PROMPT, PLE_DIM),
                             p_sample[0].transpose(1, 0, 2).reshape(N_SAMPLE, PLE_DIM)], axis=0)
    y = _combine(dest, yb, h, rw, p_all, w_ple_gate[0].astype(BF16), w_ple_proj[0].astype(BF16),
                 norm_final[None, :])

    y_prompt = y[:N_PROMPT].reshape(BATCH, SEQ, D_MODEL)
    y_sample = y[N_PROMPT:].reshape(DEC_SEQ, DEC_BATCH, D_MODEL).transpose(1, 0, 2)
    tiles_per_seq = SEQ // TM
    seq_last = jnp.arange(BATCH) * tiles_per_seq + tiles_per_seq - 1

    def unpad_lora(a):
        o = 2 * D_MODEL
        return jnp.concatenate([a[:, o:o + DECAY_LORA], a[:, o + LANES:o + LANES + A_LORA],
                                a[:, o + 2 * LANES:]], axis=1)

    conv_p = ulast.reshape(-1, SUBLANES, CONV_WIDTH)[seq_last, SUBLANES - 2:, :][None]
    conv_s = us.reshape(2, DEC_BATCH, CONV_WIDTH).transpose(1, 0, 2)[None]
    lm = last_main.reshape(-1, SUBLANES, N_MAIN)[seq_last, SUBLANES - 1, c3:]
    lt = unpad_lora(last_tail.reshape(-1, SUBLANES, N_TAIL)[seq_last, SUBLANES - 1, :])
    shift_p = jnp.concatenate([lm, lt], axis=1)[None]
    shift_s = jnp.concatenate([sraw_main[:, c3:], unpad_lora(sraw_tail)], axis=1)[None]
    wkv_p = sf_p.reshape(HEAD_SIZE, HEAD_SIZE, BATCH, N_HEADS).transpose(2, 3, 1, 0)[None]
    wkv_s = sf_s.reshape(ng, HEAD_SIZE, HEAD_SIZE, CHAIN_B, N_HEADS).transpose(0, 3, 4, 2, 1)
    wkv_s = wkv_s.reshape(DEC_BATCH, N_HEADS, HEAD_SIZE, HEAD_SIZE)[None]
    return (y_prompt, y_sample, conv_p, shift_p, wkv_p, conv_s, shift_s, wkv_s)
```

```python
import functools

import jax
import jax.numpy as jnp
from jax import lax
from jax.experimental import pallas as pl
from jax.experimental.pallas import tpu as pltpu

D_MODEL = 2048
BATCH = 4
SEQ = 2048
DEC_BATCH = 128
DEC_SEQ = 4
CONV_WIDTH = 1024
RWKV_WIDTH = 2048
HEAD_SIZE = 64
N_HEADS = RWKV_WIDTH // HEAD_SIZE
DECAY_LORA = 96
A_LORA = 96
GATE_LORA = 256
RWKV_PROJ = 3 * RWKV_WIDTH + DECAY_LORA + A_LORA + GATE_LORA
N_GROUPS = 8
EXPERTS_PER_GROUP = 8
N_EXPERTS = N_GROUPS * EXPERTS_PER_GROUP
TOP_K = 2
D_EXPERT = 512
MOE_BLOCK = 128
PLE_DIM = 256
RMS_EPS = 1e-6
GN_EPS = 64e-5

N_PROMPT = BATCH * SEQ
N_SAMPLE = DEC_BATCH * DEC_SEQ
N_TOK = N_PROMPT + N_SAMPLE
LANES = 128
SUBLANES = 8
TM = 512
TM_S = 256
TM_T = 256
TM_L = 128
CHAIN_B = LANES // N_HEADS
SCAN_TT = 64
LORA_W = 512
N_TAIL = 2 * D_MODEL + LORA_W
EXP_M05 = 0.6065306597126334
F32 = jnp.float32
BF16 = jnp.bfloat16
_NT = (((1,), (1,)), ((), ()))


def _sigmoid(x):
    return 1.0 / (1.0 + jnp.exp(-x))


def _cparams(*sem, vmem_mb=None):
    kw = dict(dimension_semantics=sem)
    if vmem_mb is not None:
        kw["vmem_limit_bytes"] = vmem_mb * 1024 * 1024
    return pltpu.CompilerParams(**kw)


def _resident(shape):
    nd = len(shape)
    return pl.BlockSpec(shape, lambda *_: (0,) * nd, pipeline_mode=pl.Buffered(1))


def _store_chain(zt, out_ref, halves):
    for half in range(halves):
        for k in range(HEAD_SIZE):
            m = jnp.concatenate([zt[bb, half, pl.ds(k, N_HEADS, stride=HEAD_SIZE), :]
                                 for bb in range(CHAIN_B)], axis=0)
            out_ref[pl.ds(half * LANES * HEAD_SIZE + k, LANES, stride=HEAD_SIZE), :] = m.T


def _load_chain(y_ref, yt, halves):
    for half in range(halves):
        for v in range(HEAD_SIZE):
            mt = y_ref[pl.ds(half * LANES * HEAD_SIZE + v, LANES, stride=HEAD_SIZE), :].T
            for bb in range(CHAIN_B):
                yt[bb, half, pl.ds(v, N_HEADS, stride=HEAD_SIZE), :] = mt[bb * N_HEADS:(bb + 1) * N_HEADS, :]


def _norm_body(xp_ref, xs_ref, g_ref, o_ref, *, n_prompt_tiles):
    i = pl.program_id(0)

    def f(x):
        ms = jnp.mean(x * x, axis=-1, keepdims=True)
        return (x * lax.rsqrt(ms + RMS_EPS) * g_ref[...]).astype(o_ref.dtype)

    @pl.when(i < n_prompt_tiles)
    def _():
        o_ref[...] = f(xp_ref[...])

    @pl.when(i >= n_prompt_tiles)
    def _():
        o_ref[...] = f(xs_ref[...])


def _norm_cast(xp, xs, g):
    npt = N_PROMPT // TM
    return pl.pallas_call(
        functools.partial(_norm_body, n_prompt_tiles=npt),
        grid=(N_TOK // TM,),
        in_specs=[pl.BlockSpec((TM, D_MODEL), lambda i: (jnp.minimum(i, npt - 1), 0)),
                  pl.BlockSpec((TM, D_MODEL), lambda i: (jnp.maximum(i - npt, 0), 0)),
                  pl.BlockSpec((1, D_MODEL), lambda i: (0, 0))],
        out_specs=pl.BlockSpec((TM, D_MODEL), lambda i: (i, 0)),
        out_shape=jax.ShapeDtypeStruct((N_TOK, D_MODEL), BF16),
        compiler_params=_cparams("arbitrary"),
        name="norm_cast",
    )(xp, xs, g)


def _inproj_body(xn_ref, w_ref, mu_ref, st_ref, z_ref, last_ref, sraw_ref, wb_ref, carry_ref, *,
                 shift_lo, shift_hi, tile_lo, n_prompt_tiles, tiles_per_seq):
    j = pl.program_id(0)
    i = pl.program_id(1) + tile_lo

    @pl.when(pl.program_id(1) == 0)
    def _():
        wb_ref[...] = w_ref[...].astype(BF16)

    z = jnp.dot(xn_ref[...], wb_ref[...], preferred_element_type=F32)
    tm = z.shape[0]
    last_ref[...] = z[tm - SUBLANES:tm]
    shifted = jnp.logical_and(j >= shift_lo, j < shift_hi)
    is_prompt = i < n_prompt_tiles

    @pl.when(jnp.logical_not(shifted))
    def _():
        z_ref[...] = z

    @pl.when(jnp.logical_and(shifted, is_prompt))
    def _():
        @pl.when(i % tiles_per_seq == 0)
        def _():
            carry_ref[...] = jnp.zeros_like(carry_ref)

        prev = pltpu.roll(z, 1, 0)
        row = lax.broadcasted_iota(jnp.int32, (tm, 1), 0)
        prev = jnp.where(row == 0, carry_ref[SUBLANES - 1:SUBLANES, :], prev)
        z_ref[...] = z + mu_ref[...] * (prev - z)
        carry_ref[...] = z[tm - SUBLANES:tm]

    @pl.when(jnp.logical_and(shifted, jnp.logical_not(is_prompt)))
    def _():
        prev = jnp.concatenate([st_ref[...], z[:tm - DEC_BATCH]], axis=0)
        z_ref[...] = z + mu_ref[...] * (prev - z)

    @pl.when(jnp.logical_not(is_prompt))
    def _():
        sraw_ref[...] = z[tm - DEC_BATCH:tm]


def _inproj(xn, w, mu, st, *, tn, col_blk_off, n_col_blocks, shift_lo, shift_hi, tile_lo, n_tiles, name):
    n_out = tn * n_col_blocks
    body = functools.partial(_inproj_body, shift_lo=shift_lo, shift_hi=shift_hi, tile_lo=tile_lo,
                             n_prompt_tiles=N_PROMPT // TM, tiles_per_seq=SEQ // TM)
    return pl.pallas_call(
        body,
        grid=(n_col_blocks, n_tiles),
        in_specs=[pl.BlockSpec((TM, D_MODEL), lambda j, i: (i + tile_lo, 0)),
                  pl.BlockSpec((D_MODEL, tn), lambda j, i: (0, j + col_blk_off)),
                  pl.BlockSpec((1, tn), lambda j, i: (0, j)),
                  pl.BlockSpec((DEC_BATCH, tn), lambda j, i: (0, j))],
        out_specs=[pl.BlockSpec((TM, tn), lambda j, i: (i, j)),
                   pl.BlockSpec((SUBLANES, tn), lambda j, i: (i, j)),
                   pl.BlockSpec((DEC_BATCH, tn), lambda j, i: (0, j))],
        out_shape=[jax.ShapeDtypeStruct((n_tiles * TM, n_out), F32),
                   jax.ShapeDtypeStruct((n_tiles * SUBLANES, n_out), F32),
                   jax.ShapeDtypeStruct((DEC_BATCH, n_out), F32)],
        scratch_shapes=[pltpu.VMEM((D_MODEL, tn), BF16), pltpu.VMEM((SUBLANES, tn), F32)],
        compiler_params=_cparams("arbitrary", "arbitrary", vmem_mb=48),
        name=name,
    )(xn, w, mu, st)


def _inproj_t_body(xn_ref, wt_ref, o_ref, zt):
    b = pl.program_id(2)
    z = lax.dot_general(wt_ref[...], xn_ref[...], _NT, preferred_element_type=F32)
    for half in range(TM_T // LANES):
        zt[b, half] = z[:, half * LANES:(half + 1) * LANES]

    @pl.when(b == CHAIN_B - 1)
    def _():
        _store_chain(zt, o_ref, TM_T // LANES)


def _inproj_t(xn, wt):
    n_tb = SEQ // TM_T
    return pl.pallas_call(
        _inproj_t_body,
        grid=(3, n_tb, CHAIN_B),
        in_specs=[pl.BlockSpec((TM_T, D_MODEL), lambda j, tb, b: (b * n_tb + tb, 0)),
                  pl.BlockSpec((RWKV_WIDTH, D_MODEL), lambda j, tb, b: (j, 0))],
        out_specs=pl.BlockSpec((None, TM_T * HEAD_SIZE, LANES), lambda j, tb, b: (j, tb, 0)),
        out_shape=jax.ShapeDtypeStruct((3, SEQ * HEAD_SIZE, LANES), F32),
        scratch_shapes=[pltpu.VMEM((CHAIN_B, TM_T // LANES, RWKV_WIDTH, LANES), F32)],
        compiler_params=_cparams("arbitrary", "arbitrary", "arbitrary", vmem_mb=56),
        name="inproj_t",
    )(xn, wt)


def _conv_body(zc_ref, cw_ref, sc_ref, wco_ref, o_ref, ulast_ref, us_ref, carry_ref, *,
               n_prompt_tiles, tiles_per_seq):
    i = pl.program_id(0)
    gate_b = zc_ref[:, 0:CONV_WIDTH]
    u = zc_ref[:, CONV_WIDTH:2 * CONV_WIDTH] * zc_ref[:, 2 * CONV_WIDTH:3 * CONV_WIDTH]
    tm = u.shape[0]
    ulast_ref[...] = u[tm - SUBLANES:tm]
    w0 = cw_ref[0:1, :]
    w1 = cw_ref[1:2, :]
    w2 = cw_ref[2:3, :]

    def finish(p1, p2):
        conv = w0 * p2 + w1 * p1 + w2 * u
        y = (gate_b * conv).astype(BF16)
        o_ref[...] = jnp.dot(y, wco_ref[...], preferred_element_type=F32)

    @pl.when(i < n_prompt_tiles)
    def _():
        @pl.when(i % tiles_per_seq == 0)
        def _():
            carry_ref[...] = jnp.zeros_like(carry_ref)

        row = lax.broadcasted_iota(jnp.int32, (tm, 1), 0)
        c1 = carry_ref[SUBLANES - 1:SUBLANES, :]
        c2 = carry_ref[SUBLANES - 2:SUBLANES - 1, :]
        p1 = jnp.where(row == 0, c1, pltpu.roll(u, 1, 0))
        p2 = jnp.where(row == 0, c2, jnp.where(row == 1, c1, pltpu.roll(u, 2, 0)))
        carry_ref[...] = u[tm - SUBLANES:tm]
        finish(p1, p2)

    @pl.when(i >= n_prompt_tiles)
    def _():
        p1 = jnp.concatenate([sc_ref[DEC_BATCH:2 * DEC_BATCH, :], u[:tm - DEC_BATCH]], axis=0)
        p2 = jnp.concatenate([sc_ref[...], u[:tm - 2 * DEC_BATCH]], axis=0)
        us_ref[...] = u[tm - 2 * DEC_BATCH:tm]
        finish(p1, p2)


def _conv_branch(z_conv, conv_w, sc, wco):
    n_tiles = N_TOK // TM
    body = functools.partial(_conv_body, n_prompt_tiles=N_PROMPT // TM, tiles_per_seq=SEQ // TM)
    return pl.pallas_call(
        body,
        grid=(n_tiles,),
        in_specs=[pl.BlockSpec((TM, 3 * CONV_WIDTH), lambda i: (i, 0)),
                  pl.BlockSpec((3, CONV_WIDTH), lambda i: (0, 0)),
                  pl.BlockSpec((2 * DEC_BATCH, CONV_WIDTH), lambda i: (0, 0)),
                  _resident((CONV_WIDTH, D_MODEL))],
        out_specs=[pl.BlockSpec((TM, D_MODEL), lambda i: (i, 0)),
                   pl.BlockSpec((SUBLANES, CONV_WIDTH), lambda i: (i, 0)),
                   pl.BlockSpec((2 * DEC_BATCH, CONV_WIDTH), lambda i: (0, 0))],
        out_shape=[jax.ShapeDtypeStruct((N_TOK, D_MODEL), F32),
                   jax.ShapeDtypeStruct((n_tiles * SUBLANES, CONV_WIDTH), F32),
                   jax.ShapeDtypeStruct((2 * DEC_BATCH, CONV_WIDTH), F32)],
        scratch_shapes=[pltpu.VMEM((SUBLANES, CONV_WIDTH), F32)],
        compiler_params=_cparams("arbitrary", vmem_mb=48),
        name="conv_branch",
    )(z_conv, conv_w, sc, wco)


def _lora_body(zl_ref, w2t_ref, a2t_ref, w2_ref, a2_ref, g2_ref,
               g_ref, wlc_ref, alc_ref, wls_ref, als_ref, zt, *, n_prompt_steps):
    s = pl.program_id(0)
    tw = jnp.tanh(zl_ref[:, 0:LANES]).astype(BF16)
    xa = zl_ref[:, LANES:2 * LANES].astype(BF16)
    xg = zl_ref[:, 2 * LANES:LORA_W]
    g_ref[...] = jnp.dot(_sigmoid(xg).astype(BF16), g2_ref[...], preferred_element_type=F32)

    @pl.when(s < n_prompt_steps)
    def _():
        b = s % CHAIN_B
        zt[0, b, 0] = lax.dot_general(w2t_ref[...], tw, _NT, preferred_element_type=F32)
        zt[1, b, 0] = lax.dot_general(a2t_ref[...], xa, _NT, preferred_element_type=F32)

        @pl.when(b == CHAIN_B - 1)
        def _():
            _store_chain(zt.at[0], wlc_ref, 1)
            _store_chain(zt.at[1], alc_ref, 1)

    @pl.when(s >= n_prompt_steps)
    def _():
        wls_ref[...] = jnp.dot(tw, w2_ref[...], preferred_element_type=F32)
        als_ref[...] = jnp.dot(xa, a2_ref[...], preferred_element_type=F32)


def _lora(z_tail, w2t, a2t, w2p, a2p, g2b):
    n_tb = SEQ // TM_L
    nps = n_tb * CHAIN_B
    lora_blk = 2 * D_MODEL // LORA_W

    def row_blk(s):
        return jnp.where(s < nps, (s % CHAIN_B) * n_tb + s // CHAIN_B, s)

    chain_spec = pl.BlockSpec((TM_L * HEAD_SIZE, LANES), lambda s: (jnp.minimum(s // CHAIN_B, n_tb - 1), 0))
    samp_spec = pl.BlockSpec((TM_L, RWKV_WIDTH), lambda s: (jnp.maximum(s - nps, 0), 0))
    chain_shape = jax.ShapeDtypeStruct((SEQ * HEAD_SIZE, LANES), F32)
    samp_shape = jax.ShapeDtypeStruct((N_SAMPLE, RWKV_WIDTH), F32)
    return pl.pallas_call(
        functools.partial(_lora_body, n_prompt_steps=nps),
        grid=(N_TOK // TM_L,),
        in_specs=[pl.BlockSpec((TM_L, LORA_W), lambda s: (row_blk(s), lora_blk)),
                  _resident((RWKV_WIDTH, LANES)), _resident((RWKV_WIDTH, LANES)),
                  _resident((LANES, RWKV_WIDTH)), _resident((LANES, RWKV_WIDTH)),
                  _resident((GATE_LORA, RWKV_WIDTH))],
        out_specs=[pl.BlockSpec((TM_L, RWKV_WIDTH), lambda s: (row_blk(s), 0)),
                   chain_spec, chain_spec, samp_spec, samp_spec],
        out_shape=[jax.ShapeDtypeStruct((N_TOK, RWKV_WIDTH), F32),
                   chain_shape, chain_shape, samp_shape, samp_shape],
        scratch_shapes=[pltpu.VMEM((2, CHAIN_B, 1, RWKV_WIDTH, LANES), F32)],
        compiler_params=_cparams("arbitrary", vmem_mb=48),
        name="lora",
    )(z_tail, w2t, a2t, w2p, a2p, g2b)


def _scan_body(r_ref, k_ref, v_ref, wl_ref, al_ref, kk_ref, ka_ref, rk_ref, lw_ref, lb_ref,
               mur_ref, muk_ref, muv_ref, w0_ref, a0_ref, s0_ref,
               y_ref, s_ref, vec_ref, prev_ref, *, tt):
    @pl.when(pl.program_id(1) == 0)
    def _():
        s_ref[...] = s0_ref[...]
        prev_ref[...] = jnp.zeros_like(prev_ref)

    def step(t, carry):
        r_raw = r_ref[t]
        k_raw = k_ref[t]
        v_raw = v_ref[t]
        r = r_raw + mur_ref[...] * (prev_ref[0] - r_raw)
        k = k_raw + muk_ref[...] * (prev_ref[1] - k_raw)
        v = v_raw + muv_ref[...] * (prev_ref[2] - v_raw)
        prev_ref[0] = r_raw
        prev_ref[1] = k_raw
        prev_ref[2] = v_raw
        decay = jnp.exp(-EXP_M05 * _sigmoid(wl_ref[t] + w0_ref[...]))
        a = _sigmoid(al_ref[t] + a0_ref[...])
        kk = k * kk_ref[...]
        nrm = jnp.sqrt(jnp.sum(kk * kk, axis=0, keepdims=True))
        kk = kk / jnp.maximum(nrm, 1e-12)
        kf = k * (1.0 + (a - 1.0) * ka_ref[...])
        vec_ref[0] = -kk
        vec_ref[1] = decay
        vec_ref[2] = kk * a
        vec_ref[3] = kf
        vec_ref[4] = r

        def row(j, kx):
            return vec_ref[j, pl.ds(kx, HEAD_SIZE, stride=0), :]

        parts = [jnp.zeros((HEAD_SIZE, LANES), F32) for _ in range(4)]
        for kx in range(HEAD_SIZE):
            parts[kx % 4] = parts[kx % 4] + s_ref[kx] * row(0, kx)
        sa = (parts[0] + parts[1]) + (parts[2] + parts[3])

        parts = [jnp.zeros((HEAD_SIZE, LANES), F32) for _ in range(4)]
        for kx in range(HEAD_SIZE):
            sn = s_ref[kx] * row(1, kx) + sa * row(2, kx) + v * row(3, kx)
            s_ref[kx] = sn
            parts[kx % 4] = parts[kx % 4] + sn * row(4, kx)
        o = (parts[0] + parts[1]) + (parts[2] + parts[3])

        mu = jnp.mean(o, axis=0, keepdims=True)
        dlt = o - mu
        var = jnp.mean(dlt * dlt, axis=0, keepdims=True)
        on = dlt * lax.rsqrt(var + GN_EPS) * lw_ref[...] + lb_ref[...]
        bonus = jnp.sum(r * kf * rk_ref[...], axis=0, keepdims=True) * v
        y_ref[t] = on + bonus
        return carry

    lax.fori_loop(0, tt, step, 0)


def _scan(seqs, params, s0, tt, name):
    g, t = s0.shape[0], seqs[0][0].shape[1]

    def seq_spec(lead):
        if lead is None:
            return pl.BlockSpec((None, tt, HEAD_SIZE, LANES), lambda gi, ti: (gi, ti, 0, 0))
        return pl.BlockSpec((None, tt, HEAD_SIZE, LANES), lambda gi, ti: (lead, ti, 0, 0))

    par_spec = pl.BlockSpec((HEAD_SIZE, LANES), lambda gi, ti: (0, 0))
    st_spec = pl.BlockSpec((None, HEAD_SIZE, HEAD_SIZE, LANES), lambda gi, ti: (gi, 0, 0, 0))
    return pl.pallas_call(
        functools.partial(_scan_body, tt=tt),
        grid=(g, t // tt),
        in_specs=[seq_spec(lead) for _, lead in seqs] + [par_spec] * len(params) + [st_spec],
        out_specs=[seq_spec(None), st_spec],
        out_shape=[jax.ShapeDtypeStruct((g, t, HEAD_SIZE, LANES), F32),
                   jax.ShapeDtypeStruct((g, HEAD_SIZE, HEAD_SIZE, LANES), F32)],
        scratch_shapes=[pltpu.VMEM((5, HEAD_SIZE, LANES), F32), pltpu.VMEM((3, HEAD_SIZE, LANES), F32)],
        compiler_params=_cparams("arbitrary", "arbitrary", vmem_mb=48),
        name=name,
    )(*[a for a, _ in seqs], *params, s0)


def _to_chain_sample(x):
    ng = DEC_BATCH // CHAIN_B
    x = x.reshape(DEC_SEQ, ng, CHAIN_B, N_HEADS, HEAD_SIZE).transpose(1, 0, 4, 2, 3)
    return x.reshape(ng, DEC_SEQ, HEAD_SIZE, LANES)


def _from_chain_sample(y):
    ng = DEC_BATCH // CHAIN_B
    y = y.reshape(ng, DEC_SEQ, HEAD_SIZE, CHAIN_B, N_HEADS).transpose(1, 0, 3, 4, 2)
    return y.reshape(N_SAMPLE, RWKV_WIDTH)


def _param_chain(p):
    return jnp.tile(p.reshape(N_HEADS, HEAD_SIZE).T, (1, CHAIN_B))


def _rwkv_out_body(yc_ref, ys_ref, g_ref, w_ref, o_ref, yt, *, n_prompt_steps):
    s = pl.program_id(0)

    def finish(y):
        o_ref[...] = jnp.dot((y * g_ref[...]).astype(BF16), w_ref[...], preferred_element_type=F32)

    @pl.when(s < n_prompt_steps)
    def _():
        b = s % CHAIN_B

        @pl.when(b == 0)
        def _():
            _load_chain(yc_ref, yt, TM_T // LANES)

        finish(jnp.concatenate([yt[b, half].T for half in range(TM_T // LANES)], axis=0))

    @pl.when(s >= n_prompt_steps)
    def _():
        finish(ys_ref[...])


def _rwkv_out(y_chain, y_s, g, w):
    n_tb = SEQ // TM_T
    nps = n_tb * CHAIN_B

    def row_blk(s):
        return jnp.where(s < nps, (s % CHAIN_B) * n_tb + s // CHAIN_B, s)

    return pl.pallas_call(
        functools.partial(_rwkv_out_body, n_prompt_steps=nps),
        grid=(N_TOK // TM_T,),
        in_specs=[pl.BlockSpec((TM_T * HEAD_SIZE, LANES), lambda s: (jnp.minimum(s // CHAIN_B, n_tb - 1), 0)),
                  pl.BlockSpec((TM_T, RWKV_WIDTH), lambda s: (jnp.maximum(s - nps, 0), 0)),
                  pl.BlockSpec((TM_T, RWKV_WIDTH), lambda s: (row_blk(s), 0)),
                  _resident((RWKV_WIDTH, D_MODEL))],
        out_specs=pl.BlockSpec((TM_T, D_MODEL), lambda s: (row_blk(s), 0)),
        out_shape=jax.ShapeDtypeStruct((N_TOK, D_MODEL), F32),
        scratch_shapes=[pltpu.VMEM((CHAIN_B, TM_T // LANES, RWKV_WIDTH, LANES), F32)],
        compiler_params=_cparams("arbitrary", vmem_mb=56),
        name="rwkv_out",
    )(y_chain, y_s, g, w)


def _mix_body(co_ref, ro_ref, gc_ref, gr_ref, xp_ref, xs_ref, wm_ref, nf_ref, wr_ref, br_ref,
              h_ref, hn_ref, ridx_ref, rw_ref, *, n_prompt_tiles):
    i = pl.program_id(0)
    mixed = _sigmoid(gc_ref[...]) * co_ref[...] + _sigmoid(gr_ref[...]) * ro_ref[...]
    mo = jnp.dot(mixed.astype(BF16), wm_ref[...], preferred_element_type=F32)

    def finish(x):
        h = x + mo
        h_ref[...] = h
        ms = jnp.mean(h * h, axis=-1, keepdims=True)
        hn = h * lax.rsqrt(ms + RMS_EPS) * nf_ref[...]
        hn_ref[...] = hn
        logits = jnp.dot(hn, wr_ref[...], preferred_element_type=F32,
                         precision=lax.Precision.HIGHEST) + br_ref[...]
        tm = logits.shape[0]
        lane = lax.broadcasted_iota(jnp.int32, (tm, LANES), 1)
        neg = jnp.float32(-jnp.inf)
        gl = jnp.where(lane < N_GROUPS, logits, neg)
        gmax = jnp.max(gl, axis=-1, keepdims=True)
        g_idx = jnp.min(jnp.where(gl == gmax, lane, LANES), axis=-1, keepdims=True)
        g_w = 1.0 / jnp.sum(jnp.exp(gl - gmax), axis=-1, keepdims=True)
        lo = N_GROUPS + g_idx * EXPERTS_PER_GROUP
        el = jnp.where(jnp.logical_and(lane >= lo, lane < lo + EXPERTS_PER_GROUP), logits, neg)
        m1 = jnp.max(el, axis=-1, keepdims=True)
        i1 = jnp.min(jnp.where(el == m1, lane, LANES), axis=-1, keepdims=True)
        el2 = jnp.where(lane == i1, neg, el)
        m2 = jnp.max(el2, axis=-1, keepdims=True)
        i2 = jnp.min(jnp.where(el2 == m2, lane, LANES), axis=-1, keepdims=True)
        t2 = jnp.exp(m2 - m1)
        den = 1.0 + t2
        ridx_ref[...] = jnp.where(lane == 0, i1 - N_GROUPS, jnp.where(lane == 1, i2 - N_GROUPS, 0))
        rw_ref[...] = jnp.where(lane == 0, (1.0 / den) * g_w, jnp.where(lane == 1, (t2 / den) * g_w, 0.0))

    @pl.when(i < n_prompt_tiles)
    def _():
        finish(xp_ref[...])

    @pl.when(i >= n_prompt_tiles)
    def _():
        finish(xs_ref[...])


def _mix_route(conv_out, rwkv_out, z_tail, xp, xs, wm, nf, wr, br):
    npt = N_PROMPT // TM_S
    tok_spec = pl.BlockSpec((TM_S, D_MODEL), lambda i: (i, 0))
    small_spec = pl.BlockSpec((TM_S, LANES), lambda i: (i, 0))
    return pl.pallas_call(
        functools.partial(_mix_body, n_prompt_tiles=npt),
        grid=(N_TOK // TM_S,),
        in_specs=[tok_spec, tok_spec,
                  pl.BlockSpec((TM_S, D_MODEL), lambda i: (i, 0)),
                  pl.BlockSpec((TM_S, D_MODEL), lambda i: (i, 1)),
                  pl.BlockSpec((TM_S, D_MODEL), lambda i: (jnp.minimum(i, npt - 1), 0)),
                  pl.BlockSpec((TM_S, D_MODEL), lambda i: (jnp.maximum(i - npt, 0), 0)),
                  _resident((D_MODEL, D_MODEL)),
                  pl.BlockSpec((1, D_MODEL), lambda i: (0, 0)),
                  _resident((D_MODEL, LANES)),
                  pl.BlockSpec((1, LANES), lambda i: (0, 0))],
        out_specs=[tok_spec, tok_spec, small_spec, small_spec],
        out_shape=[jax.ShapeDtypeStruct((N_TOK, D_MODEL), F32),
                   jax.ShapeDtypeStruct((N_TOK, D_MODEL), F32),
                   jax.ShapeDtypeStruct((N_TOK, LANES), jnp.int32),
                   jax.ShapeDtypeStruct((N_TOK, LANES), F32)],
        compiler_params=_cparams("arbitrary", vmem_mb=56),
        name="mix_route",
    )(conv_out, rwkv_out, z_tail, z_tail, xp, xs, wm, nf, wr, br)


def _expert_body(blk_e_ref, slot_tok_ref, nused_ref, first_ref, par_ref, next_e_ref,
                 hn_ref, wg_hbm, wu_hbm, wd_hbm, yb_ref,
                 xbuf, sem, wfg, wfu, wfd, wsem, wgb, wub, wdb):
    i = pl.program_id(0)
    nused = nused_ref[0]
    slot = i % 2

    def row_copy(blk, r, s):
        tok = slot_tok_ref[blk * MOE_BLOCK + r]
        return pltpu.make_async_copy(hn_ref.at[pl.ds(tok, 1), :], xbuf.at[s, pl.ds(r, 1), :], sem.at[s])

    def issue(blk, s):
        for r in range(MOE_BLOCK):
            row_copy(blk, r, s).start()

    def w_copies(e, s):
        return (pltpu.make_async_copy(wg_hbm.at[e], wfg.at[s], wsem.at[s]),
                pltpu.make_async_copy(wu_hbm.at[e], wfu.at[s], wsem.at[s]),
                pltpu.make_async_copy(wd_hbm.at[e], wfd.at[s], wsem.at[s]))

    @pl.when(jnp.logical_and(i == 0, nused > 0))
    def _():
        for c in w_copies(blk_e_ref[0], 0):
            c.start()
        issue(0, 0)

    @pl.when(i + 1 < nused)
    def _():
        issue(i + 1, 1 - slot)

    @pl.when(i < nused)
    def _():
        @pl.when(first_ref[i] == 1)
        def _():
            ws = par_ref[i]
            for c in w_copies(blk_e_ref[i], ws):
                c.wait()

            @pl.when(next_e_ref[i] >= 0)
            def _():
                for c in w_copies(next_e_ref[i], 1 - ws):
                    c.start()

            wgb[...] = wfg[ws].astype(BF16)
            wub[...] = wfu[ws].astype(BF16)
            wdb[...] = wfd[ws].astype(BF16)

        for r in range(MOE_BLOCK):
            row_copy(i, r, slot).wait()
        xe = xbuf[slot].astype(BF16)
        gate = jnp.dot(xe, wgb[...], preferred_element_type=F32)
        up = jnp.dot(xe, wub[...], preferred_element_type=F32)
        hdn = (gate * _sigmoid(gate)) * up
        yb_ref[...] = jnp.dot(hdn.astype(BF16), wdb[...], preferred_element_type=F32)

    @pl.when(i >= nused)
    def _():
        yb_ref[...] = jnp.zeros_like(yb_ref)


def _experts(plan, hn, wg, wu, wd):
    blk_e, slot_tok, nused, first, par, next_e = plan
    n_blocks = blk_e.shape[0]
    any_spec = pl.BlockSpec(memory_space=pl.ANY)
    return pl.pallas_call(
        _expert_body,
        grid_spec=pltpu.PrefetchScalarGridSpec(
            num_scalar_prefetch=6,
            grid=(n_blocks,),
            in_specs=[any_spec, any_spec, any_spec, any_spec],
            out_specs=pl.BlockSpec((MOE_BLOCK, D_MODEL), lambda i, *_: (i, 0)),
            scratch_shapes=[pltpu.VMEM((2, MOE_BLOCK, D_MODEL), F32),
                            pltpu.SemaphoreType.DMA((2,)),
                            pltpu.VMEM((2, D_MODEL, D_EXPERT), F32),
                            pltpu.VMEM((2, D_MODEL, D_EXPERT), F32),
                            pltpu.VMEM((2, D_EXPERT, D_MODEL), F32),
                            pltpu.SemaphoreType.DMA((2,)),
                            pltpu.VMEM((D_MODEL, D_EXPERT), BF16),
                            pltpu.VMEM((D_MODEL, D_EXPERT), BF16),
                            pltpu.VMEM((D_EXPERT, D_MODEL), BF16)]),
        out_shape=jax.ShapeDtypeStruct((n_blocks * MOE_BLOCK, D_MODEL), F32),
        compiler_params=_cparams("arbitrary", vmem_mb=48),
        name="experts",
    )(blk_e, slot_tok, nused, first, par, next_e, hn, wg, wu, wd)


def _combine_body(dest_ref, yb_ref, h_ref, rw_ref, p_ref, wpg_ref, wpp_ref, nf_ref, y_ref, ybuf, sem):
    i = pl.program_id(0)
    tm = h_ref.shape[0]

    def row_copy(r, s):
        d = dest_ref[(i * tm + r) * TOP_K + s]
        return pltpu.make_async_copy(yb_ref.at[pl.ds(d, 1), :], ybuf.at[s, pl.ds(r, 1), :], sem.at[0])

    def issue(r, c):
        row_copy(r, 0).start()
        row_copy(r, 1).start()
        return c

    lax.fori_loop(0, tm, issue, 0)

    def drain(r, c):
        row_copy(r, 0).wait()
        row_copy(r, 1).wait()
        return c

    lax.fori_loop(0, tm, drain, 0)

    rw = rw_ref[...]
    h2 = h_ref[...] + (ybuf[0] * rw[:, 0:1] + ybuf[1] * rw[:, 1:2])
    gate = _sigmoid(jnp.dot(h2.astype(BF16), wpg_ref[...], preferred_element_type=F32))
    pp = jnp.dot(p_ref[...].astype(BF16), wpp_ref[...], preferred_element_type=F32)
    h3 = h2 + gate * pp
    ms = jnp.mean(h3 * h3, axis=-1, keepdims=True)
    y_ref[...] = h3 * lax.rsqrt(ms + RMS_EPS) * nf_ref[...]


def _combine(dest, yb, h, rw, p_all, wpg, wpp, nf):
    return pl.pallas_call(
        _combine_body,
        grid_spec=pltpu.PrefetchScalarGridSpec(
            num_scalar_prefetch=1,
            grid=(N_TOK // TM_S,),
            in_specs=[pl.BlockSpec(memory_space=pl.ANY),
                      pl.BlockSpec((TM_S, D_MODEL), lambda i, d: (i, 0)),
                      pl.BlockSpec((TM_S, LANES), lambda i, d: (i, 0)),
                      pl.BlockSpec((TM_S, PLE_DIM), lambda i, d: (i, 0)),
                      pl.BlockSpec((D_MODEL, D_MODEL), lambda i, d: (0, 0), pipeline_mode=pl.Buffered(1)),
                      pl.BlockSpec((PLE_DIM, D_MODEL), lambda i, d: (0, 0), pipeline_mode=pl.Buffered(1)),
                      pl.BlockSpec((1, D_MODEL), lambda i, d: (0, 0))],
            out_specs=pl.BlockSpec((TM_S, D_MODEL), lambda i, d: (i, 0)),
            scratch_shapes=[pltpu.VMEM((TOP_K, TM_S, D_MODEL), F32),
                            pltpu.SemaphoreType.DMA((1,))]),
        out_shape=jax.ShapeDtypeStruct((N_TOK, D_MODEL), F32),
        compiler_params=_cparams("arbitrary", vmem_mb=48),
        name="combine_ple",
    )(dest, yb, h, rw, p_all, wpg, wpp, nf)


def _dispatch_plan(eidx):
    n_assign = N_TOK * TOP_K
    e_flat = eidx.reshape(n_assign)
    onehot = (e_flat[:, None] == jnp.arange(N_EXPERTS, dtype=jnp.int32)[None, :]).astype(jnp.int32)
    csum = jnp.cumsum(onehot, axis=0)
    counts = csum[-1]
    rank = jnp.sum(csum * onehot, axis=1) - 1
    padded = (counts + MOE_BLOCK - 1) // MOE_BLOCK * MOE_BLOCK
    pad_end = jnp.cumsum(padded)
    pad_start = pad_end - padded
    dest = pad_start[e_flat] + rank
    n_blocks = -(-n_assign // MOE_BLOCK) + N_EXPERTS
    tok = jnp.arange(n_assign, dtype=jnp.int32) // TOP_K
    slot_tok = jnp.zeros((n_blocks * MOE_BLOCK,), jnp.int32).at[dest].set(tok)
    blk = jnp.arange(n_blocks, dtype=jnp.int32)
    blk_e = jnp.minimum(jnp.searchsorted(pad_end, blk * MOE_BLOCK, side="right"), N_EXPERTS - 1).astype(jnp.int32)
    nused = (pad_end[-1] // MOE_BLOCK).astype(jnp.int32)
    prev_e = jnp.concatenate([jnp.full((1,), -1, jnp.int32), blk_e[:-1]])
    first = jnp.logical_and(blk < nused, blk_e != prev_e)
    par = ((jnp.cumsum(first.astype(jnp.int32)) - 1) % 2).astype(jnp.int32)
    idx_first = jnp.where(first, blk, n_blocks)
    later = jnp.concatenate([lax.cummin(idx_first[::-1])[::-1][1:], jnp.full((1,), n_blocks, jnp.int32)])
    next_e = jnp.where(later < n_blocks, blk_e[jnp.minimum(later, n_blocks - 1)], -1).astype(jnp.int32)
    plan = (blk_e, slot_tok, nused.reshape(1), first.astype(jnp.int32), par, next_e)
    return dest.astype(jnp.int32), plan


def kernel(x_prompt, x_sample, state_conv, state_shift, state_wkv, p_prompt, p_sample, norm_mix, w_in, conv_w, w_conv_out, shift_mu, w0, w2, a0, a2, g2, k_k, k_a, r_k, lnx_w, lnx_b, w_rwkv_out, w_mix_out, norm_ffn, w_route_group, b_route_group, w_route_expert, b_route_expert, w_exp_gate, w_exp_up, w_exp_down, w_ple_proj, w_ple_gate, norm_final):
    c3 = 3 * CONV_WIDTH
    rw3 = 3 * RWKV_WIDTH
    n_tiles = N_TOK // TM
    xp = x_prompt.reshape(N_PROMPT, D_MODEL)
    xs = x_sample.transpose(1, 0, 2).reshape(N_SAMPLE, D_MODEL)
    win = w_in[0]
    mu = shift_mu[0]
    st = state_shift[0]

    def lora_cols(a, rows):
        z32 = jnp.zeros((rows, LANES - DECAY_LORA), a.dtype)
        return jnp.concatenate([a[:, rw3:rw3 + DECAY_LORA], z32,
                                a[:, rw3 + DECAY_LORA:rw3 + DECAY_LORA + A_LORA], z32,
                                a[:, rw3 + DECAY_LORA + A_LORA:]], axis=1)

    xn = _norm_cast(xp, xs, norm_mix)
    z_conv, _, _ = _inproj(xn, win, jnp.zeros((1, c3), F32), jnp.zeros((DEC_BATCH, c3), F32),
                           tn=1024, col_blk_off=0, n_col_blocks=c3 // 1024, shift_lo=0, shift_hi=0,
                           tile_lo=0, n_tiles=n_tiles, name="inproj_conv")
    rkv_s, _, sraw_rkv = _inproj(xn, win, mu[None, :rw3], st[:, :rw3],
                                 tn=1024, col_blk_off=c3 // 1024, n_col_blocks=rw3 // 1024,
                                 shift_lo=0, shift_hi=rw3 // 1024,
                                 tile_lo=N_PROMPT // TM, n_tiles=N_SAMPLE // TM, name="inproj_rkv_sample")
    w_tail = jnp.concatenate([win[:, c3 + RWKV_PROJ:], lora_cols(win[:, c3:c3 + RWKV_PROJ], D_MODEL)], axis=1)
    mu_tail = jnp.concatenate([jnp.zeros((1, 2 * D_MODEL), F32), lora_cols(mu[None, :], 1)], axis=1)
    st_tail = jnp.concatenate([jnp.zeros((DEC_BATCH, 2 * D_MODEL), F32), lora_cols(st, DEC_BATCH)], axis=1)
    nb_tail = N_TAIL // LORA_W
    z_tail, last_tail, sraw_tail = _inproj(xn, w_tail, mu_tail, st_tail, tn=LORA_W, col_blk_off=0,
                                           n_col_blocks=nb_tail, shift_lo=nb_tail - 1, shift_hi=nb_tail,
                                           tile_lo=0, n_tiles=n_tiles, name="inproj_tail")
    rkv_chain = _inproj_t(xn, win[:, c3:c3 + rw3].T.astype(BF16))

    sc = state_conv[0].transpose(1, 0, 2).reshape(2 * DEC_BATCH, CONV_WIDTH)
    conv_out, ulast, us = _conv_branch(z_conv, conv_w[0], sc, w_conv_out[0].astype(BF16))

    def pad_rows(w):
        return jnp.concatenate([w, jnp.zeros((LANES - w.shape[0], w.shape[1]), w.dtype)], axis=0).astype(BF16)

    w2p, a2p = pad_rows(w2[0]), pad_rows(a2[0])
    g, wl_c, al_c, wl_s, al_s = _lora(z_tail, w2p.T, a2p.T, w2p, a2p, g2[0].astype(BF16))
    base = [_param_chain(p) for p in (k_k[0], k_a[0], r_k[0].reshape(RWKV_WIDTH), lnx_w[0], lnx_b[0])]
    mus = [_param_chain(mu[n * RWKV_WIDTH:(n + 1) * RWKV_WIDTH]) for n in range(3)]
    bias = [_param_chain(w0[0]), _param_chain(a0[0])]
    zero = jnp.zeros((HEAD_SIZE, LANES), F32)
    rkv4 = rkv_chain.reshape(3, SEQ, HEAD_SIZE, LANES)
    seqs_p = [(rkv4, 0), (rkv4, 1), (rkv4, 2),
              (wl_c.reshape(1, SEQ, HEAD_SIZE, LANES), None), (al_c.reshape(1, SEQ, HEAD_SIZE, LANES), None)]
    s0_p = jnp.zeros((1, HEAD_SIZE, HEAD_SIZE, LANES), F32)
    y_p, sf_p = _scan(seqs_p, base + mus + bias, s0_p, SCAN_TT, "wkv_scan_prompt")
    ng = DEC_BATCH // CHAIN_B
    s0_s = state_wkv[0].reshape(ng, CHAIN_B, N_HEADS, HEAD_SIZE, HEAD_SIZE).transpose(0, 4, 3, 1, 2)
    s0_s = s0_s.reshape(ng, HEAD_SIZE, HEAD_SIZE, LANES)
    seqs_s = [(_to_chain_sample(rkv_s[:, n * RWKV_WIDTH:(n + 1) * RWKV_WIDTH]), None) for n in range(3)]
    seqs_s += [(_to_chain_sample(wl_s), None), (_to_chain_sample(al_s), None)]
    y_s, sf_s = _scan(seqs_s, base + [zero] * 3 + bias, s0_s, DEC_SEQ, "wkv_scan_sample")
    rwkv_out = _rwkv_out(y_p.reshape(SEQ * HEAD_SIZE, LANES), _from_chain_sample(y_s), g,
                         w_rwkv_out[0].astype(BF16))

    wr = jnp.concatenate([w_route_group[0], w_route_expert[0],
                          jnp.zeros((D_MODEL, LANES - N_GROUPS - N_EXPERTS), F32)], axis=1)
    br = jnp.concatenate([b_route_group[0], b_route_expert[0],
                          jnp.zeros((LANES - N_GROUPS - N_EXPERTS,), F32)])[None, :]
    h, hn, ridx, rw = _mix_route(conv_out, rwkv_out, z_tail, xp, xs, w_mix_out[0].astype(BF16),
                                 norm_ffn, wr, br)

    dest, plan = _dispatch_plan(ridx[:, :TOP_K])
    yb = _experts(plan, hn, w_exp_gate[0], w_exp_up[0], w_exp_down[0])
    p_all = jnp.concatenate([p_prompt[0].reshape(N_PROMPT, PLE_DIM),
                             p_sample[0].transpose(1, 0, 2).reshape(N_SAMPLE, PLE_DIM)], axis=0)
    y = _combine(dest, yb, h, rw, p_all, w_ple_gate[0].astype(BF16), w_ple_proj[0].astype(BF16),
                 norm_final[None, :])

    y_prompt = y[:N_PROMPT].reshape(BATCH, SEQ, D_MODEL)
    y_sample = y[N_PROMPT:].reshape(DEC_SEQ, DEC_BATCH, D_MODEL).transpose(1, 0, 2)
    tiles_per_seq = SEQ // TM
    seq_last = jnp.arange(BATCH) * tiles_per_seq + tiles_per_seq - 1

    def unpad_lora(a):
        o = 2 * D_MODEL
        return jnp.concatenate([a[:, o:o + DECAY_LORA], a[:, o + LANES:o + LANES + A_LORA],
                                a[:, o + 2 * LANES:]], axis=1)

    conv_p = ulast.reshape(-1, SUBLANES, CONV_WIDTH)[seq_last, SUBLANES - 2:, :][None]
    conv_s = us.reshape(2, DEC_BATCH, CONV_WIDTH).transpose(1, 0, 2)[None]
    lm = rkv4[:, SEQ - 1].reshape(3, HEAD_SIZE, BATCH, N_HEADS).transpose(2, 0, 3, 1).reshape(BATCH, rw3)
    lt = unpad_lora(last_tail.reshape(-1, SUBLANES, N_TAIL)[seq_last, SUBLANES - 1, :])
    shift_p = jnp.concatenate([lm, lt], axis=1)[None]
    shift_s = jnp.concatenate([sraw_rkv, unpad_lora(sraw_tail)], axis=1)[None]
    wkv_p = sf_p.reshape(HEAD_SIZE, HEAD_SIZE, BATCH, N_HEADS).transpose(2, 3, 1, 0)[None]
    wkv_s = sf_s.reshape(ng, HEAD_SIZE, HEAD_SIZE, CHAIN_B, N_HEADS).transpose(0, 3, 4, 2, 1)
    wkv_s = wkv_s.reshape(DEC_BATCH, N_HEADS, HEAD_SIZE, HEAD_SIZE)[None]
    return (y_prompt, y_sample, conv_p, shift_p, wkv_p, conv_s, shift_s, wkv_s)
```

```python
import functools

import jax
import jax.numpy as jnp
from jax import lax
from jax.experimental import pallas as pl
from jax.experimental.pallas import tpu as pltpu

D_MODEL = 2048
BATCH = 4
SEQ = 2048
DEC_BATCH = 128
DEC_SEQ = 4
CONV_WIDTH = 1024
RWKV_WIDTH = 2048
HEAD_SIZE = 64
N_HEADS = RWKV_WIDTH // HEAD_SIZE
DECAY_LORA = 96
A_LORA = 96
GATE_LORA = 256
RWKV_PROJ = 3 * RWKV_WIDTH + DECAY_LORA + A_LORA + GATE_LORA
N_GROUPS = 8
EXPERTS_PER_GROUP = 8
N_EXPERTS = N_GROUPS * EXPERTS_PER_GROUP
TOP_K = 2
D_EXPERT = 512
MOE_BLOCK = 128
PLE_DIM = 256
RMS_EPS = 1e-6
GN_EPS = 64e-5

N_PROMPT = BATCH * SEQ
N_SAMPLE = DEC_BATCH * DEC_SEQ
N_TOK = N_PROMPT + N_SAMPLE
LANES = 128
SUBLANES = 8
TM = 512
TM_S = 256
TM_T = 256
TM_L = 128
CHAIN_B = LANES // N_HEADS
SCAN_TT = 64
LORA_W = 512
N_TAIL = 2 * D_MODEL + LORA_W
EXP_M05 = 0.6065306597126334
F32 = jnp.float32
BF16 = jnp.bfloat16
_NT = (((1,), (1,)), ((), ()))


def _sigmoid(x):
    return 1.0 / (1.0 + jnp.exp(-x))


def _cparams(*sem, vmem_mb=None):
    kw = dict(dimension_semantics=sem)
    if vmem_mb is not None:
        kw["vmem_limit_bytes"] = vmem_mb * 1024 * 1024
    return pltpu.CompilerParams(**kw)


def _resident(shape):
    nd = len(shape)
    return pl.BlockSpec(shape, lambda *_: (0,) * nd, pipeline_mode=pl.Buffered(1))


def _store_chain(zt, out_ref, halves):
    for half in range(halves):
        for k in range(HEAD_SIZE):
            m = jnp.concatenate([zt[bb, half, pl.ds(k, N_HEADS, stride=HEAD_SIZE), :]
                                 for bb in range(CHAIN_B)], axis=0)
            out_ref[pl.ds(half * LANES * HEAD_SIZE + k, LANES, stride=HEAD_SIZE), :] = m.T


def _load_chain(y_ref, yt, halves):
    for half in range(halves):
        for v in range(HEAD_SIZE):
            mt = y_ref[pl.ds(half * LANES * HEAD_SIZE + v, LANES, stride=HEAD_SIZE), :].T
            for bb in range(CHAIN_B):
                yt[bb, half, pl.ds(v, N_HEADS, stride=HEAD_SIZE), :] = mt[bb * N_HEADS:(bb + 1) * N_HEADS, :]


def _norm_body(xp_ref, xs_ref, g_ref, o_ref, *, n_prompt_tiles):
    i = pl.program_id(0)

    def f(x):
        ms = jnp.mean(x * x, axis=-1, keepdims=True)
        return (x * lax.rsqrt(ms + RMS_EPS) * g_ref[...]).astype(o_ref.dtype)

    @pl.when(i < n_prompt_tiles)
    def _():
        o_ref[...] = f(xp_ref[...])

    @pl.when(i >= n_prompt_tiles)
    def _():
        o_ref[...] = f(xs_ref[...])


def _norm_cast(xp, xs, g):
    npt = N_PROMPT // TM
    return pl.pallas_call(
        functools.partial(_norm_body, n_prompt_tiles=npt),
        grid=(N_TOK // TM,),
        in_specs=[pl.BlockSpec((TM, D_MODEL), lambda i: (jnp.minimum(i, npt - 1), 0)),
                  pl.BlockSpec((TM, D_MODEL), lambda i: (jnp.maximum(i - npt, 0), 0)),
                  pl.BlockSpec((1, D_MODEL), lambda i: (0, 0))],
        out_specs=pl.BlockSpec((TM, D_MODEL), lambda i: (i, 0)),
        out_shape=jax.ShapeDtypeStruct((N_TOK, D_MODEL), BF16),
        compiler_params=_cparams("arbitrary"),
        name="norm_cast",
    )(xp, xs, g)


def _inproj_body(xn_ref, w_ref, mu_ref, st_ref, z_ref, last_ref, sraw_ref, wb_ref, carry_ref, *,
                 shift_lo, shift_hi, tile_lo, n_prompt_tiles, tiles_per_seq):
    j = pl.program_id(0)
    i = pl.program_id(1) + tile_lo

    @pl.when(pl.program_id(1) == 0)
    def _():
        wb_ref[...] = w_ref[...].astype(BF16)

    z = jnp.dot(xn_ref[...], wb_ref[...], preferred_element_type=F32)
    tm = z.shape[0]
    last_ref[...] = z[tm - SUBLANES:tm]
    shifted = jnp.logical_and(j >= shift_lo, j < shift_hi)
    is_prompt = i < n_prompt_tiles

    @pl.when(jnp.logical_not(shifted))
    def _():
        z_ref[...] = z

    @pl.when(jnp.logical_and(shifted, is_prompt))
    def _():
        @pl.when(i % tiles_per_seq == 0)
        def _():
            carry_ref[...] = jnp.zeros_like(carry_ref)

        prev = pltpu.roll(z, 1, 0)
        row = lax.broadcasted_iota(jnp.int32, (tm, 1), 0)
        prev = jnp.where(row == 0, carry_ref[SUBLANES - 1:SUBLANES, :], prev)
        z_ref[...] = z + mu_ref[...] * (prev - z)
        carry_ref[...] = z[tm - SUBLANES:tm]

    @pl.when(jnp.logical_and(shifted, jnp.logical_not(is_prompt)))
    def _():
        prev = jnp.concatenate([st_ref[...], z[:tm - DEC_BATCH]], axis=0)
        z_ref[...] = z + mu_ref[...] * (prev - z)

    @pl.when(jnp.logical_not(is_prompt))
    def _():
        sraw_ref[...] = z[tm - DEC_BATCH:tm]


def _inproj(xn, w, mu, st, *, tn, col_blk_off, n_col_blocks, shift_lo, shift_hi, tile_lo, n_tiles, name):
    n_out = tn * n_col_blocks
    body = functools.partial(_inproj_body, shift_lo=shift_lo, shift_hi=shift_hi, tile_lo=tile_lo,
                             n_prompt_tiles=N_PROMPT // TM, tiles_per_seq=SEQ // TM)
    return pl.pallas_call(
        body,
        grid=(n_col_blocks, n_tiles),
        in_specs=[pl.BlockSpec((TM, D_MODEL), lambda j, i: (i + tile_lo, 0)),
                  pl.BlockSpec((D_MODEL, tn), lambda j, i: (0, j + col_blk_off)),
                  pl.BlockSpec((1, tn), lambda j, i: (0, j)),
                  pl.BlockSpec((DEC_BATCH, tn), lambda j, i: (0, j))],
        out_specs=[pl.BlockSpec((TM, tn), lambda j, i: (i, j)),
                   pl.BlockSpec((SUBLANES, tn), lambda j, i: (i, j)),
                   pl.BlockSpec((DEC_BATCH, tn), lambda j, i: (0, j))],
        out_shape=[jax.ShapeDtypeStruct((n_tiles * TM, n_out), F32),
                   jax.ShapeDtypeStruct((n_tiles * SUBLANES, n_out), F32),
                   jax.ShapeDtypeStruct((DEC_BATCH, n_out), F32)],
        scratch_shapes=[pltpu.VMEM((D_MODEL, tn), BF16), pltpu.VMEM((SUBLANES, tn), F32)],
        compiler_params=_cparams("arbitrary", "arbitrary", vmem_mb=48),
        name=name,
    )(xn, w, mu, st)


def _inproj_t_body(xn_ref, wt_ref, o_ref, zt):
    b = pl.program_id(2)
    z = lax.dot_general(wt_ref[...], xn_ref[...], _NT, preferred_element_type=F32)
    for half in range(TM_T // LANES):
        zt[b, half] = z[:, half * LANES:(half + 1) * LANES]

    @pl.when(b == CHAIN_B - 1)
    def _():
        _store_chain(zt, o_ref, TM_T // LANES)


def _inproj_t(xn, wt):
    n_tb = SEQ // TM_T
    return pl.pallas_call(
        _inproj_t_body,
        grid=(3, n_tb, CHAIN_B),
        in_specs=[pl.BlockSpec((TM_T, D_MODEL), lambda j, tb, b: (b * n_tb + tb, 0)),
                  pl.BlockSpec((RWKV_WIDTH, D_MODEL), lambda j, tb, b: (j, 0))],
        out_specs=pl.BlockSpec((None, TM_T * HEAD_SIZE, LANES), lambda j, tb, b: (j, tb, 0)),
        out_shape=jax.ShapeDtypeStruct((3, SEQ * HEAD_SIZE, LANES), F32),
        scratch_shapes=[pltpu.VMEM((CHAIN_B, TM_T // LANES, RWKV_WIDTH, LANES), F32)],
        compiler_params=_cparams("arbitrary", "arbitrary", "arbitrary", vmem_mb=56),
        name="inproj_t",
    )(xn, wt)


def _conv_body(zc_ref, cw_ref, sc_ref, wco_ref, o_ref, ulast_ref, us_ref, carry_ref, *,
               n_prompt_tiles, tiles_per_seq):
    i = pl.program_id(0)
    gate_b = zc_ref[:, 0:CONV_WIDTH]
    u = zc_ref[:, CONV_WIDTH:2 * CONV_WIDTH] * zc_ref[:, 2 * CONV_WIDTH:3 * CONV_WIDTH]
    tm = u.shape[0]
    ulast_ref[...] = u[tm - SUBLANES:tm]
    w0 = cw_ref[0:1, :]
    w1 = cw_ref[1:2, :]
    w2 = cw_ref[2:3, :]

    def finish(p1, p2):
        conv = w0 * p2 + w1 * p1 + w2 * u
        y = (gate_b * conv).astype(BF16)
        o_ref[...] = jnp.dot(y, wco_ref[...], preferred_element_type=F32)

    @pl.when(i < n_prompt_tiles)
    def _():
        @pl.when(i % tiles_per_seq == 0)
        def _():
            carry_ref[...] = jnp.zeros_like(carry_ref)

        row = lax.broadcasted_iota(jnp.int32, (tm, 1), 0)
        c1 = carry_ref[SUBLANES - 1:SUBLANES, :]
        c2 = carry_ref[SUBLANES - 2:SUBLANES - 1, :]
        p1 = jnp.where(row == 0, c1, pltpu.roll(u, 1, 0))
        p2 = jnp.where(row == 0, c2, jnp.where(row == 1, c1, pltpu.roll(u, 2, 0)))
        carry_ref[...] = u[tm - SUBLANES:tm]
        finish(p1, p2)

    @pl.when(i >= n_prompt_tiles)
    def _():
        p1 = jnp.concatenate([sc_ref[DEC_BATCH:2 * DEC_BATCH, :], u[:tm - DEC_BATCH]], axis=0)
        p2 = jnp.concatenate([sc_ref[...], u[:tm - 2 * DEC_BATCH]], axis=0)
        us_ref[...] = u[tm - 2 * DEC_BATCH:tm]
        finish(p1, p2)


def _conv_branch(z_conv, conv_w, sc, wco):
    n_tiles = N_TOK // TM
    body = functools.partial(_conv_body, n_prompt_tiles=N_PROMPT // TM, tiles_per_seq=SEQ // TM)
    return pl.pallas_call(
        body,
        grid=(n_tiles,),
        in_specs=[pl.BlockSpec((TM, 3 * CONV_WIDTH), lambda i: (i, 0)),
                  pl.BlockSpec((3, CONV_WIDTH), lambda i: (0, 0)),
                  pl.BlockSpec((2 * DEC_BATCH, CONV_WIDTH), lambda i: (0, 0)),
                  _resident((CONV_WIDTH, D_MODEL))],
        out_specs=[pl.BlockSpec((TM, D_MODEL), lambda i: (i, 0)),
                   pl.BlockSpec((SUBLANES, CONV_WIDTH), lambda i: (i, 0)),
                   pl.BlockSpec((2 * DEC_BATCH, CONV_WIDTH), lambda i: (0, 0))],
        out_shape=[jax.ShapeDtypeStruct((N_TOK, D_MODEL), F32),
                   jax.ShapeDtypeStruct((n_tiles * SUBLANES, CONV_WIDTH), F32),
                   jax.ShapeDtypeStruct((2 * DEC_BATCH, CONV_WIDTH), F32)],
        scratch_shapes=[pltpu.VMEM((SUBLANES, CONV_WIDTH), F32)],
        compiler_params=_cparams("arbitrary", vmem_mb=48),
        name="conv_branch",
    )(z_conv, conv_w, sc, wco)


def _lora_body(zl_ref, w2t_ref, a2t_ref, w2_ref, a2_ref, g2_ref,
               g_ref, wlc_ref, alc_ref, wls_ref, als_ref, zt, *, n_prompt_steps):
    s = pl.program_id(0)
    tw = jnp.tanh(zl_ref[:, 0:LANES]).astype(BF16)
    xa = zl_ref[:, LANES:2 * LANES].astype(BF16)
    xg = zl_ref[:, 2 * LANES:LORA_W]
    g_ref[...] = jnp.dot(_sigmoid(xg).astype(BF16), g2_ref[...], preferred_element_type=F32)

    @pl.when(s < n_prompt_steps)
    def _():
        b = s % CHAIN_B
        zt[0, b, 0] = lax.dot_general(w2t_ref[...], tw, _NT, preferred_element_type=F32)
        zt[1, b, 0] = lax.dot_general(a2t_ref[...], xa, _NT, preferred_element_type=F32)

        @pl.when(b == CHAIN_B - 1)
        def _():
            _store_chain(zt.at[0], wlc_ref, 1)
            _store_chain(zt.at[1], alc_ref, 1)

    @pl.when(s >= n_prompt_steps)
    def _():
        wls_ref[...] = jnp.dot(tw, w2_ref[...], preferred_element_type=F32)
        als_ref[...] = jnp.dot(xa, a2_ref[...], preferred_element_type=F32)


def _lora(z_tail, w2t, a2t, w2p, a2p, g2b):
    n_tb = SEQ // TM_L
    nps = n_tb * CHAIN_B
    lora_blk = 2 * D_MODEL // LORA_W

    def row_blk(s):
        return jnp.where(s < nps, (s % CHAIN_B) * n_tb + s // CHAIN_B, s)

    chain_spec = pl.BlockSpec((TM_L * HEAD_SIZE, LANES), lambda s: (jnp.minimum(s // CHAIN_B, n_tb - 1), 0))
    samp_spec = pl.BlockSpec((TM_L, RWKV_WIDTH), lambda s: (jnp.maximum(s - nps, 0), 0))
    chain_shape = jax.ShapeDtypeStruct((SEQ * HEAD_SIZE, LANES), F32)
    samp_shape = jax.ShapeDtypeStruct((N_SAMPLE, RWKV_WIDTH), F32)
    return pl.pallas_call(
        functools.partial(_lora_body, n_prompt_steps=nps),
        grid=(N_TOK // TM_L,),
        in_specs=[pl.BlockSpec((TM_L, LORA_W), lambda s: (row_blk(s), lora_blk)),
                  _resident((RWKV_WIDTH, LANES)), _resident((RWKV_WIDTH, LANES)),
                  _resident((LANES, RWKV_WIDTH)), _resident((LANES, RWKV_WIDTH)),
                  _resident((GATE_LORA, RWKV_WIDTH))],
        out_specs=[pl.BlockSpec((TM_L, RWKV_WIDTH), lambda s: (row_blk(s), 0)),
                   chain_spec, chain_spec, samp_spec, samp_spec],
        out_shape=[jax.ShapeDtypeStruct((N_TOK, RWKV_WIDTH), F32),
                   chain_shape, chain_shape, samp_shape, samp_shape],
        scratch_shapes=[pltpu.VMEM((2, CHAIN_B, 1, RWKV_WIDTH, LANES), F32)],
        compiler_params=_cparams("arbitrary", vmem_mb=48),
        name="lora",
    )(z_tail, w2t, a2t, w2p, a2p, g2b)


def _scan_body(r_ref, k_ref, v_ref, wl_ref, al_ref, kk_ref, ka_ref, rk_ref, lw_ref, lb_ref,
               mur_ref, muk_ref, muv_ref, w0_ref, a0_ref, s0_ref,
               y_ref, s_ref, vec_ref, prev_ref, *, tt):
    @pl.when(pl.program_id(1) == 0)
    def _():
        s_ref[...] = s0_ref[...]
        prev_ref[...] = jnp.zeros_like(prev_ref)

    def step(t, carry):
        r_raw = r_ref[t]
        k_raw = k_ref[t]
        v_raw = v_ref[t]
        r = r_raw + mur_ref[...] * (prev_ref[0] - r_raw)
        k = k_raw + muk_ref[...] * (prev_ref[1] - k_raw)
        v = v_raw + muv_ref[...] * (prev_ref[2] - v_raw)
        prev_ref[0] = r_raw
        prev_ref[1] = k_raw
        prev_ref[2] = v_raw
        decay = jnp.exp(-EXP_M05 * _sigmoid(wl_ref[t] + w0_ref[...]))
        a = _sigmoid(al_ref[t] + a0_ref[...])
        kk = k * kk_ref[...]
        nrm = jnp.sqrt(jnp.sum(kk * kk, axis=0, keepdims=True))
        kk = kk / jnp.maximum(nrm, 1e-12)
        kf = k * (1.0 + (a - 1.0) * ka_ref[...])
        vec_ref[0] = -kk
        vec_ref[1] = decay
        vec_ref[2] = kk * a
        vec_ref[3] = kf
        vec_ref[4] = r

        def row(j, kx):
            return vec_ref[j, pl.ds(kx, HEAD_SIZE, stride=0), :]

        parts = [jnp.zeros((HEAD_SIZE, LANES), F32) for _ in range(4)]
        for kx in range(HEAD_SIZE):
            parts[kx % 4] = parts[kx % 4] + s_ref[kx] * row(0, kx)
        sa = (parts[0] + parts[1]) + (parts[2] + parts[3])

        parts = [jnp.zeros((HEAD_SIZE, LANES), F32) for _ in range(4)]
        for kx in range(HEAD_SIZE):
            sn = s_ref[kx] * row(1, kx) + sa * row(2, kx) + v * row(3, kx)
            s_ref[kx] = sn
            parts[kx % 4] = parts[kx % 4] + sn * row(4, kx)
        o = (parts[0] + parts[1]) + (parts[2] + parts[3])

        mu = jnp.mean(o, axis=0, keepdims=True)
        dlt = o - mu
        var = jnp.mean(dlt * dlt, axis=0, keepdims=True)
        on = dlt * lax.rsqrt(var + GN_EPS) * lw_ref[...] + lb_ref[...]
        bonus = jnp.sum(r * kf * rk_ref[...], axis=0, keepdims=True) * v
        y_ref[t] = on + bonus
        return carry

    lax.fori_loop(0, tt, step, 0)


def _scan(seqs, params, s0, tt, name):
    g, t = s0.shape[0], seqs[0][0].shape[1]

    def seq_spec(lead):
        if lead is None:
            return pl.BlockSpec((None, tt, HEAD_SIZE, LANES), lambda gi, ti: (gi, ti, 0, 0))
        return pl.BlockSpec((None, tt, HEAD_SIZE, LANES), lambda gi, ti: (lead, ti, 0, 0))

    par_spec = pl.BlockSpec((HEAD_SIZE, LANES), lambda gi, ti: (0, 0))
    st_spec = pl.BlockSpec((None, HEAD_SIZE, HEAD_SIZE, LANES), lambda gi, ti: (gi, 0, 0, 0))
    return pl.pallas_call(
        functools.partial(_scan_body, tt=tt),
        grid=(g, t // tt),
        in_specs=[seq_spec(lead) for _, lead in seqs] + [par_spec] * len(params) + [st_spec],
        out_specs=[seq_spec(None), st_spec],
        out_shape=[jax.ShapeDtypeStruct((g, t, HEAD_SIZE, LANES), F32),
                   jax.ShapeDtypeStruct((g, HEAD_SIZE, HEAD_SIZE, LANES), F32)],
        scratch_shapes=[pltpu.VMEM((5, HEAD_SIZE, LANES), F32), pltpu.VMEM((3, HEAD_SIZE, LANES), F32)],
        compiler_params=_cparams("arbitrary", "arbitrary", vmem_mb=48),
        name=name,
    )(*[a for a, _ in seqs], *params, s0)


def _to_chain_sample(x):
    ng = DEC_BATCH // CHAIN_B
    x = x.reshape(DEC_SEQ, ng, CHAIN_B, N_HEADS, HEAD_SIZE).transpose(1, 0, 4, 2, 3)
    return x.reshape(ng, DEC_SEQ, HEAD_SIZE, LANES)


def _from_chain_sample(y):
    ng = DEC_BATCH // CHAIN_B
    y = y.reshape(ng, DEC_SEQ, HEAD_SIZE, CHAIN_B, N_HEADS).transpose(1, 0, 3, 4, 2)
    return y.reshape(N_SAMPLE, RWKV_WIDTH)


def _param_chain(p):
    return jnp.tile(p.reshape(N_HEADS, HEAD_SIZE).T, (1, CHAIN_B))


def _rwkv_out_body(yc_ref, ys_ref, g_ref, w_ref, o_ref, yt, *, n_prompt_steps):
    s = pl.program_id(0)

    def finish(y):
        o_ref[...] = jnp.dot((y * g_ref[...]).astype(BF16), w_ref[...], preferred_element_type=F32)

    @pl.when(s < n_prompt_steps)
    def _():
        b = s % CHAIN_B

        @pl.when(b == 0)
        def _():
            _load_chain(yc_ref, yt, TM_T // LANES)

        finish(jnp.concatenate([yt[b, half].T for half in range(TM_T // LANES)], axis=0))

    @pl.when(s >= n_prompt_steps)
    def _():
        finish(ys_ref[...])


def _rwkv_out(y_chain, y_s, g, w):
    n_tb = SEQ // TM_T
    nps = n_tb * CHAIN_B

    def row_blk(s):
        return jnp.where(s < nps, (s % CHAIN_B) * n_tb + s // CHAIN_B, s)

    return pl.pallas_call(
        functools.partial(_rwkv_out_body, n_prompt_steps=nps),
        grid=(N_TOK // TM_T,),
        in_specs=[pl.BlockSpec((TM_T * HEAD_SIZE, LANES), lambda s: (jnp.minimum(s // CHAIN_B, n_tb - 1), 0)),
                  pl.BlockSpec((TM_T, RWKV_WIDTH), lambda s: (jnp.maximum(s - nps, 0), 0)),
                  pl.BlockSpec((TM_T, RWKV_WIDTH), lambda s: (row_blk(s), 0)),
                  _resident((RWKV_WIDTH, D_MODEL))],
        out_specs=pl.BlockSpec((TM_T, D_MODEL), lambda s: (row_blk(s), 0)),
        out_shape=jax.ShapeDtypeStruct((N_TOK, D_MODEL), F32),
        scratch_shapes=[pltpu.VMEM((CHAIN_B, TM_T // LANES, RWKV_WIDTH, LANES), F32)],
        compiler_params=_cparams("arbitrary", vmem_mb=56),
        name="rwkv_out",
    )(y_chain, y_s, g, w)


def _mix_body(co_ref, ro_ref, gc_ref, gr_ref, xp_ref, xs_ref, wm_ref, nf_ref, wr_ref, br_ref,
              h_ref, hn_ref, ridx_ref, rw_ref, *, n_prompt_tiles):
    i = pl.program_id(0)
    mixed = _sigmoid(gc_ref[...]) * co_ref[...] + _sigmoid(gr_ref[...]) * ro_ref[...]
    mo = jnp.dot(mixed.astype(BF16), wm_ref[...], preferred_element_type=F32)

    def finish(x):
        h = x + mo
        h_ref[...] = h
        ms = jnp.mean(h * h, axis=-1, keepdims=True)
        hn = h * lax.rsqrt(ms + RMS_EPS) * nf_ref[...]
        hn_ref[...] = hn
        logits = jnp.dot(hn, wr_ref[...], preferred_element_type=F32,
                         precision=lax.Precision.HIGHEST) + br_ref[...]
        tm = logits.shape[0]
        lane = lax.broadcasted_iota(jnp.int32, (tm, LANES), 1)
        neg = jnp.float32(-jnp.inf)
        gl = jnp.where(lane < N_GROUPS, logits, neg)
        gmax = jnp.max(gl, axis=-1, keepdims=True)
        g_idx = jnp.min(jnp.where(gl == gmax, lane, LANES), axis=-1, keepdims=True)
        g_w = 1.0 / jnp.sum(jnp.exp(gl - gmax), axis=-1, keepdims=True)
        lo = N_GROUPS + g_idx * EXPERTS_PER_GROUP
        el = jnp.where(jnp.logical_and(lane >= lo, lane < lo + EXPERTS_PER_GROUP), logits, neg)
        m1 = jnp.max(el, axis=-1, keepdims=True)
        i1 = jnp.min(jnp.where(el == m1, lane, LANES), axis=-1, keepdims=True)
        el2 = jnp.where(lane == i1, neg, el)
        m2 = jnp.max(el2, axis=-1, keepdims=True)
        i2 = jnp.min(jnp.where(el2 == m2, lane, LANES), axis=-1, keepdims=True)
        t2 = jnp.exp(m2 - m1)
        den = 1.0 + t2
        ridx_ref[...] = jnp.where(lane == 0, i1 - N_GROUPS, jnp.where(lane == 1, i2 - N_GROUPS, 0))
        rw_ref[...] = jnp.where(lane == 0, (1.0 / den) * g_w, jnp.where(lane == 1, (t2 / den) * g_w, 0.0))

    @pl.when(i < n_prompt_tiles)
    def _():
        finish(xp_ref[...])

    @pl.when(i >= n_prompt_tiles)
    def _():
        finish(xs_ref[...])


def _mix_route(conv_out, rwkv_out, z_tail, xp, xs, wm, nf, wr, br):
    npt = N_PROMPT // TM_S
    tok_spec = pl.BlockSpec((TM_S, D_MODEL), lambda i: (i, 0))
    small_spec = pl.BlockSpec((TM_S, LANES), lambda i: (i, 0))
    return pl.pallas_call(
        functools.partial(_mix_body, n_prompt_tiles=npt),
        grid=(N_TOK // TM_S,),
        in_specs=[tok_spec, tok_spec,
                  pl.BlockSpec((TM_S, D_MODEL), lambda i: (i, 0)),
                  pl.BlockSpec((TM_S, D_MODEL), lambda i: (i, 1)),
                  pl.BlockSpec((TM_S, D_MODEL), lambda i: (jnp.minimum(i, npt - 1), 0)),
                  pl.BlockSpec((TM_S, D_MODEL), lambda i: (jnp.maximum(i - npt, 0), 0)),
                  _resident((D_MODEL, D_MODEL)),
                  pl.BlockSpec((1, D_MODEL), lambda i: (0, 0)),
                  _resident((D_MODEL, LANES)),
                  pl.BlockSpec((1, LANES), lambda i: (0, 0))],
        out_specs=[tok_spec, tok_spec, small_spec, small_spec],
        out_shape=[jax.ShapeDtypeStruct((N_TOK, D_MODEL), F32),
                   jax.ShapeDtypeStruct((N_TOK, D_MODEL), F32),
                   jax.ShapeDtypeStruct((N_TOK, LANES), jnp.int32),
                   jax.ShapeDtypeStruct((N_TOK, LANES), F32)],
        compiler_params=_cparams("arbitrary", vmem_mb=56),
        name="mix_route",
    )(conv_out, rwkv_out, z_tail, z_tail, xp, xs, wm, nf, wr, br)


def _expert_body(blk_e_ref, slot_tok_ref, nused_ref, first_ref, par_ref, next_e_ref,
                 hn_ref, wg_hbm, wu_hbm, wd_hbm, yb_ref,
                 xbuf, sem, wfg, wfu, wfd, wsem, wgb, wub, wdb):
    i = pl.program_id(0)
    nused = nused_ref[0]
    slot = i % 2

    def row_copy(blk, r, s):
        tok = slot_tok_ref[blk * MOE_BLOCK + r]
        return pltpu.make_async_copy(hn_ref.at[pl.ds(tok, 1), :], xbuf.at[s, pl.ds(r, 1), :], sem.at[s])

    def issue(blk, s):
        for r in range(MOE_BLOCK):
            row_copy(blk, r, s).start()

    def w_copies(e, s):
        return (pltpu.make_async_copy(wg_hbm.at[e], wfg.at[s], wsem.at[s]),
                pltpu.make_async_copy(wu_hbm.at[e], wfu.at[s], wsem.at[s]),
                pltpu.make_async_copy(wd_hbm.at[e], wfd.at[s], wsem.at[s]))

    @pl.when(jnp.logical_and(i == 0, nused > 0))
    def _():
        for c in w_copies(blk_e_ref[0], 0):
            c.start(priority=1)
        issue(0, 0)

    @pl.when(i + 1 < nused)
    def _():
        issue(i + 1, 1 - slot)

    @pl.when(i < nused)
    def _():
        @pl.when(first_ref[i] == 1)
        def _():
            ws = par_ref[i]
            for c in w_copies(blk_e_ref[i], ws):
                c.wait()

            @pl.when(next_e_ref[i] >= 0)
            def _():
                for c in w_copies(next_e_ref[i], 1 - ws):
                    c.start(priority=1)

            wgb[...] = wfg[ws].astype(BF16)
            wub[...] = wfu[ws].astype(BF16)
            wdb[...] = wfd[ws].astype(BF16)

        for r in range(MOE_BLOCK):
            row_copy(i, r, slot).wait()
        xe = xbuf[slot].astype(BF16)
        gate = jnp.dot(xe, wgb[...], preferred_element_type=F32)
        up = jnp.dot(xe, wub[...], preferred_element_type=F32)
        hdn = (gate * _sigmoid(gate)) * up
        yb_ref[...] = jnp.dot(hdn.astype(BF16), wdb[...], preferred_element_type=F32)

    @pl.when(i >= nused)
    def _():
        yb_ref[...] = jnp.zeros_like(yb_ref)


def _experts(plan, hn, wg, wu, wd):
    blk_e, slot_tok, nused, first, par, next_e = plan
    n_blocks = blk_e.shape[0]
    any_spec = pl.BlockSpec(memory_space=pl.ANY)
    return pl.pallas_call(
        _expert_body,
        grid_spec=pltpu.PrefetchScalarGridSpec(
            num_scalar_prefetch=6,
            grid=(n_blocks,),
            in_specs=[any_spec, any_spec, any_spec, any_spec],
            out_specs=pl.BlockSpec((MOE_BLOCK, D_MODEL), lambda i, *_: (i, 0)),
            scratch_shapes=[pltpu.VMEM((2, MOE_BLOCK, D_MODEL), F32),
                            pltpu.SemaphoreType.DMA((2,)),
                            pltpu.VMEM((2, D_MODEL, D_EXPERT), F32),
                            pltpu.VMEM((2, D_MODEL, D_EXPERT), F32),
                            pltpu.VMEM((2, D_EXPERT, D_MODEL), F32),
                            pltpu.SemaphoreType.DMA((2,)),
                            pltpu.VMEM((D_MODEL, D_EXPERT), BF16),
                            pltpu.VMEM((D_MODEL, D_EXPERT), BF16),
                            pltpu.VMEM((D_EXPERT, D_MODEL), BF16)]),
        out_shape=jax.ShapeDtypeStruct((n_blocks * MOE_BLOCK, D_MODEL), F32),
        compiler_params=_cparams("arbitrary", vmem_mb=48),
        name="experts",
    )(blk_e, slot_tok, nused, first, par, next_e, hn, wg, wu, wd)


def _combine_body(dest_ref, yb_ref, h_ref, rw_ref, p_ref, wpg_ref, wpp_ref, nf_ref, yp_ref, ys_ref,
                  ybuf, sem):
    i = pl.program_id(0)
    tm = h_ref.shape[0]
    slot = i % 2

    def row_copy(tile, r, s, sl):
        d = dest_ref[(tile * tm + r) * TOP_K + s]
        return pltpu.make_async_copy(yb_ref.at[pl.ds(d, 1), :], ybuf.at[sl, s, pl.ds(r, 1), :], sem.at[sl])

    def issue(tile, sl):
        for r in range(tm):
            row_copy(tile, r, 0, sl).start()
            row_copy(tile, r, 1, sl).start()

    @pl.when(i == 0)
    def _():
        issue(0, 0)

    @pl.when(i + 1 < pl.num_programs(0))
    def _():
        issue(i + 1, 1 - slot)

    for r in range(tm):
        row_copy(i, r, 0, slot).wait()
        row_copy(i, r, 1, slot).wait()

    rw = rw_ref[...]
    h2 = h_ref[...] + (ybuf[slot, 0] * rw[:, 0:1] + ybuf[slot, 1] * rw[:, 1:2])
    gate = _sigmoid(jnp.dot(h2.astype(BF16), wpg_ref[...], preferred_element_type=F32))
    pp = jnp.dot(p_ref[...].astype(BF16), wpp_ref[...], preferred_element_type=F32)
    h3 = h2 + gate * pp
    ms = jnp.mean(h3 * h3, axis=-1, keepdims=True)
    y = h3 * lax.rsqrt(ms + RMS_EPS) * nf_ref[...]

    @pl.when(i < N_PROMPT // TM_S)
    def _():
        yp_ref[...] = y

    @pl.when(i >= N_PROMPT // TM_S)
    def _():
        ys_ref[...] = y


def _combine(dest, yb, h, rw, p_all, wpg, wpp, nf):
    npt = N_PROMPT // TM_S
    return pl.pallas_call(
        _combine_body,
        grid_spec=pltpu.PrefetchScalarGridSpec(
            num_scalar_prefetch=1,
            grid=(N_TOK // TM_S,),
            in_specs=[pl.BlockSpec(memory_space=pl.ANY),
                      pl.BlockSpec((TM_S, D_MODEL), lambda i, d: (i, 0)),
                      pl.BlockSpec((TM_S, LANES), lambda i, d: (i, 0)),
                      pl.BlockSpec((TM_S, PLE_DIM), lambda i, d: (i, 0)),
                      pl.BlockSpec((D_MODEL, D_MODEL), lambda i, d: (0, 0), pipeline_mode=pl.Buffered(1)),
                      pl.BlockSpec((PLE_DIM, D_MODEL), lambda i, d: (0, 0), pipeline_mode=pl.Buffered(1)),
                      pl.BlockSpec((1, D_MODEL), lambda i, d: (0, 0))],
            out_specs=[pl.BlockSpec((TM_S, D_MODEL), lambda i, d: (jnp.minimum(i, npt - 1), 0)),
                       pl.BlockSpec((TM_S, D_MODEL), lambda i, d: (jnp.maximum(i - npt, 0), 0))],
            scratch_shapes=[pltpu.VMEM((2, TOP_K, TM_S, D_MODEL), F32),
                            pltpu.SemaphoreType.DMA((2,))]),
        out_shape=[jax.ShapeDtypeStruct((N_PROMPT, D_MODEL), F32),
                   jax.ShapeDtypeStruct((N_SAMPLE, D_MODEL), F32)],
        compiler_params=_cparams("arbitrary", vmem_mb=48),
        name="combine_ple",
    )(dest, yb, h, rw, p_all, wpg, wpp, nf)


def _dispatch_plan(eidx):
    n_assign = N_TOK * TOP_K
    e_flat = eidx.reshape(n_assign)
    onehot = (e_flat[:, None] == jnp.arange(N_EXPERTS, dtype=jnp.int32)[None, :]).astype(jnp.int32)
    csum = jnp.cumsum(onehot, axis=0)
    counts = csum[-1]
    rank = jnp.sum(csum * onehot, axis=1) - 1
    padded = (counts + MOE_BLOCK - 1) // MOE_BLOCK * MOE_BLOCK
    pad_end = jnp.cumsum(padded)
    pad_start = pad_end - padded
    dest = pad_start[e_flat] + rank
    n_blocks = -(-n_assign // MOE_BLOCK) + N_EXPERTS
    tok = jnp.arange(n_assign, dtype=jnp.int32) // TOP_K
    slot_tok = jnp.zeros((n_blocks * MOE_BLOCK,), jnp.int32).at[dest].set(tok)
    blk = jnp.arange(n_blocks, dtype=jnp.int32)
    blk_e = jnp.minimum(jnp.searchsorted(pad_end, blk * MOE_BLOCK, side="right"), N_EXPERTS - 1).astype(jnp.int32)
    nused = (pad_end[-1] // MOE_BLOCK).astype(jnp.int32)
    prev_e = jnp.concatenate([jnp.full((1,), -1, jnp.int32), blk_e[:-1]])
    first = jnp.logical_and(blk < nused, blk_e != prev_e)
    par = ((jnp.cumsum(first.astype(jnp.int32)) - 1) % 2).astype(jnp.int32)
    idx_first = jnp.where(first, blk, n_blocks)
    later = jnp.concatenate([lax.cummin(idx_first[::-1])[::-1][1:], jnp.full((1,), n_blocks, jnp.int32)])
    next_e = jnp.where(later < n_blocks, blk_e[jnp.minimum(later, n_blocks - 1)], -1).astype(jnp.int32)
    plan = (blk_e, slot_tok, nused.reshape(1), first.astype(jnp.int32), par, next_e)
    return dest.astype(jnp.int32), plan


def kernel(x_prompt, x_sample, state_conv, state_shift, state_wkv, p_prompt, p_sample, norm_mix, w_in, conv_w, w_conv_out, shift_mu, w0, w2, a0, a2, g2, k_k, k_a, r_k, lnx_w, lnx_b, w_rwkv_out, w_mix_out, norm_ffn, w_route_group, b_route_group, w_route_expert, b_route_expert, w_exp_gate, w_exp_up, w_exp_down, w_ple_proj, w_ple_gate, norm_final):
    c3 = 3 * CONV_WIDTH
    rw3 = 3 * RWKV_WIDTH
    n_tiles = N_TOK // TM
    xp = x_prompt.reshape(N_PROMPT, D_MODEL)
    xs = x_sample.transpose(1, 0, 2).reshape(N_SAMPLE, D_MODEL)
    win = w_in[0]
    mu = shift_mu[0]
    st = state_shift[0]

    def lora_cols(a, rows):
        z32 = jnp.zeros((rows, LANES - DECAY_LORA), a.dtype)
        return jnp.concatenate([a[:, rw3:rw3 + DECAY_LORA], z32,
                                a[:, rw3 + DECAY_LORA:rw3 + DECAY_LORA + A_LORA], z32,
                                a[:, rw3 + DECAY_LORA + A_LORA:]], axis=1)

    xn = _norm_cast(xp, xs, norm_mix)
    z_conv, _, _ = _inproj(xn, win, jnp.zeros((1, c3), F32), jnp.zeros((DEC_BATCH, c3), F32),
                           tn=1024, col_blk_off=0, n_col_blocks=c3 // 1024, shift_lo=0, shift_hi=0,
                           tile_lo=0, n_tiles=n_tiles, name="inproj_conv")
    rkv_s, _, sraw_rkv = _inproj(xn, win, mu[None, :rw3], st[:, :rw3],
                                 tn=1024, col_blk_off=c3 // 1024, n_col_blocks=rw3 // 1024,
                                 shift_lo=0, shift_hi=rw3 // 1024,
                                 tile_lo=N_PROMPT // TM, n_tiles=N_SAMPLE // TM, name="inproj_rkv_sample")
    w_tail = jnp.concatenate([win[:, c3 + RWKV_PROJ:], lora_cols(win[:, c3:c3 + RWKV_PROJ], D_MODEL)], axis=1)
    mu_tail = jnp.concatenate([jnp.zeros((1, 2 * D_MODEL), F32), lora_cols(mu[None, :], 1)], axis=1)
    st_tail = jnp.concatenate([jnp.zeros((DEC_BATCH, 2 * D_MODEL), F32), lora_cols(st, DEC_BATCH)], axis=1)
    nb_tail = 3
    z_tail, last_tail, sraw_tail = _inproj(xn, w_tail, mu_tail, st_tail, tn=N_TAIL // nb_tail, col_blk_off=0,
                                           n_col_blocks=nb_tail, shift_lo=nb_tail - 1, shift_hi=nb_tail,
                                           tile_lo=0, n_tiles=n_tiles, name="inproj_tail")
    rkv_chain = _inproj_t(xn, win[:, c3:c3 + rw3].T.astype(BF16))

    sc = state_conv[0].transpose(1, 0, 2).reshape(2 * DEC_BATCH, CONV_WIDTH)
    conv_out, ulast, us = _conv_branch(z_conv, conv_w[0], sc, w_conv_out[0].astype(BF16))

    def pad_rows(w):
        return jnp.concatenate([w, jnp.zeros((LANES - w.shape[0], w.shape[1]), w.dtype)], axis=0).astype(BF16)

    w2p, a2p = pad_rows(w2[0]), pad_rows(a2[0])
    g, wl_c, al_c, wl_s, al_s = _lora(z_tail, w2p.T, a2p.T, w2p, a2p, g2[0].astype(BF16))
    base = [_param_chain(p) for p in (k_k[0], k_a[0], r_k[0].reshape(RWKV_WIDTH), lnx_w[0], lnx_b[0])]
    mus = [_param_chain(mu[n * RWKV_WIDTH:(n + 1) * RWKV_WIDTH]) for n in range(3)]
    bias = [_param_chain(w0[0]), _param_chain(a0[0])]
    zero = jnp.zeros((HEAD_SIZE, LANES), F32)
    rkv4 = rkv_chain.reshape(3, SEQ, HEAD_SIZE, LANES)
    seqs_p = [(rkv4, 0), (rkv4, 1), (rkv4, 2),
              (wl_c.reshape(1, SEQ, HEAD_SIZE, LANES), None), (al_c.reshape(1, SEQ, HEAD_SIZE, LANES), None)]
    s0_p = jnp.zeros((1, HEAD_SIZE, HEAD_SIZE, LANES), F32)
    y_p, sf_p = _scan(seqs_p, base + mus + bias, s0_p, SCAN_TT, "wkv_scan_prompt")
    ng = DEC_BATCH // CHAIN_B
    s0_s = state_wkv[0].reshape(ng, CHAIN_B, N_HEADS, HEAD_SIZE, HEAD_SIZE).transpose(0, 4, 3, 1, 2)
    s0_s = s0_s.reshape(ng, HEAD_SIZE, HEAD_SIZE, LANES)
    seqs_s = [(_to_chain_sample(rkv_s[:, n * RWKV_WIDTH:(n + 1) * RWKV_WIDTH]), None) for n in range(3)]
    seqs_s += [(_to_chain_sample(wl_s), None), (_to_chain_sample(al_s), None)]
    y_s, sf_s = _scan(seqs_s, base + [zero] * 3 + bias, s0_s, DEC_SEQ, "wkv_scan_sample")
    rwkv_out = _rwkv_out(y_p.reshape(SEQ * HEAD_SIZE, LANES), _from_chain_sample(y_s), g,
                         w_rwkv_out[0].astype(BF16))

    wr = jnp.concatenate([w_route_group[0], w_route_expert[0],
                          jnp.zeros((D_MODEL, LANES - N_GROUPS - N_EXPERTS), F32)], axis=1)
    br = jnp.concatenate([b_route_group[0], b_route_expert[0],
                          jnp.zeros((LANES - N_GROUPS - N_EXPERTS,), F32)])[None, :]
    h, hn, ridx, rw = _mix_route(conv_out, rwkv_out, z_tail, xp, xs, w_mix_out[0].astype(BF16),
                                 norm_ffn, wr, br)

    dest, plan = _dispatch_plan(ridx[:, :TOP_K])
    yb = _experts(plan, hn, w_exp_gate[0], w_exp_up[0], w_exp_down[0])
    p_all = jnp.concatenate([p_prompt[0].reshape(N_PROMPT, PLE_DIM),
                             p_sample[0].transpose(1, 0, 2).reshape(N_SAMPLE, PLE_DIM)], axis=0)
    y_p2, y_s2 = _combine(dest, yb, h, rw, p_all, w_ple_gate[0].astype(BF16), w_ple_proj[0].astype(BF16),
                          norm_final[None, :])

    y_prompt = y_p2.reshape(BATCH, SEQ, D_MODEL)
    y_sample = y_s2.reshape(DEC_SEQ, DEC_BATCH, D_MODEL).transpose(1, 0, 2)
    tiles_per_seq = SEQ // TM
    seq_last = jnp.arange(BATCH) * tiles_per_seq + tiles_per_seq - 1

    def unpad_lora(a):
        o = 2 * D_MODEL
        return jnp.concatenate([a[:, o:o + DECAY_LORA], a[:, o + LANES:o + LANES + A_LORA],
                                a[:, o + 2 * LANES:]], axis=1)

    conv_p = ulast.reshape(-1, SUBLANES, CONV_WIDTH)[seq_last, SUBLANES - 2:, :][None]
    conv_s = us.reshape(2, DEC_BATCH, CONV_WIDTH).transpose(1, 0, 2)[None]
    lm = rkv4[:, SEQ - 1].reshape(3, HEAD_SIZE, BATCH, N_HEADS).transpose(2, 0, 3, 1).reshape(BATCH, rw3)
    lt = unpad_lora(last_tail.reshape(-1, SUBLANES, N_TAIL)[seq_last, SUBLANES - 1, :])
    shift_p = jnp.concatenate([lm, lt], axis=1)[None]
    shift_s = jnp.concatenate([sraw_rkv, unpad_lora(sraw_tail)], axis=1)[None]
    wkv_p = sf_p.reshape(HEAD_SIZE, HEAD_SIZE, BATCH, N_HEADS).transpose(2, 3, 1, 0)[None]
    wkv_s = sf_s.reshape(ng, HEAD_SIZE, HEAD_SIZE, CHAIN_B, N_HEADS).transpose(0, 3, 4, 2, 1)
    wkv_s = wkv_s.reshape(DEC_BATCH, N_HEADS, HEAD_SIZE, HEAD_SIZE)[None]
    return (y_prompt, y_sample, conv_p, shift_p, wkv_p, conv_s, shift_s, wkv_s)
```

```python
import functools

import jax
import jax.numpy as jnp
from jax import lax
from jax.experimental import pallas as pl
from jax.experimental.pallas import tpu as pltpu

D_MODEL = 2048
BATCH = 4
SEQ = 2048
DEC_BATCH = 128
DEC_SEQ = 4
CONV_WIDTH = 1024
RWKV_WIDTH = 2048
HEAD_SIZE = 64
N_HEADS = RWKV_WIDTH // HEAD_SIZE
DECAY_LORA = 96
A_LORA = 96
GATE_LORA = 256
RWKV_PROJ = 3 * RWKV_WIDTH + DECAY_LORA + A_LORA + GATE_LORA
N_GROUPS = 8
EXPERTS_PER_GROUP = 8
N_EXPERTS = N_GROUPS * EXPERTS_PER_GROUP
TOP_K = 2
D_EXPERT = 512
MOE_BLOCK = 128
PLE_DIM = 256
RMS_EPS = 1e-6
GN_EPS = 64e-5

N_PROMPT = BATCH * SEQ
N_SAMPLE = DEC_BATCH * DEC_SEQ
N_TOK = N_PROMPT + N_SAMPLE
LANES = 128
SUBLANES = 8
TM = 512
TM_S = 256
TM_T = 256
TM_L = 128
CHAIN_B = LANES // N_HEADS
SCAN_TT = 64
LORA_W = 512
N_TAIL = 2 * D_MODEL + LORA_W
EXP_M05 = 0.6065306597126334
F32 = jnp.float32
BF16 = jnp.bfloat16
_NT = (((1,), (1,)), ((), ()))


def _sigmoid(x):
    return 1.0 / (1.0 + jnp.exp(-x))


def _cparams(*sem, vmem_mb=None):
    kw = dict(dimension_semantics=sem)
    if vmem_mb is not None:
        kw["vmem_limit_bytes"] = vmem_mb * 1024 * 1024
    return pltpu.CompilerParams(**kw)


def _resident(shape):
    nd = len(shape)
    return pl.BlockSpec(shape, lambda *_: (0,) * nd, pipeline_mode=pl.Buffered(1))


def _store_chain(zt, out_ref, halves):
    for half in range(halves):
        for k in range(HEAD_SIZE):
            m = jnp.concatenate([zt[bb, half, pl.ds(k, N_HEADS, stride=HEAD_SIZE), :]
                                 for bb in range(CHAIN_B)], axis=0)
            out_ref[pl.ds(half * LANES * HEAD_SIZE + k, LANES, stride=HEAD_SIZE), :] = m.T


def _load_chain(y_ref, yt, halves):
    for half in range(halves):
        for v in range(HEAD_SIZE):
            mt = y_ref[pl.ds(half * LANES * HEAD_SIZE + v, LANES, stride=HEAD_SIZE), :].T
            for bb in range(CHAIN_B):
                yt[bb, half, pl.ds(v, N_HEADS, stride=HEAD_SIZE), :] = mt[bb * N_HEADS:(bb + 1) * N_HEADS, :]


def _norm_body(xp_ref, xs_ref, g_ref, o_ref, *, n_prompt_tiles):
    i = pl.program_id(0)

    def f(x):
        ms = jnp.mean(x * x, axis=-1, keepdims=True)
        return (x * lax.rsqrt(ms + RMS_EPS) * g_ref[...]).astype(o_ref.dtype)

    @pl.when(i < n_prompt_tiles)
    def _():
        o_ref[...] = f(xp_ref[...])

    @pl.when(i >= n_prompt_tiles)
    def _():
        o_ref[...] = f(xs_ref[...])


def _norm_cast(xp, xs, g):
    npt = N_PROMPT // TM
    return pl.pallas_call(
        functools.partial(_norm_body, n_prompt_tiles=npt),
        grid=(N_TOK // TM,),
        in_specs=[pl.BlockSpec((TM, D_MODEL), lambda i: (jnp.minimum(i, npt - 1), 0)),
                  pl.BlockSpec((TM, D_MODEL), lambda i: (jnp.maximum(i - npt, 0), 0)),
                  pl.BlockSpec((1, D_MODEL), lambda i: (0, 0))],
        out_specs=pl.BlockSpec((TM, D_MODEL), lambda i: (i, 0)),
        out_shape=jax.ShapeDtypeStruct((N_TOK, D_MODEL), BF16),
        compiler_params=_cparams("arbitrary"),
        name="norm_cast",
    )(xp, xs, g)


def _inproj_body(xn_ref, w_ref, mu_ref, st_ref, z_ref, last_ref, sraw_ref, wb_ref, carry_ref, *,
                 shift_lo, shift_hi, tile_lo, n_prompt_tiles, tiles_per_seq):
    j = pl.program_id(0)
    i = pl.program_id(1) + tile_lo

    @pl.when(pl.program_id(1) == 0)
    def _():
        wb_ref[...] = w_ref[...].astype(BF16)

    z = jnp.dot(xn_ref[...], wb_ref[...], preferred_element_type=F32)
    tm = z.shape[0]
    last_ref[...] = z[tm - SUBLANES:tm]
    shifted = jnp.logical_and(j >= shift_lo, j < shift_hi)
    is_prompt = i < n_prompt_tiles

    @pl.when(jnp.logical_not(shifted))
    def _():
        z_ref[...] = z

    @pl.when(jnp.logical_and(shifted, is_prompt))
    def _():
        @pl.when(i % tiles_per_seq == 0)
        def _():
            carry_ref[...] = jnp.zeros_like(carry_ref)

        prev = pltpu.roll(z, 1, 0)
        row = lax.broadcasted_iota(jnp.int32, (tm, 1), 0)
        prev = jnp.where(row == 0, carry_ref[SUBLANES - 1:SUBLANES, :], prev)
        z_ref[...] = z + mu_ref[...] * (prev - z)
        carry_ref[...] = z[tm - SUBLANES:tm]

    @pl.when(jnp.logical_and(shifted, jnp.logical_not(is_prompt)))
    def _():
        prev = jnp.concatenate([st_ref[...], z[:tm - DEC_BATCH]], axis=0)
        z_ref[...] = z + mu_ref[...] * (prev - z)

    @pl.when(jnp.logical_not(is_prompt))
    def _():
        sraw_ref[...] = z[tm - DEC_BATCH:tm]


def _inproj(xn, w, mu, st, *, tn, col_blk_off, n_col_blocks, shift_lo, shift_hi, tile_lo, n_tiles, name):
    n_out = tn * n_col_blocks
    body = functools.partial(_inproj_body, shift_lo=shift_lo, shift_hi=shift_hi, tile_lo=tile_lo,
                             n_prompt_tiles=N_PROMPT // TM, tiles_per_seq=SEQ // TM)
    return pl.pallas_call(
        body,
        grid=(n_col_blocks, n_tiles),
        in_specs=[pl.BlockSpec((TM, D_MODEL), lambda j, i: (i + tile_lo, 0)),
                  pl.BlockSpec((D_MODEL, tn), lambda j, i: (0, j + col_blk_off)),
                  pl.BlockSpec((1, tn), lambda j, i: (0, j)),
                  pl.BlockSpec((DEC_BATCH, tn), lambda j, i: (0, j))],
        out_specs=[pl.BlockSpec((TM, tn), lambda j, i: (i, j)),
                   pl.BlockSpec((SUBLANES, tn), lambda j, i: (i, j)),
                   pl.BlockSpec((DEC_BATCH, tn), lambda j, i: (0, j))],
        out_shape=[jax.ShapeDtypeStruct((n_tiles * TM, n_out), F32),
                   jax.ShapeDtypeStruct((n_tiles * SUBLANES, n_out), F32),
                   jax.ShapeDtypeStruct((DEC_BATCH, n_out), F32)],
        scratch_shapes=[pltpu.VMEM((D_MODEL, tn), BF16), pltpu.VMEM((SUBLANES, tn), F32)],
        compiler_params=_cparams("arbitrary", "arbitrary", vmem_mb=48),
        name=name,
    )(xn, w, mu, st)


def _inproj_t_body(xn_ref, wt_ref, o_ref, zt):
    b = pl.program_id(2)
    z = lax.dot_general(wt_ref[...], xn_ref[...], _NT, preferred_element_type=F32)
    for half in range(TM_T // LANES):
        zt[b, half] = z[:, half * LANES:(half + 1) * LANES]

    @pl.when(b == CHAIN_B - 1)
    def _():
        _store_chain(zt, o_ref, TM_T // LANES)


def _inproj_t(xn, wt):
    n_tb = SEQ // TM_T
    return pl.pallas_call(
        _inproj_t_body,
        grid=(3, n_tb, CHAIN_B),
        in_specs=[pl.BlockSpec((TM_T, D_MODEL), lambda j, tb, b: (b * n_tb + tb, 0)),
                  pl.BlockSpec((RWKV_WIDTH, D_MODEL), lambda j, tb, b: (j, 0))],
        out_specs=pl.BlockSpec((None, TM_T * HEAD_SIZE, LANES), lambda j, tb, b: (j, tb, 0)),
        out_shape=jax.ShapeDtypeStruct((3, SEQ * HEAD_SIZE, LANES), F32),
        scratch_shapes=[pltpu.VMEM((CHAIN_B, TM_T // LANES, RWKV_WIDTH, LANES), F32)],
        compiler_params=_cparams("arbitrary", "arbitrary", "arbitrary", vmem_mb=56),
        name="inproj_t",
    )(xn, wt)


def _conv_body(zc_ref, cw_ref, sc_ref, wco_ref, o_ref, ulast_ref, us_ref, carry_ref, *,
               n_prompt_tiles, tiles_per_seq):
    i = pl.program_id(0)
    gate_b = zc_ref[:, 0:CONV_WIDTH]
    u = zc_ref[:, CONV_WIDTH:2 * CONV_WIDTH] * zc_ref[:, 2 * CONV_WIDTH:3 * CONV_WIDTH]
    tm = u.shape[0]
    ulast_ref[...] = u[tm - SUBLANES:tm]
    w0 = cw_ref[0:1, :]
    w1 = cw_ref[1:2, :]
    w2 = cw_ref[2:3, :]

    def finish(p1, p2):
        conv = w0 * p2 + w1 * p1 + w2 * u
        y = (gate_b * conv).astype(BF16)
        o_ref[...] = jnp.dot(y, wco_ref[...], preferred_element_type=F32)

    @pl.when(i < n_prompt_tiles)
    def _():
        @pl.when(i % tiles_per_seq == 0)
        def _():
            carry_ref[...] = jnp.zeros_like(carry_ref)

        row = lax.broadcasted_iota(jnp.int32, (tm, 1), 0)
        c1 = carry_ref[SUBLANES - 1:SUBLANES, :]
        c2 = carry_ref[SUBLANES - 2:SUBLANES - 1, :]
        p1 = jnp.where(row == 0, c1, pltpu.roll(u, 1, 0))
        p2 = jnp.where(row == 0, c2, jnp.where(row == 1, c1, pltpu.roll(u, 2, 0)))
        carry_ref[...] = u[tm - SUBLANES:tm]
        finish(p1, p2)

    @pl.when(i >= n_prompt_tiles)
    def _():
        p1 = jnp.concatenate([sc_ref[DEC_BATCH:2 * DEC_BATCH, :], u[:tm - DEC_BATCH]], axis=0)
        p2 = jnp.concatenate([sc_ref[...], u[:tm - 2 * DEC_BATCH]], axis=0)
        us_ref[...] = u[tm - 2 * DEC_BATCH:tm]
        finish(p1, p2)


def _conv_branch(z_conv, conv_w, sc, wco):
    n_tiles = N_TOK // TM
    body = functools.partial(_conv_body, n_prompt_tiles=N_PROMPT // TM, tiles_per_seq=SEQ // TM)
    return pl.pallas_call(
        body,
        grid=(n_tiles,),
        in_specs=[pl.BlockSpec((TM, 3 * CONV_WIDTH), lambda i: (i, 0)),
                  pl.BlockSpec((3, CONV_WIDTH), lambda i: (0, 0)),
                  pl.BlockSpec((2 * DEC_BATCH, CONV_WIDTH), lambda i: (0, 0)),
                  _resident((CONV_WIDTH, D_MODEL))],
        out_specs=[pl.BlockSpec((TM, D_MODEL), lambda i: (i, 0)),
                   pl.BlockSpec((SUBLANES, CONV_WIDTH), lambda i: (i, 0)),
                   pl.BlockSpec((2 * DEC_BATCH, CONV_WIDTH), lambda i: (0, 0))],
        out_shape=[jax.ShapeDtypeStruct((N_TOK, D_MODEL), F32),
                   jax.ShapeDtypeStruct((n_tiles * SUBLANES, CONV_WIDTH), F32),
                   jax.ShapeDtypeStruct((2 * DEC_BATCH, CONV_WIDTH), F32)],
        scratch_shapes=[pltpu.VMEM((SUBLANES, CONV_WIDTH), F32)],
        compiler_params=_cparams("arbitrary", vmem_mb=48),
        name="conv_branch",
    )(z_conv, conv_w, sc, wco)


def _lora_body(zl_ref, w2t_ref, a2t_ref, w2_ref, a2_ref, g2_ref,
               g_ref, wlc_ref, alc_ref, wls_ref, als_ref, zt, *, n_prompt_steps):
    s = pl.program_id(0)
    tw = jnp.tanh(zl_ref[:, 0:LANES]).astype(BF16)
    xa = zl_ref[:, LANES:2 * LANES].astype(BF16)
    xg = zl_ref[:, 2 * LANES:LORA_W]
    g_ref[...] = jnp.dot(_sigmoid(xg).astype(BF16), g2_ref[...], preferred_element_type=F32)

    @pl.when(s < n_prompt_steps)
    def _():
        b = s % CHAIN_B
        zt[0, b, 0] = lax.dot_general(w2t_ref[...], tw, _NT, preferred_element_type=F32)
        zt[1, b, 0] = lax.dot_general(a2t_ref[...], xa, _NT, preferred_element_type=F32)

        @pl.when(b == CHAIN_B - 1)
        def _():
            _store_chain(zt.at[0], wlc_ref, 1)
            _store_chain(zt.at[1], alc_ref, 1)

    @pl.when(s >= n_prompt_steps)
    def _():
        wls_ref[...] = jnp.dot(tw, w2_ref[...], preferred_element_type=F32)
        als_ref[...] = jnp.dot(xa, a2_ref[...], preferred_element_type=F32)


def _lora(z_tail, w2t, a2t, w2p, a2p, g2b):
    n_tb = SEQ // TM_L
    nps = n_tb * CHAIN_B
    lora_blk = 2 * D_MODEL // LORA_W

    def row_blk(s):
        return jnp.where(s < nps, (s % CHAIN_B) * n_tb + s // CHAIN_B, s)

    chain_spec = pl.BlockSpec((TM_L * HEAD_SIZE, LANES), lambda s: (jnp.minimum(s // CHAIN_B, n_tb - 1), 0))
    samp_spec = pl.BlockSpec((TM_L, RWKV_WIDTH), lambda s: (jnp.maximum(s - nps, 0), 0))
    chain_shape = jax.ShapeDtypeStruct((SEQ * HEAD_SIZE, LANES), F32)
    samp_shape = jax.ShapeDtypeStruct((N_SAMPLE, RWKV_WIDTH), F32)
    return pl.pallas_call(
        functools.partial(_lora_body, n_prompt_steps=nps),
        grid=(N_TOK // TM_L,),
        in_specs=[pl.BlockSpec((TM_L, LORA_W), lambda s: (row_blk(s), lora_blk)),
                  _resident((RWKV_WIDTH, LANES)), _resident((RWKV_WIDTH, LANES)),
                  _resident((LANES, RWKV_WIDTH)), _resident((LANES, RWKV_WIDTH)),
                  _resident((GATE_LORA, RWKV_WIDTH))],
        out_specs=[pl.BlockSpec((TM_L, RWKV_WIDTH), lambda s: (row_blk(s), 0)),
                   chain_spec, chain_spec, samp_spec, samp_spec],
        out_shape=[jax.ShapeDtypeStruct((N_TOK, RWKV_WIDTH), F32),
                   chain_shape, chain_shape, samp_shape, samp_shape],
        scratch_shapes=[pltpu.VMEM((2, CHAIN_B, 1, RWKV_WIDTH, LANES), F32)],
        compiler_params=_cparams("arbitrary", vmem_mb=48),
        name="lora",
    )(z_tail, w2t, a2t, w2p, a2p, g2b)


def _scan_body(r_ref, k_ref, v_ref, wl_ref, al_ref, kk_ref, ka_ref, rk_ref, lw_ref, lb_ref,
               mur_ref, muk_ref, muv_ref, w0_ref, a0_ref, s0_ref,
               y_ref, s_ref, vec_ref, prev_ref, *, tt):
    @pl.when(pl.program_id(1) == 0)
    def _():
        s_ref[...] = s0_ref[...]
        prev_ref[...] = jnp.zeros_like(prev_ref)

    def step(t, carry):
        r_raw = r_ref[t]
        k_raw = k_ref[t]
        v_raw = v_ref[t]
        r = r_raw + mur_ref[...] * (prev_ref[0] - r_raw)
        k = k_raw + muk_ref[...] * (prev_ref[1] - k_raw)
        v = v_raw + muv_ref[...] * (prev_ref[2] - v_raw)
        prev_ref[0] = r_raw
        prev_ref[1] = k_raw
        prev_ref[2] = v_raw
        decay = jnp.exp(-EXP_M05 * _sigmoid(wl_ref[t] + w0_ref[...]))
        a = _sigmoid(al_ref[t] + a0_ref[...])
        kk = k * kk_ref[...]
        nrm = jnp.sqrt(jnp.sum(kk * kk, axis=0, keepdims=True))
        kk = kk / jnp.maximum(nrm, 1e-12)
        kf = k * (1.0 + (a - 1.0) * ka_ref[...])
        vec_ref[0] = -kk
        vec_ref[1] = decay
        vec_ref[2] = kk * a
        vec_ref[3] = kf
        vec_ref[4] = r

        def row(j, kx):
            return vec_ref[j, pl.ds(kx, HEAD_SIZE, stride=0), :]

        parts = [jnp.zeros((HEAD_SIZE, LANES), F32) for _ in range(4)]
        for kx in range(HEAD_SIZE):
            parts[kx % 4] = parts[kx % 4] + s_ref[kx] * row(0, kx)
        sa = (parts[0] + parts[1]) + (parts[2] + parts[3])

        parts = [jnp.zeros((HEAD_SIZE, LANES), F32) for _ in range(4)]
        for kx in range(HEAD_SIZE):
            sn = s_ref[kx] * row(1, kx) + sa * row(2, kx) + v * row(3, kx)
            s_ref[kx] = sn
            parts[kx % 4] = parts[kx % 4] + sn * row(4, kx)
        o = (parts[0] + parts[1]) + (parts[2] + parts[3])

        mu = jnp.mean(o, axis=0, keepdims=True)
        dlt = o - mu
        var = jnp.mean(dlt * dlt, axis=0, keepdims=True)
        on = dlt * lax.rsqrt(var + GN_EPS) * lw_ref[...] + lb_ref[...]
        bonus = jnp.sum(r * kf * rk_ref[...], axis=0, keepdims=True) * v
        y_ref[t] = on + bonus
        return carry

    lax.fori_loop(0, tt, step, 0)


def _scan(seqs, params, s0, tt, name):
    g, t = s0.shape[0], seqs[0][0].shape[1]

    def seq_spec(lead):
        if lead is None:
            return pl.BlockSpec((None, tt, HEAD_SIZE, LANES), lambda gi, ti: (gi, ti, 0, 0))
        return pl.BlockSpec((None, tt, HEAD_SIZE, LANES), lambda gi, ti: (lead, ti, 0, 0))

    par_spec = pl.BlockSpec((HEAD_SIZE, LANES), lambda gi, ti: (0, 0))
    st_spec = pl.BlockSpec((None, HEAD_SIZE, HEAD_SIZE, LANES), lambda gi, ti: (gi, 0, 0, 0))
    return pl.pallas_call(
        functools.partial(_scan_body, tt=tt),
        grid=(g, t // tt),
        in_specs=[seq_spec(lead) for _, lead in seqs] + [par_spec] * len(params) + [st_spec],
        out_specs=[seq_spec(None), st_spec],
        out_shape=[jax.ShapeDtypeStruct((g, t, HEAD_SIZE, LANES), F32),
                   jax.ShapeDtypeStruct((g, HEAD_SIZE, HEAD_SIZE, LANES), F32)],
        scratch_shapes=[pltpu.VMEM((5, HEAD_SIZE, LANES), F32), pltpu.VMEM((3, HEAD_SIZE, LANES), F32)],
        compiler_params=_cparams("arbitrary", "arbitrary", vmem_mb=48),
        name=name,
    )(*[a for a, _ in seqs], *params, s0)


def _to_chain_sample(x):
    ng = DEC_BATCH // CHAIN_B
    x = x.reshape(DEC_SEQ, ng, CHAIN_B, N_HEADS, HEAD_SIZE).transpose(1, 0, 4, 2, 3)
    return x.reshape(ng, DEC_SEQ, HEAD_SIZE, LANES)


def _from_chain_sample(y):
    ng = DEC_BATCH // CHAIN_B
    y = y.reshape(ng, DEC_SEQ, HEAD_SIZE, CHAIN_B, N_HEADS).transpose(1, 0, 3, 4, 2)
    return y.reshape(N_SAMPLE, RWKV_WIDTH)


def _param_chain(p):
    return jnp.tile(p.reshape(N_HEADS, HEAD_SIZE).T, (1, CHAIN_B))


def _rwkv_out_body(yc_ref, ys_ref, g_ref, w_ref, o_ref, yt, *, n_prompt_steps):
    s = pl.program_id(0)

    def finish(y):
        o_ref[...] = jnp.dot((y * g_ref[...]).astype(BF16), w_ref[...], preferred_element_type=F32)

    @pl.when(s < n_prompt_steps)
    def _():
        b = s % CHAIN_B

        @pl.when(b == 0)
        def _():
            _load_chain(yc_ref, yt, TM_T // LANES)

        finish(jnp.concatenate([yt[b, half].T for half in range(TM_T // LANES)], axis=0))

    @pl.when(s >= n_prompt_steps)
    def _():
        finish(ys_ref[...])


def _rwkv_out(y_chain, y_s, g, w):
    n_tb = SEQ // TM_T
    nps = n_tb * CHAIN_B

    def row_blk(s):
        return jnp.where(s < nps, (s % CHAIN_B) * n_tb + s // CHAIN_B, s)

    return pl.pallas_call(
        functools.partial(_rwkv_out_body, n_prompt_steps=nps),
        grid=(N_TOK // TM_T,),
        in_specs=[pl.BlockSpec((TM_T * HEAD_SIZE, LANES), lambda s: (jnp.minimum(s // CHAIN_B, n_tb - 1), 0)),
                  pl.BlockSpec((TM_T, RWKV_WIDTH), lambda s: (jnp.maximum(s - nps, 0), 0)),
                  pl.BlockSpec((TM_T, RWKV_WIDTH), lambda s: (row_blk(s), 0)),
                  _resident((RWKV_WIDTH, D_MODEL))],
        out_specs=pl.BlockSpec((TM_T, D_MODEL), lambda s: (row_blk(s), 0)),
        out_shape=jax.ShapeDtypeStruct((N_TOK, D_MODEL), F32),
        scratch_shapes=[pltpu.VMEM((CHAIN_B, TM_T // LANES, RWKV_WIDTH, LANES), F32)],
        compiler_params=_cparams("arbitrary", vmem_mb=56),
        name="rwkv_out",
    )(y_chain, y_s, g, w)


def _mix_body(co_ref, ro_ref, gc_ref, gr_ref, xp_ref, xs_ref, wm_ref, nf_ref, wr_ref, br_ref,
              h_ref, hn_ref, ridx_ref, rw_ref, *, n_prompt_tiles):
    i = pl.program_id(0)
    mixed = _sigmoid(gc_ref[...]) * co_ref[...] + _sigmoid(gr_ref[...]) * ro_ref[...]
    mo = jnp.dot(mixed.astype(BF16), wm_ref[...], preferred_element_type=F32)

    def finish(x):
        h = x + mo
        h_ref[...] = h
        ms = jnp.mean(h * h, axis=-1, keepdims=True)
        hn = h * lax.rsqrt(ms + RMS_EPS) * nf_ref[...]
        hn_ref[...] = hn.reshape(hn.shape[0], D_MODEL // LANES, LANES)
        logits = jnp.dot(hn, wr_ref[...], preferred_element_type=F32,
                         precision=lax.Precision.HIGHEST) + br_ref[...]
        tm = logits.shape[0]
        lane = lax.broadcasted_iota(jnp.int32, (tm, LANES), 1)
        neg = jnp.float32(-jnp.inf)
        gl = jnp.where(lane < N_GROUPS, logits, neg)
        gmax = jnp.max(gl, axis=-1, keepdims=True)
        g_idx = jnp.min(jnp.where(gl == gmax, lane, LANES), axis=-1, keepdims=True)
        g_w = 1.0 / jnp.sum(jnp.exp(gl - gmax), axis=-1, keepdims=True)
        lo = N_GROUPS + g_idx * EXPERTS_PER_GROUP
        el = jnp.where(jnp.logical_and(lane >= lo, lane < lo + EXPERTS_PER_GROUP), logits, neg)
        m1 = jnp.max(el, axis=-1, keepdims=True)
        i1 = jnp.min(jnp.where(el == m1, lane, LANES), axis=-1, keepdims=True)
        el2 = jnp.where(lane == i1, neg, el)
        m2 = jnp.max(el2, axis=-1, keepdims=True)
        i2 = jnp.min(jnp.where(el2 == m2, lane, LANES), axis=-1, keepdims=True)
        t2 = jnp.exp(m2 - m1)
        den = 1.0 + t2
        ridx_ref[...] = jnp.where(lane == 0, i1 - N_GROUPS, jnp.where(lane == 1, i2 - N_GROUPS, 0))
        rw_ref[...] = jnp.where(lane == 0, (1.0 / den) * g_w, jnp.where(lane == 1, (t2 / den) * g_w, 0.0))

    @pl.when(i < n_prompt_tiles)
    def _():
        finish(xp_ref[...])

    @pl.when(i >= n_prompt_tiles)
    def _():
        finish(xs_ref[...])


def _mix_route(conv_out, rwkv_out, z_tail, xp, xs, wm, nf, wr, br):
    npt = N_PROMPT // TM_S
    tok_spec = pl.BlockSpec((TM_S, D_MODEL), lambda i: (i, 0))
    small_spec = pl.BlockSpec((TM_S, LANES), lambda i: (i, 0))
    return pl.pallas_call(
        functools.partial(_mix_body, n_prompt_tiles=npt),
        grid=(N_TOK // TM_S,),
        in_specs=[tok_spec, tok_spec,
                  pl.BlockSpec((TM_S, D_MODEL), lambda i: (i, 0)),
                  pl.BlockSpec((TM_S, D_MODEL), lambda i: (i, 1)),
                  pl.BlockSpec((TM_S, D_MODEL), lambda i: (jnp.minimum(i, npt - 1), 0)),
                  pl.BlockSpec((TM_S, D_MODEL), lambda i: (jnp.maximum(i - npt, 0), 0)),
                  _resident((D_MODEL, D_MODEL)),
                  pl.BlockSpec((1, D_MODEL), lambda i: (0, 0)),
                  _resident((D_MODEL, LANES)),
                  pl.BlockSpec((1, LANES), lambda i: (0, 0))],
        out_specs=[tok_spec, pl.BlockSpec((TM_S, D_MODEL // LANES, LANES), lambda i: (i, 0, 0)),
                   small_spec, small_spec],
        out_shape=[jax.ShapeDtypeStruct((N_TOK, D_MODEL), F32),
                   jax.ShapeDtypeStruct((N_TOK, D_MODEL // LANES, LANES), F32),
                   jax.ShapeDtypeStruct((N_TOK, LANES), jnp.int32),
                   jax.ShapeDtypeStruct((N_TOK, LANES), F32)],
        compiler_params=_cparams("arbitrary", vmem_mb=56),
        name="mix_route",
    )(conv_out, rwkv_out, z_tail, z_tail, xp, xs, wm, nf, wr, br)


def _expert_body(blk_e_ref, slot_tok_ref, nused_ref, first_ref, par_ref, next_e_ref,
                 hn_ref, wg_hbm, wu_hbm, wd_hbm, yb_ref,
                 xbuf, sem, wfg, wfu, wfd, wsem, wgb, wub, wdb):
    i = pl.program_id(0)
    nused = nused_ref[0]
    slot = i % 2

    def row_copy(blk, r, s):
        tok = slot_tok_ref[blk * MOE_BLOCK + r]
        return pltpu.make_async_copy(hn_ref.at[tok], xbuf.at[s, r], sem.at[s])

    def issue(blk, s):
        for r in range(MOE_BLOCK):
            row_copy(blk, r, s).start()

    def w_copies(e, s):
        return (pltpu.make_async_copy(wg_hbm.at[e], wfg.at[s], wsem.at[s]),
                pltpu.make_async_copy(wu_hbm.at[e], wfu.at[s], wsem.at[s]),
                pltpu.make_async_copy(wd_hbm.at[e], wfd.at[s], wsem.at[s]))

    @pl.when(jnp.logical_and(i == 0, nused > 0))
    def _():
        for c in w_copies(blk_e_ref[0], 0):
            c.start(priority=1)
        issue(0, 0)

    @pl.when(i + 1 < nused)
    def _():
        issue(i + 1, 1 - slot)

    @pl.when(i < nused)
    def _():
        @pl.when(first_ref[i] == 1)
        def _():
            ws = par_ref[i]
            for c in w_copies(blk_e_ref[i], ws):
                c.wait()

            @pl.when(next_e_ref[i] >= 0)
            def _():
                for c in w_copies(next_e_ref[i], 1 - ws):
                    c.start(priority=1)

            wgb[...] = wfg[ws].astype(BF16)
            wub[...] = wfu[ws].astype(BF16)
            wdb[...] = wfd[ws].astype(BF16)

        for r in range(MOE_BLOCK):
            row_copy(i, r, slot).wait()
        xe = xbuf[slot].reshape(MOE_BLOCK, D_MODEL).astype(BF16)
        gate = jnp.dot(xe, wgb[...], preferred_element_type=F32)
        up = jnp.dot(xe, wub[...], preferred_element_type=F32)
        hdn = (gate * _sigmoid(gate)) * up
        yb = jnp.dot(hdn.astype(BF16), wdb[...], preferred_element_type=F32)
        yb_ref[...] = yb.reshape(MOE_BLOCK, D_MODEL // LANES, LANES)

    @pl.when(i >= nused)
    def _():
        yb_ref[...] = jnp.zeros_like(yb_ref)


def _experts(plan, hn, wg, wu, wd):
    blk_e, slot_tok, nused, first, par, next_e = plan
    n_blocks = blk_e.shape[0]
    any_spec = pl.BlockSpec(memory_space=pl.ANY)
    return pl.pallas_call(
        _expert_body,
        grid_spec=pltpu.PrefetchScalarGridSpec(
            num_scalar_prefetch=6,
            grid=(n_blocks,),
            in_specs=[any_spec, any_spec, any_spec, any_spec],
            out_specs=pl.BlockSpec((MOE_BLOCK, D_MODEL // LANES, LANES), lambda i, *_: (i, 0, 0)),
            scratch_shapes=[pltpu.VMEM((2, MOE_BLOCK, D_MODEL // LANES, LANES), F32),
                            pltpu.SemaphoreType.DMA((2,)),
                            pltpu.VMEM((2, D_MODEL, D_EXPERT), F32),
                            pltpu.VMEM((2, D_MODEL, D_EXPERT), F32),
                            pltpu.VMEM((2, D_EXPERT, D_MODEL), F32),
                            pltpu.SemaphoreType.DMA((2,)),
                            pltpu.VMEM((D_MODEL, D_EXPERT), BF16),
                            pltpu.VMEM((D_MODEL, D_EXPERT), BF16),
                            pltpu.VMEM((D_EXPERT, D_MODEL), BF16)]),
        out_shape=jax.ShapeDtypeStruct((n_blocks * MOE_BLOCK, D_MODEL // LANES, LANES), F32),
        compiler_params=_cparams("arbitrary", vmem_mb=48),
        name="experts",
    )(blk_e, slot_tok, nused, first, par, next_e, hn, wg, wu, wd)


def _combine_body(dest_ref, yb_ref, h_ref, rw_ref, p_ref, wpg_ref, wpp_ref, nf_ref, yp_ref, ys_ref,
                  ybuf, sem):
    i = pl.program_id(0)
    tm = h_ref.shape[0]
    slot = i % 2

    def row_copy(tile, r, s, sl):
        d = dest_ref[(tile * tm + r) * TOP_K + s]
        return pltpu.make_async_copy(yb_ref.at[d], ybuf.at[sl, s, r], sem.at[sl])

    def issue(tile, sl):
        for r in range(tm):
            row_copy(tile, r, 0, sl).start()
            row_copy(tile, r, 1, sl).start()

    @pl.when(i == 0)
    def _():
        issue(0, 0)

    @pl.when(i + 1 < pl.num_programs(0))
    def _():
        issue(i + 1, 1 - slot)

    for r in range(tm):
        row_copy(i, r, 0, slot).wait()
        row_copy(i, r, 1, slot).wait()

    rw = rw_ref[...]
    y0 = ybuf[slot, 0].reshape(tm, D_MODEL)
    y1 = ybuf[slot, 1].reshape(tm, D_MODEL)
    h2 = h_ref[...] + (y0 * rw[:, 0:1] + y1 * rw[:, 1:2])
    gate = _sigmoid(jnp.dot(h2.astype(BF16), wpg_ref[...], preferred_element_type=F32))
    pp = jnp.dot(p_ref[...].astype(BF16), wpp_ref[...], preferred_element_type=F32)
    h3 = h2 + gate * pp
    ms = jnp.mean(h3 * h3, axis=-1, keepdims=True)
    y = h3 * lax.rsqrt(ms + RMS_EPS) * nf_ref[...]

    @pl.when(i < N_PROMPT // TM_S)
    def _():
        yp_ref[...] = y

    @pl.when(i >= N_PROMPT // TM_S)
    def _():
        ys_ref[...] = y


def _combine(dest, yb, h, rw, p_all, wpg, wpp, nf):
    npt = N_PROMPT // TM_S
    return pl.pallas_call(
        _combine_body,
        grid_spec=pltpu.PrefetchScalarGridSpec(
            num_scalar_prefetch=1,
            grid=(N_TOK // TM_S,),
            in_specs=[pl.BlockSpec(memory_space=pl.ANY),
                      pl.BlockSpec((TM_S, D_MODEL), lambda i, d: (i, 0)),
                      pl.BlockSpec((TM_S, LANES), lambda i, d: (i, 0)),
                      pl.BlockSpec((TM_S, PLE_DIM), lambda i, d: (i, 0)),
                      pl.BlockSpec((D_MODEL, D_MODEL), lambda i, d: (0, 0), pipeline_mode=pl.Buffered(1)),
                      pl.BlockSpec((PLE_DIM, D_MODEL), lambda i, d: (0, 0), pipeline_mode=pl.Buffered(1)),
                      pl.BlockSpec((1, D_MODEL), lambda i, d: (0, 0))],
            out_specs=[pl.BlockSpec((TM_S, D_MODEL), lambda i, d: (jnp.minimum(i, npt - 1), 0)),
                       pl.BlockSpec((TM_S, D_MODEL), lambda i, d: (jnp.maximum(i - npt, 0), 0))],
            scratch_shapes=[pltpu.VMEM((2, TOP_K, TM_S, D_MODEL // LANES, LANES), F32),
                            pltpu.SemaphoreType.DMA((2,))]),
        out_shape=[jax.ShapeDtypeStruct((N_PROMPT, D_MODEL), F32),
                   jax.ShapeDtypeStruct((N_SAMPLE, D_MODEL), F32)],
        compiler_params=_cparams("arbitrary", vmem_mb=48),
        name="combine_ple",
    )(dest, yb, h, rw, p_all, wpg, wpp, nf)


def _dispatch_plan(eidx):
    n_assign = N_TOK * TOP_K
    e_flat = eidx.reshape(n_assign)
    onehot = (e_flat[:, None] == jnp.arange(N_EXPERTS, dtype=jnp.int32)[None, :]).astype(jnp.int32)
    csum = jnp.cumsum(onehot, axis=0)
    counts = csum[-1]
    rank = jnp.sum(csum * onehot, axis=1) - 1
    padded = (counts + MOE_BLOCK - 1) // MOE_BLOCK * MOE_BLOCK
    pad_end = jnp.cumsum(padded)
    pad_start = pad_end - padded
    dest = pad_start[e_flat] + rank
    n_blocks = -(-n_assign // MOE_BLOCK) + N_EXPERTS
    tok = jnp.arange(n_assign, dtype=jnp.int32) // TOP_K
    slot_tok = jnp.zeros((n_blocks * MOE_BLOCK,), jnp.int32).at[dest].set(tok)
    blk = jnp.arange(n_blocks, dtype=jnp.int32)
    blk_e = jnp.minimum(jnp.searchsorted(pad_end, blk * MOE_BLOCK, side="right"), N_EXPERTS - 1).astype(jnp.int32)
    nused = (pad_end[-1] // MOE_BLOCK).astype(jnp.int32)
    prev_e = jnp.concatenate([jnp.full((1,), -1, jnp.int32), blk_e[:-1]])
    first = jnp.logical_and(blk < nused, blk_e != prev_e)
    par = ((jnp.cumsum(first.astype(jnp.int32)) - 1) % 2).astype(jnp.int32)
    idx_first = jnp.where(first, blk, n_blocks)
    later = jnp.concatenate([lax.cummin(idx_first[::-1])[::-1][1:], jnp.full((1,), n_blocks, jnp.int32)])
    next_e = jnp.where(later < n_blocks, blk_e[jnp.minimum(later, n_blocks - 1)], -1).astype(jnp.int32)
    plan = (blk_e, slot_tok, nused.reshape(1), first.astype(jnp.int32), par, next_e)
    return dest.astype(jnp.int32), plan


def kernel(x_prompt, x_sample, state_conv, state_shift, state_wkv, p_prompt, p_sample, norm_mix, w_in, conv_w, w_conv_out, shift_mu, w0, w2, a0, a2, g2, k_k, k_a, r_k, lnx_w, lnx_b, w_rwkv_out, w_mix_out, norm_ffn, w_route_group, b_route_group, w_route_expert, b_route_expert, w_exp_gate, w_exp_up, w_exp_down, w_ple_proj, w_ple_gate, norm_final):
    c3 = 3 * CONV_WIDTH
    rw3 = 3 * RWKV_WIDTH
    n_tiles = N_TOK // TM
    xp = x_prompt.reshape(N_PROMPT, D_MODEL)
    xs = x_sample.transpose(1, 0, 2).reshape(N_SAMPLE, D_MODEL)
    win = w_in[0]
    mu = shift_mu[0]
    st = state_shift[0]

    def lora_cols(a, rows):
        z32 = jnp.zeros((rows, LANES - DECAY_LORA), a.dtype)
        return jnp.concatenate([a[:, rw3:rw3 + DECAY_LORA], z32,
                                a[:, rw3 + DECAY_LORA:rw3 + DECAY_LORA + A_LORA], z32,
                                a[:, rw3 + DECAY_LORA + A_LORA:]], axis=1)

    xn = _norm_cast(xp, xs, norm_mix)
    z_conv, _, _ = _inproj(xn, win, jnp.zeros((1, c3), F32), jnp.zeros((DEC_BATCH, c3), F32),
                           tn=1024, col_blk_off=0, n_col_blocks=c3 // 1024, shift_lo=0, shift_hi=0,
                           tile_lo=0, n_tiles=n_tiles, name="inproj_conv")
    rkv_s, _, sraw_rkv = _inproj(xn, win, mu[None, :rw3], st[:, :rw3],
                                 tn=1024, col_blk_off=c3 // 1024, n_col_blocks=rw3 // 1024,
                                 shift_lo=0, shift_hi=rw3 // 1024,
                                 tile_lo=N_PROMPT // TM, n_tiles=N_SAMPLE // TM, name="inproj_rkv_sample")
    w_tail = jnp.concatenate([win[:, c3 + RWKV_PROJ:], lora_cols(win[:, c3:c3 + RWKV_PROJ], D_MODEL)], axis=1)
    mu_tail = jnp.concatenate([jnp.zeros((1, 2 * D_MODEL), F32), lora_cols(mu[None, :], 1)], axis=1)
    st_tail = jnp.concatenate([jnp.zeros((DEC_BATCH, 2 * D_MODEL), F32), lora_cols(st, DEC_BATCH)], axis=1)
    nb_tail = 3
    z_tail, last_tail, sraw_tail = _inproj(xn, w_tail, mu_tail, st_tail, tn=N_TAIL // nb_tail, col_blk_off=0,
                                           n_col_blocks=nb_tail, shift_lo=nb_tail - 1, shift_hi=nb_tail,
                                           tile_lo=0, n_tiles=n_tiles, name="inproj_tail")
    rkv_chain = _inproj_t(xn, win[:, c3:c3 + rw3].T.astype(BF16))

    sc = state_conv[0].transpose(1, 0, 2).reshape(2 * DEC_BATCH, CONV_WIDTH)
    conv_out, ulast, us = _conv_branch(z_conv, conv_w[0], sc, w_conv_out[0].astype(BF16))

    def pad_rows(w):
        return jnp.concatenate([w, jnp.zeros((LANES - w.shape[0], w.shape[1]), w.dtype)], axis=0).astype(BF16)

    w2p, a2p = pad_rows(w2[0]), pad_rows(a2[0])
    g, wl_c, al_c, wl_s, al_s = _lora(z_tail, w2p.T, a2p.T, w2p, a2p, g2[0].astype(BF16))
    base = [_param_chain(p) for p in (k_k[0], k_a[0], r_k[0].reshape(RWKV_WIDTH), lnx_w[0], lnx_b[0])]
    mus = [_param_chain(mu[n * RWKV_WIDTH:(n + 1) * RWKV_WIDTH]) for n in range(3)]
    bias = [_param_chain(w0[0]), _param_chain(a0[0])]
    zero = jnp.zeros((HEAD_SIZE, LANES), F32)
    rkv4 = rkv_chain.reshape(3, SEQ, HEAD_SIZE, LANES)
    seqs_p = [(rkv4, 0), (rkv4, 1), (rkv4, 2),
              (wl_c.reshape(1, SEQ, HEAD_SIZE, LANES), None), (al_c.reshape(1, SEQ, HEAD_SIZE, LANES), None)]
    s0_p = jnp.zeros((1, HEAD_SIZE, HEAD_SIZE, LANES), F32)
    y_p, sf_p = _scan(seqs_p, base + mus + bias, s0_p, SCAN_TT, "wkv_scan_prompt")
    ng = DEC_BATCH // CHAIN_B
    s0_s = state_wkv[0].reshape(ng, CHAIN_B, N_HEADS, HEAD_SIZE, HEAD_SIZE).transpose(0, 4, 3, 1, 2)
    s0_s = s0_s.reshape(ng, HEAD_SIZE, HEAD_SIZE, LANES)
    seqs_s = [(_to_chain_sample(rkv_s[:, n * RWKV_WIDTH:(n + 1) * RWKV_WIDTH]), None) for n in range(3)]
    seqs_s += [(_to_chain_sample(wl_s), None), (_to_chain_sample(al_s), None)]
    y_s, sf_s = _scan(seqs_s, base + [zero] * 3 + bias, s0_s, DEC_SEQ, "wkv_scan_sample")
    rwkv_out = _rwkv_out(y_p.reshape(SEQ * HEAD_SIZE, LANES), _from_chain_sample(y_s), g,
                         w_rwkv_out[0].astype(BF16))

    wr = jnp.concatenate([w_route_group[0], w_route_expert[0],
                          jnp.zeros((D_MODEL, LANES - N_GROUPS - N_EXPERTS), F32)], axis=1)
    br = jnp.concatenate([b_route_group[0], b_route_expert[0],
                          jnp.zeros((LANES - N_GROUPS - N_EXPERTS,), F32)])[None, :]
    h, hn, ridx, rw = _mix_route(conv_out, rwkv_out, z_tail, xp, xs, w_mix_out[0].astype(BF16),
                                 norm_ffn, wr, br)

    dest, plan = _dispatch_plan(ridx[:, :TOP_K])
    yb = _experts(plan, hn, w_exp_gate[0], w_exp_up[0], w_exp_down[0])
    p_all = jnp.concatenate([p_prompt[0].reshape(N_PROMPT, PLE_DIM),
                             p_sample[0].transpose(1, 0, 2).reshape(N_SAMPLE, PLE_DIM)], axis=0)
    y_p2, y_s2 = _combine(dest, yb, h, rw, p_all, w_ple_gate[0].astype(BF16), w_ple_proj[0].astype(BF16),
                          norm_final[None, :])

    y_prompt = y_p2.reshape(BATCH, SEQ, D_MODEL)
    y_sample = y_s2.reshape(DEC_SEQ, DEC_BATCH, D_MODEL).transpose(1, 0, 2)
    tiles_per_seq = SEQ // TM
    seq_last = jnp.arange(BATCH) * tiles_per_seq + tiles_per_seq - 1

    def unpad_lora(a):
        o = 2 * D_MODEL
        return jnp.concatenate([a[:, o:o + DECAY_LORA], a[:, o + LANES:o + LANES + A_LORA],
                                a[:, o + 2 * LANES:]], axis=1)

    conv_p = ulast.reshape(-1, SUBLANES, CONV_WIDTH)[seq_last, SUBLANES - 2:, :][None]
    conv_s = us.reshape(2, DEC_BATCH, CONV_WIDTH).transpose(1, 0, 2)[None]
    lm = rkv4[:, SEQ - 1].reshape(3, HEAD_SIZE, BATCH, N_HEADS).transpose(2, 0, 3, 1).reshape(BATCH, rw3)
    lt = unpad_lora(last_tail.reshape(-1, SUBLANES, N_TAIL)[seq_last, SUBLANES - 1, :])
    shift_p = jnp.concatenate([lm, lt], axis=1)[None]
    shift_s = jnp.concatenate([sraw_rkv, unpad_lora(sraw_tail)], axis=1)[None]
    wkv_p = sf_p.reshape(HEAD_SIZE, HEAD_SIZE, BATCH, N_HEADS).transpose(2, 3, 1, 0)[None]
    wkv_s = sf_s.reshape(ng, HEAD_SIZE, HEAD_SIZE, CHAIN_B, N_HEADS).transpose(0, 3, 4, 2, 1)
    wkv_s = wkv_s.reshape(DEC_BATCH, N_HEADS, HEAD_SIZE, HEAD_SIZE)[None]
    return (y_prompt, y_sample, conv_p, shift_p, wkv_p, conv_s, shift_s, wkv_s)
```

```python
import functools

import jax
import jax.numpy as jnp
from jax import lax
from jax.experimental import pallas as pl
from jax.experimental.pallas import tpu as pltpu

D_MODEL = 2048
BATCH = 4
SEQ = 2048
DEC_BATCH = 128
DEC_SEQ = 4
CONV_WIDTH = 1024
RWKV_WIDTH = 2048
HEAD_SIZE = 64
N_HEADS = RWKV_WIDTH // HEAD_SIZE
DECAY_LORA = 96
A_LORA = 96
GATE_LORA = 256
RWKV_PROJ = 3 * RWKV_WIDTH + DECAY_LORA + A_LORA + GATE_LORA
N_GROUPS = 8
EXPERTS_PER_GROUP = 8
N_EXPERTS = N_GROUPS * EXPERTS_PER_GROUP
TOP_K = 2
D_EXPERT = 512
MOE_BLOCK = 128
PLE_DIM = 256
RMS_EPS = 1e-6
GN_EPS = 64e-5

N_PROMPT = BATCH * SEQ
N_SAMPLE = DEC_BATCH * DEC_SEQ
N_TOK = N_PROMPT + N_SAMPLE
LANES = 128
SUBLANES = 8
TM = 512
TM_S = 256
TM_T = 256
TM_L = 128
CHAIN_B = LANES // N_HEADS
SCAN_TT = 64
SCAN_TB = SUBLANES
GATHER_AHEAD = 3
N_XBUF = GATHER_AHEAD + 1
LORA_W = 512
N_TAIL = 2 * D_MODEL + LORA_W
EXP_M05 = 0.6065306597126334
F32 = jnp.float32
BF16 = jnp.bfloat16
_NT = (((1,), (1,)), ((), ()))


def _sigmoid(x):
    return 1.0 / (1.0 + jnp.exp(-x))


def _cparams(*sem, vmem_mb=None):
    kw = dict(dimension_semantics=sem)
    if vmem_mb is not None:
        kw["vmem_limit_bytes"] = vmem_mb * 1024 * 1024
    return pltpu.CompilerParams(**kw)


def _resident(shape):
    nd = len(shape)
    return pl.BlockSpec(shape, lambda *_: (0,) * nd, pipeline_mode=pl.Buffered(1))


def _chain_tile(zt, half, c):
    return jnp.concatenate([zt[bb, half, c * N_HEADS:(c + 1) * N_HEADS, :] for bb in range(CHAIN_B)], axis=0)


def _store_chain_ct(zt, out_ref, halves):
    for half in range(halves):
        for c in range(HEAD_SIZE):
            out_ref[c, half * LANES:(half + 1) * LANES, :] = _chain_tile(zt, half, c).T


def _store_chain_tc(zt, out_ref, halves):
    for half in range(halves):
        for c in range(HEAD_SIZE):
            out_ref[pl.ds(half * LANES * HEAD_SIZE + c, LANES, stride=HEAD_SIZE), :] = _chain_tile(zt, half, c).T


def _load_chain_tc(y_ref, yt, halves):
    for half in range(halves):
        for v in range(HEAD_SIZE):
            mt = y_ref[pl.ds(half * LANES * HEAD_SIZE + v, LANES, stride=HEAD_SIZE), :].T
            for bb in range(CHAIN_B):
                yt[bb, half, v * N_HEADS:(v + 1) * N_HEADS, :] = mt[bb * N_HEADS:(bb + 1) * N_HEADS, :]


def _norm_body(xp_ref, xs_ref, g_ref, o_ref, *, n_prompt_tiles):
    i = pl.program_id(0)

    def f(x):
        ms = jnp.mean(x * x, axis=-1, keepdims=True)
        return (x * lax.rsqrt(ms + RMS_EPS) * g_ref[...]).astype(o_ref.dtype)

    @pl.when(i < n_prompt_tiles)
    def _():
        o_ref[...] = f(xp_ref[...])

    @pl.when(i >= n_prompt_tiles)
    def _():
        o_ref[...] = f(xs_ref[...])


def _norm_cast(xp, xs, g):
    npt = N_PROMPT // TM
    return pl.pallas_call(
        functools.partial(_norm_body, n_prompt_tiles=npt),
        grid=(N_TOK // TM,),
        in_specs=[pl.BlockSpec((TM, D_MODEL), lambda i: (jnp.minimum(i, npt - 1), 0)),
                  pl.BlockSpec((TM, D_MODEL), lambda i: (jnp.maximum(i - npt, 0), 0)),
                  pl.BlockSpec((1, D_MODEL), lambda i: (0, 0))],
        out_specs=pl.BlockSpec((TM, D_MODEL), lambda i: (i, 0)),
        out_shape=jax.ShapeDtypeStruct((N_TOK, D_MODEL), BF16),
        compiler_params=_cparams("arbitrary"),
        name="norm_cast",
    )(xp, xs, g)


def _inproj_body(xn_ref, w_ref, mu_ref, st_ref, z_ref, last_ref, sraw_ref, wb_ref, carry_ref, *,
                 shift_lo, shift_hi, tile_lo, n_prompt_tiles, tiles_per_seq):
    j = pl.program_id(0)
    i = pl.program_id(1) + tile_lo

    @pl.when(pl.program_id(1) == 0)
    def _():
        wb_ref[...] = w_ref[...].astype(BF16)

    z = jnp.dot(xn_ref[...], wb_ref[...], preferred_element_type=F32)
    tm = z.shape[0]
    last_ref[...] = z[tm - SUBLANES:tm]
    shifted = jnp.logical_and(j >= shift_lo, j < shift_hi)
    is_prompt = i < n_prompt_tiles

    @pl.when(jnp.logical_not(shifted))
    def _():
        z_ref[...] = z

    @pl.when(jnp.logical_and(shifted, is_prompt))
    def _():
        @pl.when(i % tiles_per_seq == 0)
        def _():
            carry_ref[...] = jnp.zeros_like(carry_ref)

        prev = pltpu.roll(z, 1, 0)
        row = lax.broadcasted_iota(jnp.int32, (tm, 1), 0)
        prev = jnp.where(row == 0, carry_ref[SUBLANES - 1:SUBLANES, :], prev)
        z_ref[...] = z + mu_ref[...] * (prev - z)
        carry_ref[...] = z[tm - SUBLANES:tm]

    @pl.when(jnp.logical_and(shifted, jnp.logical_not(is_prompt)))
    def _():
        prev = jnp.concatenate([st_ref[...], z[:tm - DEC_BATCH]], axis=0)
        z_ref[...] = z + mu_ref[...] * (prev - z)

    @pl.when(jnp.logical_not(is_prompt))
    def _():
        sraw_ref[...] = z[tm - DEC_BATCH:tm]


def _inproj(xn, w, mu, st, *, tn, col_blk_off, n_col_blocks, shift_lo, shift_hi, tile_lo, n_tiles, name):
    n_out = tn * n_col_blocks
    body = functools.partial(_inproj_body, shift_lo=shift_lo, shift_hi=shift_hi, tile_lo=tile_lo,
                             n_prompt_tiles=N_PROMPT // TM, tiles_per_seq=SEQ // TM)
    return pl.pallas_call(
        body,
        grid=(n_col_blocks, n_tiles),
        in_specs=[pl.BlockSpec((TM, D_MODEL), lambda j, i: (i + tile_lo, 0)),
                  pl.BlockSpec((D_MODEL, tn), lambda j, i: (0, j + col_blk_off)),
                  pl.BlockSpec((1, tn), lambda j, i: (0, j)),
                  pl.BlockSpec((DEC_BATCH, tn), lambda j, i: (0, j))],
        out_specs=[pl.BlockSpec((TM, tn), lambda j, i: (i, j)),
                   pl.BlockSpec((SUBLANES, tn), lambda j, i: (i, j)),
                   pl.BlockSpec((DEC_BATCH, tn), lambda j, i: (0, j))],
        out_shape=[jax.ShapeDtypeStruct((n_tiles * TM, n_out), F32),
                   jax.ShapeDtypeStruct((n_tiles * SUBLANES, n_out), F32),
                   jax.ShapeDtypeStruct((DEC_BATCH, n_out), F32)],
        scratch_shapes=[pltpu.VMEM((D_MODEL, tn), BF16), pltpu.VMEM((SUBLANES, tn), F32)],
        compiler_params=_cparams("arbitrary", "arbitrary", vmem_mb=48),
        name=name,
    )(xn, w, mu, st)


def _inproj_t_body(xn_ref, wt_ref, o_ref, zt, *, time_major):
    b = pl.program_id(2)
    z = lax.dot_general(wt_ref[...], xn_ref[...], _NT, preferred_element_type=F32)
    for half in range(TM_T // LANES):
        zt[b, half] = z[:, half * LANES:(half + 1) * LANES]

    @pl.when(b == CHAIN_B - 1)
    def _():
        if time_major:
            _store_chain_tc(zt, o_ref, TM_T // LANES)
        else:
            _store_chain_ct(zt, o_ref, TM_T // LANES)


def _inproj_t(xn, wt, *, time_major, name):
    n_tb = SEQ // TM_T
    n = wt.shape[0] // RWKV_WIDTH
    if time_major:
        out_spec = pl.BlockSpec((None, TM_T * HEAD_SIZE, LANES), lambda j, tb, b: (j, tb, 0))
        out_shape = jax.ShapeDtypeStruct((n, SEQ * HEAD_SIZE, LANES), F32)
    else:
        out_spec = pl.BlockSpec((None, HEAD_SIZE, TM_T, LANES), lambda j, tb, b: (j, 0, tb, 0))
        out_shape = jax.ShapeDtypeStruct((n, HEAD_SIZE, SEQ, LANES), F32)
    return pl.pallas_call(
        functools.partial(_inproj_t_body, time_major=time_major),
        grid=(n, n_tb, CHAIN_B),
        in_specs=[pl.BlockSpec((TM_T, D_MODEL), lambda j, tb, b: (b * n_tb + tb, 0)),
                  pl.BlockSpec((RWKV_WIDTH, D_MODEL), lambda j, tb, b: (j, 0))],
        out_specs=out_spec,
        out_shape=out_shape,
        scratch_shapes=[pltpu.VMEM((CHAIN_B, TM_T // LANES, RWKV_WIDTH, LANES), F32)],
        compiler_params=_cparams("arbitrary", "arbitrary", "arbitrary", vmem_mb=56),
        name=name,
    )(xn, wt)


def _conv_body(zc_ref, cw_ref, sc_ref, wco_ref, o_ref, ulast_ref, us_ref, carry_ref, *,
               n_prompt_tiles, tiles_per_seq):
    i = pl.program_id(0)
    gate_b = zc_ref[:, 0:CONV_WIDTH]
    u = zc_ref[:, CONV_WIDTH:2 * CONV_WIDTH] * zc_ref[:, 2 * CONV_WIDTH:3 * CONV_WIDTH]
    tm = u.shape[0]
    ulast_ref[...] = u[tm - SUBLANES:tm]
    w0 = cw_ref[0:1, :]
    w1 = cw_ref[1:2, :]
    w2 = cw_ref[2:3, :]

    def finish(p1, p2):
        conv = w0 * p2 + w1 * p1 + w2 * u
        y = (gate_b * conv).astype(BF16)
        o_ref[...] = jnp.dot(y, wco_ref[...], preferred_element_type=F32)

    @pl.when(i < n_prompt_tiles)
    def _():
        @pl.when(i % tiles_per_seq == 0)
        def _():
            carry_ref[...] = jnp.zeros_like(carry_ref)

        row = lax.broadcasted_iota(jnp.int32, (tm, 1), 0)
        c1 = carry_ref[SUBLANES - 1:SUBLANES, :]
        c2 = carry_ref[SUBLANES - 2:SUBLANES - 1, :]
        p1 = jnp.where(row == 0, c1, pltpu.roll(u, 1, 0))
        p2 = jnp.where(row == 0, c2, jnp.where(row == 1, c1, pltpu.roll(u, 2, 0)))
        carry_ref[...] = u[tm - SUBLANES:tm]
        finish(p1, p2)

    @pl.when(i >= n_prompt_tiles)
    def _():
        p1 = jnp.concatenate([sc_ref[DEC_BATCH:2 * DEC_BATCH, :], u[:tm - DEC_BATCH]], axis=0)
        p2 = jnp.concatenate([sc_ref[...], u[:tm - 2 * DEC_BATCH]], axis=0)
        us_ref[...] = u[tm - 2 * DEC_BATCH:tm]
        finish(p1, p2)


def _conv_branch(z_conv, conv_w, sc, wco):
    n_tiles = N_TOK // TM
    body = functools.partial(_conv_body, n_prompt_tiles=N_PROMPT // TM, tiles_per_seq=SEQ // TM)
    return pl.pallas_call(
        body,
        grid=(n_tiles,),
        in_specs=[pl.BlockSpec((TM, 3 * CONV_WIDTH), lambda i: (i, 0)),
                  pl.BlockSpec((3, CONV_WIDTH), lambda i: (0, 0)),
                  pl.BlockSpec((2 * DEC_BATCH, CONV_WIDTH), lambda i: (0, 0)),
                  _resident((CONV_WIDTH, D_MODEL))],
        out_specs=[pl.BlockSpec((TM, D_MODEL), lambda i: (i, 0)),
                   pl.BlockSpec((SUBLANES, CONV_WIDTH), lambda i: (i, 0)),
                   pl.BlockSpec((2 * DEC_BATCH, CONV_WIDTH), lambda i: (0, 0))],
        out_shape=[jax.ShapeDtypeStruct((N_TOK, D_MODEL), F32),
                   jax.ShapeDtypeStruct((n_tiles * SUBLANES, CONV_WIDTH), F32),
                   jax.ShapeDtypeStruct((2 * DEC_BATCH, CONV_WIDTH), F32)],
        scratch_shapes=[pltpu.VMEM((SUBLANES, CONV_WIDTH), F32)],
        compiler_params=_cparams("arbitrary", vmem_mb=48),
        name="conv_branch",
    )(z_conv, conv_w, sc, wco)


def _lora_body(zl_ref, w2t_ref, a2t_ref, w2_ref, a2_ref, g2_ref,
               g_ref, wlc_ref, alc_ref, wls_ref, als_ref, zt, *, n_prompt_steps):
    s = pl.program_id(0)
    tw = jnp.tanh(zl_ref[:, 0:LANES]).astype(BF16)
    xa = zl_ref[:, LANES:2 * LANES].astype(BF16)
    xg = zl_ref[:, 2 * LANES:LORA_W]
    g_ref[...] = jnp.dot(_sigmoid(xg).astype(BF16), g2_ref[...], preferred_element_type=F32)

    @pl.when(s < n_prompt_steps)
    def _():
        b = s % CHAIN_B
        zt[0, b, 0] = lax.dot_general(w2t_ref[...], tw, _NT, preferred_element_type=F32)
        zt[1, b, 0] = lax.dot_general(a2t_ref[...], xa, _NT, preferred_element_type=F32)

        @pl.when(b == CHAIN_B - 1)
        def _():
            _store_chain_ct(zt.at[0], wlc_ref, 1)
            _store_chain_ct(zt.at[1], alc_ref, 1)

    @pl.when(s >= n_prompt_steps)
    def _():
        wls_ref[...] = jnp.dot(tw, w2_ref[...], preferred_element_type=F32)
        als_ref[...] = jnp.dot(xa, a2_ref[...], preferred_element_type=F32)


def _lora(z_tail, w2t, a2t, w2p, a2p, g2b):
    n_tb = SEQ // TM_L
    nps = n_tb * CHAIN_B
    lora_blk = 2 * D_MODEL // LORA_W

    def row_blk(s):
        return jnp.where(s < nps, (s % CHAIN_B) * n_tb + s // CHAIN_B, s)

    chain_spec = pl.BlockSpec((HEAD_SIZE, TM_L, LANES), lambda s: (0, jnp.minimum(s // CHAIN_B, n_tb - 1), 0))
    samp_spec = pl.BlockSpec((TM_L, RWKV_WIDTH), lambda s: (jnp.maximum(s - nps, 0), 0))
    chain_shape = jax.ShapeDtypeStruct((HEAD_SIZE, SEQ, LANES), F32)
    samp_shape = jax.ShapeDtypeStruct((N_SAMPLE, RWKV_WIDTH), F32)
    return pl.pallas_call(
        functools.partial(_lora_body, n_prompt_steps=nps),
        grid=(N_TOK // TM_L,),
        in_specs=[pl.BlockSpec((TM_L, LORA_W), lambda s: (row_blk(s), lora_blk)),
                  _resident((RWKV_WIDTH, LANES)), _resident((RWKV_WIDTH, LANES)),
                  _resident((LANES, RWKV_WIDTH)), _resident((LANES, RWKV_WIDTH)),
                  _resident((GATE_LORA, RWKV_WIDTH))],
        out_specs=[pl.BlockSpec((TM_L, RWKV_WIDTH), lambda s: (row_blk(s), 0)),
                   chain_spec, chain_spec, samp_spec, samp_spec],
        out_shape=[jax.ShapeDtypeStruct((N_TOK, RWKV_WIDTH), F32),
                   chain_shape, chain_shape, samp_shape, samp_shape],
        scratch_shapes=[pltpu.VMEM((2, CHAIN_B, 1, RWKV_WIDTH, LANES), F32)],
        compiler_params=_cparams("arbitrary", vmem_mb=48),
        name="lora",
    )(z_tail, w2t, a2t, w2p, a2p, g2b)


def _scan_body(r_ref, k_ref, wl_ref, al_ref, v_ref, kk_ref, ka_ref, rk_ref, mur_ref, muk_ref, w0_ref, a0_ref,
               muv_ref, lw_ref, lb_ref, s0_ref,
               y_ref, s_ref, vec_ref, bon_ref, prevb_ref, prevv_ref, *, n_batches, steps):
    @pl.when(pl.program_id(1) == 0)
    def _():
        s_ref[...] = s0_ref[...]
        prevb_ref[...] = jnp.zeros_like(prevb_ref)
        prevv_ref[...] = jnp.zeros_like(prevv_ref)

    rows = HEAD_SIZE * SCAN_TB
    first_t = lax.broadcasted_iota(jnp.int32, (rows, 1), 0) % SCAN_TB == 0

    def batch(bi, carry):
        t0 = pl.multiple_of(bi * SCAN_TB, SCAN_TB)

        def load(ref):
            return ref[:, pl.ds(t0, SCAN_TB), :].reshape(rows, LANES)

        def cube(x):
            return x.reshape(HEAD_SIZE, SCAN_TB, LANES)

        def shifted(x, slot, mu_ref):
            prev = jnp.where(first_t, pltpu.roll(prevb_ref[slot], rows - (SCAN_TB - 1), 0), pltpu.roll(x, 1, 0))
            prevb_ref[slot] = x
            return x + mu_ref[...] * (prev - x)

        r = shifted(load(r_ref), 0, mur_ref)
        k = shifted(load(k_ref), 1, muk_ref)
        decay = jnp.exp(-EXP_M05 * _sigmoid(load(wl_ref) + w0_ref[...]))
        a = _sigmoid(load(al_ref) + a0_ref[...])
        kk = cube(k * kk_ref[...])
        nrm = jnp.sqrt(jnp.sum(kk * kk, axis=0))
        kk = kk / jnp.maximum(nrm, 1e-12)[None]
        kf = k * (1.0 + (a - 1.0) * ka_ref[...])
        vec_ref[0] = -kk
        vec_ref[1] = cube(decay)
        vec_ref[2] = kk * cube(a)
        vec_ref[3] = cube(kf)
        vec_ref[4] = cube(r)
        bon_ref[...] = jnp.sum(cube(r * kf * rk_ref[...]), axis=0)

        def step(tl, c):
            t = t0 + tl
            v_raw = v_ref[t]
            v = v_raw + muv_ref[...] * (prevv_ref[...] - v_raw)
            prevv_ref[...] = v_raw

            def row(j, kx):
                return vec_ref[j, kx, pl.ds(tl, HEAD_SIZE, stride=0), :]

            parts = [jnp.zeros((HEAD_SIZE, LANES), F32) for _ in range(4)]
            for kx in range(HEAD_SIZE):
                parts[kx % 4] = parts[kx % 4] + s_ref[kx] * row(0, kx)
            sa = (parts[0] + parts[1]) + (parts[2] + parts[3])

            parts = [jnp.zeros((HEAD_SIZE, LANES), F32) for _ in range(4)]
            for kx in range(HEAD_SIZE):
                sn = s_ref[kx] * row(1, kx) + sa * row(2, kx) + v * row(3, kx)
                s_ref[kx] = sn
                parts[kx % 4] = parts[kx % 4] + sn * row(4, kx)
            o = (parts[0] + parts[1]) + (parts[2] + parts[3])

            mu = jnp.mean(o, axis=0, keepdims=True)
            dlt = o - mu
            var = jnp.mean(dlt * dlt, axis=0, keepdims=True)
            on = dlt * lax.rsqrt(var + GN_EPS) * lw_ref[...] + lb_ref[...]
            y_ref[t] = on + bon_ref[pl.ds(tl, HEAD_SIZE, stride=0), :] * v
            return c

        for tl in range(steps):
            step(tl, 0)
        return carry

    lax.fori_loop(0, n_batches, batch, 0)


def _scan(ct_seqs, v_seq, params8, params, s0, *, n_batches, steps, name):
    g = s0.shape[0]
    tt = n_batches * SCAN_TB
    ttv = v_seq[0].shape[1] if n_batches == 1 else n_batches * steps
    n_ti = ct_seqs[0][0].shape[2] // tt

    def spec(block, lead, tpos):
        def index(gi, ti):
            idx = [gi if lead is None else lead, 0, 0, 0]
            idx[tpos] = ti
            return tuple(idx)
        return pl.BlockSpec(block, index)

    ct_block = (None, HEAD_SIZE, tt, LANES)
    tv_block = (None, ttv, HEAD_SIZE, LANES)
    par8_spec = pl.BlockSpec((HEAD_SIZE * SCAN_TB, LANES), lambda gi, ti: (0, 0))
    par_spec = pl.BlockSpec((HEAD_SIZE, LANES), lambda gi, ti: (0, 0))
    st_spec = pl.BlockSpec((None, HEAD_SIZE, HEAD_SIZE, LANES), lambda gi, ti: (gi, 0, 0, 0))
    return pl.pallas_call(
        functools.partial(_scan_body, n_batches=n_batches, steps=steps),
        grid=(g, n_ti),
        in_specs=([spec(ct_block, lead, 2) for _, lead in ct_seqs] + [spec(tv_block, v_seq[1], 1)]
                  + [par8_spec] * len(params8) + [par_spec] * len(params) + [st_spec]),
        out_specs=[spec(tv_block, None, 1), st_spec],
        out_shape=[jax.ShapeDtypeStruct((g, n_ti * ttv, HEAD_SIZE, LANES), F32),
                   jax.ShapeDtypeStruct((g, HEAD_SIZE, HEAD_SIZE, LANES), F32)],
        scratch_shapes=[pltpu.VMEM((5, HEAD_SIZE, SCAN_TB, LANES), F32), pltpu.VMEM((SCAN_TB, LANES), F32),
                        pltpu.VMEM((2, HEAD_SIZE * SCAN_TB, LANES), F32), pltpu.VMEM((HEAD_SIZE, LANES), F32)],
        compiler_params=_cparams("arbitrary", "arbitrary", vmem_mb=48),
        name=name,
    )(*[a for a, _ in ct_seqs], v_seq[0], *params8, *params, s0)


def _sample_groups(x):
    return x.reshape(DEC_SEQ, DEC_BATCH // CHAIN_B, CHAIN_B, N_HEADS, HEAD_SIZE).transpose(1, 0, 2, 3, 4)


def _to_chain_sample_ct(x):
    x = _sample_groups(x).transpose(0, 4, 1, 2, 3).reshape(DEC_BATCH // CHAIN_B, HEAD_SIZE, DEC_SEQ, LANES)
    return jnp.pad(x, ((0, 0), (0, 0), (0, SCAN_TB - DEC_SEQ), (0, 0)))


def _to_chain_sample_tc(x):
    return _sample_groups(x).transpose(0, 1, 4, 2, 3).reshape(DEC_BATCH // CHAIN_B, DEC_SEQ, HEAD_SIZE, LANES)


def _from_chain_sample(y):
    ng = DEC_BATCH // CHAIN_B
    y = y.reshape(ng, DEC_SEQ, HEAD_SIZE, CHAIN_B, N_HEADS).transpose(1, 0, 3, 2, 4)
    return y.reshape(N_SAMPLE, RWKV_WIDTH)


def _param_chain(p):
    return jnp.tile(p.reshape(N_HEADS, HEAD_SIZE).T, (1, CHAIN_B))


def _param_chain8(p):
    return jnp.repeat(_param_chain(p), SCAN_TB, axis=0)


def _head_minor(w, axis):
    shape = w.shape
    w = w.reshape(shape[:axis] + (N_HEADS, HEAD_SIZE) + shape[axis + 1:])
    return jnp.swapaxes(w, axis, axis + 1).reshape(shape)


def _rwkv_out_body(yc_ref, ys_ref, g_ref, w_ref, o_ref, yt, *, n_prompt_steps):
    s = pl.program_id(0)

    def finish(y):
        o_ref[...] = jnp.dot((y * g_ref[...]).astype(BF16), w_ref[...], preferred_element_type=F32)

    @pl.when(s < n_prompt_steps)
    def _():
        b = s % CHAIN_B

        @pl.when(b == 0)
        def _():
            _load_chain_tc(yc_ref, yt, TM_T // LANES)

        finish(jnp.concatenate([yt[b, half].T for half in range(TM_T // LANES)], axis=0))

    @pl.when(s >= n_prompt_steps)
    def _():
        finish(ys_ref[...])


def _rwkv_out(y_chain, y_s, g, w):
    n_tb = SEQ // TM_T
    nps = n_tb * CHAIN_B

    def row_blk(s):
        return jnp.where(s < nps, (s % CHAIN_B) * n_tb + s // CHAIN_B, s)

    return pl.pallas_call(
        functools.partial(_rwkv_out_body, n_prompt_steps=nps),
        grid=(N_TOK // TM_T,),
        in_specs=[pl.BlockSpec((TM_T * HEAD_SIZE, LANES), lambda s: (jnp.minimum(s // CHAIN_B, n_tb - 1), 0)),
                  pl.BlockSpec((TM_T, RWKV_WIDTH), lambda s: (jnp.maximum(s - nps, 0), 0)),
                  pl.BlockSpec((TM_T, RWKV_WIDTH), lambda s: (row_blk(s), 0)),
                  _resident((RWKV_WIDTH, D_MODEL))],
        out_specs=pl.BlockSpec((TM_T, D_MODEL), lambda s: (row_blk(s), 0)),
        out_shape=jax.ShapeDtypeStruct((N_TOK, D_MODEL), F32),
        scratch_shapes=[pltpu.VMEM((CHAIN_B, TM_T // LANES, RWKV_WIDTH, LANES), F32)],
        compiler_params=_cparams("arbitrary", vmem_mb=56),
        name="rwkv_out",
    )(y_chain, y_s, g, w)


def _mix_body(co_ref, ro_ref, gc_ref, gr_ref, xp_ref, xs_ref, wm_ref, nf_ref, wr_ref, br_ref,
              h_ref, hn_ref, ridx_ref, rw_ref, *, n_prompt_tiles):
    i = pl.program_id(0)
    mixed = _sigmoid(gc_ref[...]) * co_ref[...] + _sigmoid(gr_ref[...]) * ro_ref[...]
    mo = jnp.dot(mixed.astype(BF16), wm_ref[...], preferred_element_type=F32)

    def finish(x):
        h = x + mo
        h_ref[...] = h
        ms = jnp.mean(h * h, axis=-1, keepdims=True)
        hn = h * lax.rsqrt(ms + RMS_EPS) * nf_ref[...]
        hn_ref[...] = hn.reshape(hn.shape[0], D_MODEL // LANES, LANES)
        logits = jnp.dot(hn, wr_ref[...], preferred_element_type=F32,
                         precision=lax.Precision.HIGHEST) + br_ref[...]
        tm = logits.shape[0]
        lane = lax.broadcasted_iota(jnp.int32, (tm, LANES), 1)
        neg = jnp.float32(-jnp.inf)
        gl = jnp.where(lane < N_GROUPS, logits, neg)
        gmax = jnp.max(gl, axis=-1, keepdims=True)
        g_idx = jnp.min(jnp.where(gl == gmax, lane, LANES), axis=-1, keepdims=True)
        g_w = 1.0 / jnp.sum(jnp.exp(gl - gmax), axis=-1, keepdims=True)
        lo = N_GROUPS + g_idx * EXPERTS_PER_GROUP
        el = jnp.where(jnp.logical_and(lane >= lo, lane < lo + EXPERTS_PER_GROUP), logits, neg)
        m1 = jnp.max(el, axis=-1, keepdims=True)
        i1 = jnp.min(jnp.where(el == m1, lane, LANES), axis=-1, keepdims=True)
        el2 = jnp.where(lane == i1, neg, el)
        m2 = jnp.max(el2, axis=-1, keepdims=True)
        i2 = jnp.min(jnp.where(el2 == m2, lane, LANES), axis=-1, keepdims=True)
        t2 = jnp.exp(m2 - m1)
        den = 1.0 + t2
        ridx_ref[...] = jnp.where(lane == 0, i1 - N_GROUPS, jnp.where(lane == 1, i2 - N_GROUPS, 0))
        rw_ref[...] = jnp.where(lane == 0, (1.0 / den) * g_w, jnp.where(lane == 1, (t2 / den) * g_w, 0.0))

    @pl.when(i < n_prompt_tiles)
    def _():
        finish(xp_ref[...])

    @pl.when(i >= n_prompt_tiles)
    def _():
        finish(xs_ref[...])


def _mix_route(conv_out, rwkv_out, z_tail, xp, xs, wm, nf, wr, br):
    npt = N_PROMPT // TM_S
    tok_spec = pl.BlockSpec((TM_S, D_MODEL), lambda i: (i, 0))
    small_spec = pl.BlockSpec((TM_S, LANES), lambda i: (i, 0))
    return pl.pallas_call(
        functools.partial(_mix_body, n_prompt_tiles=npt),
        grid=(N_TOK // TM_S,),
        in_specs=[tok_spec, tok_spec,
                  pl.BlockSpec((TM_S, D_MODEL), lambda i: (i, 0)),
                  pl.BlockSpec((TM_S, D_MODEL), lambda i: (i, 1)),
                  pl.BlockSpec((TM_S, D_MODEL), lambda i: (jnp.minimum(i, npt - 1), 0)),
                  pl.BlockSpec((TM_S, D_MODEL), lambda i: (jnp.maximum(i - npt, 0), 0)),
                  _resident((D_MODEL, D_MODEL)),
                  pl.BlockSpec((1, D_MODEL), lambda i: (0, 0)),
                  _resident((D_MODEL, LANES)),
                  pl.BlockSpec((1, LANES), lambda i: (0, 0))],
        out_specs=[tok_spec, pl.BlockSpec((TM_S, D_MODEL // LANES, LANES), lambda i: (i, 0, 0)),
                   small_spec, small_spec],
        out_shape=[jax.ShapeDtypeStruct((N_TOK, D_MODEL), F32),
                   jax.ShapeDtypeStruct((N_TOK, D_MODEL // LANES, LANES), F32),
                   jax.ShapeDtypeStruct((N_TOK, LANES), jnp.int32),
                   jax.ShapeDtypeStruct((N_TOK, LANES), F32)],
        compiler_params=_cparams("arbitrary", vmem_mb=56),
        name="mix_route",
    )(conv_out, rwkv_out, z_tail, z_tail, xp, xs, wm, nf, wr, br)


def _expert_body(blk_e_ref, slot_tok_ref, nused_ref, first_ref, par_ref, next_e_ref,
                 hn_ref, wg_hbm, wu_hbm, wd_hbm, yb_ref,
                 xbuf, sem, wfg, wfu, wfd, wsem, wgb, wub, wdb):
    i = pl.program_id(0)
    nused = nused_ref[0]
    slot = i % N_XBUF

    def row_copy(blk, r, s):
        tok = slot_tok_ref[blk * MOE_BLOCK + r]
        return pltpu.make_async_copy(hn_ref.at[tok], xbuf.at[s, r], sem.at[s])

    def issue(blk, s):
        for r in range(MOE_BLOCK):
            row_copy(blk, r, s).start(priority=r % 2)

    def w_copies(e, s):
        return (pltpu.make_async_copy(wg_hbm.at[e], wfg.at[s], wsem.at[s]),
                pltpu.make_async_copy(wu_hbm.at[e], wfu.at[s], wsem.at[s]),
                pltpu.make_async_copy(wd_hbm.at[e], wfd.at[s], wsem.at[s]))

    @pl.when(jnp.logical_and(i == 0, nused > 0))
    def _():
        for c in w_copies(blk_e_ref[0], 0):
            c.start(priority=1)
        for a in range(GATHER_AHEAD):
            @pl.when(a < nused)
            def _():
                issue(a, a)

    @pl.when(i + GATHER_AHEAD < nused)
    def _():
        issue(i + GATHER_AHEAD, (i + GATHER_AHEAD) % N_XBUF)

    @pl.when(i < nused)
    def _():
        @pl.when(first_ref[i] == 1)
        def _():
            ws = par_ref[i]
            for c in w_copies(blk_e_ref[i], ws):
                c.wait()

            @pl.when(next_e_ref[i] >= 0)
            def _():
                for c in w_copies(next_e_ref[i], 1 - ws):
                    c.start(priority=1)

            wgb[...] = wfg[ws].astype(BF16)
            wub[...] = wfu[ws].astype(BF16)
            wdb[...] = wfd[ws].astype(BF16)

        for r in range(MOE_BLOCK):
            row_copy(i, r, slot).wait()
        xe = xbuf[slot].reshape(MOE_BLOCK, D_MODEL).astype(BF16)
        gate = jnp.dot(xe, wgb[...], preferred_element_type=F32)
        up = jnp.dot(xe, wub[...], preferred_element_type=F32)
        hdn = (gate * _sigmoid(gate)) * up
        yb = jnp.dot(hdn.astype(BF16), wdb[...], preferred_element_type=F32)
        yb_ref[...] = yb.reshape(MOE_BLOCK, D_MODEL // LANES, LANES)

    @pl.when(i >= nused)
    def _():
        yb_ref[...] = jnp.zeros_like(yb_ref)


def _experts(plan, hn, wg, wu, wd):
    blk_e, slot_tok, nused, first, par, next_e = plan
    n_blocks = blk_e.shape[0]
    any_spec = pl.BlockSpec(memory_space=pl.ANY)
    return pl.pallas_call(
        _expert_body,
        grid_spec=pltpu.PrefetchScalarGridSpec(
            num_scalar_prefetch=6,
            grid=(n_blocks,),
            in_specs=[any_spec, any_spec, any_spec, any_spec],
            out_specs=pl.BlockSpec((MOE_BLOCK, D_MODEL // LANES, LANES), lambda i, *_: (i, 0, 0)),
            scratch_shapes=[pltpu.VMEM((N_XBUF, MOE_BLOCK, D_MODEL // LANES, LANES), F32),
                            pltpu.SemaphoreType.DMA((N_XBUF,)),
                            pltpu.VMEM((2, D_MODEL, D_EXPERT), F32),
                            pltpu.VMEM((2, D_MODEL, D_EXPERT), F32),
                            pltpu.VMEM((2, D_EXPERT, D_MODEL), F32),
                            pltpu.SemaphoreType.DMA((2,)),
                            pltpu.VMEM((D_MODEL, D_EXPERT), BF16),
                            pltpu.VMEM((D_MODEL, D_EXPERT), BF16),
                            pltpu.VMEM((D_EXPERT, D_MODEL), BF16)]),
        out_shape=jax.ShapeDtypeStruct((n_blocks * MOE_BLOCK, D_MODEL // LANES, LANES), F32),
        compiler_params=_cparams("arbitrary", vmem_mb=48),
        name="experts",
    )(blk_e, slot_tok, nused, first, par, next_e, hn, wg, wu, wd)


def _combine_body(dest_ref, yb_ref, h_ref, rw_ref, p_ref, wpg_ref, wpp_ref, nf_ref, yp_ref, ys_ref,
                  ybuf, sem):
    i = pl.program_id(0)
    tm = h_ref.shape[0]
    slot = i % 2

    def row_copy(tile, r, s, sl):
        d = dest_ref[(tile * tm + r) * TOP_K + s]
        return pltpu.make_async_copy(yb_ref.at[d], ybuf.at[sl, s, r], sem.at[sl])

    def issue(tile, sl):
        for r in range(tm):
            row_copy(tile, r, 0, sl).start()
            row_copy(tile, r, 1, sl).start()

    @pl.when(i == 0)
    def _():
        issue(0, 0)

    @pl.when(i + 1 < pl.num_programs(0))
    def _():
        issue(i + 1, 1 - slot)

    for r in range(tm):
        row_copy(i, r, 0, slot).wait()
        row_copy(i, r, 1, slot).wait()

    rw = rw_ref[...]
    y0 = ybuf[slot, 0].reshape(tm, D_MODEL)
    y1 = ybuf[slot, 1].reshape(tm, D_MODEL)
    h2 = h_ref[...] + (y0 * rw[:, 0:1] + y1 * rw[:, 1:2])
    gate = _sigmoid(jnp.dot(h2.astype(BF16), wpg_ref[...], preferred_element_type=F32))
    pp = jnp.dot(p_ref[...].astype(BF16), wpp_ref[...], preferred_element_type=F32)
    h3 = h2 + gate * pp
    ms = jnp.mean(h3 * h3, axis=-1, keepdims=True)
    y = h3 * lax.rsqrt(ms + RMS_EPS) * nf_ref[...]

    @pl.when(i < N_PROMPT // TM_S)
    def _():
        yp_ref[...] = y

    @pl.when(i >= N_PROMPT // TM_S)
    def _():
        ys_ref[...] = y


def _combine(dest, yb, h, rw, p_all, wpg, wpp, nf):
    npt = N_PROMPT // TM_S
    return pl.pallas_call(
        _combine_body,
        grid_spec=pltpu.PrefetchScalarGridSpec(
            num_scalar_prefetch=1,
            grid=(N_TOK // TM_S,),
            in_specs=[pl.BlockSpec(memory_space=pl.ANY),
                      pl.BlockSpec((TM_S, D_MODEL), lambda i, d: (i, 0)),
                      pl.BlockSpec((TM_S, LANES), lambda i, d: (i, 0)),
                      pl.BlockSpec((TM_S, PLE_DIM), lambda i, d: (i, 0)),
                      pl.BlockSpec((D_MODEL, D_MODEL), lambda i, d: (0, 0), pipeline_mode=pl.Buffered(1)),
                      pl.BlockSpec((PLE_DIM, D_MODEL), lambda i, d: (0, 0), pipeline_mode=pl.Buffered(1)),
                      pl.BlockSpec((1, D_MODEL), lambda i, d: (0, 0))],
            out_specs=[pl.BlockSpec((TM_S, D_MODEL), lambda i, d: (jnp.minimum(i, npt - 1), 0)),
                       pl.BlockSpec((TM_S, D_MODEL), lambda i, d: (jnp.maximum(i - npt, 0), 0))],
            scratch_shapes=[pltpu.VMEM((2, TOP_K, TM_S, D_MODEL // LANES, LANES), F32),
                            pltpu.SemaphoreType.DMA((2,))]),
        out_shape=[jax.ShapeDtypeStruct((N_PROMPT, D_MODEL), F32),
                   jax.ShapeDtypeStruct((N_SAMPLE, D_MODEL), F32)],
        compiler_params=_cparams("arbitrary", vmem_mb=48),
        name="combine_ple",
    )(dest, yb, h, rw, p_all, wpg, wpp, nf)


def _dispatch_plan(eidx):
    n_assign = N_TOK * TOP_K
    e_flat = eidx.reshape(n_assign)
    onehot = (e_flat[:, None] == jnp.arange(N_EXPERTS, dtype=jnp.int32)[None, :]).astype(jnp.int32)
    csum = jnp.cumsum(onehot, axis=0)
    counts = csum[-1]
    rank = jnp.sum(csum * onehot, axis=1) - 1
    padded = (counts + MOE_BLOCK - 1) // MOE_BLOCK * MOE_BLOCK
    pad_end = jnp.cumsum(padded)
    pad_start = pad_end - padded
    dest = pad_start[e_flat] + rank
    n_blocks = -(-n_assign // MOE_BLOCK) + N_EXPERTS
    tok = jnp.arange(n_assign, dtype=jnp.int32) // TOP_K
    pad_tok = jnp.arange(n_blocks * MOE_BLOCK, dtype=jnp.int32) % N_TOK
    slot_tok = pad_tok.at[dest].set(tok)
    blk = jnp.arange(n_blocks, dtype=jnp.int32)
    blk_e = jnp.minimum(jnp.searchsorted(pad_end, blk * MOE_BLOCK, side="right"), N_EXPERTS - 1).astype(jnp.int32)
    nused = (pad_end[-1] // MOE_BLOCK).astype(jnp.int32)
    prev_e = jnp.concatenate([jnp.full((1,), -1, jnp.int32), blk_e[:-1]])
    first = jnp.logical_and(blk < nused, blk_e != prev_e)
    par = ((jnp.cumsum(first.astype(jnp.int32)) - 1) % 2).astype(jnp.int32)
    idx_first = jnp.where(first, blk, n_blocks)
    later = jnp.concatenate([lax.cummin(idx_first[::-1])[::-1][1:], jnp.full((1,), n_blocks, jnp.int32)])
    next_e = jnp.where(later < n_blocks, blk_e[jnp.minimum(later, n_blocks - 1)], -1).astype(jnp.int32)
    plan = (blk_e, slot_tok, nused.reshape(1), first.astype(jnp.int32), par, next_e)
    return dest.astype(jnp.int32), plan


def kernel(x_prompt, x_sample, state_conv, state_shift, state_wkv, p_prompt, p_sample, norm_mix, w_in, conv_w, w_conv_out, shift_mu, w0, w2, a0, a2, g2, k_k, k_a, r_k, lnx_w, lnx_b, w_rwkv_out, w_mix_out, norm_ffn, w_route_group, b_route_group, w_route_expert, b_route_expert, w_exp_gate, w_exp_up, w_exp_down, w_ple_proj, w_ple_gate, norm_final):
    c3 = 3 * CONV_WIDTH
    rw3 = 3 * RWKV_WIDTH
    n_tiles = N_TOK // TM
    xp = x_prompt.reshape(N_PROMPT, D_MODEL)
    xs = x_sample.transpose(1, 0, 2).reshape(N_SAMPLE, D_MODEL)
    win = w_in[0]
    mu = shift_mu[0]
    st = state_shift[0]

    def lora_cols(a, rows):
        z32 = jnp.zeros((rows, LANES - DECAY_LORA), a.dtype)
        return jnp.concatenate([a[:, rw3:rw3 + DECAY_LORA], z32,
                                a[:, rw3 + DECAY_LORA:rw3 + DECAY_LORA + A_LORA], z32,
                                a[:, rw3 + DECAY_LORA + A_LORA:]], axis=1)

    xn = _norm_cast(xp, xs, norm_mix)
    z_conv, _, _ = _inproj(xn, win, jnp.zeros((1, c3), F32), jnp.zeros((DEC_BATCH, c3), F32),
                           tn=1024, col_blk_off=0, n_col_blocks=c3 // 1024, shift_lo=0, shift_hi=0,
                           tile_lo=0, n_tiles=n_tiles, name="inproj_conv")
    rkv_s, _, sraw_rkv = _inproj(xn, win, mu[None, :rw3], st[:, :rw3],
                                 tn=1024, col_blk_off=c3 // 1024, n_col_blocks=rw3 // 1024,
                                 shift_lo=0, shift_hi=rw3 // 1024,
                                 tile_lo=N_PROMPT // TM, n_tiles=N_SAMPLE // TM, name="inproj_rkv_sample")
    w_tail = jnp.concatenate([win[:, c3 + RWKV_PROJ:], lora_cols(win[:, c3:c3 + RWKV_PROJ], D_MODEL)], axis=1)
    mu_tail = jnp.concatenate([jnp.zeros((1, 2 * D_MODEL), F32), lora_cols(mu[None, :], 1)], axis=1)
    st_tail = jnp.concatenate([jnp.zeros((DEC_BATCH, 2 * D_MODEL), F32), lora_cols(st, DEC_BATCH)], axis=1)
    nb_tail = 3
    z_tail, last_tail, sraw_tail = _inproj(xn, w_tail, mu_tail, st_tail, tn=N_TAIL // nb_tail, col_blk_off=0,
                                           n_col_blocks=nb_tail, shift_lo=nb_tail - 1, shift_hi=nb_tail,
                                           tile_lo=0, n_tiles=n_tiles, name="inproj_tail")
    w_rk = _head_minor(win[:, c3:c3 + 2 * RWKV_WIDTH].reshape(D_MODEL, 2, RWKV_WIDTH), 2)
    rk_chain = _inproj_t(xn, w_rk.reshape(D_MODEL, 2 * RWKV_WIDTH).T.astype(BF16),
                         time_major=False, name="inproj_t_rk")
    v_chain = _inproj_t(xn, _head_minor(win[:, c3 + 2 * RWKV_WIDTH:c3 + rw3], 1).T.astype(BF16),
                        time_major=True, name="inproj_t_v")

    sc = state_conv[0].transpose(1, 0, 2).reshape(2 * DEC_BATCH, CONV_WIDTH)
    conv_out, ulast, us = _conv_branch(z_conv, conv_w[0], sc, w_conv_out[0].astype(BF16))

    def pad_rows(w):
        return jnp.concatenate([w, jnp.zeros((LANES - w.shape[0], w.shape[1]), w.dtype)], axis=0).astype(BF16)

    w2p, a2p = pad_rows(w2[0]), pad_rows(a2[0])
    g, wl_c, al_c, wl_s, al_s = _lora(z_tail, _head_minor(w2p, 1).T, _head_minor(a2p, 1).T, w2p, a2p,
                                      _head_minor(g2[0], 1).astype(BF16))
    par8 = [_param_chain8(p) for p in (k_k[0], k_a[0], r_k[0].reshape(RWKV_WIDTH))]
    mu8 = [_param_chain8(mu[n * RWKV_WIDTH:(n + 1) * RWKV_WIDTH]) for n in range(2)]
    bias8 = [_param_chain8(w0[0]), _param_chain8(a0[0])]
    zero8 = jnp.zeros((HEAD_SIZE * SCAN_TB, LANES), F32)
    mu_v = _param_chain(mu[2 * RWKV_WIDTH:rw3])
    gn = [_param_chain(lnx_w[0]), _param_chain(lnx_b[0])]
    seqs_p = [(rk_chain, 0), (rk_chain, 1), (wl_c[None], None), (al_c[None], None)]
    v4 = v_chain.reshape(1, SEQ, HEAD_SIZE, LANES)
    s0_p = jnp.zeros((1, HEAD_SIZE, HEAD_SIZE, LANES), F32)
    y_p, sf_p = _scan(seqs_p, (v4, None), par8 + mu8 + bias8, [mu_v] + gn, s0_p,
                      n_batches=SCAN_TT // SCAN_TB, steps=SCAN_TB, name="wkv_scan_prompt")
    ng = DEC_BATCH // CHAIN_B
    s0_s = state_wkv[0].reshape(ng, CHAIN_B, N_HEADS, HEAD_SIZE, HEAD_SIZE).transpose(0, 4, 3, 1, 2)
    s0_s = s0_s.reshape(ng, HEAD_SIZE, HEAD_SIZE, LANES)
    seqs_s = [(_to_chain_sample_ct(rkv_s[:, n * RWKV_WIDTH:(n + 1) * RWKV_WIDTH]), None) for n in range(2)]
    seqs_s += [(_to_chain_sample_ct(wl_s), None), (_to_chain_sample_ct(al_s), None)]
    v_s = _to_chain_sample_tc(rkv_s[:, 2 * RWKV_WIDTH:rw3])
    y_s, sf_s = _scan(seqs_s, (v_s, None), par8 + [zero8, zero8] + bias8,
                      [jnp.zeros((HEAD_SIZE, LANES), F32)] + gn, s0_s,
                      n_batches=1, steps=DEC_SEQ, name="wkv_scan_sample")
    rwkv_out = _rwkv_out(y_p.reshape(SEQ * HEAD_SIZE, LANES), _from_chain_sample(y_s), g,
                         _head_minor(w_rwkv_out[0], 0).astype(BF16))

    wr = jnp.concatenate([w_route_group[0], w_route_expert[0],
                          jnp.zeros((D_MODEL, LANES - N_GROUPS - N_EXPERTS), F32)], axis=1)
    br = jnp.concatenate([b_route_group[0], b_route_expert[0],
                          jnp.zeros((LANES - N_GROUPS - N_EXPERTS,), F32)])[None, :]
    h, hn, ridx, rw = _mix_route(conv_out, rwkv_out, z_tail, xp, xs, w_mix_out[0].astype(BF16),
                                 norm_ffn, wr, br)

    dest, plan = _dispatch_plan(ridx[:, :TOP_K])
    yb = _experts(plan, hn, w_exp_gate[0], w_exp_up[0], w_exp_down[0])
    p_all = jnp.concatenate([p_prompt[0].reshape(N_PROMPT, PLE_DIM),
                             p_sample[0].transpose(1, 0, 2).reshape(N_SAMPLE, PLE_DIM)], axis=0)
    y_p2, y_s2 = _combine(dest, yb, h, rw, p_all, w_ple_gate[0].astype(BF16), w_ple_proj[0].astype(BF16),
                          norm_final[None, :])

    y_prompt = y_p2.reshape(BATCH, SEQ, D_MODEL)
    y_sample = y_s2.reshape(DEC_SEQ, DEC_BATCH, D_MODEL).transpose(1, 0, 2)
    tiles_per_seq = SEQ // TM
    seq_last = jnp.arange(BATCH) * tiles_per_seq + tiles_per_seq - 1

    def unpad_lora(a):
        o = 2 * D_MODEL
        return jnp.concatenate([a[:, o:o + DECAY_LORA], a[:, o + LANES:o + LANES + A_LORA],
                                a[:, o + 2 * LANES:]], axis=1)

    conv_p = ulast.reshape(-1, SUBLANES, CONV_WIDTH)[seq_last, SUBLANES - 2:, :][None]
    conv_s = us.reshape(2, DEC_BATCH, CONV_WIDTH).transpose(1, 0, 2)[None]
    last_rkv = jnp.concatenate([rk_chain[:, :, SEQ - 1, :], v4[:, SEQ - 1]], axis=0)
    lm = last_rkv.reshape(3, HEAD_SIZE, BATCH, N_HEADS).transpose(2, 0, 3, 1).reshape(BATCH, rw3)
    lt = unpad_lora(last_tail.reshape(-1, SUBLANES, N_TAIL)[seq_last, SUBLANES - 1, :])
    shift_p = jnp.concatenate([lm, lt], axis=1)[None]
    shift_s = jnp.concatenate([sraw_rkv, unpad_lora(sraw_tail)], axis=1)[None]
    wkv_p = sf_p.reshape(HEAD_SIZE, HEAD_SIZE, BATCH, N_HEADS).transpose(2, 3, 1, 0)[None]
    wkv_s = sf_s.reshape(ng, HEAD_SIZE, HEAD_SIZE, CHAIN_B, N_HEADS).transpose(0, 3, 4, 2, 1)
    wkv_s = wkv_s.reshape(DEC_BATCH, N_HEADS, HEAD_SIZE, HEAD_SIZE)[None]
    return (y_prompt, y_sample, conv_p, shift_p, wkv_p, conv_s, shift_s, wkv_s)
```

```python
import functools

import jax
import jax.numpy as jnp
from jax import lax
from jax.experimental import pallas as pl
from jax.experimental.pallas import tpu as pltpu

D_MODEL = 2048
BATCH = 4
SEQ = 2048
DEC_BATCH = 128
DEC_SEQ = 4
CONV_WIDTH = 1024
RWKV_WIDTH = 2048
HEAD_SIZE = 64
N_HEADS = RWKV_WIDTH // HEAD_SIZE
DECAY_LORA = 96
A_LORA = 96
GATE_LORA = 256
RWKV_PROJ = 3 * RWKV_WIDTH + DECAY_LORA + A_LORA + GATE_LORA
N_GROUPS = 8
EXPERTS_PER_GROUP = 8
N_EXPERTS = N_GROUPS * EXPERTS_PER_GROUP
TOP_K = 2
D_EXPERT = 512
MOE_BLOCK = 128
PLE_DIM = 256
RMS_EPS = 1e-6
GN_EPS = 64e-5

N_PROMPT = BATCH * SEQ
N_SAMPLE = DEC_BATCH * DEC_SEQ
N_TOK = N_PROMPT + N_SAMPLE
LANES = 128
SUBLANES = 8
TM = 512
TM_S = 256
TM_T = 256
TM_L = 128
CHAIN_B = LANES // N_HEADS
SCAN_TT = 64
SCAN_TB = SUBLANES
GATHER_AHEAD = 3
N_XBUF = GATHER_AHEAD + 1
LORA_W = 512
N_TAIL = 2 * D_MODEL + LORA_W
EXP_M05 = 0.6065306597126334
F32 = jnp.float32
BF16 = jnp.bfloat16
_NT = (((1,), (1,)), ((), ()))


def _sigmoid(x):
    return 1.0 / (1.0 + jnp.exp(-x))


def _cparams(*sem, vmem_mb=None):
    kw = dict(dimension_semantics=sem)
    if vmem_mb is not None:
        kw["vmem_limit_bytes"] = vmem_mb * 1024 * 1024
    return pltpu.CompilerParams(**kw)


def _resident(shape):
    nd = len(shape)
    return pl.BlockSpec(shape, lambda *_: (0,) * nd, pipeline_mode=pl.Buffered(1))


def _chain_tile(zt, half, c):
    return jnp.concatenate([zt[bb, half, c * N_HEADS:(c + 1) * N_HEADS, :] for bb in range(CHAIN_B)], axis=0)


def _store_chain_ct(zt, out_ref, halves):
    for half in range(halves):
        for c in range(HEAD_SIZE):
            out_ref[c, half * LANES:(half + 1) * LANES, :] = _chain_tile(zt, half, c).T


def _store_chain_tc(zt, out_ref, halves):
    for half in range(halves):
        for c in range(HEAD_SIZE):
            out_ref[pl.ds(half * LANES * HEAD_SIZE + c, LANES, stride=HEAD_SIZE), :] = _chain_tile(zt, half, c).T


def _load_chain_tc(y_ref, yt, halves):
    for half in range(halves):
        for v in range(HEAD_SIZE):
            mt = y_ref[pl.ds(half * LANES * HEAD_SIZE + v, LANES, stride=HEAD_SIZE), :].T
            for bb in range(CHAIN_B):
                yt[bb, half, v * N_HEADS:(v + 1) * N_HEADS, :] = mt[bb * N_HEADS:(bb + 1) * N_HEADS, :]


def _norm_body(xp_ref, xs_ref, g_ref, o_ref, *, n_prompt_tiles):
    i = pl.program_id(0)

    def f(x):
        ms = jnp.mean(x * x, axis=-1, keepdims=True)
        return (x * lax.rsqrt(ms + RMS_EPS) * g_ref[...]).astype(o_ref.dtype)

    @pl.when(i < n_prompt_tiles)
    def _():
        o_ref[...] = f(xp_ref[...])

    @pl.when(i >= n_prompt_tiles)
    def _():
        o_ref[...] = f(xs_ref[...])


def _norm_cast(xp, xs, g):
    npt = N_PROMPT // TM
    return pl.pallas_call(
        functools.partial(_norm_body, n_prompt_tiles=npt),
        grid=(N_TOK // TM,),
        in_specs=[pl.BlockSpec((TM, D_MODEL), lambda i: (jnp.minimum(i, npt - 1), 0)),
                  pl.BlockSpec((TM, D_MODEL), lambda i: (jnp.maximum(i - npt, 0), 0)),
                  pl.BlockSpec((1, D_MODEL), lambda i: (0, 0))],
        out_specs=pl.BlockSpec((TM, D_MODEL), lambda i: (i, 0)),
        out_shape=jax.ShapeDtypeStruct((N_TOK, D_MODEL), BF16),
        compiler_params=_cparams("arbitrary"),
        name="norm_cast",
    )(xp, xs, g)


def _inproj_body(xn_ref, w_ref, mu_ref, st_ref, z_ref, last_ref, sraw_ref, wb_ref, carry_ref, *,
                 shift_lo, shift_hi, tile_lo, n_prompt_tiles, tiles_per_seq):
    j = pl.program_id(0)
    i = pl.program_id(1) + tile_lo

    @pl.when(pl.program_id(1) == 0)
    def _():
        wb_ref[...] = w_ref[...].astype(BF16)

    z = jnp.dot(xn_ref[...], wb_ref[...], preferred_element_type=F32)
    tm = z.shape[0]
    last_ref[...] = z[tm - SUBLANES:tm]
    shifted = jnp.logical_and(j >= shift_lo, j < shift_hi)
    is_prompt = i < n_prompt_tiles

    @pl.when(jnp.logical_not(shifted))
    def _():
        z_ref[...] = z

    @pl.when(jnp.logical_and(shifted, is_prompt))
    def _():
        @pl.when(i % tiles_per_seq == 0)
        def _():
            carry_ref[...] = jnp.zeros_like(carry_ref)

        prev = pltpu.roll(z, 1, 0)
        row = lax.broadcasted_iota(jnp.int32, (tm, 1), 0)
        prev = jnp.where(row == 0, carry_ref[SUBLANES - 1:SUBLANES, :], prev)
        z_ref[...] = z + mu_ref[...] * (prev - z)
        carry_ref[...] = z[tm - SUBLANES:tm]

    @pl.when(jnp.logical_and(shifted, jnp.logical_not(is_prompt)))
    def _():
        prev = jnp.concatenate([st_ref[...], z[:tm - DEC_BATCH]], axis=0)
        z_ref[...] = z + mu_ref[...] * (prev - z)

    @pl.when(jnp.logical_not(is_prompt))
    def _():
        sraw_ref[...] = z[tm - DEC_BATCH:tm]


def _inproj(xn, w, mu, st, *, tn, col_blk_off, n_col_blocks, shift_lo, shift_hi, tile_lo, n_tiles, name):
    n_out = tn * n_col_blocks
    body = functools.partial(_inproj_body, shift_lo=shift_lo, shift_hi=shift_hi, tile_lo=tile_lo,
                             n_prompt_tiles=N_PROMPT // TM, tiles_per_seq=SEQ // TM)
    return pl.pallas_call(
        body,
        grid=(n_col_blocks, n_tiles),
        in_specs=[pl.BlockSpec((TM, D_MODEL), lambda j, i: (i + tile_lo, 0)),
                  pl.BlockSpec((D_MODEL, tn), lambda j, i: (0, j + col_blk_off)),
                  pl.BlockSpec((1, tn), lambda j, i: (0, j)),
                  pl.BlockSpec((DEC_BATCH, tn), lambda j, i: (0, j))],
        out_specs=[pl.BlockSpec((TM, tn), lambda j, i: (i, j)),
                   pl.BlockSpec((SUBLANES, tn), lambda j, i: (i, j)),
                   pl.BlockSpec((DEC_BATCH, tn), lambda j, i: (0, j))],
        out_shape=[jax.ShapeDtypeStruct((n_tiles * TM, n_out), F32),
                   jax.ShapeDtypeStruct((n_tiles * SUBLANES, n_out), F32),
                   jax.ShapeDtypeStruct((DEC_BATCH, n_out), F32)],
        scratch_shapes=[pltpu.VMEM((D_MODEL, tn), BF16), pltpu.VMEM((SUBLANES, tn), F32)],
        compiler_params=_cparams("arbitrary", "arbitrary", vmem_mb=48),
        name=name,
    )(xn, w, mu, st)


def _inproj_t_body(xn_ref, wt_ref, o_ref, zt, *, time_major):
    b = pl.program_id(2)
    z = lax.dot_general(wt_ref[...], xn_ref[...], _NT, preferred_element_type=F32)
    for half in range(TM_T // LANES):
        zt[b, half] = z[:, half * LANES:(half + 1) * LANES]

    @pl.when(b == CHAIN_B - 1)
    def _():
        if time_major:
            _store_chain_tc(zt, o_ref, TM_T // LANES)
        else:
            _store_chain_ct(zt, o_ref, TM_T // LANES)


def _inproj_t(xn, wt, *, time_major, name):
    n_tb = SEQ // TM_T
    n = wt.shape[0] // RWKV_WIDTH
    if time_major:
        out_spec = pl.BlockSpec((None, TM_T * HEAD_SIZE, LANES), lambda j, tb, b: (j, tb, 0))
        out_shape = jax.ShapeDtypeStruct((n, SEQ * HEAD_SIZE, LANES), F32)
    else:
        out_spec = pl.BlockSpec((None, HEAD_SIZE, TM_T, LANES), lambda j, tb, b: (j, 0, tb, 0))
        out_shape = jax.ShapeDtypeStruct((n, HEAD_SIZE, SEQ, LANES), F32)
    return pl.pallas_call(
        functools.partial(_inproj_t_body, time_major=time_major),
        grid=(n, n_tb, CHAIN_B),
        in_specs=[pl.BlockSpec((TM_T, D_MODEL), lambda j, tb, b: (b * n_tb + tb, 0)),
                  pl.BlockSpec((RWKV_WIDTH, D_MODEL), lambda j, tb, b: (j, 0))],
        out_specs=out_spec,
        out_shape=out_shape,
        scratch_shapes=[pltpu.VMEM((CHAIN_B, TM_T // LANES, RWKV_WIDTH, LANES), F32)],
        compiler_params=_cparams("arbitrary", "arbitrary", "arbitrary", vmem_mb=56),
        name=name,
    )(xn, wt)


def _conv_body(zc_ref, cw_ref, sc_ref, wco_ref, o_ref, ulast_ref, us_ref, carry_ref, *,
               n_prompt_tiles, tiles_per_seq):
    i = pl.program_id(0)
    gate_b = zc_ref[:, 0:CONV_WIDTH]
    u = zc_ref[:, CONV_WIDTH:2 * CONV_WIDTH] * zc_ref[:, 2 * CONV_WIDTH:3 * CONV_WIDTH]
    tm = u.shape[0]
    ulast_ref[...] = u[tm - SUBLANES:tm]
    w0 = cw_ref[0:1, :]
    w1 = cw_ref[1:2, :]
    w2 = cw_ref[2:3, :]

    def finish(p1, p2):
        conv = w0 * p2 + w1 * p1 + w2 * u
        y = (gate_b * conv).astype(BF16)
        o_ref[...] = jnp.dot(y, wco_ref[...], preferred_element_type=F32)

    @pl.when(i < n_prompt_tiles)
    def _():
        @pl.when(i % tiles_per_seq == 0)
        def _():
            carry_ref[...] = jnp.zeros_like(carry_ref)

        row = lax.broadcasted_iota(jnp.int32, (tm, 1), 0)
        c1 = carry_ref[SUBLANES - 1:SUBLANES, :]
        c2 = carry_ref[SUBLANES - 2:SUBLANES - 1, :]
        p1 = jnp.where(row == 0, c1, pltpu.roll(u, 1, 0))
        p2 = jnp.where(row == 0, c2, jnp.where(row == 1, c1, pltpu.roll(u, 2, 0)))
        carry_ref[...] = u[tm - SUBLANES:tm]
        finish(p1, p2)

    @pl.when(i >= n_prompt_tiles)
    def _():
        p1 = jnp.concatenate([sc_ref[DEC_BATCH:2 * DEC_BATCH, :], u[:tm - DEC_BATCH]], axis=0)
        p2 = jnp.concatenate([sc_ref[...], u[:tm - 2 * DEC_BATCH]], axis=0)
        us_ref[...] = u[tm - 2 * DEC_BATCH:tm]
        finish(p1, p2)


def _conv_branch(z_conv, conv_w, sc, wco):
    n_tiles = N_TOK // TM
    body = functools.partial(_conv_body, n_prompt_tiles=N_PROMPT // TM, tiles_per_seq=SEQ // TM)
    return pl.pallas_call(
        body,
        grid=(n_tiles,),
        in_specs=[pl.BlockSpec((TM, 3 * CONV_WIDTH), lambda i: (i, 0)),
                  pl.BlockSpec((3, CONV_WIDTH), lambda i: (0, 0)),
                  pl.BlockSpec((2 * DEC_BATCH, CONV_WIDTH), lambda i: (0, 0)),
                  _resident((CONV_WIDTH, D_MODEL))],
        out_specs=[pl.BlockSpec((TM, D_MODEL), lambda i: (i, 0)),
                   pl.BlockSpec((SUBLANES, CONV_WIDTH), lambda i: (i, 0)),
                   pl.BlockSpec((2 * DEC_BATCH, CONV_WIDTH), lambda i: (0, 0))],
        out_shape=[jax.ShapeDtypeStruct((N_TOK, D_MODEL), F32),
                   jax.ShapeDtypeStruct((n_tiles * SUBLANES, CONV_WIDTH), F32),
                   jax.ShapeDtypeStruct((2 * DEC_BATCH, CONV_WIDTH), F32)],
        scratch_shapes=[pltpu.VMEM((SUBLANES, CONV_WIDTH), F32)],
        compiler_params=_cparams("arbitrary", vmem_mb=48),
        name="conv_branch",
    )(z_conv, conv_w, sc, wco)


def _lora_body(zl_ref, w2t_ref, a2t_ref, w2_ref, a2_ref, g2_ref,
               g_ref, wlc_ref, alc_ref, wls_ref, als_ref, zt, *, n_prompt_steps):
    s = pl.program_id(0)
    tw = jnp.tanh(zl_ref[:, 0:LANES]).astype(BF16)
    xa = zl_ref[:, LANES:2 * LANES].astype(BF16)
    xg = zl_ref[:, 2 * LANES:LORA_W]
    g_ref[...] = jnp.dot(_sigmoid(xg).astype(BF16), g2_ref[...], preferred_element_type=F32)

    @pl.when(s < n_prompt_steps)
    def _():
        b = s % CHAIN_B
        zt[0, b, 0] = lax.dot_general(w2t_ref[...], tw, _NT, preferred_element_type=F32)
        zt[1, b, 0] = lax.dot_general(a2t_ref[...], xa, _NT, preferred_element_type=F32)

        @pl.when(b == CHAIN_B - 1)
        def _():
            _store_chain_ct(zt.at[0], wlc_ref, 1)
            _store_chain_ct(zt.at[1], alc_ref, 1)

    @pl.when(s >= n_prompt_steps)
    def _():
        wls_ref[...] = jnp.dot(tw, w2_ref[...], preferred_element_type=F32)
        als_ref[...] = jnp.dot(xa, a2_ref[...], preferred_element_type=F32)


def _lora(z_tail, w2t, a2t, w2p, a2p, g2b):
    n_tb = SEQ // TM_L
    nps = n_tb * CHAIN_B
    lora_blk = 2 * D_MODEL // LORA_W

    def row_blk(s):
        return jnp.where(s < nps, (s % CHAIN_B) * n_tb + s // CHAIN_B, s)

    chain_spec = pl.BlockSpec((HEAD_SIZE, TM_L, LANES), lambda s: (0, jnp.minimum(s // CHAIN_B, n_tb - 1), 0))
    samp_spec = pl.BlockSpec((TM_L, RWKV_WIDTH), lambda s: (jnp.maximum(s - nps, 0), 0))
    chain_shape = jax.ShapeDtypeStruct((HEAD_SIZE, SEQ, LANES), F32)
    samp_shape = jax.ShapeDtypeStruct((N_SAMPLE, RWKV_WIDTH), F32)
    return pl.pallas_call(
        functools.partial(_lora_body, n_prompt_steps=nps),
        grid=(N_TOK // TM_L,),
        in_specs=[pl.BlockSpec((TM_L, LORA_W), lambda s: (row_blk(s), lora_blk)),
                  _resident((RWKV_WIDTH, LANES)), _resident((RWKV_WIDTH, LANES)),
                  _resident((LANES, RWKV_WIDTH)), _resident((LANES, RWKV_WIDTH)),
                  _resident((GATE_LORA, RWKV_WIDTH))],
        out_specs=[pl.BlockSpec((TM_L, RWKV_WIDTH), lambda s: (row_blk(s), 0)),
                   chain_spec, chain_spec, samp_spec, samp_spec],
        out_shape=[jax.ShapeDtypeStruct((N_TOK, RWKV_WIDTH), F32),
                   chain_shape, chain_shape, samp_shape, samp_shape],
        scratch_shapes=[pltpu.VMEM((2, CHAIN_B, 1, RWKV_WIDTH, LANES), F32)],
        compiler_params=_cparams("arbitrary", vmem_mb=48),
        name="lora",
    )(z_tail, w2t, a2t, w2p, a2p, g2b)


def _scan_body(r_ref, k_ref, wl_ref, al_ref, v_ref, kk_ref, ka_ref, rk_ref, mur_ref, muk_ref, w0_ref, a0_ref,
               muv_ref, lw_ref, lb_ref, s0_ref,
               y_ref, s_ref, vec_ref, bon_ref, prevb_ref, prevv_ref, *, n_batches, steps):
    @pl.when(pl.program_id(1) == 0)
    def _():
        s_ref[...] = s0_ref[...]
        prevb_ref[...] = jnp.zeros_like(prevb_ref)
        prevv_ref[...] = jnp.zeros_like(prevv_ref)

    rows = HEAD_SIZE * SCAN_TB
    first_t = lax.broadcasted_iota(jnp.int32, (rows, 1), 0) % SCAN_TB == 0

    def batch(bi, carry):
        t0 = pl.multiple_of(bi * SCAN_TB, SCAN_TB)

        def load(ref):
            return ref[:, pl.ds(t0, SCAN_TB), :].reshape(rows, LANES)

        def cube(x):
            return x.reshape(HEAD_SIZE, SCAN_TB, LANES)

        def shifted(x, slot, mu_ref):
            prev = jnp.where(first_t, pltpu.roll(prevb_ref[slot], rows - (SCAN_TB - 1), 0), pltpu.roll(x, 1, 0))
            prevb_ref[slot] = x
            return x + mu_ref[...] * (prev - x)

        r = shifted(load(r_ref), 0, mur_ref)
        k = shifted(load(k_ref), 1, muk_ref)
        decay = jnp.exp(-EXP_M05 * _sigmoid(load(wl_ref) + w0_ref[...]))
        a = _sigmoid(load(al_ref) + a0_ref[...])
        kk = cube(k * kk_ref[...])
        nrm = jnp.sqrt(jnp.sum(kk * kk, axis=0))
        kk = kk / jnp.maximum(nrm, 1e-12)[None]
        kf = k * (1.0 + (a - 1.0) * ka_ref[...])
        vec_ref[0] = -kk
        vec_ref[1] = cube(decay)
        vec_ref[2] = kk * cube(a)
        vec_ref[3] = cube(kf)
        vec_ref[4] = cube(r)
        bon_ref[...] = jnp.sum(cube(r * kf * rk_ref[...]), axis=0)

        def step(tl, c):
            t = t0 + tl
            v_raw = v_ref[t]
            v = v_raw + muv_ref[...] * (prevv_ref[...] - v_raw)
            prevv_ref[...] = v_raw

            def row(j, kx):
                return vec_ref[j, kx, pl.ds(tl, HEAD_SIZE, stride=0), :]

            parts = [jnp.zeros((HEAD_SIZE, LANES), F32) for _ in range(4)]
            for kx in range(HEAD_SIZE):
                parts[kx % 4] = parts[kx % 4] + s_ref[kx] * row(0, kx)
            sa = (parts[0] + parts[1]) + (parts[2] + parts[3])

            parts = [jnp.zeros((HEAD_SIZE, LANES), F32) for _ in range(4)]
            for kx in range(HEAD_SIZE):
                sn = s_ref[kx] * row(1, kx) + sa * row(2, kx) + v * row(3, kx)
                s_ref[kx] = sn
                parts[kx % 4] = parts[kx % 4] + sn * row(4, kx)
            o = (parts[0] + parts[1]) + (parts[2] + parts[3])

            mu = jnp.mean(o, axis=0, keepdims=True)
            dlt = o - mu
            var = jnp.mean(dlt * dlt, axis=0, keepdims=True)
            on = dlt * lax.rsqrt(var + GN_EPS) * lw_ref[...] + lb_ref[...]
            y_ref[t] = on + bon_ref[pl.ds(tl, HEAD_SIZE, stride=0), :] * v
            return c

        for tl in range(steps):
            step(tl, 0)
        return carry

    lax.fori_loop(0, n_batches, batch, 0)


def _scan(ct_seqs, v_seq, params8, params, s0, *, n_batches, steps, name):
    g = s0.shape[0]
    tt = n_batches * SCAN_TB
    ttv = v_seq[0].shape[1] if n_batches == 1 else n_batches * steps
    n_ti = ct_seqs[0][0].shape[2] // tt

    def spec(block, lead, tpos):
        def index(gi, ti):
            idx = [gi if lead is None else lead, 0, 0, 0]
            idx[tpos] = ti
            return tuple(idx)
        return pl.BlockSpec(block, index)

    ct_block = (None, HEAD_SIZE, tt, LANES)
    tv_block = (None, ttv, HEAD_SIZE, LANES)
    par8_spec = pl.BlockSpec((HEAD_SIZE * SCAN_TB, LANES), lambda gi, ti: (0, 0))
    par_spec = pl.BlockSpec((HEAD_SIZE, LANES), lambda gi, ti: (0, 0))
    st_spec = pl.BlockSpec((None, HEAD_SIZE, HEAD_SIZE, LANES), lambda gi, ti: (gi, 0, 0, 0))
    return pl.pallas_call(
        functools.partial(_scan_body, n_batches=n_batches, steps=steps),
        grid=(g, n_ti),
        in_specs=([spec(ct_block, lead, 2) for _, lead in ct_seqs] + [spec(tv_block, v_seq[1], 1)]
                  + [par8_spec] * len(params8) + [par_spec] * len(params) + [st_spec]),
        out_specs=[spec(tv_block, None, 1), st_spec],
        out_shape=[jax.ShapeDtypeStruct((g, n_ti * ttv, HEAD_SIZE, LANES), F32),
                   jax.ShapeDtypeStruct((g, HEAD_SIZE, HEAD_SIZE, LANES), F32)],
        scratch_shapes=[pltpu.VMEM((5, HEAD_SIZE, SCAN_TB, LANES), F32), pltpu.VMEM((SCAN_TB, LANES), F32),
                        pltpu.VMEM((2, HEAD_SIZE * SCAN_TB, LANES), F32), pltpu.VMEM((HEAD_SIZE, LANES), F32)],
        compiler_params=_cparams("arbitrary", "arbitrary", vmem_mb=48),
        name=name,
    )(*[a for a, _ in ct_seqs], v_seq[0], *params8, *params, s0)


def _sample_groups(x):
    return x.reshape(DEC_SEQ, DEC_BATCH // CHAIN_B, CHAIN_B, N_HEADS, HEAD_SIZE).transpose(1, 0, 2, 3, 4)


def _to_chain_sample_ct(x):
    x = _sample_groups(x).transpose(0, 4, 1, 2, 3).reshape(DEC_BATCH // CHAIN_B, HEAD_SIZE, DEC_SEQ, LANES)
    return jnp.pad(x, ((0, 0), (0, 0), (0, SCAN_TB - DEC_SEQ), (0, 0)))


def _to_chain_sample_tc(x):
    return _sample_groups(x).transpose(0, 1, 4, 2, 3).reshape(DEC_BATCH // CHAIN_B, DEC_SEQ, HEAD_SIZE, LANES)


def _from_chain_sample(y):
    ng = DEC_BATCH // CHAIN_B
    y = y.reshape(ng, DEC_SEQ, HEAD_SIZE, CHAIN_B, N_HEADS).transpose(1, 0, 3, 2, 4)
    return y.reshape(N_SAMPLE, RWKV_WIDTH)


def _param_chain(p):
    return jnp.tile(p.reshape(N_HEADS, HEAD_SIZE).T, (1, CHAIN_B))


def _param_chain8(p):
    return jnp.repeat(_param_chain(p), SCAN_TB, axis=0)


def _head_minor(w, axis):
    shape = w.shape
    w = w.reshape(shape[:axis] + (N_HEADS, HEAD_SIZE) + shape[axis + 1:])
    return jnp.swapaxes(w, axis, axis + 1).reshape(shape)


def _rwkv_out_body(yc_ref, ys_ref, g_ref, w_ref, o_ref, yt, *, n_prompt_steps):
    s = pl.program_id(0)

    def finish(y):
        o_ref[...] = jnp.dot((y * g_ref[...]).astype(BF16), w_ref[...], preferred_element_type=F32)

    @pl.when(s < n_prompt_steps)
    def _():
        b = s % CHAIN_B

        @pl.when(b == 0)
        def _():
            _load_chain_tc(yc_ref, yt, TM_T // LANES)

        finish(jnp.concatenate([yt[b, half].T for half in range(TM_T // LANES)], axis=0))

    @pl.when(s >= n_prompt_steps)
    def _():
        finish(ys_ref[...])


def _rwkv_out(y_chain, y_s, g, w):
    n_tb = SEQ // TM_T
    nps = n_tb * CHAIN_B

    def row_blk(s):
        return jnp.where(s < nps, (s % CHAIN_B) * n_tb + s // CHAIN_B, s)

    return pl.pallas_call(
        functools.partial(_rwkv_out_body, n_prompt_steps=nps),
        grid=(N_TOK // TM_T,),
        in_specs=[pl.BlockSpec((TM_T * HEAD_SIZE, LANES), lambda s: (jnp.minimum(s // CHAIN_B, n_tb - 1), 0)),
                  pl.BlockSpec((TM_T, RWKV_WIDTH), lambda s: (jnp.maximum(s - nps, 0), 0)),
                  pl.BlockSpec((TM_T, RWKV_WIDTH), lambda s: (row_blk(s), 0)),
                  _resident((RWKV_WIDTH, D_MODEL))],
        out_specs=pl.BlockSpec((TM_T, D_MODEL), lambda s: (row_blk(s), 0)),
        out_shape=jax.ShapeDtypeStruct((N_TOK, D_MODEL), F32),
        scratch_shapes=[pltpu.VMEM((CHAIN_B, TM_T // LANES, RWKV_WIDTH, LANES), F32)],
        compiler_params=_cparams("arbitrary", vmem_mb=56),
        name="rwkv_out",
    )(y_chain, y_s, g, w)


def _mix_body(co_ref, ro_ref, gc_ref, gr_ref, xp_ref, xs_ref, wm_ref, nf_ref, wr_ref, br_ref,
              h_ref, hn_ref, ridx_ref, rw_ref, *, n_prompt_tiles):
    i = pl.program_id(0)
    mixed = _sigmoid(gc_ref[...]) * co_ref[...] + _sigmoid(gr_ref[...]) * ro_ref[...]
    mo = jnp.dot(mixed.astype(BF16), wm_ref[...], preferred_element_type=F32)

    def finish(x):
        h = x + mo
        h_ref[...] = h
        ms = jnp.mean(h * h, axis=-1, keepdims=True)
        hn = h * lax.rsqrt(ms + RMS_EPS) * nf_ref[...]
        hnb = hn.astype(BF16)
        hn_ref[...] = hnb.reshape(hn.shape[0], D_MODEL // LANES, LANES)
        logits = jnp.dot(hnb, wr_ref[...], preferred_element_type=F32) + br_ref[...]
        tm = logits.shape[0]
        lane = lax.broadcasted_iota(jnp.int32, (tm, LANES), 1)
        neg = jnp.float32(-jnp.inf)
        gl = jnp.where(lane < N_GROUPS, logits, neg)
        gmax = jnp.max(gl, axis=-1, keepdims=True)
        g_idx = jnp.min(jnp.where(gl == gmax, lane, LANES), axis=-1, keepdims=True)
        g_w = 1.0 / jnp.sum(jnp.exp(gl - gmax), axis=-1, keepdims=True)
        lo = N_GROUPS + g_idx * EXPERTS_PER_GROUP
        el = jnp.where(jnp.logical_and(lane >= lo, lane < lo + EXPERTS_PER_GROUP), logits, neg)
        m1 = jnp.max(el, axis=-1, keepdims=True)
        i1 = jnp.min(jnp.where(el == m1, lane, LANES), axis=-1, keepdims=True)
        el2 = jnp.where(lane == i1, neg, el)
        m2 = jnp.max(el2, axis=-1, keepdims=True)
        i2 = jnp.min(jnp.where(el2 == m2, lane, LANES), axis=-1, keepdims=True)
        t2 = jnp.exp(m2 - m1)
        den = 1.0 + t2
        ridx_ref[...] = jnp.where(lane == 0, i1 - N_GROUPS, jnp.where(lane == 1, i2 - N_GROUPS, 0))
        rw_ref[...] = jnp.where(lane == 0, (1.0 / den) * g_w, jnp.where(lane == 1, (t2 / den) * g_w, 0.0))

    @pl.when(i < n_prompt_tiles)
    def _():
        finish(xp_ref[...])

    @pl.when(i >= n_prompt_tiles)
    def _():
        finish(xs_ref[...])


def _mix_route(conv_out, rwkv_out, z_tail, xp, xs, wm, nf, wr, br):
    npt = N_PROMPT // TM_S
    tok_spec = pl.BlockSpec((TM_S, D_MODEL), lambda i: (i, 0))
    small_spec = pl.BlockSpec((TM_S, LANES), lambda i: (i, 0))
    return pl.pallas_call(
        functools.partial(_mix_body, n_prompt_tiles=npt),
        grid=(N_TOK // TM_S,),
        in_specs=[tok_spec, tok_spec,
                  pl.BlockSpec((TM_S, D_MODEL), lambda i: (i, 0)),
                  pl.BlockSpec((TM_S, D_MODEL), lambda i: (i, 1)),
                  pl.BlockSpec((TM_S, D_MODEL), lambda i: (jnp.minimum(i, npt - 1), 0)),
                  pl.BlockSpec((TM_S, D_MODEL), lambda i: (jnp.maximum(i - npt, 0), 0)),
                  _resident((D_MODEL, D_MODEL)),
                  pl.BlockSpec((1, D_MODEL), lambda i: (0, 0)),
                  _resident((D_MODEL, LANES)),
                  pl.BlockSpec((1, LANES), lambda i: (0, 0))],
        out_specs=[tok_spec, pl.BlockSpec((TM_S, D_MODEL // LANES, LANES), lambda i: (i, 0, 0)),
                   small_spec, small_spec],
        out_shape=[jax.ShapeDtypeStruct((N_TOK, D_MODEL), F32),
                   jax.ShapeDtypeStruct((N_TOK, D_MODEL // LANES, LANES), BF16),
                   jax.ShapeDtypeStruct((N_TOK, LANES), jnp.int32),
                   jax.ShapeDtypeStruct((N_TOK, LANES), F32)],
        compiler_params=_cparams("arbitrary", vmem_mb=56),
        name="mix_route",
    )(conv_out, rwkv_out, z_tail, z_tail, xp, xs, wm, nf, wr, br)


def _expert_body(blk_e_ref, slot_tok_ref, nused_ref, first_ref, par_ref, next_e_ref,
                 hn_ref, wg_hbm, wu_hbm, wd_hbm, yb_ref,
                 xbuf, sem, wfg, wfu, wfd, wsem, wgb, wub, wdb):
    i = pl.program_id(0)
    nused = nused_ref[0]
    slot = i % N_XBUF

    def row_copy(blk, r, s):
        tok = slot_tok_ref[blk * MOE_BLOCK + r]
        return pltpu.make_async_copy(hn_ref.at[tok], xbuf.at[s, r], sem.at[s])

    def issue(blk, s):
        for r in range(MOE_BLOCK):
            row_copy(blk, r, s).start(priority=r % 2)

    def w_copies(e, s):
        return (pltpu.make_async_copy(wg_hbm.at[e], wfg.at[s], wsem.at[s]),
                pltpu.make_async_copy(wu_hbm.at[e], wfu.at[s], wsem.at[s]),
                pltpu.make_async_copy(wd_hbm.at[e], wfd.at[s], wsem.at[s]))

    @pl.when(jnp.logical_and(i == 0, nused > 0))
    def _():
        for c in w_copies(blk_e_ref[0], 0):
            c.start(priority=1)
        for a in range(GATHER_AHEAD):
            @pl.when(a < nused)
            def _():
                issue(a, a)

    @pl.when(i + GATHER_AHEAD < nused)
    def _():
        issue(i + GATHER_AHEAD, (i + GATHER_AHEAD) % N_XBUF)

    @pl.when(i < nused)
    def _():
        @pl.when(first_ref[i] == 1)
        def _():
            ws = par_ref[i]
            for c in w_copies(blk_e_ref[i], ws):
                c.wait()

            @pl.when(next_e_ref[i] >= 0)
            def _():
                for c in w_copies(next_e_ref[i], 1 - ws):
                    c.start(priority=1)

            wgb[...] = wfg[ws].astype(BF16)
            wub[...] = wfu[ws].astype(BF16)
            wdb[...] = wfd[ws].astype(BF16)

        for r in range(MOE_BLOCK):
            row_copy(i, r, slot).wait()
        xe = xbuf[slot].reshape(MOE_BLOCK, D_MODEL)
        gate = jnp.dot(xe, wgb[...], preferred_element_type=F32)
        up = jnp.dot(xe, wub[...], preferred_element_type=F32)
        hdn = (gate * _sigmoid(gate)) * up
        yb = jnp.dot(hdn.astype(BF16), wdb[...], preferred_element_type=F32)
        yb_ref[...] = yb.reshape(MOE_BLOCK, D_MODEL // LANES, LANES)

    @pl.when(i >= nused)
    def _():
        yb_ref[...] = jnp.zeros_like(yb_ref)


def _experts(plan, hn, wg, wu, wd):
    blk_e, slot_tok, nused, first, par, next_e = plan
    n_blocks = blk_e.shape[0]
    any_spec = pl.BlockSpec(memory_space=pl.ANY)
    return pl.pallas_call(
        _expert_body,
        grid_spec=pltpu.PrefetchScalarGridSpec(
            num_scalar_prefetch=6,
            grid=(n_blocks,),
            in_specs=[any_spec, any_spec, any_spec, any_spec],
            out_specs=pl.BlockSpec((MOE_BLOCK, D_MODEL // LANES, LANES), lambda i, *_: (i, 0, 0)),
            scratch_shapes=[pltpu.VMEM((N_XBUF, MOE_BLOCK, D_MODEL // LANES, LANES), BF16),
                            pltpu.SemaphoreType.DMA((N_XBUF,)),
                            pltpu.VMEM((2, D_MODEL, D_EXPERT), F32),
                            pltpu.VMEM((2, D_MODEL, D_EXPERT), F32),
                            pltpu.VMEM((2, D_EXPERT, D_MODEL), F32),
                            pltpu.SemaphoreType.DMA((2,)),
                            pltpu.VMEM((D_MODEL, D_EXPERT), BF16),
                            pltpu.VMEM((D_MODEL, D_EXPERT), BF16),
                            pltpu.VMEM((D_EXPERT, D_MODEL), BF16)]),
        out_shape=jax.ShapeDtypeStruct((n_blocks * MOE_BLOCK, D_MODEL // LANES, LANES), F32),
        compiler_params=_cparams("arbitrary", vmem_mb=48),
        name="experts",
    )(blk_e, slot_tok, nused, first, par, next_e, hn, wg, wu, wd)


def _combine_body(dest_ref, yb_ref, h_ref, rw_ref, p_ref, wpg_ref, wpp_ref, nf_ref, yp_ref, ys_ref,
                  ybuf, sem):
    i = pl.program_id(0)
    tm = h_ref.shape[0]
    slot = i % 2

    def row_copy(tile, r, s, sl):
        d = dest_ref[(tile * tm + r) * TOP_K + s]
        return pltpu.make_async_copy(yb_ref.at[d], ybuf.at[sl, s, r], sem.at[sl])

    def issue(tile, sl):
        for r in range(tm):
            row_copy(tile, r, 0, sl).start()
            row_copy(tile, r, 1, sl).start()

    @pl.when(i == 0)
    def _():
        issue(0, 0)

    @pl.when(i + 1 < pl.num_programs(0))
    def _():
        issue(i + 1, 1 - slot)

    for r in range(tm):
        row_copy(i, r, 0, slot).wait()
        row_copy(i, r, 1, slot).wait()

    rw = rw_ref[...]
    y0 = ybuf[slot, 0].reshape(tm, D_MODEL)
    y1 = ybuf[slot, 1].reshape(tm, D_MODEL)
    h2 = h_ref[...] + (y0 * rw[:, 0:1] + y1 * rw[:, 1:2])
    gate = _sigmoid(jnp.dot(h2.astype(BF16), wpg_ref[...], preferred_element_type=F32))
    pp = jnp.dot(p_ref[...].astype(BF16), wpp_ref[...], preferred_element_type=F32)
    h3 = h2 + gate * pp
    ms = jnp.mean(h3 * h3, axis=-1, keepdims=True)
    y = h3 * lax.rsqrt(ms + RMS_EPS) * nf_ref[...]

    @pl.when(i < N_PROMPT // TM_S)
    def _():
        yp_ref[...] = y

    @pl.when(i >= N_PROMPT // TM_S)
    def _():
        ys_ref[...] = y


def _combine(dest, yb, h, rw, p_all, wpg, wpp, nf):
    npt = N_PROMPT // TM_S
    return pl.pallas_call(
        _combine_body,
        grid_spec=pltpu.PrefetchScalarGridSpec(
            num_scalar_prefetch=1,
            grid=(N_TOK // TM_S,),
            in_specs=[pl.BlockSpec(memory_space=pl.ANY),
                      pl.BlockSpec((TM_S, D_MODEL), lambda i, d: (i, 0)),
                      pl.BlockSpec((TM_S, LANES), lambda i, d: (i, 0)),
                      pl.BlockSpec((TM_S, PLE_DIM), lambda i, d: (i, 0)),
                      pl.BlockSpec((D_MODEL, D_MODEL), lambda i, d: (0, 0), pipeline_mode=pl.Buffered(1)),
                      pl.BlockSpec((PLE_DIM, D_MODEL), lambda i, d: (0, 0), pipeline_mode=pl.Buffered(1)),
                      pl.BlockSpec((1, D_MODEL), lambda i, d: (0, 0))],
            out_specs=[pl.BlockSpec((TM_S, D_MODEL), lambda i, d: (jnp.minimum(i, npt - 1), 0)),
                       pl.BlockSpec((TM_S, D_MODEL), lambda i, d: (jnp.maximum(i - npt, 0), 0))],
            scratch_shapes=[pltpu.VMEM((2, TOP_K, TM_S, D_MODEL // LANES, LANES), F32),
                            pltpu.SemaphoreType.DMA((2,))]),
        out_shape=[jax.ShapeDtypeStruct((N_PROMPT, D_MODEL), F32),
                   jax.ShapeDtypeStruct((N_SAMPLE, D_MODEL), F32)],
        compiler_params=_cparams("arbitrary", vmem_mb=48),
        name="combine_ple",
    )(dest, yb, h, rw, p_all, wpg, wpp, nf)


def _dispatch_plan(eidx):
    n_assign = N_TOK * TOP_K
    e_flat = eidx.reshape(n_assign)
    onehot = (e_flat[:, None] == jnp.arange(N_EXPERTS, dtype=jnp.int32)[None, :]).astype(jnp.int32)
    csum = jnp.cumsum(onehot, axis=0)
    counts = csum[-1]
    rank = jnp.sum(csum * onehot, axis=1) - 1
    padded = (counts + MOE_BLOCK - 1) // MOE_BLOCK * MOE_BLOCK
    pad_end = jnp.cumsum(padded)
    pad_start = pad_end - padded
    dest = pad_start[e_flat] + rank
    n_blocks = -(-n_assign // MOE_BLOCK) + N_EXPERTS
    tok = jnp.arange(n_assign, dtype=jnp.int32) // TOP_K
    pad_tok = jnp.arange(n_blocks * MOE_BLOCK, dtype=jnp.int32) % N_TOK
    slot_tok = pad_tok.at[dest].set(tok)
    blk = jnp.arange(n_blocks, dtype=jnp.int32)
    blk_e = jnp.minimum(jnp.searchsorted(pad_end, blk * MOE_BLOCK, side="right"), N_EXPERTS - 1).astype(jnp.int32)
    nused = (pad_end[-1] // MOE_BLOCK).astype(jnp.int32)
    prev_e = jnp.concatenate([jnp.full((1,), -1, jnp.int32), blk_e[:-1]])
    first = jnp.logical_and(blk < nused, blk_e != prev_e)
    par = ((jnp.cumsum(first.astype(jnp.int32)) - 1) % 2).astype(jnp.int32)
    idx_first = jnp.where(first, blk, n_blocks)
    later = jnp.concatenate([lax.cummin(idx_first[::-1])[::-1][1:], jnp.full((1,), n_blocks, jnp.int32)])
    next_e = jnp.where(later < n_blocks, blk_e[jnp.minimum(later, n_blocks - 1)], -1).astype(jnp.int32)
    plan = (blk_e, slot_tok, nused.reshape(1), first.astype(jnp.int32), par, next_e)
    return dest.astype(jnp.int32), plan


def kernel(x_prompt, x_sample, state_conv, state_shift, state_wkv, p_prompt, p_sample, norm_mix, w_in, conv_w, w_conv_out, shift_mu, w0, w2, a0, a2, g2, k_k, k_a, r_k, lnx_w, lnx_b, w_rwkv_out, w_mix_out, norm_ffn, w_route_group, b_route_group, w_route_expert, b_route_expert, w_exp_gate, w_exp_up, w_exp_down, w_ple_proj, w_ple_gate, norm_final):
    c3 = 3 * CONV_WIDTH
    rw3 = 3 * RWKV_WIDTH
    n_tiles = N_TOK // TM
    xp = x_prompt.reshape(N_PROMPT, D_MODEL)
    xs = x_sample.transpose(1, 0, 2).reshape(N_SAMPLE, D_MODEL)
    win = w_in[0]
    mu = shift_mu[0]
    st = state_shift[0]

    def lora_cols(a, rows):
        z32 = jnp.zeros((rows, LANES - DECAY_LORA), a.dtype)
        return jnp.concatenate([a[:, rw3:rw3 + DECAY_LORA], z32,
                                a[:, rw3 + DECAY_LORA:rw3 + DECAY_LORA + A_LORA], z32,
                                a[:, rw3 + DECAY_LORA + A_LORA:]], axis=1)

    xn = _norm_cast(xp, xs, norm_mix)
    z_conv, _, _ = _inproj(xn, win, jnp.zeros((1, c3), F32), jnp.zeros((DEC_BATCH, c3), F32),
                           tn=1024, col_blk_off=0, n_col_blocks=c3 // 1024, shift_lo=0, shift_hi=0,
                           tile_lo=0, n_tiles=n_tiles, name="inproj_conv")
    rkv_s, _, sraw_rkv = _inproj(xn, win, mu[None, :rw3], st[:, :rw3],
                                 tn=1024, col_blk_off=c3 // 1024, n_col_blocks=rw3 // 1024,
                                 shift_lo=0, shift_hi=rw3 // 1024,
                                 tile_lo=N_PROMPT // TM, n_tiles=N_SAMPLE // TM, name="inproj_rkv_sample")
    w_tail = jnp.concatenate([win[:, c3 + RWKV_PROJ:], lora_cols(win[:, c3:c3 + RWKV_PROJ], D_MODEL)], axis=1)
    mu_tail = jnp.concatenate([jnp.zeros((1, 2 * D_MODEL), F32), lora_cols(mu[None, :], 1)], axis=1)
    st_tail = jnp.concatenate([jnp.zeros((DEC_BATCH, 2 * D_MODEL), F32), lora_cols(st, DEC_BATCH)], axis=1)
    nb_tail = 3
    z_tail, last_tail, sraw_tail = _inproj(xn, w_tail, mu_tail, st_tail, tn=N_TAIL // nb_tail, col_blk_off=0,
                                           n_col_blocks=nb_tail, shift_lo=nb_tail - 1, shift_hi=nb_tail,
                                           tile_lo=0, n_tiles=n_tiles, name="inproj_tail")
    w_rk = _head_minor(win[:, c3:c3 + 2 * RWKV_WIDTH].reshape(D_MODEL, 2, RWKV_WIDTH), 2)
    rk_chain = _inproj_t(xn, w_rk.reshape(D_MODEL, 2 * RWKV_WIDTH).T.astype(BF16),
                         time_major=False, name="inproj_t_rk")
    v_chain = _inproj_t(xn, _head_minor(win[:, c3 + 2 * RWKV_WIDTH:c3 + rw3], 1).T.astype(BF16),
                        time_major=True, name="inproj_t_v")

    sc = state_conv[0].transpose(1, 0, 2).reshape(2 * DEC_BATCH, CONV_WIDTH)
    conv_out, ulast, us = _conv_branch(z_conv, conv_w[0], sc, w_conv_out[0].astype(BF16))

    def pad_rows(w):
        return jnp.concatenate([w, jnp.zeros((LANES - w.shape[0], w.shape[1]), w.dtype)], axis=0).astype(BF16)

    w2p, a2p = pad_rows(w2[0]), pad_rows(a2[0])
    g, wl_c, al_c, wl_s, al_s = _lora(z_tail, _head_minor(w2p, 1).T, _head_minor(a2p, 1).T, w2p, a2p,
                                      _head_minor(g2[0], 1).astype(BF16))
    par8 = [_param_chain8(p) for p in (k_k[0], k_a[0], r_k[0].reshape(RWKV_WIDTH))]
    mu8 = [_param_chain8(mu[n * RWKV_WIDTH:(n + 1) * RWKV_WIDTH]) for n in range(2)]
    bias8 = [_param_chain8(w0[0]), _param_chain8(a0[0])]
    zero8 = jnp.zeros((HEAD_SIZE * SCAN_TB, LANES), F32)
    mu_v = _param_chain(mu[2 * RWKV_WIDTH:rw3])
    gn = [_param_chain(lnx_w[0]), _param_chain(lnx_b[0])]
    seqs_p = [(rk_chain, 0), (rk_chain, 1), (wl_c[None], None), (al_c[None], None)]
    v4 = v_chain.reshape(1, SEQ, HEAD_SIZE, LANES)
    s0_p = jnp.zeros((1, HEAD_SIZE, HEAD_SIZE, LANES), F32)
    y_p, sf_p = _scan(seqs_p, (v4, None), par8 + mu8 + bias8, [mu_v] + gn, s0_p,
                      n_batches=SCAN_TT // SCAN_TB, steps=SCAN_TB, name="wkv_scan_prompt")
    ng = DEC_BATCH // CHAIN_B
    s0_s = state_wkv[0].reshape(ng, CHAIN_B, N_HEADS, HEAD_SIZE, HEAD_SIZE).transpose(0, 4, 3, 1, 2)
    s0_s = s0_s.reshape(ng, HEAD_SIZE, HEAD_SIZE, LANES)
    seqs_s = [(_to_chain_sample_ct(rkv_s[:, n * RWKV_WIDTH:(n + 1) * RWKV_WIDTH]), None) for n in range(2)]
    seqs_s += [(_to_chain_sample_ct(wl_s), None), (_to_chain_sample_ct(al_s), None)]
    v_s = _to_chain_sample_tc(rkv_s[:, 2 * RWKV_WIDTH:rw3])
    y_s, sf_s = _scan(seqs_s, (v_s, None), par8 + [zero8, zero8] + bias8,
                      [jnp.zeros((HEAD_SIZE, LANES), F32)] + gn, s0_s,
                      n_batches=1, steps=DEC_SEQ, name="wkv_scan_sample")
    rwkv_out = _rwkv_out(y_p.reshape(SEQ * HEAD_SIZE, LANES), _from_chain_sample(y_s), g,
                         _head_minor(w_rwkv_out[0], 0).astype(BF16))

    wr = jnp.concatenate([w_route_group[0], w_route_expert[0],
                          jnp.zeros((D_MODEL, LANES - N_GROUPS - N_EXPERTS), F32)], axis=1)
    br = jnp.concatenate([b_route_group[0], b_route_expert[0],
                          jnp.zeros((LANES - N_GROUPS - N_EXPERTS,), F32)])[None, :]
    h, hn, ridx, rw = _mix_route(conv_out, rwkv_out, z_tail, xp, xs, w_mix_out[0].astype(BF16),
                                 norm_ffn, wr.astype(BF16), br)

    dest, plan = _dispatch_plan(ridx[:, :TOP_K])
    yb = _experts(plan, hn, w_exp_gate[0], w_exp_up[0], w_exp_down[0])
    p_all = jnp.concatenate([p_prompt[0].reshape(N_PROMPT, PLE_DIM),
                             p_sample[0].transpose(1, 0, 2).reshape(N_SAMPLE, PLE_DIM)], axis=0)
    y_p2, y_s2 = _combine(dest, yb, h, rw, p_all, w_ple_gate[0].astype(BF16), w_ple_proj[0].astype(BF16),
                          norm_final[None, :])

    y_prompt = y_p2.reshape(BATCH, SEQ, D_MODEL)
    y_sample = y_s2.reshape(DEC_SEQ, DEC_BATCH, D_MODEL).transpose(1, 0, 2)
    tiles_per_seq = SEQ // TM
    seq_last = jnp.arange(BATCH) * tiles_per_seq + tiles_per_seq - 1

    def unpad_lora(a):
        o = 2 * D_MODEL
        return jnp.concatenate([a[:, o:o + DECAY_LORA], a[:, o + LANES:o + LANES + A_LORA],
                                a[:, o + 2 * LANES:]], axis=1)

    conv_p = ulast.reshape(-1, SUBLANES, CONV_WIDTH)[seq_last, SUBLANES - 2:, :][None]
    conv_s = us.reshape(2, DEC_BATCH, CONV_WIDTH).transpose(1, 0, 2)[None]
    last_rkv = jnp.concatenate([rk_chain[:, :, SEQ - 1, :], v4[:, SEQ - 1]], axis=0)
    lm = last_rkv.reshape(3, HEAD_SIZE, BATCH, N_HEADS).transpose(2, 0, 3, 1).reshape(BATCH, rw3)
    lt = unpad_lora(last_tail.reshape(-1, SUBLANES, N_TAIL)[seq_last, SUBLANES - 1, :])
    shift_p = jnp.concatenate([lm, lt], axis=1)[None]
    shift_s = jnp.concatenate([sraw_rkv, unpad_lora(sraw_tail)], axis=1)[None]
    wkv_p = sf_p.reshape(HEAD_SIZE, HEAD_SIZE, BATCH, N_HEADS).transpose(2, 3, 1, 0)[None]
    wkv_s = sf_s.reshape(ng, HEAD_SIZE, HEAD_SIZE, CHAIN_B, N_HEADS).transpose(0, 3, 4, 2, 1)
    wkv_s = wkv_s.reshape(DEC_BATCH, N_HEADS, HEAD_SIZE, HEAD_SIZE)[None]
    return (y_prompt, y_sample, conv_p, shift_p, wkv_p, conv_s, shift_s, wkv_s)
```

```python
import functools

import jax
import jax.numpy as jnp
from jax import lax
from jax.experimental import pallas as pl
from jax.experimental.pallas import tpu as pltpu

D_MODEL = 2048
BATCH = 4
SEQ = 2048
DEC_BATCH = 128
DEC_SEQ = 4
CONV_WIDTH = 1024
RWKV_WIDTH = 2048
HEAD_SIZE = 64
N_HEADS = RWKV_WIDTH // HEAD_SIZE
DECAY_LORA = 96
A_LORA = 96
GATE_LORA = 256
RWKV_PROJ = 3 * RWKV_WIDTH + DECAY_LORA + A_LORA + GATE_LORA
N_GROUPS = 8
EXPERTS_PER_GROUP = 8
N_EXPERTS = N_GROUPS * EXPERTS_PER_GROUP
TOP_K = 2
D_EXPERT = 512
MOE_BLOCK = 128
PLE_DIM = 256
RMS_EPS = 1e-6
GN_EPS = 64e-5

N_PROMPT = BATCH * SEQ
N_SAMPLE = DEC_BATCH * DEC_SEQ
N_TOK = N_PROMPT + N_SAMPLE
LANES = 128
SUBLANES = 8
TM = 512
TM_S = 256
TM_T = 256
TM_L = 128
CHAIN_B = LANES // N_HEADS
SCAN_TT = 64
SCAN_TB = SUBLANES
SCAN_ACC = 2
GATHER_AHEAD = 3
N_XBUF = GATHER_AHEAD + 1
LORA_W = 512
EXP_M05 = 0.6065306597126334
F32 = jnp.float32
BF16 = jnp.bfloat16
_NT = (((1,), (1,)), ((), ()))


def _sigmoid(x):
    return 1.0 / (1.0 + jnp.exp(-x))


def _cparams(*sem, vmem_mb=None):
    kw = dict(dimension_semantics=sem)
    if vmem_mb is not None:
        kw["vmem_limit_bytes"] = vmem_mb * 1024 * 1024
    return pltpu.CompilerParams(**kw)


def _resident(shape):
    nd = len(shape)
    return pl.BlockSpec(shape, lambda *_: (0,) * nd, pipeline_mode=pl.Buffered(1))


def _chain_tile(zt, half, c):
    start = c * N_HEADS
    rows = pl.ds(start if isinstance(c, int) else pl.multiple_of(start, N_HEADS), N_HEADS)
    return jnp.concatenate([zt[bb, half, rows, :] for bb in range(CHAIN_B)], axis=0)


def _store_chain_ct(zt, out_ref, halves, c_lo=0, n_c=HEAD_SIZE):
    for half in range(halves):
        for ci in range(n_c):
            out_ref[c_lo + ci, half * LANES:(half + 1) * LANES, :] = _chain_tile(zt, half, c_lo + ci).T


def _store_chain_tc(zt, out_ref, halves, c_lo=0, n_c=HEAD_SIZE):
    for half in range(halves):
        for ci in range(n_c):
            rows = pl.ds(half * LANES * HEAD_SIZE + c_lo + ci, LANES, stride=HEAD_SIZE)
            out_ref[rows, :] = _chain_tile(zt, half, c_lo + ci).T


def _load_chain_tc(y_ref, yt, halves):
    for half in range(halves):
        for v in range(HEAD_SIZE):
            mt = y_ref[pl.ds(half * LANES * HEAD_SIZE + v, LANES, stride=HEAD_SIZE), :].T
            for bb in range(CHAIN_B):
                yt[bb, half, v * N_HEADS:(v + 1) * N_HEADS, :] = mt[bb * N_HEADS:(bb + 1) * N_HEADS, :]


def _norm_body(xp_ref, xs_ref, g_ref, o_ref, *, n_prompt_tiles):
    i = pl.program_id(0)

    def f(x):
        ms = jnp.mean(x * x, axis=-1, keepdims=True)
        return (x * lax.rsqrt(ms + RMS_EPS) * g_ref[...]).astype(o_ref.dtype)

    @pl.when(i < n_prompt_tiles)
    def _():
        o_ref[...] = f(xp_ref[...])

    @pl.when(i >= n_prompt_tiles)
    def _():
        o_ref[...] = f(xs_ref[...])


def _norm_cast(xp, xs, g):
    npt = N_PROMPT // TM
    return pl.pallas_call(
        functools.partial(_norm_body, n_prompt_tiles=npt),
        grid=(N_TOK // TM,),
        in_specs=[pl.BlockSpec((TM, D_MODEL), lambda i: (jnp.minimum(i, npt - 1), 0)),
                  pl.BlockSpec((TM, D_MODEL), lambda i: (jnp.maximum(i - npt, 0), 0)),
                  pl.BlockSpec((1, D_MODEL), lambda i: (0, 0))],
        out_specs=pl.BlockSpec((TM, D_MODEL), lambda i: (i, 0)),
        out_shape=jax.ShapeDtypeStruct((N_TOK, D_MODEL), BF16),
        compiler_params=_cparams("arbitrary"),
        name="norm_cast",
    )(xp, xs, g)


def _inproj_body(xn_ref, w_ref, mu_ref, st_ref, z_ref, last_ref, sraw_ref, wb_ref, carry_ref, *,
                 shift_lo, shift_hi, tile_lo, n_prompt_tiles, tiles_per_seq):
    j = pl.program_id(0)
    i = pl.program_id(1) + tile_lo

    @pl.when(pl.program_id(1) == 0)
    def _():
        wb_ref[...] = w_ref[...].astype(BF16)

    z = jnp.dot(xn_ref[...], wb_ref[...], preferred_element_type=F32)
    tm = z.shape[0]
    last_ref[...] = z[tm - SUBLANES:tm]
    shifted = jnp.logical_and(j >= shift_lo, j < shift_hi)
    is_prompt = i < n_prompt_tiles

    @pl.when(jnp.logical_not(shifted))
    def _():
        z_ref[...] = z

    @pl.when(jnp.logical_and(shifted, is_prompt))
    def _():
        @pl.when(i % tiles_per_seq == 0)
        def _():
            carry_ref[...] = jnp.zeros_like(carry_ref)

        prev = pltpu.roll(z, 1, 0)
        row = lax.broadcasted_iota(jnp.int32, (tm, 1), 0)
        prev = jnp.where(row == 0, carry_ref[SUBLANES - 1:SUBLANES, :], prev)
        z_ref[...] = z + mu_ref[...] * (prev - z)
        carry_ref[...] = z[tm - SUBLANES:tm]

    @pl.when(jnp.logical_and(shifted, jnp.logical_not(is_prompt)))
    def _():
        prev = jnp.concatenate([st_ref[...], z[:tm - DEC_BATCH]], axis=0)
        z_ref[...] = z + mu_ref[...] * (prev - z)

    @pl.when(jnp.logical_not(is_prompt))
    def _():
        sraw_ref[...] = z[tm - DEC_BATCH:tm]


def _inproj(xn, w, mu, st, *, tn, col_blk_off, n_col_blocks, shift_lo, shift_hi, tile_lo, n_tiles, name):
    n_out = tn * n_col_blocks
    body = functools.partial(_inproj_body, shift_lo=shift_lo, shift_hi=shift_hi, tile_lo=tile_lo,
                             n_prompt_tiles=N_PROMPT // TM, tiles_per_seq=SEQ // TM)
    return pl.pallas_call(
        body,
        grid=(n_col_blocks, n_tiles),
        in_specs=[pl.BlockSpec((TM, D_MODEL), lambda j, i: (i + tile_lo, 0)),
                  pl.BlockSpec((D_MODEL, tn), lambda j, i: (0, j + col_blk_off)),
                  pl.BlockSpec((1, tn), lambda j, i: (0, j)),
                  pl.BlockSpec((DEC_BATCH, tn), lambda j, i: (0, j))],
        out_specs=[pl.BlockSpec((TM, tn), lambda j, i: (i, j)),
                   pl.BlockSpec((SUBLANES, tn), lambda j, i: (i, j)),
                   pl.BlockSpec((DEC_BATCH, tn), lambda j, i: (0, j))],
        out_shape=[jax.ShapeDtypeStruct((n_tiles * TM, n_out), F32),
                   jax.ShapeDtypeStruct((n_tiles * SUBLANES, n_out), F32),
                   jax.ShapeDtypeStruct((DEC_BATCH, n_out), F32)],
        scratch_shapes=[pltpu.VMEM((D_MODEL, tn), BF16), pltpu.VMEM((SUBLANES, tn), F32)],
        compiler_params=_cparams("arbitrary", "arbitrary", vmem_mb=48),
        name=name,
    )(xn, w, mu, st)


def _inproj_t_body(xn_ref, wt_ref, o_ref, zt, *, time_major, n_tb):
    tb = pl.program_id(1)
    b = pl.program_id(2)
    halves = TM_T // LANES
    n_c = HEAD_SIZE // CHAIN_B

    def matmul():
        z = lax.dot_general(wt_ref[...], xn_ref[...], _NT, preferred_element_type=F32)
        for half in range(halves):
            zt[tb % 2, b, half] = z[:, half * LANES:(half + 1) * LANES]

    def retile():
        store = _store_chain_tc if time_major else _store_chain_ct
        store(zt.at[1 - tb % 2], o_ref, halves, c_lo=b * n_c, n_c=n_c)

    @pl.when(tb == 0)
    def _():
        matmul()

    @pl.when(jnp.logical_and(tb > 0, tb < n_tb))
    def _():
        retile()
        matmul()

    @pl.when(tb == n_tb)
    def _():
        retile()


def _inproj_t(xn, wt, *, time_major, name):
    n_tb = SEQ // TM_T
    n = wt.shape[0] // RWKV_WIDTH

    def prev_tb(tb):
        return jnp.maximum(tb - 1, 0)

    if time_major:
        out_spec = pl.BlockSpec((None, TM_T * HEAD_SIZE, LANES), lambda j, tb, b: (j, prev_tb(tb), 0))
        out_shape = jax.ShapeDtypeStruct((n, SEQ * HEAD_SIZE, LANES), F32)
    else:
        out_spec = pl.BlockSpec((None, HEAD_SIZE, TM_T, LANES), lambda j, tb, b: (j, 0, prev_tb(tb), 0))
        out_shape = jax.ShapeDtypeStruct((n, HEAD_SIZE, SEQ, LANES), F32)
    return pl.pallas_call(
        functools.partial(_inproj_t_body, time_major=time_major, n_tb=n_tb),
        grid=(n, n_tb + 1, CHAIN_B),
        in_specs=[pl.BlockSpec((TM_T, D_MODEL), lambda j, tb, b: (b * n_tb + jnp.minimum(tb, n_tb - 1), 0)),
                  pl.BlockSpec((RWKV_WIDTH, D_MODEL), lambda j, tb, b: (j, 0), pipeline_mode=pl.Buffered(1))],
        out_specs=out_spec,
        out_shape=out_shape,
        scratch_shapes=[pltpu.VMEM((2, CHAIN_B, TM_T // LANES, RWKV_WIDTH, LANES), F32)],
        compiler_params=_cparams("arbitrary", "arbitrary", "arbitrary", vmem_mb=56),
        name=name,
    )(xn, wt)


def _conv_body(zc_ref, cw_ref, sc_ref, wco_ref, o_ref, ulast_ref, us_ref, carry_ref, *,
               n_prompt_tiles, tiles_per_seq):
    i = pl.program_id(0)
    gate_b = zc_ref[:, 0:CONV_WIDTH]
    u = zc_ref[:, CONV_WIDTH:2 * CONV_WIDTH] * zc_ref[:, 2 * CONV_WIDTH:3 * CONV_WIDTH]
    tm = u.shape[0]
    ulast_ref[...] = u[tm - SUBLANES:tm]
    w0 = cw_ref[0:1, :]
    w1 = cw_ref[1:2, :]
    w2 = cw_ref[2:3, :]

    def finish(p1, p2):
        conv = w0 * p2 + w1 * p1 + w2 * u
        y = (gate_b * conv).astype(BF16)
        o_ref[...] = jnp.dot(y, wco_ref[...], preferred_element_type=F32)

    @pl.when(i < n_prompt_tiles)
    def _():
        @pl.when(i % tiles_per_seq == 0)
        def _():
            carry_ref[...] = jnp.zeros_like(carry_ref)

        row = lax.broadcasted_iota(jnp.int32, (tm, 1), 0)
        c1 = carry_ref[SUBLANES - 1:SUBLANES, :]
        c2 = carry_ref[SUBLANES - 2:SUBLANES - 1, :]
        p1 = jnp.where(row == 0, c1, pltpu.roll(u, 1, 0))
        p2 = jnp.where(row == 0, c2, jnp.where(row == 1, c1, pltpu.roll(u, 2, 0)))
        carry_ref[...] = u[tm - SUBLANES:tm]
        finish(p1, p2)

    @pl.when(i >= n_prompt_tiles)
    def _():
        p1 = jnp.concatenate([sc_ref[DEC_BATCH:2 * DEC_BATCH, :], u[:tm - DEC_BATCH]], axis=0)
        p2 = jnp.concatenate([sc_ref[...], u[:tm - 2 * DEC_BATCH]], axis=0)
        us_ref[...] = u[tm - 2 * DEC_BATCH:tm]
        finish(p1, p2)


def _conv_branch(z_conv, conv_w, sc, wco):
    n_tiles = N_TOK // TM
    body = functools.partial(_conv_body, n_prompt_tiles=N_PROMPT // TM, tiles_per_seq=SEQ // TM)
    return pl.pallas_call(
        body,
        grid=(n_tiles,),
        in_specs=[pl.BlockSpec((TM, 3 * CONV_WIDTH), lambda i: (i, 0)),
                  pl.BlockSpec((3, CONV_WIDTH), lambda i: (0, 0)),
                  pl.BlockSpec((2 * DEC_BATCH, CONV_WIDTH), lambda i: (0, 0)),
                  _resident((CONV_WIDTH, D_MODEL))],
        out_specs=[pl.BlockSpec((TM, D_MODEL), lambda i: (i, 0)),
                   pl.BlockSpec((SUBLANES, CONV_WIDTH), lambda i: (i, 0)),
                   pl.BlockSpec((2 * DEC_BATCH, CONV_WIDTH), lambda i: (0, 0))],
        out_shape=[jax.ShapeDtypeStruct((N_TOK, D_MODEL), F32),
                   jax.ShapeDtypeStruct((n_tiles * SUBLANES, CONV_WIDTH), F32),
                   jax.ShapeDtypeStruct((2 * DEC_BATCH, CONV_WIDTH), F32)],
        scratch_shapes=[pltpu.VMEM((SUBLANES, CONV_WIDTH), F32)],
        compiler_params=_cparams("arbitrary", vmem_mb=48),
        name="conv_branch",
    )(z_conv, conv_w, sc, wco)


def _lora_body(zl_ref, w2t_ref, a2t_ref, w2_ref, a2_ref, g2_ref,
               g_ref, wlc_ref, alc_ref, wls_ref, als_ref, zt, *, n_prompt_steps):
    s = pl.program_id(0)
    tw = jnp.tanh(zl_ref[:, 0:LANES]).astype(BF16)
    xa = zl_ref[:, 0:2 * LANES].astype(BF16)
    xg = zl_ref[:, LANES:LORA_W]
    g_ref[...] = jnp.dot(_sigmoid(xg).astype(BF16), g2_ref[...], preferred_element_type=F32)

    @pl.when(s < n_prompt_steps)
    def _():
        b = s % CHAIN_B
        zt[0, b, 0] = lax.dot_general(w2t_ref[...], tw, _NT, preferred_element_type=F32)
        zt[1, b, 0] = lax.dot_general(a2t_ref[...], xa, _NT, preferred_element_type=F32)

        @pl.when(b == CHAIN_B - 1)
        def _():
            _store_chain_ct(zt.at[0], wlc_ref, 1)
            _store_chain_ct(zt.at[1], alc_ref, 1)

    @pl.when(s >= n_prompt_steps)
    def _():
        wls_ref[...] = jnp.dot(tw, w2_ref[...], preferred_element_type=F32)
        als_ref[...] = jnp.dot(xa, a2_ref[...], preferred_element_type=F32)


def _lora(z_lora, w2t, a2t, w2p, a2p, g2b):
    n_tb = SEQ // TM_L
    nps = n_tb * CHAIN_B

    def row_blk(s):
        return jnp.where(s < nps, (s % CHAIN_B) * n_tb + s // CHAIN_B, s)

    chain_spec = pl.BlockSpec((HEAD_SIZE, TM_L, LANES), lambda s: (0, jnp.minimum(s // CHAIN_B, n_tb - 1), 0))
    samp_spec = pl.BlockSpec((TM_L, RWKV_WIDTH), lambda s: (jnp.maximum(s - nps, 0), 0))
    chain_shape = jax.ShapeDtypeStruct((HEAD_SIZE, SEQ, LANES), F32)
    samp_shape = jax.ShapeDtypeStruct((N_SAMPLE, RWKV_WIDTH), F32)
    return pl.pallas_call(
        functools.partial(_lora_body, n_prompt_steps=nps),
        grid=(N_TOK // TM_L,),
        in_specs=[pl.BlockSpec((TM_L, LORA_W), lambda s: (row_blk(s), 0)),
                  _resident(w2t.shape), _resident(a2t.shape), _resident(w2p.shape), _resident(a2p.shape),
                  _resident(g2b.shape)],
        out_specs=[pl.BlockSpec((TM_L, RWKV_WIDTH), lambda s: (row_blk(s), 0)),
                   chain_spec, chain_spec, samp_spec, samp_spec],
        out_shape=[jax.ShapeDtypeStruct((N_TOK, RWKV_WIDTH), F32),
                   chain_shape, chain_shape, samp_shape, samp_shape],
        scratch_shapes=[pltpu.VMEM((2, CHAIN_B, 1, RWKV_WIDTH, LANES), F32)],
        compiler_params=_cparams("arbitrary", vmem_mb=48),
        name="lora",
    )(z_lora, w2t, a2t, w2p, a2p, g2b)


def _scan_body(r_ref, k_ref, wl_ref, al_ref, v_ref, kk_ref, ka_ref, rk_ref, mur_ref, muk_ref, w0_ref, a0_ref,
               muv_ref, lw_ref, lb_ref, s0_ref,
               y_ref, s_ref, vec_ref, bon_ref, prevb_ref, prevv_ref, *, n_batches, steps):
    @pl.when(pl.program_id(1) == 0)
    def _():
        s_ref[...] = s0_ref[...]
        prevb_ref[...] = jnp.zeros_like(prevb_ref)
        prevv_ref[...] = jnp.zeros_like(prevv_ref)

    rows = HEAD_SIZE * SCAN_TB
    first_t = lax.broadcasted_iota(jnp.int32, (rows, 1), 0) % SCAN_TB == 0

    def batch(bi, carry):
        t0 = pl.multiple_of(bi * SCAN_TB, SCAN_TB)

        def load(ref):
            return ref[:, pl.ds(t0, SCAN_TB), :].reshape(rows, LANES)

        def cube(x):
            return x.reshape(HEAD_SIZE, SCAN_TB, LANES)

        def shifted(x, slot, mu_ref):
            prev = jnp.where(first_t, pltpu.roll(prevb_ref[slot], rows - (SCAN_TB - 1), 0), pltpu.roll(x, 1, 0))
            prevb_ref[slot] = x
            return x + mu_ref[...] * (prev - x)

        r = shifted(load(r_ref), 0, mur_ref)
        k = shifted(load(k_ref), 1, muk_ref)
        decay = jnp.exp(-EXP_M05 * _sigmoid(load(wl_ref) + w0_ref[...]))
        a = _sigmoid(load(al_ref) + a0_ref[...])
        kk = cube(k * kk_ref[...])
        nrm = jnp.sqrt(jnp.sum(kk * kk, axis=0))
        kk = kk * (1.0 / jnp.maximum(nrm, 1e-12))[None]
        kf = k * (1.0 + (a - 1.0) * ka_ref[...])
        vec_ref[0] = -kk
        vec_ref[1] = cube(decay)
        vec_ref[2] = kk * cube(a)
        vec_ref[3] = cube(kf)
        vec_ref[4] = cube(r)
        bon_ref[...] = jnp.sum(cube(r * kf * rk_ref[...]), axis=0)

        def step(tl, c):
            t = t0 + tl
            v_raw = v_ref[t]
            v = v_raw + muv_ref[...] * (prevv_ref[...] - v_raw)
            prevv_ref[...] = v_raw

            def row(j, kx):
                return vec_ref[j, kx, pl.ds(tl, HEAD_SIZE, stride=0), :]

            parts = [s_ref[kx] * row(0, kx) for kx in range(SCAN_ACC)]
            for kx in range(SCAN_ACC, HEAD_SIZE):
                parts[kx % SCAN_ACC] = parts[kx % SCAN_ACC] + s_ref[kx] * row(0, kx)
            sa = functools.reduce(lambda x, y: x + y, parts)

            parts = []
            for kx in range(HEAD_SIZE):
                sn = s_ref[kx] * row(1, kx) + sa * row(2, kx) + v * row(3, kx)
                s_ref[kx] = sn
                if kx < SCAN_ACC:
                    parts.append(sn * row(4, kx))
                else:
                    parts[kx % SCAN_ACC] = parts[kx % SCAN_ACC] + sn * row(4, kx)
            o = functools.reduce(lambda x, y: x + y, parts)

            mu = jnp.mean(o, axis=0, keepdims=True)
            dlt = o - mu
            var = jnp.mean(dlt * dlt, axis=0, keepdims=True)
            on = dlt * lax.rsqrt(var + GN_EPS) * lw_ref[...] + lb_ref[...]
            y_ref[t] = on + bon_ref[pl.ds(tl, HEAD_SIZE, stride=0), :] * v
            return c

        for tl in range(steps):
            step(tl, 0)
        return carry

    lax.fori_loop(0, n_batches, batch, 0)


def _scan(ct_seqs, v_seq, params8, params, s0, *, n_batches, steps, name):
    g = s0.shape[0]
    tt = n_batches * SCAN_TB
    ttv = v_seq[0].shape[1] if n_batches == 1 else n_batches * steps
    n_ti = ct_seqs[0][0].shape[2] // tt

    def spec(block, lead, tpos):
        def index(gi, ti):
            idx = [gi if lead is None else lead, 0, 0, 0]
            idx[tpos] = ti
            return tuple(idx)
        return pl.BlockSpec(block, index)

    ct_block = (None, HEAD_SIZE, tt, LANES)
    tv_block = (None, ttv, HEAD_SIZE, LANES)
    par8_spec = pl.BlockSpec((HEAD_SIZE * SCAN_TB, LANES), lambda gi, ti: (0, 0))
    par_spec = pl.BlockSpec((HEAD_SIZE, LANES), lambda gi, ti: (0, 0))
    st_spec = pl.BlockSpec((None, HEAD_SIZE, HEAD_SIZE, LANES), lambda gi, ti: (gi, 0, 0, 0))
    return pl.pallas_call(
        functools.partial(_scan_body, n_batches=n_batches, steps=steps),
        grid=(g, n_ti),
        in_specs=([spec(ct_block, lead, 2) for _, lead in ct_seqs] + [spec(tv_block, v_seq[1], 1)]
                  + [par8_spec] * len(params8) + [par_spec] * len(params) + [st_spec]),
        out_specs=[spec(tv_block, None, 1), st_spec],
        out_shape=[jax.ShapeDtypeStruct((g, n_ti * ttv, HEAD_SIZE, LANES), F32),
                   jax.ShapeDtypeStruct((g, HEAD_SIZE, HEAD_SIZE, LANES), F32)],
        scratch_shapes=[pltpu.VMEM((5, HEAD_SIZE, SCAN_TB, LANES), F32), pltpu.VMEM((SCAN_TB, LANES), F32),
                        pltpu.VMEM((2, HEAD_SIZE * SCAN_TB, LANES), F32), pltpu.VMEM((HEAD_SIZE, LANES), F32)],
        compiler_params=_cparams("arbitrary", "arbitrary", vmem_mb=48),
        name=name,
    )(*[a for a, _ in ct_seqs], v_seq[0], *params8, *params, s0)


def _sample_groups(x):
    return x.reshape(DEC_SEQ, DEC_BATCH // CHAIN_B, CHAIN_B, N_HEADS, HEAD_SIZE).transpose(1, 0, 2, 3, 4)


def _to_chain_sample_ct(x):
    x = _sample_groups(x).transpose(0, 4, 1, 2, 3).reshape(DEC_BATCH // CHAIN_B, HEAD_SIZE, DEC_SEQ, LANES)
    return jnp.pad(x, ((0, 0), (0, 0), (0, SCAN_TB - DEC_SEQ), (0, 0)))


def _to_chain_sample_tc(x):
    return _sample_groups(x).transpose(0, 1, 4, 2, 3).reshape(DEC_BATCH // CHAIN_B, DEC_SEQ, HEAD_SIZE, LANES)


def _from_chain_sample(y):
    ng = DEC_BATCH // CHAIN_B
    y = y.reshape(ng, DEC_SEQ, HEAD_SIZE, CHAIN_B, N_HEADS).transpose(1, 0, 3, 2, 4)
    return y.reshape(N_SAMPLE, RWKV_WIDTH)


def _param_chain(p):
    return jnp.tile(p.reshape(N_HEADS, HEAD_SIZE).T, (1, CHAIN_B))


def _param_chain8(p):
    return jnp.repeat(_param_chain(p), SCAN_TB, axis=0)


def _head_minor(w, axis):
    shape = w.shape
    w = w.reshape(shape[:axis] + (N_HEADS, HEAD_SIZE) + shape[axis + 1:])
    return jnp.swapaxes(w, axis, axis + 1).reshape(shape)


def _rwkv_out_body(yc_ref, ys_ref, g_ref, w_ref, o_ref, yt, *, n_prompt_steps):
    s = pl.program_id(0)

    def finish(y):
        o_ref[...] = jnp.dot((y * g_ref[...]).astype(BF16), w_ref[...], preferred_element_type=F32)

    @pl.when(s < n_prompt_steps)
    def _():
        b = s % CHAIN_B

        @pl.when(b == 0)
        def _():
            _load_chain_tc(yc_ref, yt, TM_T // LANES)

        finish(jnp.concatenate([yt[b, half].T for half in range(TM_T // LANES)], axis=0))

    @pl.when(s >= n_prompt_steps)
    def _():
        finish(ys_ref[...])


def _rwkv_out(y_chain, y_s, g, w):
    n_tb = SEQ // TM_T
    nps = n_tb * CHAIN_B

    def row_blk(s):
        return jnp.where(s < nps, (s % CHAIN_B) * n_tb + s // CHAIN_B, s)

    return pl.pallas_call(
        functools.partial(_rwkv_out_body, n_prompt_steps=nps),
        grid=(N_TOK // TM_T,),
        in_specs=[pl.BlockSpec((TM_T * HEAD_SIZE, LANES), lambda s: (jnp.minimum(s // CHAIN_B, n_tb - 1), 0)),
                  pl.BlockSpec((TM_T, RWKV_WIDTH), lambda s: (jnp.maximum(s - nps, 0), 0)),
                  pl.BlockSpec((TM_T, RWKV_WIDTH), lambda s: (row_blk(s), 0)),
                  _resident((RWKV_WIDTH, D_MODEL))],
        out_specs=pl.BlockSpec((TM_T, D_MODEL), lambda s: (row_blk(s), 0)),
        out_shape=jax.ShapeDtypeStruct((N_TOK, D_MODEL), F32),
        scratch_shapes=[pltpu.VMEM((CHAIN_B, TM_T // LANES, RWKV_WIDTH, LANES), F32)],
        compiler_params=_cparams("arbitrary", vmem_mb=56),
        name="rwkv_out",
    )(y_chain, y_s, g, w)


def _mix_body(co_ref, ro_ref, gc_ref, gr_ref, xp_ref, xs_ref, wm_ref, nf_ref, wr_ref, br_ref,
              h_ref, hn_ref, ridx_ref, rw_ref, *, n_prompt_tiles):
    i = pl.program_id(0)
    mixed = _sigmoid(gc_ref[...]) * co_ref[...] + _sigmoid(gr_ref[...]) * ro_ref[...]
    mo = jnp.dot(mixed.astype(BF16), wm_ref[...], preferred_element_type=F32)

    def finish(x):
        h = x + mo
        h_ref[...] = h
        ms = jnp.mean(h * h, axis=-1, keepdims=True)
        hn = h * lax.rsqrt(ms + RMS_EPS) * nf_ref[...]
        hnb = hn.astype(BF16)
        hn_ref[...] = hnb.reshape(hn.shape[0], D_MODEL // LANES, LANES)
        logits = jnp.dot(hnb, wr_ref[...], preferred_element_type=F32) + br_ref[...]
        tm = logits.shape[0]
        lane = lax.broadcasted_iota(jnp.int32, (tm, LANES), 1)
        neg = jnp.float32(-jnp.inf)
        gl = jnp.where(lane < N_GROUPS, logits, neg)
        gmax = jnp.max(gl, axis=-1, keepdims=True)
        g_idx = jnp.min(jnp.where(gl == gmax, lane, LANES), axis=-1, keepdims=True)
        g_w = 1.0 / jnp.sum(jnp.exp(gl - gmax), axis=-1, keepdims=True)
        lo = N_GROUPS + g_idx * EXPERTS_PER_GROUP
        el = jnp.where(jnp.logical_and(lane >= lo, lane < lo + EXPERTS_PER_GROUP), logits, neg)
        m1 = jnp.max(el, axis=-1, keepdims=True)
        i1 = jnp.min(jnp.where(el == m1, lane, LANES), axis=-1, keepdims=True)
        el2 = jnp.where(lane == i1, neg, el)
        m2 = jnp.max(el2, axis=-1, keepdims=True)
        i2 = jnp.min(jnp.where(el2 == m2, lane, LANES), axis=-1, keepdims=True)
        t2 = jnp.exp(m2 - m1)
        den = 1.0 + t2
        ridx_ref[...] = jnp.where(lane == 0, i1 - N_GROUPS, jnp.where(lane == 1, i2 - N_GROUPS, 0))
        rw_ref[...] = jnp.where(lane == 0, (1.0 / den) * g_w, jnp.where(lane == 1, (t2 / den) * g_w, 0.0))

    @pl.when(i < n_prompt_tiles)
    def _():
        finish(xp_ref[...])

    @pl.when(i >= n_prompt_tiles)
    def _():
        finish(xs_ref[...])


def _mix_route(conv_out, rwkv_out, z_tail, xp, xs, wm, nf, wr, br):
    npt = N_PROMPT // TM_S
    tok_spec = pl.BlockSpec((TM_S, D_MODEL), lambda i: (i, 0))
    small_spec = pl.BlockSpec((TM_S, LANES), lambda i: (i, 0))
    return pl.pallas_call(
        functools.partial(_mix_body, n_prompt_tiles=npt),
        grid=(N_TOK // TM_S,),
        in_specs=[tok_spec, tok_spec,
                  pl.BlockSpec((TM_S, D_MODEL), lambda i: (i, 0)),
                  pl.BlockSpec((TM_S, D_MODEL), lambda i: (i, 1)),
                  pl.BlockSpec((TM_S, D_MODEL), lambda i: (jnp.minimum(i, npt - 1), 0)),
                  pl.BlockSpec((TM_S, D_MODEL), lambda i: (jnp.maximum(i - npt, 0), 0)),
                  _resident((D_MODEL, D_MODEL)),
                  pl.BlockSpec((1, D_MODEL), lambda i: (0, 0)),
                  _resident((D_MODEL, LANES)),
                  pl.BlockSpec((1, LANES), lambda i: (0, 0))],
        out_specs=[tok_spec, pl.BlockSpec((TM_S, D_MODEL // LANES, LANES), lambda i: (i, 0, 0)),
                   small_spec, small_spec],
        out_shape=[jax.ShapeDtypeStruct((N_TOK, D_MODEL), F32),
                   jax.ShapeDtypeStruct((N_TOK, D_MODEL // LANES, LANES), BF16),
                   jax.ShapeDtypeStruct((N_TOK, LANES), jnp.int32),
                   jax.ShapeDtypeStruct((N_TOK, LANES), F32)],
        compiler_params=_cparams("arbitrary", vmem_mb=56),
        name="mix_route",
    )(conv_out, rwkv_out, z_tail, z_tail, xp, xs, wm, nf, wr, br)


def _expert_body(blk_e_ref, slot_tok_ref, nused_ref, first_ref, par_ref, next_e_ref,
                 hn_ref, wg_hbm, wu_hbm, wd_hbm, yb_ref,
                 xbuf, sem, wfg, wfu, wfd, wsem, wgb, wub, wdb):
    i = pl.program_id(0)
    nused = nused_ref[0]
    slot = i % N_XBUF

    def row_copy(blk, r, s):
        tok = slot_tok_ref[blk * MOE_BLOCK + r]
        return pltpu.make_async_copy(hn_ref.at[tok], xbuf.at[s, r], sem.at[s])

    def issue(blk, s):
        for r in range(MOE_BLOCK):
            row_copy(blk, r, s).start(priority=r % 2)

    def w_copies(e, s):
        return (pltpu.make_async_copy(wg_hbm.at[e], wfg.at[s], wsem.at[s]),
                pltpu.make_async_copy(wu_hbm.at[e], wfu.at[s], wsem.at[s]),
                pltpu.make_async_copy(wd_hbm.at[e], wfd.at[s], wsem.at[s]))

    @pl.when(jnp.logical_and(i == 0, nused > 0))
    def _():
        for c in w_copies(blk_e_ref[0], 0):
            c.start(priority=1)
        for a in range(GATHER_AHEAD):
            @pl.when(a < nused)
            def _():
                issue(a, a)

    @pl.when(i + GATHER_AHEAD < nused)
    def _():
        issue(i + GATHER_AHEAD, (i + GATHER_AHEAD) % N_XBUF)

    @pl.when(i < nused)
    def _():
        @pl.when(first_ref[i] == 1)
        def _():
            ws = par_ref[i]
            for c in w_copies(blk_e_ref[i], ws):
                c.wait()

            @pl.when(next_e_ref[i] >= 0)
            def _():
                for c in w_copies(next_e_ref[i], 1 - ws):
                    c.start(priority=1)

            wgb[...] = wfg[ws].astype(BF16)
            wub[...] = wfu[ws].astype(BF16)
            wdb[...] = wfd[ws].astype(BF16)

        for r in range(MOE_BLOCK):
            row_copy(i, r, slot).wait()
        xe = xbuf[slot].reshape(MOE_BLOCK, D_MODEL)
        gate = jnp.dot(xe, wgb[...], preferred_element_type=F32)
        up = jnp.dot(xe, wub[...], preferred_element_type=F32)
        hdn = (gate * _sigmoid(gate)) * up
        yb = jnp.dot(hdn.astype(BF16), wdb[...], preferred_element_type=F32)
        yb_ref[...] = yb.reshape(MOE_BLOCK, D_MODEL // LANES, LANES)

    @pl.when(i >= nused)
    def _():
        yb_ref[...] = jnp.zeros_like(yb_ref)


def _experts(plan, hn, wg, wu, wd):
    blk_e, slot_tok, nused, first, par, next_e = plan
    n_blocks = blk_e.shape[0]
    any_spec = pl.BlockSpec(memory_space=pl.ANY)
    return pl.pallas_call(
        _expert_body,
        grid_spec=pltpu.PrefetchScalarGridSpec(
            num_scalar_prefetch=6,
            grid=(n_blocks,),
            in_specs=[any_spec, any_spec, any_spec, any_spec],
            out_specs=pl.BlockSpec((MOE_BLOCK, D_MODEL // LANES, LANES), lambda i, *_: (i, 0, 0)),
            scratch_shapes=[pltpu.VMEM((N_XBUF, MOE_BLOCK, D_MODEL // LANES, LANES), BF16),
                            pltpu.SemaphoreType.DMA((N_XBUF,)),
                            pltpu.VMEM((2, D_MODEL, D_EXPERT), F32),
                            pltpu.VMEM((2, D_MODEL, D_EXPERT), F32),
                            pltpu.VMEM((2, D_EXPERT, D_MODEL), F32),
                            pltpu.SemaphoreType.DMA((2,)),
                            pltpu.VMEM((D_MODEL, D_EXPERT), BF16),
                            pltpu.VMEM((D_MODEL, D_EXPERT), BF16),
                            pltpu.VMEM((D_EXPERT, D_MODEL), BF16)]),
        out_shape=jax.ShapeDtypeStruct((n_blocks * MOE_BLOCK, D_MODEL // LANES, LANES), F32),
        compiler_params=_cparams("arbitrary", vmem_mb=48),
        name="experts",
    )(blk_e, slot_tok, nused, first, par, next_e, hn, wg, wu, wd)


def _combine_body(dest_ref, yb_ref, h_ref, rw_ref, p_ref, wpg_ref, wpp_ref, nf_ref, yp_ref, ys_ref,
                  ybuf, sem):
    i = pl.program_id(0)
    tm = h_ref.shape[0]
    slot = i % 2

    def row_copy(tile, r, s, sl):
        d = dest_ref[(tile * tm + r) * TOP_K + s]
        return pltpu.make_async_copy(yb_ref.at[d], ybuf.at[sl, s, r], sem.at[sl])

    def issue(tile, sl):
        for r in range(tm):
            row_copy(tile, r, 0, sl).start()
            row_copy(tile, r, 1, sl).start()

    @pl.when(i == 0)
    def _():
        issue(0, 0)

    @pl.when(i + 1 < pl.num_programs(0))
    def _():
        issue(i + 1, 1 - slot)

    for r in range(tm):
        row_copy(i, r, 0, slot).wait()
        row_copy(i, r, 1, slot).wait()

    rw = rw_ref[...]
    y0 = ybuf[slot, 0].reshape(tm, D_MODEL)
    y1 = ybuf[slot, 1].reshape(tm, D_MODEL)
    h2 = h_ref[...] + (y0 * rw[:, 0:1] + y1 * rw[:, 1:2])
    gate = _sigmoid(jnp.dot(h2.astype(BF16), wpg_ref[...], preferred_element_type=F32))
    pp = jnp.dot(p_ref[...].astype(BF16), wpp_ref[...], preferred_element_type=F32)
    h3 = h2 + gate * pp
    ms = jnp.mean(h3 * h3, axis=-1, keepdims=True)
    y = h3 * lax.rsqrt(ms + RMS_EPS) * nf_ref[...]

    @pl.when(i < N_PROMPT // TM_S)
    def _():
        yp_ref[...] = y

    @pl.when(i >= N_PROMPT // TM_S)
    def _():
        ys_ref[...] = y


def _combine(dest, yb, h, rw, p_all, wpg, wpp, nf):
    npt = N_PROMPT // TM_S
    return pl.pallas_call(
        _combine_body,
        grid_spec=pltpu.PrefetchScalarGridSpec(
            num_scalar_prefetch=1,
            grid=(N_TOK // TM_S,),
            in_specs=[pl.BlockSpec(memory_space=pl.ANY),
                      pl.BlockSpec((TM_S, D_MODEL), lambda i, d: (i, 0)),
                      pl.BlockSpec((TM_S, LANES), lambda i, d: (i, 0)),
                      pl.BlockSpec((TM_S, PLE_DIM), lambda i, d: (i, 0)),
                      pl.BlockSpec((D_MODEL, D_MODEL), lambda i, d: (0, 0), pipeline_mode=pl.Buffered(1)),
                      pl.BlockSpec((PLE_DIM, D_MODEL), lambda i, d: (0, 0), pipeline_mode=pl.Buffered(1)),
                      pl.BlockSpec((1, D_MODEL), lambda i, d: (0, 0))],
            out_specs=[pl.BlockSpec((TM_S, D_MODEL), lambda i, d: (jnp.minimum(i, npt - 1), 0)),
                       pl.BlockSpec((TM_S, D_MODEL), lambda i, d: (jnp.maximum(i - npt, 0), 0))],
            scratch_shapes=[pltpu.VMEM((2, TOP_K, TM_S, D_MODEL // LANES, LANES), F32),
                            pltpu.SemaphoreType.DMA((2,))]),
        out_shape=[jax.ShapeDtypeStruct((N_PROMPT, D_MODEL), F32),
                   jax.ShapeDtypeStruct((N_SAMPLE, D_MODEL), F32)],
        compiler_params=_cparams("arbitrary", vmem_mb=48),
        name="combine_ple",
    )(dest, yb, h, rw, p_all, wpg, wpp, nf)


def _dispatch_plan(eidx):
    n_assign = N_TOK * TOP_K
    e_flat = eidx.reshape(n_assign)
    onehot = (e_flat[:, None] == jnp.arange(N_EXPERTS, dtype=jnp.int32)[None, :]).astype(jnp.int32)
    csum = jnp.cumsum(onehot, axis=0)
    counts = csum[-1]
    rank = jnp.sum(csum * onehot, axis=1) - 1
    padded = (counts + MOE_BLOCK - 1) // MOE_BLOCK * MOE_BLOCK
    pad_end = jnp.cumsum(padded)
    pad_start = pad_end - padded
    dest = pad_start[e_flat] + rank
    n_blocks = -(-n_assign // MOE_BLOCK) + N_EXPERTS
    tok = jnp.arange(n_assign, dtype=jnp.int32) // TOP_K
    pad_tok = jnp.arange(n_blocks * MOE_BLOCK, dtype=jnp.int32) % N_TOK
    slot_tok = pad_tok.at[dest].set(tok)
    blk = jnp.arange(n_blocks, dtype=jnp.int32)
    blk_e = jnp.minimum(jnp.searchsorted(pad_end, blk * MOE_BLOCK, side="right"), N_EXPERTS - 1).astype(jnp.int32)
    nused = (pad_end[-1] // MOE_BLOCK).astype(jnp.int32)
    prev_e = jnp.concatenate([jnp.full((1,), -1, jnp.int32), blk_e[:-1]])
    first = jnp.logical_and(blk < nused, blk_e != prev_e)
    par = ((jnp.cumsum(first.astype(jnp.int32)) - 1) % 2).astype(jnp.int32)
    idx_first = jnp.where(first, blk, n_blocks)
    later = jnp.concatenate([lax.cummin(idx_first[::-1])[::-1][1:], jnp.full((1,), n_blocks, jnp.int32)])
    next_e = jnp.where(later < n_blocks, blk_e[jnp.minimum(later, n_blocks - 1)], -1).astype(jnp.int32)
    plan = (blk_e, slot_tok, nused.reshape(1), first.astype(jnp.int32), par, next_e)
    return dest.astype(jnp.int32), plan


def kernel(x_prompt, x_sample, state_conv, state_shift, state_wkv, p_prompt, p_sample, norm_mix, w_in, conv_w, w_conv_out, shift_mu, w0, w2, a0, a2, g2, k_k, k_a, r_k, lnx_w, lnx_b, w_rwkv_out, w_mix_out, norm_ffn, w_route_group, b_route_group, w_route_expert, b_route_expert, w_exp_gate, w_exp_up, w_exp_down, w_ple_proj, w_ple_gate, norm_final):
    c3 = 3 * CONV_WIDTH
    rw3 = 3 * RWKV_WIDTH
    n_tiles = N_TOK // TM
    xp = x_prompt.reshape(N_PROMPT, D_MODEL)
    xs = x_sample.transpose(1, 0, 2).reshape(N_SAMPLE, D_MODEL)
    win = w_in[0]
    mu = shift_mu[0]
    st = state_shift[0]

    xn = _norm_cast(xp, xs, norm_mix)
    z_conv, _, _ = _inproj(xn, win, jnp.zeros((1, c3), F32), jnp.zeros((DEC_BATCH, c3), F32),
                           tn=1536, col_blk_off=0, n_col_blocks=c3 // 1536, shift_lo=0, shift_hi=0,
                           tile_lo=0, n_tiles=n_tiles, name="inproj_conv")
    rkv_s, _, sraw_rkv = _inproj(xn, win, mu[None, :rw3], st[:, :rw3],
                                 tn=1024, col_blk_off=c3 // 1024, n_col_blocks=rw3 // 1024,
                                 shift_lo=0, shift_hi=rw3 // 1024,
                                 tile_lo=N_PROMPT // TM, n_tiles=N_SAMPLE // TM, name="inproj_rkv_sample")
    n_lr = RWKV_PROJ - rw3
    mu_l = jnp.concatenate([mu[rw3:], jnp.zeros((LORA_W - n_lr,), F32)])[None, :]
    st_l = jnp.concatenate([st[:, rw3:], jnp.zeros((DEC_BATCH, LORA_W - n_lr), F32)], axis=1)
    z_lora, last_l, sraw_l = _inproj(xn, win, mu_l, st_l, tn=LORA_W, col_blk_off=(c3 + rw3) // LORA_W,
                                     n_col_blocks=1, shift_lo=0, shift_hi=1,
                                     tile_lo=0, n_tiles=n_tiles, name="inproj_lora")
    z_gate, _, _ = _inproj(xn, win[:, c3 + RWKV_PROJ:], jnp.zeros((1, 2 * D_MODEL), F32),
                           jnp.zeros((DEC_BATCH, 2 * D_MODEL), F32), tn=1024, col_blk_off=0,
                           n_col_blocks=2 * D_MODEL // 1024, shift_lo=0, shift_hi=0,
                           tile_lo=0, n_tiles=n_tiles, name="inproj_gate")
    w_rk = _head_minor(win[:, c3:c3 + 2 * RWKV_WIDTH].reshape(D_MODEL, 2, RWKV_WIDTH), 2)
    rk_chain = _inproj_t(xn, w_rk.reshape(D_MODEL, 2 * RWKV_WIDTH).T.astype(BF16),
                         time_major=False, name="inproj_t_rk")
    v_chain = _inproj_t(xn, _head_minor(win[:, c3 + 2 * RWKV_WIDTH:c3 + rw3], 1).T.astype(BF16),
                        time_major=True, name="inproj_t_v")

    sc = state_conv[0].transpose(1, 0, 2).reshape(2 * DEC_BATCH, CONV_WIDTH)
    conv_out, ulast, us = _conv_branch(z_conv, conv_w[0], sc, w_conv_out[0].astype(BF16))

    def pad_rows(w, before, total):
        return jnp.pad(w, ((before, total - before - w.shape[0]), (0, 0))).astype(BF16)

    w2p = pad_rows(w2[0], 0, LANES)
    a2p = pad_rows(a2[0], DECAY_LORA, 2 * LANES)
    g2p = pad_rows(_head_minor(g2[0], 1), DECAY_LORA + A_LORA - LANES, LORA_W - LANES)
    g, wl_c, al_c, wl_s, al_s = _lora(z_lora, _head_minor(w2p, 1).T, _head_minor(a2p, 1).T, w2p, a2p, g2p)
    par8 = [_param_chain8(p) for p in (k_k[0], k_a[0], r_k[0].reshape(RWKV_WIDTH))]
    mu8 = [_param_chain8(mu[n * RWKV_WIDTH:(n + 1) * RWKV_WIDTH]) for n in range(2)]
    bias8 = [_param_chain8(w0[0]), _param_chain8(a0[0])]
    zero8 = jnp.zeros((HEAD_SIZE * SCAN_TB, LANES), F32)
    mu_v = _param_chain(mu[2 * RWKV_WIDTH:rw3])
    gn = [_param_chain(lnx_w[0]), _param_chain(lnx_b[0])]
    seqs_p = [(rk_chain, 0), (rk_chain, 1), (wl_c[None], None), (al_c[None], None)]
    v4 = v_chain.reshape(1, SEQ, HEAD_SIZE, LANES)
    s0_p = jnp.zeros((1, HEAD_SIZE, HEAD_SIZE, LANES), F32)
    y_p, sf_p = _scan(seqs_p, (v4, None), par8 + mu8 + bias8, [mu_v] + gn, s0_p,
                      n_batches=SCAN_TT // SCAN_TB, steps=SCAN_TB, name="wkv_scan_prompt")
    ng = DEC_BATCH // CHAIN_B
    s0_s = state_wkv[0].reshape(ng, CHAIN_B, N_HEADS, HEAD_SIZE, HEAD_SIZE).transpose(0, 4, 3, 1, 2)
    s0_s = s0_s.reshape(ng, HEAD_SIZE, HEAD_SIZE, LANES)
    seqs_s = [(_to_chain_sample_ct(rkv_s[:, n * RWKV_WIDTH:(n + 1) * RWKV_WIDTH]), None) for n in range(2)]
    seqs_s += [(_to_chain_sample_ct(wl_s), None), (_to_chain_sample_ct(al_s), None)]
    v_s = _to_chain_sample_tc(rkv_s[:, 2 * RWKV_WIDTH:rw3])
    y_s, sf_s = _scan(seqs_s, (v_s, None), par8 + [zero8, zero8] + bias8,
                      [jnp.zeros((HEAD_SIZE, LANES), F32)] + gn, s0_s,
                      n_batches=1, steps=DEC_SEQ, name="wkv_scan_sample")
    rwkv_out = _rwkv_out(y_p.reshape(SEQ * HEAD_SIZE, LANES), _from_chain_sample(y_s), g,
                         _head_minor(w_rwkv_out[0], 0).astype(BF16))

    wr = jnp.concatenate([w_route_group[0], w_route_expert[0],
                          jnp.zeros((D_MODEL, LANES - N_GROUPS - N_EXPERTS), F32)], axis=1)
    br = jnp.concatenate([b_route_group[0], b_route_expert[0],
                          jnp.zeros((LANES - N_GROUPS - N_EXPERTS,), F32)])[None, :]
    h, hn, ridx, rw = _mix_route(conv_out, rwkv_out, z_gate, xp, xs, w_mix_out[0].astype(BF16),
                                 norm_ffn, wr.astype(BF16), br)

    dest, plan = _dispatch_plan(ridx[:, :TOP_K])
    yb = _experts(plan, hn, w_exp_gate[0], w_exp_up[0], w_exp_down[0])
    p_all = jnp.concatenate([p_prompt[0].reshape(N_PROMPT, PLE_DIM),
                             p_sample[0].transpose(1, 0, 2).reshape(N_SAMPLE, PLE_DIM)], axis=0)
    y_p2, y_s2 = _combine(dest, yb, h, rw, p_all, w_ple_gate[0].astype(BF16), w_ple_proj[0].astype(BF16),
                          norm_final[None, :])

    y_prompt = y_p2.reshape(BATCH, SEQ, D_MODEL)
    y_sample = y_s2.reshape(DEC_SEQ, DEC_BATCH, D_MODEL).transpose(1, 0, 2)
    tiles_per_seq = SEQ // TM
    seq_last = jnp.arange(BATCH) * tiles_per_seq + tiles_per_seq - 1

    conv_p = ulast.reshape(-1, SUBLANES, CONV_WIDTH)[seq_last, SUBLANES - 2:, :][None]
    conv_s = us.reshape(2, DEC_BATCH, CONV_WIDTH).transpose(1, 0, 2)[None]
    last_rkv = jnp.concatenate([rk_chain[:, :, SEQ - 1, :], v4[:, SEQ - 1]], axis=0)
    lm = last_rkv.reshape(3, HEAD_SIZE, BATCH, N_HEADS).transpose(2, 0, 3, 1).reshape(BATCH, rw3)
    lt = last_l.reshape(-1, SUBLANES, LORA_W)[seq_last, SUBLANES - 1, :n_lr]
    shift_p = jnp.concatenate([lm, lt], axis=1)[None]
    shift_s = jnp.concatenate([sraw_rkv, sraw_l[:, :n_lr]], axis=1)[None]
    wkv_p = sf_p.reshape(HEAD_SIZE, HEAD_SIZE, BATCH, N_HEADS).transpose(2, 3, 1, 0)[None]
    wkv_s = sf_s.reshape(ng, HEAD_SIZE, HEAD_SIZE, CHAIN_B, N_HEADS).transpose(0, 3, 4, 2, 1)
    wkv_s = wkv_s.reshape(DEC_BATCH, N_HEADS, HEAD_SIZE, HEAD_SIZE)[None]
    return (y_prompt, y_sample, conv_p, shift_p, wkv_p, conv_s, shift_s, wkv_s)
```

```python
import functools

import jax
import jax.numpy as jnp
from jax import lax
from jax.experimental import pallas as pl
from jax.experimental.pallas import tpu as pltpu

D_MODEL = 2048
BATCH = 4
SEQ = 2048
DEC_BATCH = 128
DEC_SEQ = 4
CONV_WIDTH = 1024
RWKV_WIDTH = 2048
HEAD_SIZE = 64
N_HEADS = RWKV_WIDTH // HEAD_SIZE
DECAY_LORA = 96
A_LORA = 96
GATE_LORA = 256
RWKV_PROJ = 3 * RWKV_WIDTH + DECAY_LORA + A_LORA + GATE_LORA
N_GROUPS = 8
EXPERTS_PER_GROUP = 8
N_EXPERTS = N_GROUPS * EXPERTS_PER_GROUP
TOP_K = 2
D_EXPERT = 512
MOE_BLOCK = 128
PLE_DIM = 256
RMS_EPS = 1e-6
GN_EPS = 64e-5

N_PROMPT = BATCH * SEQ
N_SAMPLE = DEC_BATCH * DEC_SEQ
N_TOK = N_PROMPT + N_SAMPLE
LANES = 128
SUBLANES = 8
TM = 512
TM_S = 256
TM_T = 256
TM_L = 128
CHAIN_B = LANES // N_HEADS
SCAN_TT = 64
SCAN_TB = SUBLANES
SCAN_ACC = 2
GATHER_AHEAD = 3
N_XBUF = GATHER_AHEAD + 1
LORA_W = 512
EXP_M05 = 0.6065306597126334
F32 = jnp.float32
BF16 = jnp.bfloat16
_NT = (((1,), (1,)), ((), ()))


def _sigmoid(x):
    return 1.0 / (1.0 + jnp.exp(-x))


def _cparams(*sem, vmem_mb=None):
    kw = dict(dimension_semantics=sem)
    if vmem_mb is not None:
        kw["vmem_limit_bytes"] = vmem_mb * 1024 * 1024
    return pltpu.CompilerParams(**kw)


def _resident(shape):
    nd = len(shape)
    return pl.BlockSpec(shape, lambda *_: (0,) * nd, pipeline_mode=pl.Buffered(1))


def _chain_tile(zt, half, c):
    start = c * N_HEADS
    rows = pl.ds(start if isinstance(c, int) else pl.multiple_of(start, N_HEADS), N_HEADS)
    return jnp.concatenate([zt[bb, half, rows, :] for bb in range(CHAIN_B)], axis=0)


def _store_chain_ct(zt, out_ref, halves, c_lo=0, n_c=HEAD_SIZE):
    for half in range(halves):
        for ci in range(n_c):
            out_ref[c_lo + ci, half * LANES:(half + 1) * LANES, :] = _chain_tile(zt, half, c_lo + ci).T


def _store_chain_tc(zt, out_ref, halves, c_lo=0, n_c=HEAD_SIZE):
    for half in range(halves):
        for ci in range(n_c):
            rows = pl.ds(half * LANES * HEAD_SIZE + c_lo + ci, LANES, stride=HEAD_SIZE)
            out_ref[rows, :] = _chain_tile(zt, half, c_lo + ci).T


def _load_chain_tc(y_ref, yt, halves):
    for half in range(halves):
        for v in range(HEAD_SIZE):
            mt = y_ref[pl.ds(half * LANES * HEAD_SIZE + v, LANES, stride=HEAD_SIZE), :].T
            for bb in range(CHAIN_B):
                yt[bb, half, v * N_HEADS:(v + 1) * N_HEADS, :] = mt[bb * N_HEADS:(bb + 1) * N_HEADS, :]


def _norm_body(xp_ref, xs_ref, g_ref, o_ref, *, n_prompt_tiles):
    i = pl.program_id(0)

    def f(x):
        ms = jnp.mean(x * x, axis=-1, keepdims=True)
        return (x * lax.rsqrt(ms + RMS_EPS) * g_ref[...]).astype(o_ref.dtype)

    @pl.when(i < n_prompt_tiles)
    def _():
        o_ref[...] = f(xp_ref[...])

    @pl.when(i >= n_prompt_tiles)
    def _():
        o_ref[...] = f(xs_ref[...])


def _norm_cast(xp, xs, g):
    npt = N_PROMPT // TM
    return pl.pallas_call(
        functools.partial(_norm_body, n_prompt_tiles=npt),
        grid=(N_TOK // TM,),
        in_specs=[pl.BlockSpec((TM, D_MODEL), lambda i: (jnp.minimum(i, npt - 1), 0)),
                  pl.BlockSpec((TM, D_MODEL), lambda i: (jnp.maximum(i - npt, 0), 0)),
                  pl.BlockSpec((1, D_MODEL), lambda i: (0, 0))],
        out_specs=pl.BlockSpec((TM, D_MODEL), lambda i: (i, 0)),
        out_shape=jax.ShapeDtypeStruct((N_TOK, D_MODEL), BF16),
        compiler_params=_cparams("arbitrary"),
        name="norm_cast",
    )(xp, xs, g)


def _inproj_body(xn_ref, w_ref, mu_ref, st_ref, z_ref, last_ref, sraw_ref, wb_ref, carry_ref, *,
                 shift_lo, shift_hi, tile_lo, n_prompt_tiles, tiles_per_seq):
    j = pl.program_id(0)
    i = pl.program_id(1) + tile_lo

    @pl.when(pl.program_id(1) == 0)
    def _():
        wb_ref[...] = w_ref[...].astype(BF16)

    z = jnp.dot(xn_ref[...], wb_ref[...], preferred_element_type=F32)
    tm = z.shape[0]
    last_ref[...] = z[tm - SUBLANES:tm]
    shifted = jnp.logical_and(j >= shift_lo, j < shift_hi)
    is_prompt = i < n_prompt_tiles

    @pl.when(jnp.logical_not(shifted))
    def _():
        z_ref[...] = z

    @pl.when(jnp.logical_and(shifted, is_prompt))
    def _():
        @pl.when(i % tiles_per_seq == 0)
        def _():
            carry_ref[...] = jnp.zeros_like(carry_ref)

        prev = pltpu.roll(z, 1, 0)
        row = lax.broadcasted_iota(jnp.int32, (tm, 1), 0)
        prev = jnp.where(row == 0, carry_ref[SUBLANES - 1:SUBLANES, :], prev)
        z_ref[...] = z + mu_ref[...] * (prev - z)
        carry_ref[...] = z[tm - SUBLANES:tm]

    @pl.when(jnp.logical_and(shifted, jnp.logical_not(is_prompt)))
    def _():
        prev = jnp.concatenate([st_ref[...], z[:tm - DEC_BATCH]], axis=0)
        z_ref[...] = z + mu_ref[...] * (prev - z)

    @pl.when(jnp.logical_not(is_prompt))
    def _():
        sraw_ref[...] = z[tm - DEC_BATCH:tm]


def _inproj(xn, w, mu, st, *, tn, col_blk_off, n_col_blocks, shift_lo, shift_hi, tile_lo, n_tiles, name):
    n_out = tn * n_col_blocks
    body = functools.partial(_inproj_body, shift_lo=shift_lo, shift_hi=shift_hi, tile_lo=tile_lo,
                             n_prompt_tiles=N_PROMPT // TM, tiles_per_seq=SEQ // TM)
    return pl.pallas_call(
        body,
        grid=(n_col_blocks, n_tiles),
        in_specs=[pl.BlockSpec((TM, D_MODEL), lambda j, i: (i + tile_lo, 0)),
                  pl.BlockSpec((D_MODEL, tn), lambda j, i: (0, j + col_blk_off)),
                  pl.BlockSpec((1, tn), lambda j, i: (0, j)),
                  pl.BlockSpec((DEC_BATCH, tn), lambda j, i: (0, j))],
        out_specs=[pl.BlockSpec((TM, tn), lambda j, i: (i, j)),
                   pl.BlockSpec((SUBLANES, tn), lambda j, i: (i, j)),
                   pl.BlockSpec((DEC_BATCH, tn), lambda j, i: (0, j))],
        out_shape=[jax.ShapeDtypeStruct((n_tiles * TM, n_out), F32),
                   jax.ShapeDtypeStruct((n_tiles * SUBLANES, n_out), F32),
                   jax.ShapeDtypeStruct((DEC_BATCH, n_out), F32)],
        scratch_shapes=[pltpu.VMEM((D_MODEL, tn), BF16), pltpu.VMEM((SUBLANES, tn), F32)],
        compiler_params=_cparams("arbitrary", "arbitrary", vmem_mb=48),
        name=name,
    )(xn, w, mu, st)


def _inproj_t_body(xn_ref, wt_ref, o_ref, zt, *, time_major, n_tb):
    tb = pl.program_id(1)
    b = pl.program_id(2)
    halves = TM_T // LANES
    n_c = HEAD_SIZE // CHAIN_B

    def matmul():
        z = lax.dot_general(wt_ref[...], xn_ref[...], _NT, preferred_element_type=F32)
        for half in range(halves):
            zt[tb % 2, b, half] = z[:, half * LANES:(half + 1) * LANES]

    def retile():
        store = _store_chain_tc if time_major else _store_chain_ct
        store(zt.at[1 - tb % 2], o_ref, halves, c_lo=b * n_c, n_c=n_c)

    @pl.when(tb == 0)
    def _():
        matmul()

    @pl.when(jnp.logical_and(tb > 0, tb < n_tb))
    def _():
        retile()
        matmul()

    @pl.when(tb == n_tb)
    def _():
        retile()


def _inproj_t(xn, wt, *, time_major, name):
    n_tb = SEQ // TM_T
    n = wt.shape[0] // RWKV_WIDTH

    def prev_tb(tb):
        return jnp.maximum(tb - 1, 0)

    if time_major:
        out_spec = pl.BlockSpec((None, TM_T * HEAD_SIZE, LANES), lambda j, tb, b: (j, prev_tb(tb), 0))
        out_shape = jax.ShapeDtypeStruct((n, SEQ * HEAD_SIZE, LANES), F32)
    else:
        out_spec = pl.BlockSpec((None, HEAD_SIZE, TM_T, LANES), lambda j, tb, b: (j, 0, prev_tb(tb), 0))
        out_shape = jax.ShapeDtypeStruct((n, HEAD_SIZE, SEQ, LANES), F32)
    return pl.pallas_call(
        functools.partial(_inproj_t_body, time_major=time_major, n_tb=n_tb),
        grid=(n, n_tb + 1, CHAIN_B),
        in_specs=[pl.BlockSpec((TM_T, D_MODEL), lambda j, tb, b: (b * n_tb + jnp.minimum(tb, n_tb - 1), 0)),
                  pl.BlockSpec((RWKV_WIDTH, D_MODEL), lambda j, tb, b: (j, 0), pipeline_mode=pl.Buffered(1))],
        out_specs=out_spec,
        out_shape=out_shape,
        scratch_shapes=[pltpu.VMEM((2, CHAIN_B, TM_T // LANES, RWKV_WIDTH, LANES), F32)],
        compiler_params=_cparams("arbitrary", "arbitrary", "arbitrary", vmem_mb=56),
        name=name,
    )(xn, wt)


def _conv_body(xn_ref, wc_ref, cw_ref, sc_ref, wco_ref, wgc_ref, o_ref, ulast_ref, us_ref, carry_ref, *,
               n_prompt_tiles, tiles_per_seq):
    i = pl.program_id(0)
    gate_b = jnp.dot(xn_ref[...], wc_ref[:, 0:CONV_WIDTH], preferred_element_type=F32)
    u = (jnp.dot(xn_ref[...], wc_ref[:, CONV_WIDTH:2 * CONV_WIDTH], preferred_element_type=F32)
         * jnp.dot(xn_ref[...], wc_ref[:, 2 * CONV_WIDTH:3 * CONV_WIDTH], preferred_element_type=F32))
    tm = u.shape[0]
    ulast_ref[...] = u[tm - SUBLANES:tm]
    w0 = cw_ref[0:1, :]
    w1 = cw_ref[1:2, :]
    w2 = cw_ref[2:3, :]

    def finish(p1, p2):
        conv = w0 * p2 + w1 * p1 + w2 * u
        y = (gate_b * conv).astype(BF16)
        g_conv = jnp.dot(xn_ref[...], wgc_ref[...], preferred_element_type=F32)
        o_ref[...] = _sigmoid(g_conv) * jnp.dot(y, wco_ref[...], preferred_element_type=F32)

    @pl.when(i < n_prompt_tiles)
    def _():
        @pl.when(i % tiles_per_seq == 0)
        def _():
            carry_ref[...] = jnp.zeros_like(carry_ref)

        row = lax.broadcasted_iota(jnp.int32, (tm, 1), 0)
        c1 = carry_ref[SUBLANES - 1:SUBLANES, :]
        c2 = carry_ref[SUBLANES - 2:SUBLANES - 1, :]
        p1 = jnp.where(row == 0, c1, pltpu.roll(u, 1, 0))
        p2 = jnp.where(row == 0, c2, jnp.where(row == 1, c1, pltpu.roll(u, 2, 0)))
        carry_ref[...] = u[tm - SUBLANES:tm]
        finish(p1, p2)

    @pl.when(i >= n_prompt_tiles)
    def _():
        p1 = jnp.concatenate([sc_ref[DEC_BATCH:2 * DEC_BATCH, :], u[:tm - DEC_BATCH]], axis=0)
        p2 = jnp.concatenate([sc_ref[...], u[:tm - 2 * DEC_BATCH]], axis=0)
        us_ref[...] = u[tm - 2 * DEC_BATCH:tm]
        finish(p1, p2)


def _conv_branch(xn, wc, conv_w, sc, wco, wgc):
    n_tiles = N_TOK // TM
    body = functools.partial(_conv_body, n_prompt_tiles=N_PROMPT // TM, tiles_per_seq=SEQ // TM)
    return pl.pallas_call(
        body,
        grid=(n_tiles,),
        in_specs=[pl.BlockSpec((TM, D_MODEL), lambda i: (i, 0)),
                  _resident((D_MODEL, 3 * CONV_WIDTH)),
                  pl.BlockSpec((3, CONV_WIDTH), lambda i: (0, 0)),
                  pl.BlockSpec((2 * DEC_BATCH, CONV_WIDTH), lambda i: (0, 0)),
                  _resident((CONV_WIDTH, D_MODEL)), _resident((D_MODEL, D_MODEL))],
        out_specs=[pl.BlockSpec((TM, D_MODEL), lambda i: (i, 0)),
                   pl.BlockSpec((SUBLANES, CONV_WIDTH), lambda i: (i, 0)),
                   pl.BlockSpec((2 * DEC_BATCH, CONV_WIDTH), lambda i: (0, 0))],
        out_shape=[jax.ShapeDtypeStruct((N_TOK, D_MODEL), F32),
                   jax.ShapeDtypeStruct((n_tiles * SUBLANES, CONV_WIDTH), F32),
                   jax.ShapeDtypeStruct((2 * DEC_BATCH, CONV_WIDTH), F32)],
        scratch_shapes=[pltpu.VMEM((SUBLANES, CONV_WIDTH), F32)],
        compiler_params=_cparams("arbitrary", vmem_mb=48),
        name="conv_branch",
    )(xn, wc, conv_w, sc, wco, wgc)


def _lora_body(zl_ref, w2t_ref, a2t_ref, w2_ref, a2_ref, g2_ref,
               g_ref, wlc_ref, alc_ref, wls_ref, als_ref, zt, *, n_prompt_steps):
    s = pl.program_id(0)
    tw = jnp.tanh(zl_ref[:, 0:LANES]).astype(BF16)
    xa = zl_ref[:, 0:2 * LANES].astype(BF16)
    xg = zl_ref[:, LANES:LORA_W]
    g_ref[...] = jnp.dot(_sigmoid(xg).astype(BF16), g2_ref[...], preferred_element_type=F32)

    @pl.when(s < n_prompt_steps)
    def _():
        b = s % CHAIN_B
        zt[0, b, 0] = lax.dot_general(w2t_ref[...], tw, _NT, preferred_element_type=F32)
        zt[1, b, 0] = lax.dot_general(a2t_ref[...], xa, _NT, preferred_element_type=F32)

        @pl.when(b == CHAIN_B - 1)
        def _():
            _store_chain_ct(zt.at[0], wlc_ref, 1)
            _store_chain_ct(zt.at[1], alc_ref, 1)

    @pl.when(s >= n_prompt_steps)
    def _():
        wls_ref[...] = jnp.dot(tw, w2_ref[...], preferred_element_type=F32)
        als_ref[...] = jnp.dot(xa, a2_ref[...], preferred_element_type=F32)


def _lora(z_lora, w2t, a2t, w2p, a2p, g2b):
    n_tb = SEQ // TM_L
    nps = n_tb * CHAIN_B

    def row_blk(s):
        return jnp.where(s < nps, (s % CHAIN_B) * n_tb + s // CHAIN_B, s)

    chain_spec = pl.BlockSpec((HEAD_SIZE, TM_L, LANES), lambda s: (0, jnp.minimum(s // CHAIN_B, n_tb - 1), 0))
    samp_spec = pl.BlockSpec((TM_L, RWKV_WIDTH), lambda s: (jnp.maximum(s - nps, 0), 0))
    chain_shape = jax.ShapeDtypeStruct((HEAD_SIZE, SEQ, LANES), F32)
    samp_shape = jax.ShapeDtypeStruct((N_SAMPLE, RWKV_WIDTH), F32)
    return pl.pallas_call(
        functools.partial(_lora_body, n_prompt_steps=nps),
        grid=(N_TOK // TM_L,),
        in_specs=[pl.BlockSpec((TM_L, LORA_W), lambda s: (row_blk(s), 0)),
                  _resident(w2t.shape), _resident(a2t.shape), _resident(w2p.shape), _resident(a2p.shape),
                  _resident(g2b.shape)],
        out_specs=[pl.BlockSpec((TM_L, RWKV_WIDTH), lambda s: (row_blk(s), 0)),
                   chain_spec, chain_spec, samp_spec, samp_spec],
        out_shape=[jax.ShapeDtypeStruct((N_TOK, RWKV_WIDTH), F32),
                   chain_shape, chain_shape, samp_shape, samp_shape],
        scratch_shapes=[pltpu.VMEM((2, CHAIN_B, 1, RWKV_WIDTH, LANES), F32)],
        compiler_params=_cparams("arbitrary", vmem_mb=48),
        name="lora",
    )(z_lora, w2t, a2t, w2p, a2p, g2b)


def _scan_body(r_ref, k_ref, wl_ref, al_ref, v_ref, kk_ref, ka_ref, rk_ref, mur_ref, muk_ref, w0_ref, a0_ref,
               muv_ref, lw_ref, lb_ref, s0_ref,
               y_ref, s_ref, vec_ref, bon_ref, prevb_ref, prevv_ref, *, n_batches, steps):
    @pl.when(pl.program_id(1) == 0)
    def _():
        s_ref[...] = s0_ref[...]
        prevb_ref[...] = jnp.zeros_like(prevb_ref)
        prevv_ref[...] = jnp.zeros_like(prevv_ref)

    rows = HEAD_SIZE * SCAN_TB
    first_t = lax.broadcasted_iota(jnp.int32, (rows, 1), 0) % SCAN_TB == 0

    def batch(bi, carry):
        t0 = pl.multiple_of(bi * SCAN_TB, SCAN_TB)

        def load(ref):
            return ref[:, pl.ds(t0, SCAN_TB), :].reshape(rows, LANES)

        def cube(x):
            return x.reshape(HEAD_SIZE, SCAN_TB, LANES)

        def shifted(x, slot, mu_ref):
            prev = jnp.where(first_t, pltpu.roll(prevb_ref[slot], rows - (SCAN_TB - 1), 0), pltpu.roll(x, 1, 0))
            prevb_ref[slot] = x
            return x + mu_ref[...] * (prev - x)

        r = shifted(load(r_ref), 0, mur_ref)
        k = shifted(load(k_ref), 1, muk_ref)
        decay = jnp.exp(-EXP_M05 * _sigmoid(load(wl_ref) + w0_ref[...]))
        a = _sigmoid(load(al_ref) + a0_ref[...])
        kk = cube(k * kk_ref[...])
        nrm = jnp.sqrt(jnp.sum(kk * kk, axis=0))
        kk = kk * (1.0 / jnp.maximum(nrm, 1e-12))[None]
        kf = k * (1.0 + (a - 1.0) * ka_ref[...])
        vec_ref[0] = -kk
        vec_ref[1] = cube(decay)
        vec_ref[2] = kk * cube(a)
        vec_ref[3] = cube(kf)
        vec_ref[4] = cube(r)
        bon_ref[...] = jnp.sum(cube(r * kf * rk_ref[...]), axis=0)

        def step(tl, c):
            t = t0 + tl
            v_raw = v_ref[t]
            v = v_raw + muv_ref[...] * (prevv_ref[...] - v_raw)
            prevv_ref[...] = v_raw

            def row(j, kx):
                return vec_ref[j, kx, pl.ds(tl, HEAD_SIZE, stride=0), :]

            parts = [s_ref[kx] * row(0, kx) for kx in range(SCAN_ACC)]
            for kx in range(SCAN_ACC, HEAD_SIZE):
                parts[kx % SCAN_ACC] = parts[kx % SCAN_ACC] + s_ref[kx] * row(0, kx)
            sa = functools.reduce(lambda x, y: x + y, parts)

            parts = []
            for kx in range(HEAD_SIZE):
                sn = s_ref[kx] * row(1, kx) + sa * row(2, kx) + v * row(3, kx)
                s_ref[kx] = sn
                if kx < SCAN_ACC:
                    parts.append(sn * row(4, kx))
                else:
                    parts[kx % SCAN_ACC] = parts[kx % SCAN_ACC] + sn * row(4, kx)
            o = functools.reduce(lambda x, y: x + y, parts)

            mu = jnp.mean(o, axis=0, keepdims=True)
            dlt = o - mu
            var = jnp.mean(dlt * dlt, axis=0, keepdims=True)
            on = dlt * lax.rsqrt(var + GN_EPS) * lw_ref[...] + lb_ref[...]
            y_ref[t] = on + bon_ref[pl.ds(tl, HEAD_SIZE, stride=0), :] * v
            return c

        for tl in range(steps):
            step(tl, 0)
        return carry

    lax.fori_loop(0, n_batches, batch, 0)


def _scan(ct_seqs, v_seq, params8, params, s0, *, n_batches, steps, name):
    g = s0.shape[0]
    tt = n_batches * SCAN_TB
    ttv = v_seq[0].shape[1] if n_batches == 1 else n_batches * steps
    n_ti = ct_seqs[0][0].shape[2] // tt

    def spec(block, lead, tpos):
        def index(gi, ti):
            idx = [gi if lead is None else lead, 0, 0, 0]
            idx[tpos] = ti
            return tuple(idx)
        return pl.BlockSpec(block, index)

    ct_block = (None, HEAD_SIZE, tt, LANES)
    tv_block = (None, ttv, HEAD_SIZE, LANES)
    par8_spec = pl.BlockSpec((HEAD_SIZE * SCAN_TB, LANES), lambda gi, ti: (0, 0))
    par_spec = pl.BlockSpec((HEAD_SIZE, LANES), lambda gi, ti: (0, 0))
    st_spec = pl.BlockSpec((None, HEAD_SIZE, HEAD_SIZE, LANES), lambda gi, ti: (gi, 0, 0, 0))
    return pl.pallas_call(
        functools.partial(_scan_body, n_batches=n_batches, steps=steps),
        grid=(g, n_ti),
        in_specs=([spec(ct_block, lead, 2) for _, lead in ct_seqs] + [spec(tv_block, v_seq[1], 1)]
                  + [par8_spec] * len(params8) + [par_spec] * len(params) + [st_spec]),
        out_specs=[spec(tv_block, None, 1), st_spec],
        out_shape=[jax.ShapeDtypeStruct((g, n_ti * ttv, HEAD_SIZE, LANES), F32),
                   jax.ShapeDtypeStruct((g, HEAD_SIZE, HEAD_SIZE, LANES), F32)],
        scratch_shapes=[pltpu.VMEM((5, HEAD_SIZE, SCAN_TB, LANES), F32), pltpu.VMEM((SCAN_TB, LANES), F32),
                        pltpu.VMEM((2, HEAD_SIZE * SCAN_TB, LANES), F32), pltpu.VMEM((HEAD_SIZE, LANES), F32)],
        compiler_params=_cparams("arbitrary", "arbitrary", vmem_mb=48),
        name=name,
    )(*[a for a, _ in ct_seqs], v_seq[0], *params8, *params, s0)


def _sample_groups(x):
    return x.reshape(DEC_SEQ, DEC_BATCH // CHAIN_B, CHAIN_B, N_HEADS, HEAD_SIZE).transpose(1, 0, 2, 3, 4)


def _to_chain_sample_ct(x):
    x = _sample_groups(x).transpose(0, 4, 1, 2, 3).reshape(DEC_BATCH // CHAIN_B, HEAD_SIZE, DEC_SEQ, LANES)
    return jnp.pad(x, ((0, 0), (0, 0), (0, SCAN_TB - DEC_SEQ), (0, 0)))


def _to_chain_sample_tc(x):
    return _sample_groups(x).transpose(0, 1, 4, 2, 3).reshape(DEC_BATCH // CHAIN_B, DEC_SEQ, HEAD_SIZE, LANES)


def _from_chain_sample(y):
    ng = DEC_BATCH // CHAIN_B
    y = y.reshape(ng, DEC_SEQ, HEAD_SIZE, CHAIN_B, N_HEADS).transpose(1, 0, 3, 2, 4)
    return y.reshape(N_SAMPLE, RWKV_WIDTH)


def _param_chain(p):
    return jnp.tile(p.reshape(N_HEADS, HEAD_SIZE).T, (1, CHAIN_B))


def _param_chain8(p):
    return jnp.repeat(_param_chain(p), SCAN_TB, axis=0)


def _head_minor(w, axis):
    shape = w.shape
    w = w.reshape(shape[:axis] + (N_HEADS, HEAD_SIZE) + shape[axis + 1:])
    return jnp.swapaxes(w, axis, axis + 1).reshape(shape)


def _rwkv_out_body(yc_ref, ys_ref, g_ref, w_ref, o_ref, yt, *, n_prompt_steps):
    s = pl.program_id(0)

    def finish(y):
        o_ref[...] = jnp.dot((y * g_ref[...]).astype(BF16), w_ref[...], preferred_element_type=F32)

    @pl.when(s < n_prompt_steps)
    def _():
        b = s % CHAIN_B

        @pl.when(b == 0)
        def _():
            _load_chain_tc(yc_ref, yt, TM_T // LANES)

        finish(jnp.concatenate([yt[b, half].T for half in range(TM_T // LANES)], axis=0))

    @pl.when(s >= n_prompt_steps)
    def _():
        finish(ys_ref[...])


def _rwkv_out(y_chain, y_s, g, w):
    n_tb = SEQ // TM_T
    nps = n_tb * CHAIN_B

    def row_blk(s):
        return jnp.where(s < nps, (s % CHAIN_B) * n_tb + s // CHAIN_B, s)

    return pl.pallas_call(
        functools.partial(_rwkv_out_body, n_prompt_steps=nps),
        grid=(N_TOK // TM_T,),
        in_specs=[pl.BlockSpec((TM_T * HEAD_SIZE, LANES), lambda s: (jnp.minimum(s // CHAIN_B, n_tb - 1), 0)),
                  pl.BlockSpec((TM_T, RWKV_WIDTH), lambda s: (jnp.maximum(s - nps, 0), 0)),
                  pl.BlockSpec((TM_T, RWKV_WIDTH), lambda s: (row_blk(s), 0)),
                  _resident((RWKV_WIDTH, D_MODEL))],
        out_specs=pl.BlockSpec((TM_T, D_MODEL), lambda s: (row_blk(s), 0)),
        out_shape=jax.ShapeDtypeStruct((N_TOK, D_MODEL), F32),
        scratch_shapes=[pltpu.VMEM((CHAIN_B, TM_T // LANES, RWKV_WIDTH, LANES), F32)],
        compiler_params=_cparams("arbitrary", vmem_mb=56),
        name="rwkv_out",
    )(y_chain, y_s, g, w)


def _mix_body(co_ref, ro_ref, xn_ref, wgr_ref, xp_ref, xs_ref, wm_ref, nf_ref, wr_ref, br_ref,
              h_ref, hn_ref, ridx_ref, rw_ref, *, n_prompt_tiles):
    i = pl.program_id(0)
    g_rwkv = jnp.dot(xn_ref[...], wgr_ref[...], preferred_element_type=F32)
    mixed = co_ref[...] + _sigmoid(g_rwkv) * ro_ref[...]
    mo = jnp.dot(mixed.astype(BF16), wm_ref[...], preferred_element_type=F32)

    def finish(x):
        h = x + mo
        h_ref[...] = h
        ms = jnp.mean(h * h, axis=-1, keepdims=True)
        hn = h * lax.rsqrt(ms + RMS_EPS) * nf_ref[...]
        hnb = hn.astype(BF16)
        hn_ref[...] = hnb.reshape(hn.shape[0], D_MODEL // LANES, LANES)
        logits = jnp.dot(hnb, wr_ref[...], preferred_element_type=F32) + br_ref[...]
        tm = logits.shape[0]
        lane = lax.broadcasted_iota(jnp.int32, (tm, LANES), 1)
        neg = jnp.float32(-jnp.inf)
        gl = jnp.where(lane < N_GROUPS, logits, neg)
        gmax = jnp.max(gl, axis=-1, keepdims=True)
        g_idx = jnp.min(jnp.where(gl == gmax, lane, LANES), axis=-1, keepdims=True)
        g_w = 1.0 / jnp.sum(jnp.exp(gl - gmax), axis=-1, keepdims=True)
        lo = N_GROUPS + g_idx * EXPERTS_PER_GROUP
        el = jnp.where(jnp.logical_and(lane >= lo, lane < lo + EXPERTS_PER_GROUP), logits, neg)
        m1 = jnp.max(el, axis=-1, keepdims=True)
        i1 = jnp.min(jnp.where(el == m1, lane, LANES), axis=-1, keepdims=True)
        el2 = jnp.where(lane == i1, neg, el)
        m2 = jnp.max(el2, axis=-1, keepdims=True)
        i2 = jnp.min(jnp.where(el2 == m2, lane, LANES), axis=-1, keepdims=True)
        t2 = jnp.exp(m2 - m1)
        den = 1.0 + t2
        ridx_ref[...] = jnp.where(lane == 0, i1 - N_GROUPS, jnp.where(lane == 1, i2 - N_GROUPS, 0))
        rw_ref[...] = jnp.where(lane == 0, (1.0 / den) * g_w, jnp.where(lane == 1, (t2 / den) * g_w, 0.0))

    @pl.when(i < n_prompt_tiles)
    def _():
        finish(xp_ref[...])

    @pl.when(i >= n_prompt_tiles)
    def _():
        finish(xs_ref[...])


def _mix_route(conv_out, rwkv_out, xn, wgr, xp, xs, wm, nf, wr, br):
    npt = N_PROMPT // TM_S
    tok_spec = pl.BlockSpec((TM_S, D_MODEL), lambda i: (i, 0))
    small_spec = pl.BlockSpec((TM_S, LANES), lambda i: (i, 0))
    return pl.pallas_call(
        functools.partial(_mix_body, n_prompt_tiles=npt),
        grid=(N_TOK // TM_S,),
        in_specs=[tok_spec, tok_spec, tok_spec, _resident((D_MODEL, D_MODEL)),
                  pl.BlockSpec((TM_S, D_MODEL), lambda i: (jnp.minimum(i, npt - 1), 0)),
                  pl.BlockSpec((TM_S, D_MODEL), lambda i: (jnp.maximum(i - npt, 0), 0)),
                  _resident((D_MODEL, D_MODEL)),
                  pl.BlockSpec((1, D_MODEL), lambda i: (0, 0)),
                  _resident((D_MODEL, LANES)),
                  pl.BlockSpec((1, LANES), lambda i: (0, 0))],
        out_specs=[tok_spec, pl.BlockSpec((TM_S, D_MODEL // LANES, LANES), lambda i: (i, 0, 0)),
                   small_spec, small_spec],
        out_shape=[jax.ShapeDtypeStruct((N_TOK, D_MODEL), F32),
                   jax.ShapeDtypeStruct((N_TOK, D_MODEL // LANES, LANES), BF16),
                   jax.ShapeDtypeStruct((N_TOK, LANES), jnp.int32),
                   jax.ShapeDtypeStruct((N_TOK, LANES), F32)],
        compiler_params=_cparams("arbitrary", vmem_mb=56),
        name="mix_route",
    )(conv_out, rwkv_out, xn, wgr, xp, xs, wm, nf, wr, br)


def _expert_body(blk_e_ref, slot_tok_ref, nused_ref, first_ref, par_ref, next_e_ref,
                 hn_ref, wg_hbm, wu_hbm, wd_hbm, yb_ref,
                 xbuf, sem, wfg, wfu, wfd, wsem, wgb, wub, wdb):
    i = pl.program_id(0)
    nused = nused_ref[0]
    slot = i % N_XBUF

    def row_copy(blk, r, s):
        tok = slot_tok_ref[blk * MOE_BLOCK + r]
        return pltpu.make_async_copy(hn_ref.at[tok], xbuf.at[s, r], sem.at[s])

    def issue(blk, s):
        for r in range(MOE_BLOCK):
            row_copy(blk, r, s).start(priority=r % 2)

    def w_copies(e, s):
        return (pltpu.make_async_copy(wg_hbm.at[e], wfg.at[s], wsem.at[s]),
                pltpu.make_async_copy(wu_hbm.at[e], wfu.at[s], wsem.at[s]),
                pltpu.make_async_copy(wd_hbm.at[e], wfd.at[s], wsem.at[s]))

    @pl.when(jnp.logical_and(i == 0, nused > 0))
    def _():
        for c in w_copies(blk_e_ref[0], 0):
            c.start(priority=1)
        for a in range(GATHER_AHEAD):
            @pl.when(a < nused)
            def _():
                issue(a, a)

    @pl.when(i + GATHER_AHEAD < nused)
    def _():
        issue(i + GATHER_AHEAD, (i + GATHER_AHEAD) % N_XBUF)

    @pl.when(i < nused)
    def _():
        @pl.when(first_ref[i] == 1)
        def _():
            ws = par_ref[i]
            for c in w_copies(blk_e_ref[i], ws):
                c.wait()

            @pl.when(next_e_ref[i] >= 0)
            def _():
                for c in w_copies(next_e_ref[i], 1 - ws):
                    c.start(priority=1)

            wgb[...] = wfg[ws].astype(BF16)
            wub[...] = wfu[ws].astype(BF16)
            wdb[...] = wfd[ws].astype(BF16)

        for r in range(MOE_BLOCK):
            row_copy(i, r, slot).wait()
        xe = xbuf[slot].reshape(MOE_BLOCK, D_MODEL)
        gate = jnp.dot(xe, wgb[...], preferred_element_type=F32)
        up = jnp.dot(xe, wub[...], preferred_element_type=F32)
        hdn = (gate * _sigmoid(gate)) * up
        yb = jnp.dot(hdn.astype(BF16), wdb[...], preferred_element_type=F32)
        yb_ref[...] = yb.reshape(MOE_BLOCK, D_MODEL // LANES, LANES)

    @pl.when(i >= nused)
    def _():
        yb_ref[...] = jnp.zeros_like(yb_ref)


def _experts(plan, hn, wg, wu, wd):
    blk_e, slot_tok, nused, first, par, next_e = plan
    n_blocks = blk_e.shape[0]
    any_spec = pl.BlockSpec(memory_space=pl.ANY)
    return pl.pallas_call(
        _expert_body,
        grid_spec=pltpu.PrefetchScalarGridSpec(
            num_scalar_prefetch=6,
            grid=(n_blocks,),
            in_specs=[any_spec, any_spec, any_spec, any_spec],
            out_specs=pl.BlockSpec((MOE_BLOCK, D_MODEL // LANES, LANES), lambda i, *_: (i, 0, 0)),
            scratch_shapes=[pltpu.VMEM((N_XBUF, MOE_BLOCK, D_MODEL // LANES, LANES), BF16),
                            pltpu.SemaphoreType.DMA((N_XBUF,)),
                            pltpu.VMEM((2, D_MODEL, D_EXPERT), F32),
                            pltpu.VMEM((2, D_MODEL, D_EXPERT), F32),
                            pltpu.VMEM((2, D_EXPERT, D_MODEL), F32),
                            pltpu.SemaphoreType.DMA((2,)),
                            pltpu.VMEM((D_MODEL, D_EXPERT), BF16),
                            pltpu.VMEM((D_MODEL, D_EXPERT), BF16),
                            pltpu.VMEM((D_EXPERT, D_MODEL), BF16)]),
        out_shape=jax.ShapeDtypeStruct((n_blocks * MOE_BLOCK, D_MODEL // LANES, LANES), F32),
        compiler_params=_cparams("arbitrary", vmem_mb=48),
        name="experts",
    )(blk_e, slot_tok, nused, first, par, next_e, hn, wg, wu, wd)


def _combine_body(dest_ref, yb_ref, h_ref, rw_ref, p_ref, wpg_ref, wpp_ref, nf_ref, yp_ref, ys_ref,
                  ybuf, sem):
    i = pl.program_id(0)
    tm = h_ref.shape[0]
    slot = i % 2

    def row_copy(tile, r, s, sl):
        d = dest_ref[(tile * tm + r) * TOP_K + s]
        return pltpu.make_async_copy(yb_ref.at[d], ybuf.at[sl, s, r], sem.at[sl])

    def issue(tile, sl):
        for r in range(tm):
            row_copy(tile, r, 0, sl).start()
            row_copy(tile, r, 1, sl).start()

    @pl.when(i == 0)
    def _():
        issue(0, 0)

    @pl.when(i + 1 < pl.num_programs(0))
    def _():
        issue(i + 1, 1 - slot)

    for r in range(tm):
        row_copy(i, r, 0, slot).wait()
        row_copy(i, r, 1, slot).wait()

    rw = rw_ref[...]
    y0 = ybuf[slot, 0].reshape(tm, D_MODEL)
    y1 = ybuf[slot, 1].reshape(tm, D_MODEL)
    h2 = h_ref[...] + (y0 * rw[:, 0:1] + y1 * rw[:, 1:2])
    gate = _sigmoid(jnp.dot(h2.astype(BF16), wpg_ref[...], preferred_element_type=F32))
    pp = jnp.dot(p_ref[...].astype(BF16), wpp_ref[...], preferred_element_type=F32)
    h3 = h2 + gate * pp
    ms = jnp.mean(h3 * h3, axis=-1, keepdims=True)
    y = h3 * lax.rsqrt(ms + RMS_EPS) * nf_ref[...]

    @pl.when(i < N_PROMPT // TM_S)
    def _():
        yp_ref[...] = y

    @pl.when(i >= N_PROMPT // TM_S)
    def _():
        ys_ref[...] = y


def _combine(dest, yb, h, rw, p_all, wpg, wpp, nf):
    npt = N_PROMPT // TM_S
    return pl.pallas_call(
        _combine_body,
        grid_spec=pltpu.PrefetchScalarGridSpec(
            num_scalar_prefetch=1,
            grid=(N_TOK // TM_S,),
            in_specs=[pl.BlockSpec(memory_space=pl.ANY),
                      pl.BlockSpec((TM_S, D_MODEL), lambda i, d: (i, 0)),
                      pl.BlockSpec((TM_S, LANES), lambda i, d: (i, 0)),
                      pl.BlockSpec((TM_S, PLE_DIM), lambda i, d: (i, 0)),
                      pl.BlockSpec((D_MODEL, D_MODEL), lambda i, d: (0, 0), pipeline_mode=pl.Buffered(1)),
                      pl.BlockSpec((PLE_DIM, D_MODEL), lambda i, d: (0, 0), pipeline_mode=pl.Buffered(1)),
                      pl.BlockSpec((1, D_MODEL), lambda i, d: (0, 0))],
            out_specs=[pl.BlockSpec((TM_S, D_MODEL), lambda i, d: (jnp.minimum(i, npt - 1), 0)),
                       pl.BlockSpec((TM_S, D_MODEL), lambda i, d: (jnp.maximum(i - npt, 0), 0))],
            scratch_shapes=[pltpu.VMEM((2, TOP_K, TM_S, D_MODEL // LANES, LANES), F32),
                            pltpu.SemaphoreType.DMA((2,))]),
        out_shape=[jax.ShapeDtypeStruct((N_PROMPT, D_MODEL), F32),
                   jax.ShapeDtypeStruct((N_SAMPLE, D_MODEL), F32)],
        compiler_params=_cparams("arbitrary", vmem_mb=48),
        name="combine_ple",
    )(dest, yb, h, rw, p_all, wpg, wpp, nf)


def _dispatch_plan(eidx):
    n_assign = N_TOK * TOP_K
    e_flat = eidx.reshape(n_assign)
    onehot = (e_flat[:, None] == jnp.arange(N_EXPERTS, dtype=jnp.int32)[None, :]).astype(jnp.int32)
    csum = jnp.cumsum(onehot, axis=0)
    counts = csum[-1]
    rank = jnp.sum(csum * onehot, axis=1) - 1
    padded = (counts + MOE_BLOCK - 1) // MOE_BLOCK * MOE_BLOCK
    pad_end = jnp.cumsum(padded)
    pad_start = pad_end - padded
    dest = pad_start[e_flat] + rank
    n_blocks = -(-n_assign // MOE_BLOCK) + N_EXPERTS
    tok = jnp.arange(n_assign, dtype=jnp.int32) // TOP_K
    pad_tok = jnp.arange(n_blocks * MOE_BLOCK, dtype=jnp.int32) % N_TOK
    slot_tok = pad_tok.at[dest].set(tok)
    blk = jnp.arange(n_blocks, dtype=jnp.int32)
    blk_e = jnp.minimum(jnp.searchsorted(pad_end, blk * MOE_BLOCK, side="right"), N_EXPERTS - 1).astype(jnp.int32)
    nused = (pad_end[-1] // MOE_BLOCK).astype(jnp.int32)
    prev_e = jnp.concatenate([jnp.full((1,), -1, jnp.int32), blk_e[:-1]])
    first = jnp.logical_and(blk < nused, blk_e != prev_e)
    par = ((jnp.cumsum(first.astype(jnp.int32)) - 1) % 2).astype(jnp.int32)
    idx_first = jnp.where(first, blk, n_blocks)
    later = jnp.concatenate([lax.cummin(idx_first[::-1])[::-1][1:], jnp.full((1,), n_blocks, jnp.int32)])
    next_e = jnp.where(later < n_blocks, blk_e[jnp.minimum(later, n_blocks - 1)], -1).astype(jnp.int32)
    plan = (blk_e, slot_tok, nused.reshape(1), first.astype(jnp.int32), par, next_e)
    return dest.astype(jnp.int32), plan


def kernel(x_prompt, x_sample, state_conv, state_shift, state_wkv, p_prompt, p_sample, norm_mix, w_in, conv_w, w_conv_out, shift_mu, w0, w2, a0, a2, g2, k_k, k_a, r_k, lnx_w, lnx_b, w_rwkv_out, w_mix_out, norm_ffn, w_route_group, b_route_group, w_route_expert, b_route_expert, w_exp_gate, w_exp_up, w_exp_down, w_ple_proj, w_ple_gate, norm_final):
    c3 = 3 * CONV_WIDTH
    rw3 = 3 * RWKV_WIDTH
    n_tiles = N_TOK // TM
    xp = x_prompt.reshape(N_PROMPT, D_MODEL)
    xs = x_sample.transpose(1, 0, 2).reshape(N_SAMPLE, D_MODEL)
    win = w_in[0]
    mu = shift_mu[0]
    st = state_shift[0]

    xn = _norm_cast(xp, xs, norm_mix)
    rkv_s, _, sraw_rkv = _inproj(xn, win, mu[None, :rw3], st[:, :rw3],
                                 tn=1024, col_blk_off=c3 // 1024, n_col_blocks=rw3 // 1024,
                                 shift_lo=0, shift_hi=rw3 // 1024,
                                 tile_lo=N_PROMPT // TM, n_tiles=N_SAMPLE // TM, name="inproj_rkv_sample")
    n_lr = RWKV_PROJ - rw3
    mu_l = jnp.concatenate([mu[rw3:], jnp.zeros((LORA_W - n_lr,), F32)])[None, :]
    st_l = jnp.concatenate([st[:, rw3:], jnp.zeros((DEC_BATCH, LORA_W - n_lr), F32)], axis=1)
    z_lora, last_l, sraw_l = _inproj(xn, win, mu_l, st_l, tn=LORA_W, col_blk_off=(c3 + rw3) // LORA_W,
                                     n_col_blocks=1, shift_lo=0, shift_hi=1,
                                     tile_lo=0, n_tiles=n_tiles, name="inproj_lora")
    w_gc = win[:, c3 + RWKV_PROJ:c3 + RWKV_PROJ + D_MODEL].astype(BF16)
    w_gr = win[:, c3 + RWKV_PROJ + D_MODEL:].astype(BF16)
    w_rk = _head_minor(win[:, c3:c3 + 2 * RWKV_WIDTH].reshape(D_MODEL, 2, RWKV_WIDTH), 2)
    rk_chain = _inproj_t(xn, w_rk.reshape(D_MODEL, 2 * RWKV_WIDTH).T.astype(BF16),
                         time_major=False, name="inproj_t_rk")
    v_chain = _inproj_t(xn, _head_minor(win[:, c3 + 2 * RWKV_WIDTH:c3 + rw3], 1).T.astype(BF16),
                        time_major=True, name="inproj_t_v")

    sc = state_conv[0].transpose(1, 0, 2).reshape(2 * DEC_BATCH, CONV_WIDTH)
    conv_out, ulast, us = _conv_branch(xn, win[:, :c3].astype(BF16), conv_w[0], sc,
                                       w_conv_out[0].astype(BF16), w_gc)

    def pad_rows(w, before, total):
        return jnp.pad(w, ((before, total - before - w.shape[0]), (0, 0))).astype(BF16)

    w2p = pad_rows(w2[0], 0, LANES)
    a2p = pad_rows(a2[0], DECAY_LORA, 2 * LANES)
    g2p = pad_rows(_head_minor(g2[0], 1), DECAY_LORA + A_LORA - LANES, LORA_W - LANES)
    g, wl_c, al_c, wl_s, al_s = _lora(z_lora, _head_minor(w2p, 1).T, _head_minor(a2p, 1).T, w2p, a2p, g2p)
    par8 = [_param_chain8(p) for p in (k_k[0], k_a[0], r_k[0].reshape(RWKV_WIDTH))]
    mu8 = [_param_chain8(mu[n * RWKV_WIDTH:(n + 1) * RWKV_WIDTH]) for n in range(2)]
    bias8 = [_param_chain8(w0[0]), _param_chain8(a0[0])]
    zero8 = jnp.zeros((HEAD_SIZE * SCAN_TB, LANES), F32)
    mu_v = _param_chain(mu[2 * RWKV_WIDTH:rw3])
    gn = [_param_chain(lnx_w[0]), _param_chain(lnx_b[0])]
    seqs_p = [(rk_chain, 0), (rk_chain, 1), (wl_c[None], None), (al_c[None], None)]
    v4 = v_chain.reshape(1, SEQ, HEAD_SIZE, LANES)
    s0_p = jnp.zeros((1, HEAD_SIZE, HEAD_SIZE, LANES), F32)
    y_p, sf_p = _scan(seqs_p, (v4, None), par8 + mu8 + bias8, [mu_v] + gn, s0_p,
                      n_batches=SCAN_TT // SCAN_TB, steps=SCAN_TB, name="wkv_scan_prompt")
    ng = DEC_BATCH // CHAIN_B
    s0_s = state_wkv[0].reshape(ng, CHAIN_B, N_HEADS, HEAD_SIZE, HEAD_SIZE).transpose(0, 4, 3, 1, 2)
    s0_s = s0_s.reshape(ng, HEAD_SIZE, HEAD_SIZE, LANES)
    seqs_s = [(_to_chain_sample_ct(rkv_s[:, n * RWKV_WIDTH:(n + 1) * RWKV_WIDTH]), None) for n in range(2)]
    seqs_s += [(_to_chain_sample_ct(wl_s), None), (_to_chain_sample_ct(al_s), None)]
    v_s = _to_chain_sample_tc(rkv_s[:, 2 * RWKV_WIDTH:rw3])
    y_s, sf_s = _scan(seqs_s, (v_s, None), par8 + [zero8, zero8] + bias8,
                      [jnp.zeros((HEAD_SIZE, LANES), F32)] + gn, s0_s,
                      n_batches=1, steps=DEC_SEQ, name="wkv_scan_sample")
    rwkv_out = _rwkv_out(y_p.reshape(SEQ * HEAD_SIZE, LANES), _from_chain_sample(y_s), g,
                         _head_minor(w_rwkv_out[0], 0).astype(BF16))

    wr = jnp.concatenate([w_route_group[0], w_route_expert[0],
                          jnp.zeros((D_MODEL, LANES - N_GROUPS - N_EXPERTS), F32)], axis=1)
    br = jnp.concatenate([b_route_group[0], b_route_expert[0],
                          jnp.zeros((LANES - N_GROUPS - N_EXPERTS,), F32)])[None, :]
    h, hn, ridx, rw = _mix_route(conv_out, rwkv_out, xn, w_gr, xp, xs, w_mix_out[0].astype(BF16),
                                 norm_ffn, wr.astype(BF16), br)

    dest, plan = _dispatch_plan(ridx[:, :TOP_K])
    yb = _experts(plan, hn, w_exp_gate[0], w_exp_up[0], w_exp_down[0])
    p_all = jnp.concatenate([p_prompt[0].reshape(N_PROMPT, PLE_DIM),
                             p_sample[0].transpose(1, 0, 2).reshape(N_SAMPLE, PLE_DIM)], axis=0)
    y_p2, y_s2 = _combine(dest, yb, h, rw, p_all, w_ple_gate[0].astype(BF16), w_ple_proj[0].astype(BF16),
                          norm_final[None, :])

    y_prompt = y_p2.reshape(BATCH, SEQ, D_MODEL)
    y_sample = y_s2.reshape(DEC_SEQ, DEC_BATCH, D_MODEL).transpose(1, 0, 2)
    tiles_per_seq = SEQ // TM
    seq_last = jnp.arange(BATCH) * tiles_per_seq + tiles_per_seq - 1

    conv_p = ulast.reshape(-1, SUBLANES, CONV_WIDTH)[seq_last, SUBLANES - 2:, :][None]
    conv_s = us.reshape(2, DEC_BATCH, CONV_WIDTH).transpose(1, 0, 2)[None]
    last_rkv = jnp.concatenate([rk_chain[:, :, SEQ - 1, :], v4[:, SEQ - 1]], axis=0)
    lm = last_rkv.reshape(3, HEAD_SIZE, BATCH, N_HEADS).transpose(2, 0, 3, 1).reshape(BATCH, rw3)
    lt = last_l.reshape(-1, SUBLANES, LORA_W)[seq_last, SUBLANES - 1, :n_lr]
    shift_p = jnp.concatenate([lm, lt], axis=1)[None]
    shift_s = jnp.concatenate([sraw_rkv, sraw_l[:, :n_lr]], axis=1)[None]
    wkv_p = sf_p.reshape(HEAD_SIZE, HEAD_SIZE, BATCH, N_HEADS).transpose(2, 3, 1, 0)[None]
    wkv_s = sf_s.reshape(ng, HEAD_SIZE, HEAD_SIZE, CHAIN_B, N_HEADS).transpose(0, 3, 4, 2, 1)
    wkv_s = wkv_s.reshape(DEC_BATCH, N_HEADS, HEAD_SIZE, HEAD_SIZE)[None]
    return (y_prompt, y_sample, conv_p, shift_p, wkv_p, conv_s, shift_s, wkv_s)
```

```python
import functools

import jax
import jax.numpy as jnp
from jax import lax
from jax.experimental import pallas as pl
from jax.experimental.pallas import tpu as pltpu

D_MODEL = 2048
BATCH = 4
SEQ = 2048
DEC_BATCH = 128
DEC_SEQ = 4
CONV_WIDTH = 1024
RWKV_WIDTH = 2048
HEAD_SIZE = 64
N_HEADS = RWKV_WIDTH // HEAD_SIZE
DECAY_LORA = 96
A_LORA = 96
GATE_LORA = 256
RWKV_PROJ = 3 * RWKV_WIDTH + DECAY_LORA + A_LORA + GATE_LORA
N_GROUPS = 8
EXPERTS_PER_GROUP = 8
N_EXPERTS = N_GROUPS * EXPERTS_PER_GROUP
TOP_K = 2
D_EXPERT = 512
MOE_BLOCK = 128
PLE_DIM = 256
RMS_EPS = 1e-6
GN_EPS = 64e-5

N_PROMPT = BATCH * SEQ
N_SAMPLE = DEC_BATCH * DEC_SEQ
N_TOK = N_PROMPT + N_SAMPLE
LANES = 128
SUBLANES = 8
TM = 512
TM_S = 256
TM_T = 256
TM_L = 128
CHAIN_B = LANES // N_HEADS
SCAN_TT = 64
SCAN_TB = SUBLANES
SCAN_ACC = 2
GATHER_AHEAD = 3
N_XBUF = GATHER_AHEAD + 1
LORA_W = 512
EXP_M05 = 0.6065306597126334
F32 = jnp.float32
BF16 = jnp.bfloat16
_NT = (((1,), (1,)), ((), ()))


def _sigmoid(x):
    return 1.0 / (1.0 + jnp.exp(-x))


def _cparams(*sem, vmem_mb=None):
    kw = dict(dimension_semantics=sem)
    if vmem_mb is not None:
        kw["vmem_limit_bytes"] = vmem_mb * 1024 * 1024
    return pltpu.CompilerParams(**kw)


def _resident(shape):
    nd = len(shape)
    return pl.BlockSpec(shape, lambda *_: (0,) * nd, pipeline_mode=pl.Buffered(1))


def _chain_tile(zt, half, c):
    start = c * N_HEADS
    rows = pl.ds(start if isinstance(c, int) else pl.multiple_of(start, N_HEADS), N_HEADS)
    return jnp.concatenate([zt[bb, half, rows, :] for bb in range(CHAIN_B)], axis=0)


def _store_chain_ct(zt, out_ref, halves, c_lo=0, n_c=HEAD_SIZE):
    for half in range(halves):
        for ci in range(n_c):
            out_ref[c_lo + ci, half * LANES:(half + 1) * LANES, :] = _chain_tile(zt, half, c_lo + ci).T


def _store_chain_tc(zt, out_ref, halves, c_lo=0, n_c=HEAD_SIZE):
    for half in range(halves):
        for ci in range(n_c):
            rows = pl.ds(half * LANES * HEAD_SIZE + c_lo + ci, LANES, stride=HEAD_SIZE)
            out_ref[rows, :] = _chain_tile(zt, half, c_lo + ci).T


def _load_chain_tc(y_ref, yt, halves):
    for half in range(halves):
        for v in range(HEAD_SIZE):
            mt = y_ref[pl.ds(half * LANES * HEAD_SIZE + v, LANES, stride=HEAD_SIZE), :].T
            for bb in range(CHAIN_B):
                yt[bb, half, v * N_HEADS:(v + 1) * N_HEADS, :] = mt[bb * N_HEADS:(bb + 1) * N_HEADS, :]


def _norm_body(xp_ref, xs_ref, g_ref, o_ref, *, n_prompt_tiles):
    i = pl.program_id(0)

    def f(x):
        ms = jnp.mean(x * x, axis=-1, keepdims=True)
        return (x * lax.rsqrt(ms + RMS_EPS) * g_ref[...]).astype(o_ref.dtype)

    @pl.when(i < n_prompt_tiles)
    def _():
        o_ref[...] = f(xp_ref[...])

    @pl.when(i >= n_prompt_tiles)
    def _():
        o_ref[...] = f(xs_ref[...])


def _norm_cast(xp, xs, g):
    npt = N_PROMPT // TM
    return pl.pallas_call(
        functools.partial(_norm_body, n_prompt_tiles=npt),
        grid=(N_TOK // TM,),
        in_specs=[pl.BlockSpec((TM, D_MODEL), lambda i: (jnp.minimum(i, npt - 1), 0)),
                  pl.BlockSpec((TM, D_MODEL), lambda i: (jnp.maximum(i - npt, 0), 0)),
                  pl.BlockSpec((1, D_MODEL), lambda i: (0, 0))],
        out_specs=pl.BlockSpec((TM, D_MODEL), lambda i: (i, 0)),
        out_shape=jax.ShapeDtypeStruct((N_TOK, D_MODEL), BF16),
        compiler_params=_cparams("arbitrary"),
        name="norm_cast",
    )(xp, xs, g)


def _inproj_body(xn_ref, w_ref, mu_ref, st_ref, z_ref, last_ref, sraw_ref, wb_ref, carry_ref, *,
                 shift_lo, shift_hi, tile_lo, n_prompt_tiles, tiles_per_seq):
    j = pl.program_id(0)
    i = pl.program_id(1) + tile_lo

    @pl.when(pl.program_id(1) == 0)
    def _():
        wb_ref[...] = w_ref[...].astype(BF16)

    z = jnp.dot(xn_ref[...], wb_ref[...], preferred_element_type=F32)
    tm = z.shape[0]
    last_ref[...] = z[tm - SUBLANES:tm]
    shifted = jnp.logical_and(j >= shift_lo, j < shift_hi)
    is_prompt = i < n_prompt_tiles

    @pl.when(jnp.logical_not(shifted))
    def _():
        z_ref[...] = z

    @pl.when(jnp.logical_and(shifted, is_prompt))
    def _():
        @pl.when(i % tiles_per_seq == 0)
        def _():
            carry_ref[...] = jnp.zeros_like(carry_ref)

        prev = pltpu.roll(z, 1, 0)
        row = lax.broadcasted_iota(jnp.int32, (tm, 1), 0)
        prev = jnp.where(row == 0, carry_ref[SUBLANES - 1:SUBLANES, :], prev)
        z_ref[...] = z + mu_ref[...] * (prev - z)
        carry_ref[...] = z[tm - SUBLANES:tm]

    @pl.when(jnp.logical_and(shifted, jnp.logical_not(is_prompt)))
    def _():
        prev = jnp.concatenate([st_ref[...], z[:tm - DEC_BATCH]], axis=0)
        z_ref[...] = z + mu_ref[...] * (prev - z)

    @pl.when(jnp.logical_not(is_prompt))
    def _():
        sraw_ref[...] = z[tm - DEC_BATCH:tm]


def _inproj(xn, w, mu, st, *, tn, col_blk_off, n_col_blocks, shift_lo, shift_hi, tile_lo, n_tiles, name):
    n_out = tn * n_col_blocks
    body = functools.partial(_inproj_body, shift_lo=shift_lo, shift_hi=shift_hi, tile_lo=tile_lo,
                             n_prompt_tiles=N_PROMPT // TM, tiles_per_seq=SEQ // TM)
    return pl.pallas_call(
        body,
        grid=(n_col_blocks, n_tiles),
        in_specs=[pl.BlockSpec((TM, D_MODEL), lambda j, i: (i + tile_lo, 0)),
                  pl.BlockSpec((D_MODEL, tn), lambda j, i: (0, j + col_blk_off)),
                  pl.BlockSpec((1, tn), lambda j, i: (0, j)),
                  pl.BlockSpec((DEC_BATCH, tn), lambda j, i: (0, j))],
        out_specs=[pl.BlockSpec((TM, tn), lambda j, i: (i, j)),
                   pl.BlockSpec((SUBLANES, tn), lambda j, i: (i, j)),
                   pl.BlockSpec((DEC_BATCH, tn), lambda j, i: (0, j))],
        out_shape=[jax.ShapeDtypeStruct((n_tiles * TM, n_out), F32),
                   jax.ShapeDtypeStruct((n_tiles * SUBLANES, n_out), F32),
                   jax.ShapeDtypeStruct((DEC_BATCH, n_out), F32)],
        scratch_shapes=[pltpu.VMEM((D_MODEL, tn), BF16), pltpu.VMEM((SUBLANES, tn), F32)],
        compiler_params=_cparams("arbitrary", "arbitrary", vmem_mb=48),
        name=name,
    )(xn, w, mu, st)


def _inproj_t_body(xn_ref, wt_ref, o_ref, zt, *, time_major, n_tb):
    tb = pl.program_id(1)
    b = pl.program_id(2)
    halves = TM_T // LANES
    n_c = HEAD_SIZE // CHAIN_B

    def matmul():
        z = lax.dot_general(wt_ref[...], xn_ref[...], _NT, preferred_element_type=F32)
        for half in range(halves):
            zt[tb % 2, b, half] = z[:, half * LANES:(half + 1) * LANES]

    def retile():
        store = _store_chain_tc if time_major else _store_chain_ct
        store(zt.at[1 - tb % 2], o_ref, halves, c_lo=b * n_c, n_c=n_c)

    @pl.when(tb == 0)
    def _():
        matmul()

    @pl.when(jnp.logical_and(tb > 0, tb < n_tb))
    def _():
        retile()
        matmul()

    @pl.when(tb == n_tb)
    def _():
        retile()


def _inproj_t(xn, wt, *, time_major, name):
    n_tb = SEQ // TM_T
    n = wt.shape[0] // RWKV_WIDTH

    def prev_tb(tb):
        return jnp.maximum(tb - 1, 0)

    if time_major:
        out_spec = pl.BlockSpec((None, TM_T * HEAD_SIZE, LANES), lambda j, tb, b: (j, prev_tb(tb), 0))
        out_shape = jax.ShapeDtypeStruct((n, SEQ * HEAD_SIZE, LANES), F32)
    else:
        out_spec = pl.BlockSpec((None, HEAD_SIZE, TM_T, LANES), lambda j, tb, b: (j, 0, prev_tb(tb), 0))
        out_shape = jax.ShapeDtypeStruct((n, HEAD_SIZE, SEQ, LANES), F32)
    return pl.pallas_call(
        functools.partial(_inproj_t_body, time_major=time_major, n_tb=n_tb),
        grid=(n, n_tb + 1, CHAIN_B),
        in_specs=[pl.BlockSpec((TM_T, D_MODEL), lambda j, tb, b: (b * n_tb + jnp.minimum(tb, n_tb - 1), 0)),
                  pl.BlockSpec((RWKV_WIDTH, D_MODEL), lambda j, tb, b: (j, 0), pipeline_mode=pl.Buffered(1))],
        out_specs=out_spec,
        out_shape=out_shape,
        scratch_shapes=[pltpu.VMEM((2, CHAIN_B, TM_T // LANES, RWKV_WIDTH, LANES), F32)],
        compiler_params=_cparams("arbitrary", "arbitrary", "arbitrary", vmem_mb=56),
        name=name,
    )(xn, wt)


def _conv_body(xn_ref, wc_ref, cw_ref, sc_ref, wco_ref, wgc_ref, o_ref, ulast_ref, us_ref, carry_ref, *,
               n_prompt_tiles, tiles_per_seq):
    i = pl.program_id(0)
    gate_b = jnp.dot(xn_ref[...], wc_ref[:, 0:CONV_WIDTH], preferred_element_type=F32)
    u = (jnp.dot(xn_ref[...], wc_ref[:, CONV_WIDTH:2 * CONV_WIDTH], preferred_element_type=F32)
         * jnp.dot(xn_ref[...], wc_ref[:, 2 * CONV_WIDTH:3 * CONV_WIDTH], preferred_element_type=F32))
    tm = u.shape[0]
    ulast_ref[...] = u[tm - SUBLANES:tm]
    w0 = cw_ref[0:1, :]
    w1 = cw_ref[1:2, :]
    w2 = cw_ref[2:3, :]

    def finish(p1, p2):
        conv = w0 * p2 + w1 * p1 + w2 * u
        y = (gate_b * conv).astype(BF16)
        g_conv = jnp.dot(xn_ref[...], wgc_ref[...], preferred_element_type=F32)
        o_ref[...] = _sigmoid(g_conv) * jnp.dot(y, wco_ref[...], preferred_element_type=F32)

    @pl.when(i < n_prompt_tiles)
    def _():
        @pl.when(i % tiles_per_seq == 0)
        def _():
            carry_ref[...] = jnp.zeros_like(carry_ref)

        row = lax.broadcasted_iota(jnp.int32, (tm, 1), 0)
        c1 = carry_ref[SUBLANES - 1:SUBLANES, :]
        c2 = carry_ref[SUBLANES - 2:SUBLANES - 1, :]
        p1 = jnp.where(row == 0, c1, pltpu.roll(u, 1, 0))
        p2 = jnp.where(row == 0, c2, jnp.where(row == 1, c1, pltpu.roll(u, 2, 0)))
        carry_ref[...] = u[tm - SUBLANES:tm]
        finish(p1, p2)

    @pl.when(i >= n_prompt_tiles)
    def _():
        p1 = jnp.concatenate([sc_ref[DEC_BATCH:2 * DEC_BATCH, :], u[:tm - DEC_BATCH]], axis=0)
        p2 = jnp.concatenate([sc_ref[...], u[:tm - 2 * DEC_BATCH]], axis=0)
        us_ref[...] = u[tm - 2 * DEC_BATCH:tm]
        finish(p1, p2)


def _conv_branch(xn, wc, conv_w, sc, wco, wgc):
    n_tiles = N_TOK // TM
    body = functools.partial(_conv_body, n_prompt_tiles=N_PROMPT // TM, tiles_per_seq=SEQ // TM)
    return pl.pallas_call(
        body,
        grid=(n_tiles,),
        in_specs=[pl.BlockSpec((TM, D_MODEL), lambda i: (i, 0)),
                  _resident((D_MODEL, 3 * CONV_WIDTH)),
                  pl.BlockSpec((3, CONV_WIDTH), lambda i: (0, 0)),
                  pl.BlockSpec((2 * DEC_BATCH, CONV_WIDTH), lambda i: (0, 0)),
                  _resident((CONV_WIDTH, D_MODEL)), _resident((D_MODEL, D_MODEL))],
        out_specs=[pl.BlockSpec((TM, D_MODEL), lambda i: (i, 0)),
                   pl.BlockSpec((SUBLANES, CONV_WIDTH), lambda i: (i, 0)),
                   pl.BlockSpec((2 * DEC_BATCH, CONV_WIDTH), lambda i: (0, 0))],
        out_shape=[jax.ShapeDtypeStruct((N_TOK, D_MODEL), F32),
                   jax.ShapeDtypeStruct((n_tiles * SUBLANES, CONV_WIDTH), F32),
                   jax.ShapeDtypeStruct((2 * DEC_BATCH, CONV_WIDTH), F32)],
        scratch_shapes=[pltpu.VMEM((SUBLANES, CONV_WIDTH), F32)],
        compiler_params=_cparams("arbitrary", vmem_mb=48),
        name="conv_branch",
    )(xn, wc, conv_w, sc, wco, wgc)


def _lora_body(xn_ref, wl_ref, mu_ref, st_ref, w2t_ref, a2t_ref, w2_ref, a2_ref, g2_ref,
               g_ref, wlc_ref, alc_ref, wls_ref, als_ref, last_ref, sraw_ref, zt, carry_p, carry_s, *,
               n_prompt_steps):
    s = pl.program_id(0)
    z = jnp.dot(xn_ref[...], wl_ref[...], preferred_element_type=F32)
    tm = z.shape[0]
    last_ref[...] = z[tm - SUBLANES:tm]

    def project(zl):
        tw = jnp.tanh(zl[:, 0:LANES]).astype(BF16)
        xa = zl[:, 0:2 * LANES].astype(BF16)
        g_ref[...] = jnp.dot(_sigmoid(zl[:, LANES:LORA_W]).astype(BF16), g2_ref[...], preferred_element_type=F32)
        return tw, xa

    @pl.when(s < n_prompt_steps)
    def _():
        b = s % CHAIN_B

        @pl.when(s < CHAIN_B)
        def _():
            carry_p[b] = jnp.zeros((SUBLANES, LORA_W), F32)

        row = lax.broadcasted_iota(jnp.int32, (tm, 1), 0)
        prev = jnp.where(row == 0, carry_p[b, SUBLANES - 1:SUBLANES, :], pltpu.roll(z, 1, 0))
        carry_p[b] = z[tm - SUBLANES:tm]
        tw, xa = project(z + mu_ref[...] * (prev - z))
        zt[0, b, 0] = lax.dot_general(w2t_ref[...], tw, _NT, preferred_element_type=F32)
        zt[1, b, 0] = lax.dot_general(a2t_ref[...], xa, _NT, preferred_element_type=F32)

        @pl.when(b == CHAIN_B - 1)
        def _():
            _store_chain_ct(zt.at[0], wlc_ref, 1)
            _store_chain_ct(zt.at[1], alc_ref, 1)

    @pl.when(s >= n_prompt_steps)
    def _():
        @pl.when(s == n_prompt_steps)
        def _():
            carry_s[...] = st_ref[...]

        prev = carry_s[...]
        carry_s[...] = z
        sraw_ref[...] = z
        tw, xa = project(z + mu_ref[...] * (prev - z))
        wls_ref[...] = jnp.dot(tw, w2_ref[...], preferred_element_type=F32)
        als_ref[...] = jnp.dot(xa, a2_ref[...], preferred_element_type=F32)


def _lora(xn, w_l, mu_l, st_l, w2t, a2t, w2p, a2p, g2b):
    n_tb = SEQ // TM_L
    nps = n_tb * CHAIN_B
    n_steps = N_TOK // TM_L

    def row_blk(s):
        return jnp.where(s < nps, (s % CHAIN_B) * n_tb + s // CHAIN_B, s)

    chain_spec = pl.BlockSpec((HEAD_SIZE, TM_L, LANES), lambda s: (0, jnp.minimum(s // CHAIN_B, n_tb - 1), 0))
    samp_spec = pl.BlockSpec((TM_L, RWKV_WIDTH), lambda s: (jnp.maximum(s - nps, 0), 0))
    chain_shape = jax.ShapeDtypeStruct((HEAD_SIZE, SEQ, LANES), F32)
    samp_shape = jax.ShapeDtypeStruct((N_SAMPLE, RWKV_WIDTH), F32)
    return pl.pallas_call(
        functools.partial(_lora_body, n_prompt_steps=nps),
        grid=(n_steps,),
        in_specs=[pl.BlockSpec((TM_L, D_MODEL), lambda s: (row_blk(s), 0)),
                  _resident(w_l.shape),
                  pl.BlockSpec((1, LORA_W), lambda s: (0, 0)),
                  pl.BlockSpec((DEC_BATCH, LORA_W), lambda s: (0, 0)),
                  _resident(w2t.shape), _resident(a2t.shape), _resident(w2p.shape), _resident(a2p.shape),
                  _resident(g2b.shape)],
        out_specs=[pl.BlockSpec((TM_L, RWKV_WIDTH), lambda s: (row_blk(s), 0)),
                   chain_spec, chain_spec, samp_spec, samp_spec,
                   pl.BlockSpec((SUBLANES, LORA_W), lambda s: (s, 0)),
                   pl.BlockSpec((DEC_BATCH, LORA_W), lambda s: (0, 0))],
        out_shape=[jax.ShapeDtypeStruct((N_TOK, RWKV_WIDTH), F32),
                   chain_shape, chain_shape, samp_shape, samp_shape,
                   jax.ShapeDtypeStruct((n_steps * SUBLANES, LORA_W), F32),
                   jax.ShapeDtypeStruct((DEC_BATCH, LORA_W), F32)],
        scratch_shapes=[pltpu.VMEM((2, CHAIN_B, 1, RWKV_WIDTH, LANES), F32),
                        pltpu.VMEM((CHAIN_B, SUBLANES, LORA_W), F32),
                        pltpu.VMEM((DEC_BATCH, LORA_W), F32)],
        compiler_params=_cparams("arbitrary", vmem_mb=48),
        name="lora",
    )(xn, w_l, mu_l, st_l, w2t, a2t, w2p, a2p, g2b)


def _scan_body(r_ref, k_ref, wl_ref, al_ref, v_ref, kk_ref, ka_ref, rk_ref, mur_ref, muk_ref, w0_ref, a0_ref,
               muv_ref, lw_ref, lb_ref, s0_ref,
               y_ref, s_ref, vec_ref, bon_ref, prevb_ref, prevv_ref, *, n_batches, steps):
    @pl.when(pl.program_id(1) == 0)
    def _():
        s_ref[...] = s0_ref[...]
        prevb_ref[...] = jnp.zeros_like(prevb_ref)
        prevv_ref[...] = jnp.zeros_like(prevv_ref)

    rows = HEAD_SIZE * SCAN_TB
    first_t = lax.broadcasted_iota(jnp.int32, (rows, 1), 0) % SCAN_TB == 0

    def batch(bi, carry):
        t0 = pl.multiple_of(bi * SCAN_TB, SCAN_TB)

        def load(ref):
            return ref[:, pl.ds(t0, SCAN_TB), :].reshape(rows, LANES)

        def cube(x):
            return x.reshape(HEAD_SIZE, SCAN_TB, LANES)

        def shifted(x, slot, mu_ref):
            prev = jnp.where(first_t, pltpu.roll(prevb_ref[slot], rows - (SCAN_TB - 1), 0), pltpu.roll(x, 1, 0))
            prevb_ref[slot] = x
            return x + mu_ref[...] * (prev - x)

        r = shifted(load(r_ref), 0, mur_ref)
        k = shifted(load(k_ref), 1, muk_ref)
        decay = jnp.exp(-EXP_M05 * _sigmoid(load(wl_ref) + w0_ref[...]))
        a = _sigmoid(load(al_ref) + a0_ref[...])
        kk = cube(k * kk_ref[...])
        nrm = jnp.sqrt(jnp.sum(kk * kk, axis=0))
        kk = kk * (1.0 / jnp.maximum(nrm, 1e-12))[None]
        kf = k * (1.0 + (a - 1.0) * ka_ref[...])
        vec_ref[0] = -kk
        vec_ref[1] = cube(decay)
        vec_ref[2] = kk * cube(a)
        vec_ref[3] = cube(kf)
        vec_ref[4] = cube(r)
        bon_ref[...] = jnp.sum(cube(r * kf * rk_ref[...]), axis=0)

        def step(tl, c):
            t = t0 + tl
            v_raw = v_ref[t]
            v = v_raw + muv_ref[...] * (prevv_ref[...] - v_raw)
            prevv_ref[...] = v_raw

            def row(j, kx):
                return vec_ref[j, kx, pl.ds(tl, HEAD_SIZE, stride=0), :]

            parts = [s_ref[kx] * row(0, kx) for kx in range(SCAN_ACC)]
            for kx in range(SCAN_ACC, HEAD_SIZE):
                parts[kx % SCAN_ACC] = parts[kx % SCAN_ACC] + s_ref[kx] * row(0, kx)
            sa = functools.reduce(lambda x, y: x + y, parts)

            parts = []
            for kx in range(HEAD_SIZE):
                sn = s_ref[kx] * row(1, kx) + sa * row(2, kx) + v * row(3, kx)
                s_ref[kx] = sn
                if kx < SCAN_ACC:
                    parts.append(sn * row(4, kx))
                else:
                    parts[kx % SCAN_ACC] = parts[kx % SCAN_ACC] + sn * row(4, kx)
            o = functools.reduce(lambda x, y: x + y, parts)

            mu = jnp.mean(o, axis=0, keepdims=True)
            dlt = o - mu
            var = jnp.mean(dlt * dlt, axis=0, keepdims=True)
            on = dlt * lax.rsqrt(var + GN_EPS) * lw_ref[...] + lb_ref[...]
            y_ref[t] = on + bon_ref[pl.ds(tl, HEAD_SIZE, stride=0), :] * v
            return c

        for tl in range(steps):
            step(tl, 0)
        return carry

    lax.fori_loop(0, n_batches, batch, 0)


def _scan(ct_seqs, v_seq, params8, params, s0, *, n_batches, steps, name):
    g = s0.shape[0]
    tt = n_batches * SCAN_TB
    ttv = v_seq[0].shape[1] if n_batches == 1 else n_batches * steps
    n_ti = ct_seqs[0][0].shape[2] // tt

    def spec(block, lead, tpos):
        def index(gi, ti):
            idx = [gi if lead is None else lead, 0, 0, 0]
            idx[tpos] = ti
            return tuple(idx)
        return pl.BlockSpec(block, index)

    ct_block = (None, HEAD_SIZE, tt, LANES)
    tv_block = (None, ttv, HEAD_SIZE, LANES)
    par8_spec = pl.BlockSpec((HEAD_SIZE * SCAN_TB, LANES), lambda gi, ti: (0, 0))
    par_spec = pl.BlockSpec((HEAD_SIZE, LANES), lambda gi, ti: (0, 0))
    st_spec = pl.BlockSpec((None, HEAD_SIZE, HEAD_SIZE, LANES), lambda gi, ti: (gi, 0, 0, 0))
    return pl.pallas_call(
        functools.partial(_scan_body, n_batches=n_batches, steps=steps),
        grid=(g, n_ti),
        in_specs=([spec(ct_block, lead, 2) for _, lead in ct_seqs] + [spec(tv_block, v_seq[1], 1)]
                  + [par8_spec] * len(params8) + [par_spec] * len(params) + [st_spec]),
        out_specs=[spec(tv_block, None, 1), st_spec],
        out_shape=[jax.ShapeDtypeStruct((g, n_ti * ttv, HEAD_SIZE, LANES), F32),
                   jax.ShapeDtypeStruct((g, HEAD_SIZE, HEAD_SIZE, LANES), F32)],
        scratch_shapes=[pltpu.VMEM((5, HEAD_SIZE, SCAN_TB, LANES), F32), pltpu.VMEM((SCAN_TB, LANES), F32),
                        pltpu.VMEM((2, HEAD_SIZE * SCAN_TB, LANES), F32), pltpu.VMEM((HEAD_SIZE, LANES), F32)],
        compiler_params=_cparams("arbitrary", "arbitrary", vmem_mb=48),
        name=name,
    )(*[a for a, _ in ct_seqs], v_seq[0], *params8, *params, s0)


def _sample_groups(x):
    return x.reshape(DEC_SEQ, DEC_BATCH // CHAIN_B, CHAIN_B, N_HEADS, HEAD_SIZE).transpose(1, 0, 2, 3, 4)


def _to_chain_sample_ct(x):
    x = _sample_groups(x).transpose(0, 4, 1, 2, 3).reshape(DEC_BATCH // CHAIN_B, HEAD_SIZE, DEC_SEQ, LANES)
    return jnp.pad(x, ((0, 0), (0, 0), (0, SCAN_TB - DEC_SEQ), (0, 0)))


def _to_chain_sample_tc(x):
    return _sample_groups(x).transpose(0, 1, 4, 2, 3).reshape(DEC_BATCH // CHAIN_B, DEC_SEQ, HEAD_SIZE, LANES)


def _from_chain_sample(y):
    ng = DEC_BATCH // CHAIN_B
    y = y.reshape(ng, DEC_SEQ, HEAD_SIZE, CHAIN_B, N_HEADS).transpose(1, 0, 3, 2, 4)
    return y.reshape(N_SAMPLE, RWKV_WIDTH)


def _param_chain(p):
    return jnp.tile(p.reshape(N_HEADS, HEAD_SIZE).T, (1, CHAIN_B))


def _param_chain8(p):
    return jnp.repeat(_param_chain(p), SCAN_TB, axis=0)


def _head_minor(w, axis):
    shape = w.shape
    w = w.reshape(shape[:axis] + (N_HEADS, HEAD_SIZE) + shape[axis + 1:])
    return jnp.swapaxes(w, axis, axis + 1).reshape(shape)


def _rwkv_out_body(yc_ref, ys_ref, g_ref, w_ref, o_ref, yt, *, n_prompt_steps):
    s = pl.program_id(0)

    def finish(y):
        o_ref[...] = jnp.dot((y * g_ref[...]).astype(BF16), w_ref[...], preferred_element_type=F32)

    @pl.when(s < n_prompt_steps)
    def _():
        b = s % CHAIN_B

        @pl.when(b == 0)
        def _():
            _load_chain_tc(yc_ref, yt, TM_T // LANES)

        finish(jnp.concatenate([yt[b, half].T for half in range(TM_T // LANES)], axis=0))

    @pl.when(s >= n_prompt_steps)
    def _():
        finish(ys_ref[...])


def _rwkv_out(y_chain, y_s, g, w):
    n_tb = SEQ // TM_T
    nps = n_tb * CHAIN_B

    def row_blk(s):
        return jnp.where(s < nps, (s % CHAIN_B) * n_tb + s // CHAIN_B, s)

    return pl.pallas_call(
        functools.partial(_rwkv_out_body, n_prompt_steps=nps),
        grid=(N_TOK // TM_T,),
        in_specs=[pl.BlockSpec((TM_T * HEAD_SIZE, LANES), lambda s: (jnp.minimum(s // CHAIN_B, n_tb - 1), 0)),
                  pl.BlockSpec((TM_T, RWKV_WIDTH), lambda s: (jnp.maximum(s - nps, 0), 0)),
                  pl.BlockSpec((TM_T, RWKV_WIDTH), lambda s: (row_blk(s), 0)),
                  _resident((RWKV_WIDTH, D_MODEL))],
        out_specs=pl.BlockSpec((TM_T, D_MODEL), lambda s: (row_blk(s), 0)),
        out_shape=jax.ShapeDtypeStruct((N_TOK, D_MODEL), F32),
        scratch_shapes=[pltpu.VMEM((CHAIN_B, TM_T // LANES, RWKV_WIDTH, LANES), F32)],
        compiler_params=_cparams("arbitrary", vmem_mb=56),
        name="rwkv_out",
    )(y_chain, y_s, g, w)


def _mix_body(co_ref, ro_ref, xn_ref, wgr_ref, xp_ref, xs_ref, wm_ref, nf_ref, wr_ref, br_ref,
              h_ref, hn_ref, ridx_ref, rw_ref, *, n_prompt_tiles):
    i = pl.program_id(0)
    g_rwkv = jnp.dot(xn_ref[...], wgr_ref[...], preferred_element_type=F32)
    mixed = co_ref[...] + _sigmoid(g_rwkv) * ro_ref[...]
    mo = jnp.dot(mixed.astype(BF16), wm_ref[...], preferred_element_type=F32)

    def finish(x):
        h = x + mo
        h_ref[...] = h
        ms = jnp.mean(h * h, axis=-1, keepdims=True)
        hn = h * lax.rsqrt(ms + RMS_EPS) * nf_ref[...]
        hnb = hn.astype(BF16)
        hn_ref[...] = hnb.reshape(hn.shape[0], D_MODEL // LANES, LANES)
        logits = jnp.dot(hnb, wr_ref[...], preferred_element_type=F32) + br_ref[...]
        tm = logits.shape[0]
        lane = lax.broadcasted_iota(jnp.int32, (tm, LANES), 1)
        neg = jnp.float32(-jnp.inf)
        gl = jnp.where(lane < N_GROUPS, logits, neg)
        gmax = jnp.max(gl, axis=-1, keepdims=True)
        g_idx = jnp.min(jnp.where(gl == gmax, lane, LANES), axis=-1, keepdims=True)
        g_w = 1.0 / jnp.sum(jnp.exp(gl - gmax), axis=-1, keepdims=True)
        lo = N_GROUPS + g_idx * EXPERTS_PER_GROUP
        el = jnp.where(jnp.logical_and(lane >= lo, lane < lo + EXPERTS_PER_GROUP), logits, neg)
        m1 = jnp.max(el, axis=-1, keepdims=True)
        i1 = jnp.min(jnp.where(el == m1, lane, LANES), axis=-1, keepdims=True)
        el2 = jnp.where(lane == i1, neg, el)
        m2 = jnp.max(el2, axis=-1, keepdims=True)
        i2 = jnp.min(jnp.where(el2 == m2, lane, LANES), axis=-1, keepdims=True)
        t2 = jnp.exp(m2 - m1)
        den = 1.0 + t2
        ridx_ref[...] = jnp.where(lane == 0, i1 - N_GROUPS, jnp.where(lane == 1, i2 - N_GROUPS, 0))
        rw_ref[...] = jnp.where(lane == 0, (1.0 / den) * g_w, jnp.where(lane == 1, (t2 / den) * g_w, 0.0))

    @pl.when(i < n_prompt_tiles)
    def _():
        finish(xp_ref[...])

    @pl.when(i >= n_prompt_tiles)
    def _():
        finish(xs_ref[...])


def _mix_route(conv_out, rwkv_out, xn, wgr, xp, xs, wm, nf, wr, br):
    npt = N_PROMPT // TM_S
    tok_spec = pl.BlockSpec((TM_S, D_MODEL), lambda i: (i, 0))
    small_spec = pl.BlockSpec((TM_S, LANES), lambda i: (i, 0))
    return pl.pallas_call(
        functools.partial(_mix_body, n_prompt_tiles=npt),
        grid=(N_TOK // TM_S,),
        in_specs=[tok_spec, tok_spec, tok_spec, _resident((D_MODEL, D_MODEL)),
                  pl.BlockSpec((TM_S, D_MODEL), lambda i: (jnp.minimum(i, npt - 1), 0)),
                  pl.BlockSpec((TM_S, D_MODEL), lambda i: (jnp.maximum(i - npt, 0), 0)),
                  _resident((D_MODEL, D_MODEL)),
                  pl.BlockSpec((1, D_MODEL), lambda i: (0, 0)),
                  _resident((D_MODEL, LANES)),
                  pl.BlockSpec((1, LANES), lambda i: (0, 0))],
        out_specs=[tok_spec, pl.BlockSpec((TM_S, D_MODEL // LANES, LANES), lambda i: (i, 0, 0)),
                   small_spec, small_spec],
        out_shape=[jax.ShapeDtypeStruct((N_TOK, D_MODEL), F32),
                   jax.ShapeDtypeStruct((N_TOK, D_MODEL // LANES, LANES), BF16),
                   jax.ShapeDtypeStruct((N_TOK, LANES), jnp.int32),
                   jax.ShapeDtypeStruct((N_TOK, LANES), F32)],
        compiler_params=_cparams("arbitrary", vmem_mb=56),
        name="mix_route",
    )(conv_out, rwkv_out, xn, wgr, xp, xs, wm, nf, wr, br)


def _expert_body(blk_e_ref, slot_tok_ref, nused_ref, first_ref, par_ref, next_e_ref,
                 hn_ref, wg_hbm, wu_hbm, wd_hbm, yb_ref,
                 xbuf, sem, wfg, wfu, wfd, wsem, wgb, wub, wdb):
    i = pl.program_id(0)
    nused = nused_ref[0]
    slot = i % N_XBUF

    def row_copy(blk, r, s):
        tok = slot_tok_ref[blk * MOE_BLOCK + r]
        return pltpu.make_async_copy(hn_ref.at[tok], xbuf.at[s, r], sem.at[s])

    def issue(blk, s):
        for r in range(MOE_BLOCK):
            row_copy(blk, r, s).start(priority=r % 2)

    def w_copies(e, s):
        return (pltpu.make_async_copy(wg_hbm.at[e], wfg.at[s], wsem.at[s]),
                pltpu.make_async_copy(wu_hbm.at[e], wfu.at[s], wsem.at[s]),
                pltpu.make_async_copy(wd_hbm.at[e], wfd.at[s], wsem.at[s]))

    @pl.when(jnp.logical_and(i == 0, nused > 0))
    def _():
        for c in w_copies(blk_e_ref[0], 0):
            c.start(priority=1)
        for a in range(GATHER_AHEAD):
            @pl.when(a < nused)
            def _():
                issue(a, a)

    @pl.when(i + GATHER_AHEAD < nused)
    def _():
        issue(i + GATHER_AHEAD, (i + GATHER_AHEAD) % N_XBUF)

    @pl.when(i < nused)
    def _():
        @pl.when(first_ref[i] == 1)
        def _():
            ws = par_ref[i]
            for c in w_copies(blk_e_ref[i], ws):
                c.wait()

            @pl.when(next_e_ref[i] >= 0)
            def _():
                for c in w_copies(next_e_ref[i], 1 - ws):
                    c.start(priority=1)

            wgb[...] = wfg[ws].astype(BF16)
            wub[...] = wfu[ws].astype(BF16)
            wdb[...] = wfd[ws].astype(BF16)

        for r in range(MOE_BLOCK):
            row_copy(i, r, slot).wait()
        xe = xbuf[slot].reshape(MOE_BLOCK, D_MODEL)
        gate = jnp.dot(xe, wgb[...], preferred_element_type=F32)
        up = jnp.dot(xe, wub[...], preferred_element_type=F32)
        hdn = (gate * _sigmoid(gate)) * up
        yb = jnp.dot(hdn.astype(BF16), wdb[...], preferred_element_type=F32)
        yb_ref[...] = yb.reshape(MOE_BLOCK, D_MODEL // LANES, LANES)

    @pl.when(i >= nused)
    def _():
        yb_ref[...] = jnp.zeros_like(yb_ref)


def _experts(plan, hn, wg, wu, wd):
    blk_e, slot_tok, nused, first, par, next_e = plan
    n_blocks = blk_e.shape[0]
    any_spec = pl.BlockSpec(memory_space=pl.ANY)
    return pl.pallas_call(
        _expert_body,
        grid_spec=pltpu.PrefetchScalarGridSpec(
            num_scalar_prefetch=6,
            grid=(n_blocks,),
            in_specs=[any_spec, any_spec, any_spec, any_spec],
            out_specs=pl.BlockSpec((MOE_BLOCK, D_MODEL // LANES, LANES), lambda i, *_: (i, 0, 0)),
            scratch_shapes=[pltpu.VMEM((N_XBUF, MOE_BLOCK, D_MODEL // LANES, LANES), BF16),
                            pltpu.SemaphoreType.DMA((N_XBUF,)),
                            pltpu.VMEM((2, D_MODEL, D_EXPERT), F32),
                            pltpu.VMEM((2, D_MODEL, D_EXPERT), F32),
                            pltpu.VMEM((2, D_EXPERT, D_MODEL), F32),
                            pltpu.SemaphoreType.DMA((2,)),
                            pltpu.VMEM((D_MODEL, D_EXPERT), BF16),
                            pltpu.VMEM((D_MODEL, D_EXPERT), BF16),
                            pltpu.VMEM((D_EXPERT, D_MODEL), BF16)]),
        out_shape=jax.ShapeDtypeStruct((n_blocks * MOE_BLOCK, D_MODEL // LANES, LANES), F32),
        compiler_params=_cparams("arbitrary", vmem_mb=48),
        name="experts",
    )(blk_e, slot_tok, nused, first, par, next_e, hn, wg, wu, wd)


def _combine_body(dest_ref, yb_ref, h_ref, rw_ref, p_ref, wpg_ref, wpp_ref, nf_ref, yp_ref, ys_ref,
                  ybuf, sem):
    i = pl.program_id(0)
    tm = h_ref.shape[0]
    slot = i % 2

    def row_copy(tile, r, s, sl):
        d = dest_ref[(tile * tm + r) * TOP_K + s]
        return pltpu.make_async_copy(yb_ref.at[d], ybuf.at[sl, s, r], sem.at[sl])

    def issue(tile, sl):
        for r in range(tm):
            row_copy(tile, r, 0, sl).start()
            row_copy(tile, r, 1, sl).start()

    @pl.when(i == 0)
    def _():
        issue(0, 0)

    @pl.when(i + 1 < pl.num_programs(0))
    def _():
        issue(i + 1, 1 - slot)

    for r in range(tm):
        row_copy(i, r, 0, slot).wait()
        row_copy(i, r, 1, slot).wait()

    rw = rw_ref[...]
    y0 = ybuf[slot, 0].reshape(tm, D_MODEL)
    y1 = ybuf[slot, 1].reshape(tm, D_MODEL)
    h2 = h_ref[...] + (y0 * rw[:, 0:1] + y1 * rw[:, 1:2])
    gate = _sigmoid(jnp.dot(h2.astype(BF16), wpg_ref[...], preferred_element_type=F32))
    pp = jnp.dot(p_ref[...].astype(BF16), wpp_ref[...], preferred_element_type=F32)
    h3 = h2 + gate * pp
    ms = jnp.mean(h3 * h3, axis=-1, keepdims=True)
    y = h3 * lax.rsqrt(ms + RMS_EPS) * nf_ref[...]

    @pl.when(i < N_PROMPT // TM_S)
    def _():
        yp_ref[...] = y

    @pl.when(i >= N_PROMPT // TM_S)
    def _():
        ys_ref[...] = y


def _combine(dest, yb, h, rw, p_all, wpg, wpp, nf):
    npt = N_PROMPT // TM_S
    return pl.pallas_call(
        _combine_body,
        grid_spec=pltpu.PrefetchScalarGridSpec(
            num_scalar_prefetch=1,
            grid=(N_TOK // TM_S,),
            in_specs=[pl.BlockSpec(memory_space=pl.ANY),
                      pl.BlockSpec((TM_S, D_MODEL), lambda i, d: (i, 0)),
                      pl.BlockSpec((TM_S, LANES), lambda i, d: (i, 0)),
                      pl.BlockSpec((TM_S, PLE_DIM), lambda i, d: (i, 0)),
                      pl.BlockSpec((D_MODEL, D_MODEL), lambda i, d: (0, 0), pipeline_mode=pl.Buffered(1)),
                      pl.BlockSpec((PLE_DIM, D_MODEL), lambda i, d: (0, 0), pipeline_mode=pl.Buffered(1)),
                      pl.BlockSpec((1, D_MODEL), lambda i, d: (0, 0))],
            out_specs=[pl.BlockSpec((TM_S, D_MODEL), lambda i, d: (jnp.minimum(i, npt - 1), 0)),
                       pl.BlockSpec((TM_S, D_MODEL), lambda i, d: (jnp.maximum(i - npt, 0), 0))],
            scratch_shapes=[pltpu.VMEM((2, TOP_K, TM_S, D_MODEL // LANES, LANES), F32),
                            pltpu.SemaphoreType.DMA((2,))]),
        out_shape=[jax.ShapeDtypeStruct((N_PROMPT, D_MODEL), F32),
                   jax.ShapeDtypeStruct((N_SAMPLE, D_MODEL), F32)],
        compiler_params=_cparams("arbitrary", vmem_mb=48),
        name="combine_ple",
    )(dest, yb, h, rw, p_all, wpg, wpp, nf)


def _dispatch_plan(eidx):
    n_assign = N_TOK * TOP_K
    e_flat = eidx.reshape(n_assign)
    onehot = (e_flat[:, None] == jnp.arange(N_EXPERTS, dtype=jnp.int32)[None, :]).astype(jnp.int32)
    csum = jnp.cumsum(onehot, axis=0)
    counts = csum[-1]
    rank = jnp.sum(csum * onehot, axis=1) - 1
    padded = (counts + MOE_BLOCK - 1) // MOE_BLOCK * MOE_BLOCK
    pad_end = jnp.cumsum(padded)
    pad_start = pad_end - padded
    dest = pad_start[e_flat] + rank
    n_blocks = -(-n_assign // MOE_BLOCK) + N_EXPERTS
    tok = jnp.arange(n_assign, dtype=jnp.int32) // TOP_K
    pad_tok = jnp.arange(n_blocks * MOE_BLOCK, dtype=jnp.int32) % N_TOK
    slot_tok = pad_tok.at[dest].set(tok)
    blk = jnp.arange(n_blocks, dtype=jnp.int32)
    blk_e = jnp.minimum(jnp.searchsorted(pad_end, blk * MOE_BLOCK, side="right"), N_EXPERTS - 1).astype(jnp.int32)
    nused = (pad_end[-1] // MOE_BLOCK).astype(jnp.int32)
    prev_e = jnp.concatenate([jnp.full((1,), -1, jnp.int32), blk_e[:-1]])
    first = jnp.logical_and(blk < nused, blk_e != prev_e)
    par = ((jnp.cumsum(first.astype(jnp.int32)) - 1) % 2).astype(jnp.int32)
    idx_first = jnp.where(first, blk, n_blocks)
    later = jnp.concatenate([lax.cummin(idx_first[::-1])[::-1][1:], jnp.full((1,), n_blocks, jnp.int32)])
    next_e = jnp.where(later < n_blocks, blk_e[jnp.minimum(later, n_blocks - 1)], -1).astype(jnp.int32)
    plan = (blk_e, slot_tok, nused.reshape(1), first.astype(jnp.int32), par, next_e)
    return dest.astype(jnp.int32), plan


def kernel(x_prompt, x_sample, state_conv, state_shift, state_wkv, p_prompt, p_sample, norm_mix, w_in, conv_w, w_conv_out, shift_mu, w0, w2, a0, a2, g2, k_k, k_a, r_k, lnx_w, lnx_b, w_rwkv_out, w_mix_out, norm_ffn, w_route_group, b_route_group, w_route_expert, b_route_expert, w_exp_gate, w_exp_up, w_exp_down, w_ple_proj, w_ple_gate, norm_final):
    c3 = 3 * CONV_WIDTH
    rw3 = 3 * RWKV_WIDTH
    xp = x_prompt.reshape(N_PROMPT, D_MODEL)
    xs = x_sample.transpose(1, 0, 2).reshape(N_SAMPLE, D_MODEL)
    win = w_in[0]
    mu = shift_mu[0]
    st = state_shift[0]

    xn = _norm_cast(xp, xs, norm_mix)
    rkv_s, _, sraw_rkv = _inproj(xn, win, mu[None, :rw3], st[:, :rw3],
                                 tn=1024, col_blk_off=c3 // 1024, n_col_blocks=rw3 // 1024,
                                 shift_lo=0, shift_hi=rw3 // 1024,
                                 tile_lo=N_PROMPT // TM, n_tiles=N_SAMPLE // TM, name="inproj_rkv_sample")
    n_lr = RWKV_PROJ - rw3
    mu_l = jnp.concatenate([mu[rw3:], jnp.zeros((LORA_W - n_lr,), F32)])[None, :]
    st_l = jnp.concatenate([st[:, rw3:], jnp.zeros((DEC_BATCH, LORA_W - n_lr), F32)], axis=1)
    w_l = win[:, c3 + rw3:c3 + rw3 + LORA_W].astype(BF16)
    w_gc = win[:, c3 + RWKV_PROJ:c3 + RWKV_PROJ + D_MODEL].astype(BF16)
    w_gr = win[:, c3 + RWKV_PROJ + D_MODEL:].astype(BF16)
    w_rk = _head_minor(win[:, c3:c3 + 2 * RWKV_WIDTH].reshape(D_MODEL, 2, RWKV_WIDTH), 2)
    rk_chain = _inproj_t(xn, w_rk.reshape(D_MODEL, 2 * RWKV_WIDTH).T.astype(BF16),
                         time_major=False, name="inproj_t_rk")
    v_chain = _inproj_t(xn, _head_minor(win[:, c3 + 2 * RWKV_WIDTH:c3 + rw3], 1).T.astype(BF16),
                        time_major=True, name="inproj_t_v")

    sc = state_conv[0].transpose(1, 0, 2).reshape(2 * DEC_BATCH, CONV_WIDTH)
    conv_out, ulast, us = _conv_branch(xn, win[:, :c3].astype(BF16), conv_w[0], sc,
                                       w_conv_out[0].astype(BF16), w_gc)

    def pad_rows(w, before, total):
        return jnp.pad(w, ((before, total - before - w.shape[0]), (0, 0))).astype(BF16)

    w2p = pad_rows(w2[0], 0, LANES)
    a2p = pad_rows(a2[0], DECAY_LORA, 2 * LANES)
    g2p = pad_rows(_head_minor(g2[0], 1), DECAY_LORA + A_LORA - LANES, LORA_W - LANES)
    g, wl_c, al_c, wl_s, al_s, last_l, sraw_l = _lora(xn, w_l, mu_l, st_l, _head_minor(w2p, 1).T,
                                                      _head_minor(a2p, 1).T, w2p, a2p, g2p)
    par8 = [_param_chain8(p) for p in (k_k[0], k_a[0], r_k[0].reshape(RWKV_WIDTH))]
    mu8 = [_param_chain8(mu[n * RWKV_WIDTH:(n + 1) * RWKV_WIDTH]) for n in range(2)]
    bias8 = [_param_chain8(w0[0]), _param_chain8(a0[0])]
    zero8 = jnp.zeros((HEAD_SIZE * SCAN_TB, LANES), F32)
    mu_v = _param_chain(mu[2 * RWKV_WIDTH:rw3])
    gn = [_param_chain(lnx_w[0]), _param_chain(lnx_b[0])]
    seqs_p = [(rk_chain, 0), (rk_chain, 1), (wl_c[None], None), (al_c[None], None)]
    v4 = v_chain.reshape(1, SEQ, HEAD_SIZE, LANES)
    s0_p = jnp.zeros((1, HEAD_SIZE, HEAD_SIZE, LANES), F32)
    y_p, sf_p = _scan(seqs_p, (v4, None), par8 + mu8 + bias8, [mu_v] + gn, s0_p,
                      n_batches=SCAN_TT // SCAN_TB, steps=SCAN_TB, name="wkv_scan_prompt")
    ng = DEC_BATCH // CHAIN_B
    s0_s = state_wkv[0].reshape(ng, CHAIN_B, N_HEADS, HEAD_SIZE, HEAD_SIZE).transpose(0, 4, 3, 1, 2)
    s0_s = s0_s.reshape(ng, HEAD_SIZE, HEAD_SIZE, LANES)
    seqs_s = [(_to_chain_sample_ct(rkv_s[:, n * RWKV_WIDTH:(n + 1) * RWKV_WIDTH]), None) for n in range(2)]
    seqs_s += [(_to_chain_sample_ct(wl_s), None), (_to_chain_sample_ct(al_s), None)]
    v_s = _to_chain_sample_tc(rkv_s[:, 2 * RWKV_WIDTH:rw3])
    y_s, sf_s = _scan(seqs_s, (v_s, None), par8 + [zero8, zero8] + bias8,
                      [jnp.zeros((HEAD_SIZE, LANES), F32)] + gn, s0_s,
                      n_batches=1, steps=DEC_SEQ, name="wkv_scan_sample")
    rwkv_out = _rwkv_out(y_p.reshape(SEQ * HEAD_SIZE, LANES), _from_chain_sample(y_s), g,
                         _head_minor(w_rwkv_out[0], 0).astype(BF16))

    wr = jnp.concatenate([w_route_group[0], w_route_expert[0],
                          jnp.zeros((D_MODEL, LANES - N_GROUPS - N_EXPERTS), F32)], axis=1)
    br = jnp.concatenate([b_route_group[0], b_route_expert[0],
                          jnp.zeros((LANES - N_GROUPS - N_EXPERTS,), F32)])[None, :]
    h, hn, ridx, rw = _mix_route(conv_out, rwkv_out, xn, w_gr, xp, xs, w_mix_out[0].astype(BF16),
                                 norm_ffn, wr.astype(BF16), br)

    dest, plan = _dispatch_plan(ridx[:, :TOP_K])
    yb = _experts(plan, hn, w_exp_gate[0], w_exp_up[0], w_exp_down[0])
    p_all = jnp.concatenate([p_prompt[0].reshape(N_PROMPT, PLE_DIM),
                             p_sample[0].transpose(1, 0, 2).reshape(N_SAMPLE, PLE_DIM)], axis=0)
    y_p2, y_s2 = _combine(dest, yb, h, rw, p_all, w_ple_gate[0].astype(BF16), w_ple_proj[0].astype(BF16),
                          norm_final[None, :])

    y_prompt = y_p2.reshape(BATCH, SEQ, D_MODEL)
    y_sample = y_s2.reshape(DEC_SEQ, DEC_BATCH, D_MODEL).transpose(1, 0, 2)
    tiles_per_seq = SEQ // TM
    seq_last = jnp.arange(BATCH) * tiles_per_seq + tiles_per_seq - 1

    conv_p = ulast.reshape(-1, SUBLANES, CONV_WIDTH)[seq_last, SUBLANES - 2:, :][None]
    conv_s = us.reshape(2, DEC_BATCH, CONV_WIDTH).transpose(1, 0, 2)[None]
    last_rkv = jnp.concatenate([rk_chain[:, :, SEQ - 1, :], v4[:, SEQ - 1]], axis=0)
    lm = last_rkv.reshape(3, HEAD_SIZE, BATCH, N_HEADS).transpose(2, 0, 3, 1).reshape(BATCH, rw3)
    lora_last = (SEQ // TM_L - 1) * CHAIN_B + jnp.arange(BATCH)
    lt = last_l.reshape(-1, SUBLANES, LORA_W)[lora_last, SUBLANES - 1, :n_lr]
    shift_p = jnp.concatenate([lm, lt], axis=1)[None]
    shift_s = jnp.concatenate([sraw_rkv, sraw_l[:, :n_lr]], axis=1)[None]
    wkv_p = sf_p.reshape(HEAD_SIZE, HEAD_SIZE, BATCH, N_HEADS).transpose(2, 3, 1, 0)[None]
    wkv_s = sf_s.reshape(ng, HEAD_SIZE, HEAD_SIZE, CHAIN_B, N_HEADS).transpose(0, 3, 4, 2, 1)
    wkv_s = wkv_s.reshape(DEC_BATCH, N_HEADS, HEAD_SIZE, HEAD_SIZE)[None]
    return (y_prompt, y_sample, conv_p, shift_p, wkv_p, conv_s, shift_s, wkv_s)
```

```python
import functools

import jax
import jax.numpy as jnp
from jax import lax
from jax.experimental import pallas as pl
from jax.experimental.pallas import tpu as pltpu

D_MODEL = 2048
BATCH = 4
SEQ = 2048
DEC_BATCH = 128
DEC_SEQ = 4
CONV_WIDTH = 1024
RWKV_WIDTH = 2048
HEAD_SIZE = 64
N_HEADS = RWKV_WIDTH // HEAD_SIZE
DECAY_LORA = 96
A_LORA = 96
GATE_LORA = 256
RWKV_PROJ = 3 * RWKV_WIDTH + DECAY_LORA + A_LORA + GATE_LORA
N_GROUPS = 8
EXPERTS_PER_GROUP = 8
N_EXPERTS = N_GROUPS * EXPERTS_PER_GROUP
TOP_K = 2
D_EXPERT = 512
MOE_BLOCK = 128
PLE_DIM = 256
RMS_EPS = 1e-6
GN_EPS = 64e-5

N_PROMPT = BATCH * SEQ
N_SAMPLE = DEC_BATCH * DEC_SEQ
N_TOK = N_PROMPT + N_SAMPLE
LANES = 128
SUBLANES = 8
TM = 512
TM_S = 256
TM_T = 256
TM_L = 128
CHAIN_B = LANES // N_HEADS
SCAN_TT = 64
SCAN_TB = SUBLANES
SCAN_ACC = 2
GATHER_AHEAD = 3
N_XBUF = GATHER_AHEAD + 1
LORA_W = 512
EXP_M05 = 0.6065306597126334
F32 = jnp.float32
BF16 = jnp.bfloat16
_NT = (((1,), (1,)), ((), ()))


def _sigmoid(x):
    return 1.0 / (1.0 + jnp.exp(-x))


def _cparams(*sem, vmem_mb=None):
    kw = dict(dimension_semantics=sem)
    if vmem_mb is not None:
        kw["vmem_limit_bytes"] = vmem_mb * 1024 * 1024
    return pltpu.CompilerParams(**kw)


def _resident(shape):
    nd = len(shape)
    return pl.BlockSpec(shape, lambda *_: (0,) * nd, pipeline_mode=pl.Buffered(1))


def _chain_tile(zt, half, c):
    start = c * N_HEADS
    rows = pl.ds(start if isinstance(c, int) else pl.multiple_of(start, N_HEADS), N_HEADS)
    return jnp.concatenate([zt[bb, half, rows, :] for bb in range(CHAIN_B)], axis=0)


def _store_chain_ct(zt, out_ref, halves, c_lo=0, n_c=HEAD_SIZE):
    for half in range(halves):
        for ci in range(n_c):
            out_ref[c_lo + ci, half * LANES:(half + 1) * LANES, :] = _chain_tile(zt, half, c_lo + ci).T


def _store_chain_tc(zt, out_ref, halves, c_lo=0, n_c=HEAD_SIZE):
    for half in range(halves):
        for ci in range(n_c):
            rows = pl.ds(half * LANES * HEAD_SIZE + c_lo + ci, LANES, stride=HEAD_SIZE)
            out_ref[rows, :] = _chain_tile(zt, half, c_lo + ci).T


def _load_chain_tc(y_ref, yt, halves):
    for half in range(halves):
        for v in range(HEAD_SIZE):
            mt = y_ref[pl.ds(half * LANES * HEAD_SIZE + v, LANES, stride=HEAD_SIZE), :].T
            for bb in range(CHAIN_B):
                yt[bb, half, v * N_HEADS:(v + 1) * N_HEADS, :] = mt[bb * N_HEADS:(bb + 1) * N_HEADS, :]


def _norm_body(xp_ref, xs_ref, g_ref, o_ref, *, n_prompt_tiles):
    i = pl.program_id(0)

    def f(x):
        ms = jnp.mean(x * x, axis=-1, keepdims=True)
        return (x * lax.rsqrt(ms + RMS_EPS) * g_ref[...]).astype(o_ref.dtype)

    @pl.when(i < n_prompt_tiles)
    def _():
        o_ref[...] = f(xp_ref[...])

    @pl.when(i >= n_prompt_tiles)
    def _():
        o_ref[...] = f(xs_ref[...])


def _norm_cast(xp, xs, g):
    npt = N_PROMPT // TM
    return pl.pallas_call(
        functools.partial(_norm_body, n_prompt_tiles=npt),
        grid=(N_TOK // TM,),
        in_specs=[pl.BlockSpec((TM, D_MODEL), lambda i: (jnp.minimum(i, npt - 1), 0)),
                  pl.BlockSpec((TM, D_MODEL), lambda i: (jnp.maximum(i - npt, 0), 0)),
                  pl.BlockSpec((1, D_MODEL), lambda i: (0, 0))],
        out_specs=pl.BlockSpec((TM, D_MODEL), lambda i: (i, 0)),
        out_shape=jax.ShapeDtypeStruct((N_TOK, D_MODEL), BF16),
        compiler_params=_cparams("arbitrary"),
        name="norm_cast",
    )(xp, xs, g)


def _inproj_body(xn_ref, w_ref, mu_ref, st_ref, z_ref, last_ref, sraw_ref, wb_ref, carry_ref, *,
                 shift_lo, shift_hi, tile_lo, n_prompt_tiles, tiles_per_seq):
    j = pl.program_id(0)
    i = pl.program_id(1) + tile_lo

    @pl.when(pl.program_id(1) == 0)
    def _():
        wb_ref[...] = w_ref[...].astype(BF16)

    z = jnp.dot(xn_ref[...], wb_ref[...], preferred_element_type=F32)
    tm = z.shape[0]
    last_ref[...] = z[tm - SUBLANES:tm]
    shifted = jnp.logical_and(j >= shift_lo, j < shift_hi)
    is_prompt = i < n_prompt_tiles

    @pl.when(jnp.logical_not(shifted))
    def _():
        z_ref[...] = z

    @pl.when(jnp.logical_and(shifted, is_prompt))
    def _():
        @pl.when(i % tiles_per_seq == 0)
        def _():
            carry_ref[...] = jnp.zeros_like(carry_ref)

        prev = pltpu.roll(z, 1, 0)
        row = lax.broadcasted_iota(jnp.int32, (tm, 1), 0)
        prev = jnp.where(row == 0, carry_ref[SUBLANES - 1:SUBLANES, :], prev)
        z_ref[...] = z + mu_ref[...] * (prev - z)
        carry_ref[...] = z[tm - SUBLANES:tm]

    @pl.when(jnp.logical_and(shifted, jnp.logical_not(is_prompt)))
    def _():
        prev = jnp.concatenate([st_ref[...], z[:tm - DEC_BATCH]], axis=0)
        z_ref[...] = z + mu_ref[...] * (prev - z)

    @pl.when(jnp.logical_not(is_prompt))
    def _():
        sraw_ref[...] = z[tm - DEC_BATCH:tm]


def _inproj(xn, w, mu, st, *, tn, col_blk_off, n_col_blocks, shift_lo, shift_hi, tile_lo, n_tiles, name):
    n_out = tn * n_col_blocks
    body = functools.partial(_inproj_body, shift_lo=shift_lo, shift_hi=shift_hi, tile_lo=tile_lo,
                             n_prompt_tiles=N_PROMPT // TM, tiles_per_seq=SEQ // TM)
    return pl.pallas_call(
        body,
        grid=(n_col_blocks, n_tiles),
        in_specs=[pl.BlockSpec((TM, D_MODEL), lambda j, i: (i + tile_lo, 0)),
                  pl.BlockSpec((D_MODEL, tn), lambda j, i: (0, j + col_blk_off)),
                  pl.BlockSpec((1, tn), lambda j, i: (0, j)),
                  pl.BlockSpec((DEC_BATCH, tn), lambda j, i: (0, j))],
        out_specs=[pl.BlockSpec((TM, tn), lambda j, i: (i, j)),
                   pl.BlockSpec((SUBLANES, tn), lambda j, i: (i, j)),
                   pl.BlockSpec((DEC_BATCH, tn), lambda j, i: (0, j))],
        out_shape=[jax.ShapeDtypeStruct((n_tiles * TM, n_out), F32),
                   jax.ShapeDtypeStruct((n_tiles * SUBLANES, n_out), F32),
                   jax.ShapeDtypeStruct((DEC_BATCH, n_out), F32)],
        scratch_shapes=[pltpu.VMEM((D_MODEL, tn), BF16), pltpu.VMEM((SUBLANES, tn), F32)],
        compiler_params=_cparams("arbitrary", "arbitrary", vmem_mb=48),
        name=name,
    )(xn, w, mu, st)


def _inproj_t_body(xn_ref, wt_ref, o_ref, zt, *, time_major, n_tb):
    tb = pl.program_id(1)
    b = pl.program_id(2)
    halves = TM_T // LANES
    n_c = HEAD_SIZE // CHAIN_B

    def matmul():
        z = lax.dot_general(wt_ref[...], xn_ref[...], _NT, preferred_element_type=F32)
        for half in range(halves):
            zt[tb % 2, b, half] = z[:, half * LANES:(half + 1) * LANES]

    def retile():
        store = _store_chain_tc if time_major else _store_chain_ct
        store(zt.at[1 - tb % 2], o_ref, halves, c_lo=b * n_c, n_c=n_c)

    @pl.when(tb == 0)
    def _():
        matmul()

    @pl.when(jnp.logical_and(tb > 0, tb < n_tb))
    def _():
        retile()
        matmul()

    @pl.when(tb == n_tb)
    def _():
        retile()


def _inproj_t(xn, wt, *, time_major, name):
    n_tb = SEQ // TM_T
    n = wt.shape[0] // RWKV_WIDTH

    def prev_tb(tb):
        return jnp.maximum(tb - 1, 0)

    if time_major:
        out_spec = pl.BlockSpec((None, TM_T * HEAD_SIZE, LANES), lambda j, tb, b: (j, prev_tb(tb), 0))
        out_shape = jax.ShapeDtypeStruct((n, SEQ * HEAD_SIZE, LANES), F32)
    else:
        out_spec = pl.BlockSpec((None, HEAD_SIZE, TM_T, LANES), lambda j, tb, b: (j, 0, prev_tb(tb), 0))
        out_shape = jax.ShapeDtypeStruct((n, HEAD_SIZE, SEQ, LANES), F32)
    return pl.pallas_call(
        functools.partial(_inproj_t_body, time_major=time_major, n_tb=n_tb),
        grid=(n, n_tb + 1, CHAIN_B),
        in_specs=[pl.BlockSpec((TM_T, D_MODEL), lambda j, tb, b: (b * n_tb + jnp.minimum(tb, n_tb - 1), 0)),
                  pl.BlockSpec((RWKV_WIDTH, D_MODEL), lambda j, tb, b: (j, 0), pipeline_mode=pl.Buffered(1))],
        out_specs=out_spec,
        out_shape=out_shape,
        scratch_shapes=[pltpu.VMEM((2, CHAIN_B, TM_T // LANES, RWKV_WIDTH, LANES), F32)],
        compiler_params=_cparams("arbitrary", "arbitrary", "arbitrary", vmem_mb=56),
        name=name,
    )(xn, wt)


def _conv_body(xn_ref, wc_ref, cw_ref, sc_ref, wco_ref, wgc_ref, o_ref, ulast_ref, us_ref, carry_ref, *,
               n_prompt_tiles, tiles_per_seq):
    i = pl.program_id(0)
    gate_b = jnp.dot(xn_ref[...], wc_ref[:, 0:CONV_WIDTH], preferred_element_type=F32)
    u = (jnp.dot(xn_ref[...], wc_ref[:, CONV_WIDTH:2 * CONV_WIDTH], preferred_element_type=F32)
         * jnp.dot(xn_ref[...], wc_ref[:, 2 * CONV_WIDTH:3 * CONV_WIDTH], preferred_element_type=F32))
    tm = u.shape[0]
    ulast_ref[...] = u[tm - SUBLANES:tm]
    w0 = cw_ref[0:1, :]
    w1 = cw_ref[1:2, :]
    w2 = cw_ref[2:3, :]

    def finish(p1, p2):
        conv = w0 * p2 + w1 * p1 + w2 * u
        y = (gate_b * conv).astype(BF16)
        g_conv = jnp.dot(xn_ref[...], wgc_ref[...], preferred_element_type=F32)
        o_ref[...] = _sigmoid(g_conv) * jnp.dot(y, wco_ref[...], preferred_element_type=F32)

    @pl.when(i < n_prompt_tiles)
    def _():
        @pl.when(i % tiles_per_seq == 0)
        def _():
            carry_ref[...] = jnp.zeros_like(carry_ref)

        row = lax.broadcasted_iota(jnp.int32, (tm, 1), 0)
        c1 = carry_ref[SUBLANES - 1:SUBLANES, :]
        c2 = carry_ref[SUBLANES - 2:SUBLANES - 1, :]
        p1 = jnp.where(row == 0, c1, pltpu.roll(u, 1, 0))
        p2 = jnp.where(row == 0, c2, jnp.where(row == 1, c1, pltpu.roll(u, 2, 0)))
        carry_ref[...] = u[tm - SUBLANES:tm]
        finish(p1, p2)

    @pl.when(i >= n_prompt_tiles)
    def _():
        p1 = jnp.concatenate([sc_ref[DEC_BATCH:2 * DEC_BATCH, :], u[:tm - DEC_BATCH]], axis=0)
        p2 = jnp.concatenate([sc_ref[...], u[:tm - 2 * DEC_BATCH]], axis=0)
        us_ref[...] = u[tm - 2 * DEC_BATCH:tm]
        finish(p1, p2)


def _conv_branch(xn, wc, conv_w, sc, wco, wgc):
    n_tiles = N_TOK // TM
    body = functools.partial(_conv_body, n_prompt_tiles=N_PROMPT // TM, tiles_per_seq=SEQ // TM)
    return pl.pallas_call(
        body,
        grid=(n_tiles,),
        in_specs=[pl.BlockSpec((TM, D_MODEL), lambda i: (i, 0)),
                  _resident((D_MODEL, 3 * CONV_WIDTH)),
                  pl.BlockSpec((3, CONV_WIDTH), lambda i: (0, 0)),
                  pl.BlockSpec((2 * DEC_BATCH, CONV_WIDTH), lambda i: (0, 0)),
                  _resident((CONV_WIDTH, D_MODEL)), _resident((D_MODEL, D_MODEL))],
        out_specs=[pl.BlockSpec((TM, D_MODEL), lambda i: (i, 0)),
                   pl.BlockSpec((SUBLANES, CONV_WIDTH), lambda i: (i, 0)),
                   pl.BlockSpec((2 * DEC_BATCH, CONV_WIDTH), lambda i: (0, 0))],
        out_shape=[jax.ShapeDtypeStruct((N_TOK, D_MODEL), F32),
                   jax.ShapeDtypeStruct((n_tiles * SUBLANES, CONV_WIDTH), F32),
                   jax.ShapeDtypeStruct((2 * DEC_BATCH, CONV_WIDTH), F32)],
        scratch_shapes=[pltpu.VMEM((SUBLANES, CONV_WIDTH), F32)],
        compiler_params=_cparams("arbitrary", vmem_mb=48),
        name="conv_branch",
    )(xn, wc, conv_w, sc, wco, wgc)


def _lora_body(xn_ref, wl_ref, mu_ref, st_ref, w2t_ref, a2t_ref, w2_ref, a2_ref,
               xg_ref, wlc_ref, alc_ref, wls_ref, als_ref, last_ref, sraw_ref, zt, carry_p, carry_s, *,
               n_prompt_steps):
    s = pl.program_id(0)
    z = jnp.dot(xn_ref[...], wl_ref[...], preferred_element_type=F32)
    tm = z.shape[0]
    last_ref[...] = z[tm - SUBLANES:tm]

    def project(zl):
        tw = jnp.tanh(zl[:, 0:LANES]).astype(BF16)
        xa = zl[:, 0:2 * LANES].astype(BF16)
        xg_ref[...] = zl[:, LANES:LORA_W]
        return tw, xa

    @pl.when(s < n_prompt_steps)
    def _():
        b = s % CHAIN_B

        @pl.when(s < CHAIN_B)
        def _():
            carry_p[b] = jnp.zeros((SUBLANES, LORA_W), F32)

        row = lax.broadcasted_iota(jnp.int32, (tm, 1), 0)
        prev = jnp.where(row == 0, carry_p[b, SUBLANES - 1:SUBLANES, :], pltpu.roll(z, 1, 0))
        carry_p[b] = z[tm - SUBLANES:tm]
        tw, xa = project(z + mu_ref[...] * (prev - z))
        zt[0, b, 0] = lax.dot_general(w2t_ref[...], tw, _NT, preferred_element_type=F32)
        zt[1, b, 0] = lax.dot_general(a2t_ref[...], xa, _NT, preferred_element_type=F32)

        @pl.when(b == CHAIN_B - 1)
        def _():
            _store_chain_ct(zt.at[0], wlc_ref, 1)
            _store_chain_ct(zt.at[1], alc_ref, 1)

    @pl.when(s >= n_prompt_steps)
    def _():
        @pl.when(s == n_prompt_steps)
        def _():
            carry_s[...] = st_ref[...]

        prev = carry_s[...]
        carry_s[...] = z
        sraw_ref[...] = z
        tw, xa = project(z + mu_ref[...] * (prev - z))
        wls_ref[...] = jnp.dot(tw, w2_ref[...], preferred_element_type=F32)
        als_ref[...] = jnp.dot(xa, a2_ref[...], preferred_element_type=F32)


def _lora(xn, w_l, mu_l, st_l, w2t, a2t, w2p, a2p):
    n_tb = SEQ // TM_L
    nps = n_tb * CHAIN_B
    n_steps = N_TOK // TM_L

    def row_blk(s):
        return jnp.where(s < nps, (s % CHAIN_B) * n_tb + s // CHAIN_B, s)

    chain_spec = pl.BlockSpec((HEAD_SIZE, TM_L, LANES), lambda s: (0, jnp.minimum(s // CHAIN_B, n_tb - 1), 0))
    samp_spec = pl.BlockSpec((TM_L, RWKV_WIDTH), lambda s: (jnp.maximum(s - nps, 0), 0))
    chain_shape = jax.ShapeDtypeStruct((HEAD_SIZE, SEQ, LANES), F32)
    samp_shape = jax.ShapeDtypeStruct((N_SAMPLE, RWKV_WIDTH), F32)
    return pl.pallas_call(
        functools.partial(_lora_body, n_prompt_steps=nps),
        grid=(n_steps,),
        in_specs=[pl.BlockSpec((TM_L, D_MODEL), lambda s: (row_blk(s), 0)),
                  _resident(w_l.shape),
                  pl.BlockSpec((1, LORA_W), lambda s: (0, 0)),
                  pl.BlockSpec((DEC_BATCH, LORA_W), lambda s: (0, 0)),
                  _resident(w2t.shape), _resident(a2t.shape), _resident(w2p.shape), _resident(a2p.shape)],
        out_specs=[pl.BlockSpec((TM_L, LORA_W - LANES), lambda s: (row_blk(s), 0)),
                   chain_spec, chain_spec, samp_spec, samp_spec,
                   pl.BlockSpec((SUBLANES, LORA_W), lambda s: (s, 0)),
                   pl.BlockSpec((DEC_BATCH, LORA_W), lambda s: (0, 0))],
        out_shape=[jax.ShapeDtypeStruct((N_TOK, LORA_W - LANES), F32),
                   chain_shape, chain_shape, samp_shape, samp_shape,
                   jax.ShapeDtypeStruct((n_steps * SUBLANES, LORA_W), F32),
                   jax.ShapeDtypeStruct((DEC_BATCH, LORA_W), F32)],
        scratch_shapes=[pltpu.VMEM((2, CHAIN_B, 1, RWKV_WIDTH, LANES), F32),
                        pltpu.VMEM((CHAIN_B, SUBLANES, LORA_W), F32),
                        pltpu.VMEM((DEC_BATCH, LORA_W), F32)],
        compiler_params=_cparams("arbitrary", vmem_mb=48),
        name="lora",
    )(xn, w_l, mu_l, st_l, w2t, a2t, w2p, a2p)


def _scan_body(r_ref, k_ref, wl_ref, al_ref, v_ref, kk_ref, ka_ref, rk_ref, mur_ref, muk_ref, w0_ref, a0_ref,
               muv_ref, lw_ref, lb_ref, s0_ref,
               y_ref, s_ref, vec_ref, bon_ref, prevb_ref, prevv_ref, *, n_batches, steps):
    @pl.when(pl.program_id(1) == 0)
    def _():
        s_ref[...] = s0_ref[...]
        prevb_ref[...] = jnp.zeros_like(prevb_ref)
        prevv_ref[...] = jnp.zeros_like(prevv_ref)

    rows = HEAD_SIZE * SCAN_TB
    first_t = lax.broadcasted_iota(jnp.int32, (rows, 1), 0) % SCAN_TB == 0

    def batch(bi, carry):
        t0 = pl.multiple_of(bi * SCAN_TB, SCAN_TB)

        def load(ref):
            return ref[:, pl.ds(t0, SCAN_TB), :].reshape(rows, LANES)

        def cube(x):
            return x.reshape(HEAD_SIZE, SCAN_TB, LANES)

        def shifted(x, slot, mu_ref):
            prev = jnp.where(first_t, pltpu.roll(prevb_ref[slot], rows - (SCAN_TB - 1), 0), pltpu.roll(x, 1, 0))
            prevb_ref[slot] = x
            return x + mu_ref[...] * (prev - x)

        r = shifted(load(r_ref), 0, mur_ref)
        k = shifted(load(k_ref), 1, muk_ref)
        decay = jnp.exp(-EXP_M05 * _sigmoid(load(wl_ref) + w0_ref[...]))
        a = _sigmoid(load(al_ref) + a0_ref[...])
        kk = cube(k * kk_ref[...])
        nrm = jnp.sqrt(jnp.sum(kk * kk, axis=0))
        kk = kk * (1.0 / jnp.maximum(nrm, 1e-12))[None]
        kf = k * (1.0 + (a - 1.0) * ka_ref[...])
        vec_ref[0] = -kk
        vec_ref[1] = cube(decay)
        vec_ref[2] = kk * cube(a)
        vec_ref[3] = cube(kf)
        vec_ref[4] = cube(r)
        bon_ref[...] = jnp.sum(cube(r * kf * rk_ref[...]), axis=0)

        def step(tl, c):
            t = t0 + tl
            v_raw = v_ref[t]
            v = v_raw + muv_ref[...] * (prevv_ref[...] - v_raw)
            prevv_ref[...] = v_raw

            def row(j, kx):
                return vec_ref[j, kx, pl.ds(tl, HEAD_SIZE, stride=0), :]

            parts = [s_ref[kx] * row(0, kx) for kx in range(SCAN_ACC)]
            for kx in range(SCAN_ACC, HEAD_SIZE):
                parts[kx % SCAN_ACC] = parts[kx % SCAN_ACC] + s_ref[kx] * row(0, kx)
            sa = functools.reduce(lambda x, y: x + y, parts)

            parts = []
            for kx in range(HEAD_SIZE):
                sn = s_ref[kx] * row(1, kx) + sa * row(2, kx) + v * row(3, kx)
                s_ref[kx] = sn
                if kx < SCAN_ACC:
                    parts.append(sn * row(4, kx))
                else:
                    parts[kx % SCAN_ACC] = parts[kx % SCAN_ACC] + sn * row(4, kx)
            o = functools.reduce(lambda x, y: x + y, parts)

            mu = jnp.mean(o, axis=0, keepdims=True)
            dlt = o - mu
            var = jnp.mean(dlt * dlt, axis=0, keepdims=True)
            on = dlt * lax.rsqrt(var + GN_EPS) * lw_ref[...] + lb_ref[...]
            y_ref[t] = on + bon_ref[pl.ds(tl, HEAD_SIZE, stride=0), :] * v
            return c

        for tl in range(steps):
            step(tl, 0)
        return carry

    lax.fori_loop(0, n_batches, batch, 0)


def _scan(ct_seqs, v_seq, params8, params, s0, *, n_batches, steps, name):
    g = s0.shape[0]
    tt = n_batches * SCAN_TB
    ttv = v_seq[0].shape[1] if n_batches == 1 else n_batches * steps
    n_ti = ct_seqs[0][0].shape[2] // tt

    def spec(block, lead, tpos):
        def index(gi, ti):
            idx = [gi if lead is None else lead, 0, 0, 0]
            idx[tpos] = ti
            return tuple(idx)
        return pl.BlockSpec(block, index)

    ct_block = (None, HEAD_SIZE, tt, LANES)
    tv_block = (None, ttv, HEAD_SIZE, LANES)
    par8_spec = pl.BlockSpec((HEAD_SIZE * SCAN_TB, LANES), lambda gi, ti: (0, 0))
    par_spec = pl.BlockSpec((HEAD_SIZE, LANES), lambda gi, ti: (0, 0))
    st_spec = pl.BlockSpec((None, HEAD_SIZE, HEAD_SIZE, LANES), lambda gi, ti: (gi, 0, 0, 0))
    return pl.pallas_call(
        functools.partial(_scan_body, n_batches=n_batches, steps=steps),
        grid=(g, n_ti),
        in_specs=([spec(ct_block, lead, 2) for _, lead in ct_seqs] + [spec(tv_block, v_seq[1], 1)]
                  + [par8_spec] * len(params8) + [par_spec] * len(params) + [st_spec]),
        out_specs=[spec(tv_block, None, 1), st_spec],
        out_shape=[jax.ShapeDtypeStruct((g, n_ti * ttv, HEAD_SIZE, LANES), F32),
                   jax.ShapeDtypeStruct((g, HEAD_SIZE, HEAD_SIZE, LANES), F32)],
        scratch_shapes=[pltpu.VMEM((5, HEAD_SIZE, SCAN_TB, LANES), F32), pltpu.VMEM((SCAN_TB, LANES), F32),
                        pltpu.VMEM((2, HEAD_SIZE * SCAN_TB, LANES), F32), pltpu.VMEM((HEAD_SIZE, LANES), F32)],
        compiler_params=_cparams("arbitrary", "arbitrary", vmem_mb=48),
        name=name,
    )(*[a for a, _ in ct_seqs], v_seq[0], *params8, *params, s0)


def _sample_groups(x):
    return x.reshape(DEC_SEQ, DEC_BATCH // CHAIN_B, CHAIN_B, N_HEADS, HEAD_SIZE).transpose(1, 0, 2, 3, 4)


def _to_chain_sample_ct(x):
    x = _sample_groups(x).transpose(0, 4, 1, 2, 3).reshape(DEC_BATCH // CHAIN_B, HEAD_SIZE, DEC_SEQ, LANES)
    return jnp.pad(x, ((0, 0), (0, 0), (0, SCAN_TB - DEC_SEQ), (0, 0)))


def _to_chain_sample_tc(x):
    return _sample_groups(x).transpose(0, 1, 4, 2, 3).reshape(DEC_BATCH // CHAIN_B, DEC_SEQ, HEAD_SIZE, LANES)


def _from_chain_sample(y):
    ng = DEC_BATCH // CHAIN_B
    y = y.reshape(ng, DEC_SEQ, HEAD_SIZE, CHAIN_B, N_HEADS).transpose(1, 0, 3, 2, 4)
    return y.reshape(N_SAMPLE, RWKV_WIDTH)


def _param_chain(p):
    return jnp.tile(p.reshape(N_HEADS, HEAD_SIZE).T, (1, CHAIN_B))


def _param_chain8(p):
    return jnp.repeat(_param_chain(p), SCAN_TB, axis=0)


def _head_minor(w, axis):
    shape = w.shape
    w = w.reshape(shape[:axis] + (N_HEADS, HEAD_SIZE) + shape[axis + 1:])
    return jnp.swapaxes(w, axis, axis + 1).reshape(shape)


def _rwkv_out_body(yc_ref, ys_ref, xg_ref, g2_ref, w_ref, o_ref, yt, *, n_prompt_steps):
    s = pl.program_id(0)

    def finish(y):
        g = jnp.dot(_sigmoid(xg_ref[...]).astype(BF16), g2_ref[...], preferred_element_type=F32)
        o_ref[...] = jnp.dot((y * g).astype(BF16), w_ref[...], preferred_element_type=F32)

    @pl.when(s < n_prompt_steps)
    def _():
        b = s % CHAIN_B

        @pl.when(b == 0)
        def _():
            _load_chain_tc(yc_ref, yt, TM_T // LANES)

        finish(jnp.concatenate([yt[b, half].T for half in range(TM_T // LANES)], axis=0))

    @pl.when(s >= n_prompt_steps)
    def _():
        finish(ys_ref[...])


def _rwkv_out(y_chain, y_s, xg, g2p, w):
    n_tb = SEQ // TM_T
    nps = n_tb * CHAIN_B

    def row_blk(s):
        return jnp.where(s < nps, (s % CHAIN_B) * n_tb + s // CHAIN_B, s)

    return pl.pallas_call(
        functools.partial(_rwkv_out_body, n_prompt_steps=nps),
        grid=(N_TOK // TM_T,),
        in_specs=[pl.BlockSpec((TM_T * HEAD_SIZE, LANES), lambda s: (jnp.minimum(s // CHAIN_B, n_tb - 1), 0)),
                  pl.BlockSpec((TM_T, RWKV_WIDTH), lambda s: (jnp.maximum(s - nps, 0), 0)),
                  pl.BlockSpec((TM_T, LORA_W - LANES), lambda s: (row_blk(s), 0)),
                  _resident(g2p.shape), _resident((RWKV_WIDTH, D_MODEL))],
        out_specs=pl.BlockSpec((TM_T, D_MODEL), lambda s: (row_blk(s), 0)),
        out_shape=jax.ShapeDtypeStruct((N_TOK, D_MODEL), F32),
        scratch_shapes=[pltpu.VMEM((CHAIN_B, TM_T // LANES, RWKV_WIDTH, LANES), F32)],
        compiler_params=_cparams("arbitrary", vmem_mb=56),
        name="rwkv_out",
    )(y_chain, y_s, xg, g2p, w)


def _mix_body(co_ref, ro_ref, xn_ref, wgr_ref, xp_ref, xs_ref, wm_ref, nf_ref, wr_ref, br_ref,
              h_ref, hn_ref, ridx_ref, rw_ref, *, n_prompt_tiles):
    i = pl.program_id(0)
    g_rwkv = jnp.dot(xn_ref[...], wgr_ref[...], preferred_element_type=F32)
    mixed = co_ref[...] + _sigmoid(g_rwkv) * ro_ref[...]
    mo = jnp.dot(mixed.astype(BF16), wm_ref[...], preferred_element_type=F32)

    def finish(x):
        h = x + mo
        h_ref[...] = h
        ms = jnp.mean(h * h, axis=-1, keepdims=True)
        hn = h * lax.rsqrt(ms + RMS_EPS) * nf_ref[...]
        hnb = hn.astype(BF16)
        hn_ref[...] = hnb.reshape(hn.shape[0], D_MODEL // LANES, LANES)
        logits = jnp.dot(hnb, wr_ref[...], preferred_element_type=F32) + br_ref[...]
        tm = logits.shape[0]
        lane = lax.broadcasted_iota(jnp.int32, (tm, LANES), 1)
        neg = jnp.float32(-jnp.inf)
        gl = jnp.where(lane < N_GROUPS, logits, neg)
        gmax = jnp.max(gl, axis=-1, keepdims=True)
        g_idx = jnp.min(jnp.where(gl == gmax, lane, LANES), axis=-1, keepdims=True)
        g_w = 1.0 / jnp.sum(jnp.exp(gl - gmax), axis=-1, keepdims=True)
        lo = N_GROUPS + g_idx * EXPERTS_PER_GROUP
        el = jnp.where(jnp.logical_and(lane >= lo, lane < lo + EXPERTS_PER_GROUP), logits, neg)
        m1 = jnp.max(el, axis=-1, keepdims=True)
        i1 = jnp.min(jnp.where(el == m1, lane, LANES), axis=-1, keepdims=True)
        el2 = jnp.where(lane == i1, neg, el)
        m2 = jnp.max(el2, axis=-1, keepdims=True)
        i2 = jnp.min(jnp.where(el2 == m2, lane, LANES), axis=-1, keepdims=True)
        t2 = jnp.exp(m2 - m1)
        den = 1.0 + t2
        ridx_ref[...] = jnp.where(lane == 0, i1 - N_GROUPS, jnp.where(lane == 1, i2 - N_GROUPS, 0))
        rw_ref[...] = jnp.where(lane == 0, (1.0 / den) * g_w, jnp.where(lane == 1, (t2 / den) * g_w, 0.0))

    @pl.when(i < n_prompt_tiles)
    def _():
        finish(xp_ref[...])

    @pl.when(i >= n_prompt_tiles)
    def _():
        finish(xs_ref[...])


def _mix_route(conv_out, rwkv_out, xn, wgr, xp, xs, wm, nf, wr, br):
    npt = N_PROMPT // TM_S
    tok_spec = pl.BlockSpec((TM_S, D_MODEL), lambda i: (i, 0))
    small_spec = pl.BlockSpec((TM_S, LANES), lambda i: (i, 0))
    return pl.pallas_call(
        functools.partial(_mix_body, n_prompt_tiles=npt),
        grid=(N_TOK // TM_S,),
        in_specs=[tok_spec, tok_spec, tok_spec, _resident((D_MODEL, D_MODEL)),
                  pl.BlockSpec((TM_S, D_MODEL), lambda i: (jnp.minimum(i, npt - 1), 0)),
                  pl.BlockSpec((TM_S, D_MODEL), lambda i: (jnp.maximum(i - npt, 0), 0)),
                  _resident((D_MODEL, D_MODEL)),
                  pl.BlockSpec((1, D_MODEL), lambda i: (0, 0)),
                  _resident((D_MODEL, LANES)),
                  pl.BlockSpec((1, LANES), lambda i: (0, 0))],
        out_specs=[tok_spec, pl.BlockSpec((TM_S, D_MODEL // LANES, LANES), lambda i: (i, 0, 0)),
                   small_spec, small_spec],
        out_shape=[jax.ShapeDtypeStruct((N_TOK, D_MODEL), F32),
                   jax.ShapeDtypeStruct((N_TOK, D_MODEL // LANES, LANES), BF16),
                   jax.ShapeDtypeStruct((N_TOK, LANES), jnp.int32),
                   jax.ShapeDtypeStruct((N_TOK, LANES), F32)],
        compiler_params=_cparams("arbitrary", vmem_mb=56),
        name="mix_route",
    )(conv_out, rwkv_out, xn, wgr, xp, xs, wm, nf, wr, br)


def _expert_body(blk_e_ref, slot_tok_ref, nused_ref, first_ref, par_ref, next_e_ref,
                 hn_ref, wg_hbm, wu_hbm, wd_hbm, yb_ref,
                 xbuf, sem, wfg, wfu, wfd, wsem, wgb, wub, wdb):
    i = pl.program_id(0)
    nused = nused_ref[0]
    slot = i % N_XBUF

    def row_copy(blk, r, s):
        tok = slot_tok_ref[blk * MOE_BLOCK + r]
        return pltpu.make_async_copy(hn_ref.at[tok], xbuf.at[s, r], sem.at[s])

    def issue(blk, s):
        for r in range(MOE_BLOCK):
            row_copy(blk, r, s).start(priority=r % 2)

    def w_copies(e, s):
        return (pltpu.make_async_copy(wg_hbm.at[e], wfg.at[s], wsem.at[s]),
                pltpu.make_async_copy(wu_hbm.at[e], wfu.at[s], wsem.at[s]),
                pltpu.make_async_copy(wd_hbm.at[e], wfd.at[s], wsem.at[s]))

    @pl.when(jnp.logical_and(i == 0, nused > 0))
    def _():
        for c in w_copies(blk_e_ref[0], 0):
            c.start(priority=1)
        for a in range(GATHER_AHEAD):
            @pl.when(a < nused)
            def _():
                issue(a, a)

    @pl.when(i + GATHER_AHEAD < nused)
    def _():
        issue(i + GATHER_AHEAD, (i + GATHER_AHEAD) % N_XBUF)

    @pl.when(i < nused)
    def _():
        @pl.when(first_ref[i] == 1)
        def _():
            ws = par_ref[i]
            for c in w_copies(blk_e_ref[i], ws):
                c.wait()

            @pl.when(next_e_ref[i] >= 0)
            def _():
                for c in w_copies(next_e_ref[i], 1 - ws):
                    c.start(priority=1)

            wgb[...] = wfg[ws].astype(BF16)
            wub[...] = wfu[ws].astype(BF16)
            wdb[...] = wfd[ws].astype(BF16)

        for r in range(MOE_BLOCK):
            row_copy(i, r, slot).wait()
        xe = xbuf[slot].reshape(MOE_BLOCK, D_MODEL)
        gate = jnp.dot(xe, wgb[...], preferred_element_type=F32)
        up = jnp.dot(xe, wub[...], preferred_element_type=F32)
        hdn = (gate * _sigmoid(gate)) * up
        yb = jnp.dot(hdn.astype(BF16), wdb[...], preferred_element_type=F32)
        yb_ref[...] = yb.reshape(MOE_BLOCK, D_MODEL // LANES, LANES)

    @pl.when(i >= nused)
    def _():
        yb_ref[...] = jnp.zeros_like(yb_ref)


def _experts(plan, hn, wg, wu, wd):
    blk_e, slot_tok, nused, first, par, next_e = plan
    n_blocks = blk_e.shape[0]
    any_spec = pl.BlockSpec(memory_space=pl.ANY)
    return pl.pallas_call(
        _expert_body,
        grid_spec=pltpu.PrefetchScalarGridSpec(
            num_scalar_prefetch=6,
            grid=(n_blocks,),
            in_specs=[any_spec, any_spec, any_spec, any_spec],
            out_specs=pl.BlockSpec((MOE_BLOCK, D_MODEL // LANES, LANES), lambda i, *_: (i, 0, 0)),
            scratch_shapes=[pltpu.VMEM((N_XBUF, MOE_BLOCK, D_MODEL // LANES, LANES), BF16),
                            pltpu.SemaphoreType.DMA((N_XBUF,)),
                            pltpu.VMEM((2, D_MODEL, D_EXPERT), F32),
                            pltpu.VMEM((2, D_MODEL, D_EXPERT), F32),
                            pltpu.VMEM((2, D_EXPERT, D_MODEL), F32),
                            pltpu.SemaphoreType.DMA((2,)),
                            pltpu.VMEM((D_MODEL, D_EXPERT), BF16),
                            pltpu.VMEM((D_MODEL, D_EXPERT), BF16),
                            pltpu.VMEM((D_EXPERT, D_MODEL), BF16)]),
        out_shape=jax.ShapeDtypeStruct((n_blocks * MOE_BLOCK, D_MODEL // LANES, LANES), F32),
        compiler_params=_cparams("arbitrary", vmem_mb=48),
        name="experts",
    )(blk_e, slot_tok, nused, first, par, next_e, hn, wg, wu, wd)


def _combine_body(dest_ref, yb_ref, h_ref, rw_ref, p_ref, wpg_ref, wpp_ref, nf_ref, yp_ref, ys_ref,
                  ybuf, sem):
    i = pl.program_id(0)
    tm = h_ref.shape[0]
    slot = i % 2

    def row_copy(tile, r, s, sl):
        d = dest_ref[(tile * tm + r) * TOP_K + s]
        return pltpu.make_async_copy(yb_ref.at[d], ybuf.at[sl, s, r], sem.at[sl])

    def issue(tile, sl):
        for r in range(tm):
            row_copy(tile, r, 0, sl).start()
            row_copy(tile, r, 1, sl).start()

    @pl.when(i == 0)
    def _():
        issue(0, 0)

    @pl.when(i + 1 < pl.num_programs(0))
    def _():
        issue(i + 1, 1 - slot)

    for r in range(tm):
        row_copy(i, r, 0, slot).wait()
        row_copy(i, r, 1, slot).wait()

    rw = rw_ref[...]
    y0 = ybuf[slot, 0].reshape(tm, D_MODEL)
    y1 = ybuf[slot, 1].reshape(tm, D_MODEL)
    h2 = h_ref[...] + (y0 * rw[:, 0:1] + y1 * rw[:, 1:2])
    gate = _sigmoid(jnp.dot(h2.astype(BF16), wpg_ref[...], preferred_element_type=F32))
    pp = jnp.dot(p_ref[...].astype(BF16), wpp_ref[...], preferred_element_type=F32)
    h3 = h2 + gate * pp
    ms = jnp.mean(h3 * h3, axis=-1, keepdims=True)
    y = h3 * lax.rsqrt(ms + RMS_EPS) * nf_ref[...]

    @pl.when(i < N_PROMPT // TM_S)
    def _():
        yp_ref[...] = y

    @pl.when(i >= N_PROMPT // TM_S)
    def _():
        ys_ref[...] = y


def _combine(dest, yb, h, rw, p_all, wpg, wpp, nf):
    npt = N_PROMPT // TM_S
    return pl.pallas_call(
        _combine_body,
        grid_spec=pltpu.PrefetchScalarGridSpec(
            num_scalar_prefetch=1,
            grid=(N_TOK // TM_S,),
            in_specs=[pl.BlockSpec(memory_space=pl.ANY),
                      pl.BlockSpec((TM_S, D_MODEL), lambda i, d: (i, 0)),
                      pl.BlockSpec((TM_S, LANES), lambda i, d: (i, 0)),
                      pl.BlockSpec((TM_S, PLE_DIM), lambda i, d: (i, 0)),
                      pl.BlockSpec((D_MODEL, D_MODEL), lambda i, d: (0, 0), pipeline_mode=pl.Buffered(1)),
                      pl.BlockSpec((PLE_DIM, D_MODEL), lambda i, d: (0, 0), pipeline_mode=pl.Buffered(1)),
                      pl.BlockSpec((1, D_MODEL), lambda i, d: (0, 0))],
            out_specs=[pl.BlockSpec((TM_S, D_MODEL), lambda i, d: (jnp.minimum(i, npt - 1), 0)),
                       pl.BlockSpec((TM_S, D_MODEL), lambda i, d: (jnp.maximum(i - npt, 0), 0))],
            scratch_shapes=[pltpu.VMEM((2, TOP_K, TM_S, D_MODEL // LANES, LANES), F32),
                            pltpu.SemaphoreType.DMA((2,))]),
        out_shape=[jax.ShapeDtypeStruct((N_PROMPT, D_MODEL), F32),
                   jax.ShapeDtypeStruct((N_SAMPLE, D_MODEL), F32)],
        compiler_params=_cparams("arbitrary", vmem_mb=48),
        name="combine_ple",
    )(dest, yb, h, rw, p_all, wpg, wpp, nf)


def _dispatch_plan(eidx):
    n_assign = N_TOK * TOP_K
    e_flat = eidx.reshape(n_assign)
    onehot = (e_flat[:, None] == jnp.arange(N_EXPERTS, dtype=jnp.int32)[None, :]).astype(jnp.int32)
    csum = jnp.cumsum(onehot, axis=0)
    counts = csum[-1]
    rank = jnp.sum(csum * onehot, axis=1) - 1
    padded = (counts + MOE_BLOCK - 1) // MOE_BLOCK * MOE_BLOCK
    pad_end = jnp.cumsum(padded)
    pad_start = pad_end - padded
    dest = pad_start[e_flat] + rank
    n_blocks = -(-n_assign // MOE_BLOCK) + N_EXPERTS
    tok = jnp.arange(n_assign, dtype=jnp.int32) // TOP_K
    pad_tok = jnp.arange(n_blocks * MOE_BLOCK, dtype=jnp.int32) % N_TOK
    slot_tok = pad_tok.at[dest].set(tok)
    blk = jnp.arange(n_blocks, dtype=jnp.int32)
    blk_e = jnp.minimum(jnp.searchsorted(pad_end, blk * MOE_BLOCK, side="right"), N_EXPERTS - 1).astype(jnp.int32)
    nused = (pad_end[-1] // MOE_BLOCK).astype(jnp.int32)
    prev_e = jnp.concatenate([jnp.full((1,), -1, jnp.int32), blk_e[:-1]])
    first = jnp.logical_and(blk < nused, blk_e != prev_e)
    par = ((jnp.cumsum(first.astype(jnp.int32)) - 1) % 2).astype(jnp.int32)
    idx_first = jnp.where(first, blk, n_blocks)
    later = jnp.concatenate([lax.cummin(idx_first[::-1])[::-1][1:], jnp.full((1,), n_blocks, jnp.int32)])
    next_e = jnp.where(later < n_blocks, blk_e[jnp.minimum(later, n_blocks - 1)], -1).astype(jnp.int32)
    plan = (blk_e, slot_tok, nused.reshape(1), first.astype(jnp.int32), par, next_e)
    return dest.astype(jnp.int32), plan


def kernel(x_prompt, x_sample, state_conv, state_shift, state_wkv, p_prompt, p_sample, norm_mix, w_in, conv_w, w_conv_out, shift_mu, w0, w2, a0, a2, g2, k_k, k_a, r_k, lnx_w, lnx_b, w_rwkv_out, w_mix_out, norm_ffn, w_route_group, b_route_group, w_route_expert, b_route_expert, w_exp_gate, w_exp_up, w_exp_down, w_ple_proj, w_ple_gate, norm_final):
    c3 = 3 * CONV_WIDTH
    rw3 = 3 * RWKV_WIDTH
    xp = x_prompt.reshape(N_PROMPT, D_MODEL)
    xs = x_sample.transpose(1, 0, 2).reshape(N_SAMPLE, D_MODEL)
    win = w_in[0]
    mu = shift_mu[0]
    st = state_shift[0]

    xn = _norm_cast(xp, xs, norm_mix)
    rkv_s, _, sraw_rkv = _inproj(xn, win, mu[None, :rw3], st[:, :rw3],
                                 tn=1024, col_blk_off=c3 // 1024, n_col_blocks=rw3 // 1024,
                                 shift_lo=0, shift_hi=rw3 // 1024,
                                 tile_lo=N_PROMPT // TM, n_tiles=N_SAMPLE // TM, name="inproj_rkv_sample")
    n_lr = RWKV_PROJ - rw3
    mu_l = jnp.concatenate([mu[rw3:], jnp.zeros((LORA_W - n_lr,), F32)])[None, :]
    st_l = jnp.concatenate([st[:, rw3:], jnp.zeros((DEC_BATCH, LORA_W - n_lr), F32)], axis=1)
    w_l = win[:, c3 + rw3:c3 + rw3 + LORA_W].astype(BF16)
    w_gc = win[:, c3 + RWKV_PROJ:c3 + RWKV_PROJ + D_MODEL].astype(BF16)
    w_gr = win[:, c3 + RWKV_PROJ + D_MODEL:].astype(BF16)
    w_rk = _head_minor(win[:, c3:c3 + 2 * RWKV_WIDTH].reshape(D_MODEL, 2, RWKV_WIDTH), 2)
    rk_chain = _inproj_t(xn, w_rk.reshape(D_MODEL, 2 * RWKV_WIDTH).T.astype(BF16),
                         time_major=False, name="inproj_t_rk")
    v_chain = _inproj_t(xn, _head_minor(win[:, c3 + 2 * RWKV_WIDTH:c3 + rw3], 1).T.astype(BF16),
                        time_major=True, name="inproj_t_v")

    sc = state_conv[0].transpose(1, 0, 2).reshape(2 * DEC_BATCH, CONV_WIDTH)
    conv_out, ulast, us = _conv_branch(xn, win[:, :c3].astype(BF16), conv_w[0], sc,
                                       w_conv_out[0].astype(BF16), w_gc)

    def pad_rows(w, before, total):
        return jnp.pad(w, ((before, total - before - w.shape[0]), (0, 0))).astype(BF16)

    w2p = pad_rows(w2[0], 0, LANES)
    a2p = pad_rows(a2[0], DECAY_LORA, 2 * LANES)
    g2p = pad_rows(_head_minor(g2[0], 1), DECAY_LORA + A_LORA - LANES, LORA_W - LANES)
    xg, wl_c, al_c, wl_s, al_s, last_l, sraw_l = _lora(xn, w_l, mu_l, st_l, _head_minor(w2p, 1).T,
                                                       _head_minor(a2p, 1).T, w2p, a2p)
    par8 = [_param_chain8(p) for p in (k_k[0], k_a[0], r_k[0].reshape(RWKV_WIDTH))]
    mu8 = [_param_chain8(mu[n * RWKV_WIDTH:(n + 1) * RWKV_WIDTH]) for n in range(2)]
    bias8 = [_param_chain8(w0[0]), _param_chain8(a0[0])]
    zero8 = jnp.zeros((HEAD_SIZE * SCAN_TB, LANES), F32)
    mu_v = _param_chain(mu[2 * RWKV_WIDTH:rw3])
    gn = [_param_chain(lnx_w[0]), _param_chain(lnx_b[0])]
    seqs_p = [(rk_chain, 0), (rk_chain, 1), (wl_c[None], None), (al_c[None], None)]
    v4 = v_chain.reshape(1, SEQ, HEAD_SIZE, LANES)
    s0_p = jnp.zeros((1, HEAD_SIZE, HEAD_SIZE, LANES), F32)
    y_p, sf_p = _scan(seqs_p, (v4, None), par8 + mu8 + bias8, [mu_v] + gn, s0_p,
                      n_batches=SCAN_TT // SCAN_TB, steps=SCAN_TB, name="wkv_scan_prompt")
    ng = DEC_BATCH // CHAIN_B
    s0_s = state_wkv[0].reshape(ng, CHAIN_B, N_HEADS, HEAD_SIZE, HEAD_SIZE).transpose(0, 4, 3, 1, 2)
    s0_s = s0_s.reshape(ng, HEAD_SIZE, HEAD_SIZE, LANES)
    seqs_s = [(_to_chain_sample_ct(rkv_s[:, n * RWKV_WIDTH:(n + 1) * RWKV_WIDTH]), None) for n in range(2)]
    seqs_s += [(_to_chain_sample_ct(wl_s), None), (_to_chain_sample_ct(al_s), None)]
    v_s = _to_chain_sample_tc(rkv_s[:, 2 * RWKV_WIDTH:rw3])
    y_s, sf_s = _scan(seqs_s, (v_s, None), par8 + [zero8, zero8] + bias8,
                      [jnp.zeros((HEAD_SIZE, LANES), F32)] + gn, s0_s,
                      n_batches=1, steps=DEC_SEQ, name="wkv_scan_sample")
    rwkv_out = _rwkv_out(y_p.reshape(SEQ * HEAD_SIZE, LANES), _from_chain_sample(y_s), xg, g2p,
                         _head_minor(w_rwkv_out[0], 0).astype(BF16))

    wr = jnp.concatenate([w_route_group[0], w_route_expert[0],
                          jnp.zeros((D_MODEL, LANES - N_GROUPS - N_EXPERTS), F32)], axis=1)
    br = jnp.concatenate([b_route_group[0], b_route_expert[0],
                          jnp.zeros((LANES - N_GROUPS - N_EXPERTS,), F32)])[None, :]
    h, hn, ridx, rw = _mix_route(conv_out, rwkv_out, xn, w_gr, xp, xs, w_mix_out[0].astype(BF16),
                                 norm_ffn, wr.astype(BF16), br)

    dest, plan = _dispatch_plan(ridx[:, :TOP_K])
    yb = _experts(plan, hn, w_exp_gate[0], w_exp_up[0], w_exp_down[0])
    p_all = jnp.concatenate([p_prompt[0].reshape(N_PROMPT, PLE_DIM),
                             p_sample[0].transpose(1, 0, 2).reshape(N_SAMPLE, PLE_DIM)], axis=0)
    y_p2, y_s2 = _combine(dest, yb, h, rw, p_all, w_ple_gate[0].astype(BF16), w_ple_proj[0].astype(BF16),
                          norm_final[None, :])

    y_prompt = y_p2.reshape(BATCH, SEQ, D_MODEL)
    y_sample = y_s2.reshape(DEC_SEQ, DEC_BATCH, D_MODEL).transpose(1, 0, 2)
    tiles_per_seq = SEQ // TM
    seq_last = jnp.arange(BATCH) * tiles_per_seq + tiles_per_seq - 1

    conv_p = ulast.reshape(-1, SUBLANES, CONV_WIDTH)[seq_last, SUBLANES - 2:, :][None]
    conv_s = us.reshape(2, DEC_BATCH, CONV_WIDTH).transpose(1, 0, 2)[None]
    last_rkv = jnp.concatenate([rk_chain[:, :, SEQ - 1, :], v4[:, SEQ - 1]], axis=0)
    lm = last_rkv.reshape(3, HEAD_SIZE, BATCH, N_HEADS).transpose(2, 0, 3, 1).reshape(BATCH, rw3)
    lora_last = (SEQ // TM_L - 1) * CHAIN_B + jnp.arange(BATCH)
    lt = last_l.reshape(-1, SUBLANES, LORA_W)[lora_last, SUBLANES - 1, :n_lr]
    shift_p = jnp.concatenate([lm, lt], axis=1)[None]
    shift_s = jnp.concatenate([sraw_rkv, sraw_l[:, :n_lr]], axis=1)[None]
    wkv_p = sf_p.reshape(HEAD_SIZE, HEAD_SIZE, BATCH, N_HEADS).transpose(2, 3, 1, 0)[None]
    wkv_s = sf_s.reshape(ng, HEAD_SIZE, HEAD_SIZE, CHAIN_B, N_HEADS).transpose(0, 3, 4, 2, 1)
    wkv_s = wkv_s.reshape(DEC_BATCH, N_HEADS, HEAD_SIZE, HEAD_SIZE)[None]
    return (y_prompt, y_sample, conv_p, shift_p, wkv_p, conv_s, shift_s, wkv_s)
```

```python
import functools

import jax
import jax.numpy as jnp
from jax import lax
from jax.experimental import pallas as pl
from jax.experimental.pallas import tpu as pltpu

D_MODEL = 2048
BATCH = 4
SEQ = 2048
DEC_BATCH = 128
DEC_SEQ = 4
CONV_WIDTH = 1024
RWKV_WIDTH = 2048
HEAD_SIZE = 64
N_HEADS = RWKV_WIDTH // HEAD_SIZE
DECAY_LORA = 96
A_LORA = 96
GATE_LORA = 256
RWKV_PROJ = 3 * RWKV_WIDTH + DECAY_LORA + A_LORA + GATE_LORA
N_GROUPS = 8
EXPERTS_PER_GROUP = 8
N_EXPERTS = N_GROUPS * EXPERTS_PER_GROUP
TOP_K = 2
D_EXPERT = 512
MOE_BLOCK = 128
PLE_DIM = 256
RMS_EPS = 1e-6
GN_EPS = 64e-5

N_PROMPT = BATCH * SEQ
N_SAMPLE = DEC_BATCH * DEC_SEQ
N_TOK = N_PROMPT + N_SAMPLE
LANES = 128
SUBLANES = 8
TM = 512
TM_S = 256
TM_T = 256
TM_L = 128
CHAIN_B = LANES // N_HEADS
SCAN_TT = 64
SCAN_TB = SUBLANES
SCAN_ACC = 2
GATHER_AHEAD = 3
N_XBUF = GATHER_AHEAD + 1
LORA_W = 512
EXP_M05 = 0.6065306597126334
F32 = jnp.float32
BF16 = jnp.bfloat16
_NT = (((1,), (1,)), ((), ()))


def _sigmoid(x):
    return 1.0 / (1.0 + jnp.exp(-x))


def _cparams(*sem, vmem_mb=None):
    kw = dict(dimension_semantics=sem)
    if vmem_mb is not None:
        kw["vmem_limit_bytes"] = vmem_mb * 1024 * 1024
    return pltpu.CompilerParams(**kw)


def _resident(shape):
    nd = len(shape)
    return pl.BlockSpec(shape, lambda *_: (0,) * nd, pipeline_mode=pl.Buffered(1))


def _chain_tile(zt, half, c):
    start = c * N_HEADS
    rows = pl.ds(start if isinstance(c, int) else pl.multiple_of(start, N_HEADS), N_HEADS)
    return jnp.concatenate([zt[bb, half, rows, :] for bb in range(CHAIN_B)], axis=0)


def _store_chain_ct(zt, out_ref, halves, c_lo=0, n_c=HEAD_SIZE):
    for half in range(halves):
        for ci in range(n_c):
            out_ref[c_lo + ci, half * LANES:(half + 1) * LANES, :] = _chain_tile(zt, half, c_lo + ci).T


def _store_chain_tc(zt, out_ref, halves, c_lo=0, n_c=HEAD_SIZE):
    for half in range(halves):
        for ci in range(n_c):
            rows = pl.ds(half * LANES * HEAD_SIZE + c_lo + ci, LANES, stride=HEAD_SIZE)
            out_ref[rows, :] = _chain_tile(zt, half, c_lo + ci).T


def _load_chain_tc(y_ref, yt, halves):
    for half in range(halves):
        for v in range(HEAD_SIZE):
            mt = y_ref[pl.ds(half * LANES * HEAD_SIZE + v, LANES, stride=HEAD_SIZE), :].T
            for bb in range(CHAIN_B):
                yt[bb, half, v * N_HEADS:(v + 1) * N_HEADS, :] = mt[bb * N_HEADS:(bb + 1) * N_HEADS, :]


def _norm_body(xp_ref, xs_ref, g_ref, o_ref, *, n_prompt_tiles):
    i = pl.program_id(0)

    def f(x):
        ms = jnp.mean(x * x, axis=-1, keepdims=True)
        return (x * lax.rsqrt(ms + RMS_EPS) * g_ref[...]).astype(o_ref.dtype)

    @pl.when(i < n_prompt_tiles)
    def _():
        o_ref[...] = f(xp_ref[...])

    @pl.when(i >= n_prompt_tiles)
    def _():
        o_ref[...] = f(xs_ref[...])


def _norm_cast(xp, xs, g):
    npt = N_PROMPT // TM
    return pl.pallas_call(
        functools.partial(_norm_body, n_prompt_tiles=npt),
        grid=(N_TOK // TM,),
        in_specs=[pl.BlockSpec((TM, D_MODEL), lambda i: (jnp.minimum(i, npt - 1), 0)),
                  pl.BlockSpec((TM, D_MODEL), lambda i: (jnp.maximum(i - npt, 0), 0)),
                  pl.BlockSpec((1, D_MODEL), lambda i: (0, 0))],
        out_specs=pl.BlockSpec((TM, D_MODEL), lambda i: (i, 0)),
        out_shape=jax.ShapeDtypeStruct((N_TOK, D_MODEL), BF16),
        compiler_params=_cparams("arbitrary"),
        name="norm_cast",
    )(xp, xs, g)


def _inproj_body(xn_ref, w_ref, mu_ref, st_ref, z_ref, last_ref, sraw_ref, wb_ref, carry_ref, *,
                 shift_lo, shift_hi, tile_lo, n_prompt_tiles, tiles_per_seq):
    j = pl.program_id(0)
    i = pl.program_id(1) + tile_lo

    @pl.when(pl.program_id(1) == 0)
    def _():
        wb_ref[...] = w_ref[...].astype(BF16)

    z = jnp.dot(xn_ref[...], wb_ref[...], preferred_element_type=F32)
    tm = z.shape[0]
    last_ref[...] = z[tm - SUBLANES:tm]
    shifted = jnp.logical_and(j >= shift_lo, j < shift_hi)
    is_prompt = i < n_prompt_tiles

    @pl.when(jnp.logical_not(shifted))
    def _():
        z_ref[...] = z

    @pl.when(jnp.logical_and(shifted, is_prompt))
    def _():
        @pl.when(i % tiles_per_seq == 0)
        def _():
            carry_ref[...] = jnp.zeros_like(carry_ref)

        prev = pltpu.roll(z, 1, 0)
        row = lax.broadcasted_iota(jnp.int32, (tm, 1), 0)
        prev = jnp.where(row == 0, carry_ref[SUBLANES - 1:SUBLANES, :], prev)
        z_ref[...] = z + mu_ref[...] * (prev - z)
        carry_ref[...] = z[tm - SUBLANES:tm]

    @pl.when(jnp.logical_and(shifted, jnp.logical_not(is_prompt)))
    def _():
        prev = jnp.concatenate([st_ref[...], z[:tm - DEC_BATCH]], axis=0)
        z_ref[...] = z + mu_ref[...] * (prev - z)

    @pl.when(jnp.logical_not(is_prompt))
    def _():
        sraw_ref[...] = z[tm - DEC_BATCH:tm]


def _inproj(xn, w, mu, st, *, tn, col_blk_off, n_col_blocks, shift_lo, shift_hi, tile_lo, n_tiles, name):
    n_out = tn * n_col_blocks
    body = functools.partial(_inproj_body, shift_lo=shift_lo, shift_hi=shift_hi, tile_lo=tile_lo,
                             n_prompt_tiles=N_PROMPT // TM, tiles_per_seq=SEQ // TM)
    return pl.pallas_call(
        body,
        grid=(n_col_blocks, n_tiles),
        in_specs=[pl.BlockSpec((TM, D_MODEL), lambda j, i: (i + tile_lo, 0)),
                  pl.BlockSpec((D_MODEL, tn), lambda j, i: (0, j + col_blk_off)),
                  pl.BlockSpec((1, tn), lambda j, i: (0, j)),
                  pl.BlockSpec((DEC_BATCH, tn), lambda j, i: (0, j))],
        out_specs=[pl.BlockSpec((TM, tn), lambda j, i: (i, j)),
                   pl.BlockSpec((SUBLANES, tn), lambda j, i: (i, j)),
                   pl.BlockSpec((DEC_BATCH, tn), lambda j, i: (0, j))],
        out_shape=[jax.ShapeDtypeStruct((n_tiles * TM, n_out), F32),
                   jax.ShapeDtypeStruct((n_tiles * SUBLANES, n_out), F32),
                   jax.ShapeDtypeStruct((DEC_BATCH, n_out), F32)],
        scratch_shapes=[pltpu.VMEM((D_MODEL, tn), BF16), pltpu.VMEM((SUBLANES, tn), F32)],
        compiler_params=_cparams("arbitrary", "arbitrary", vmem_mb=48),
        name=name,
    )(xn, w, mu, st)


def _inproj_t_body(xn_ref, wt_ref, o_ref, zt, *, time_major, n_tb):
    tb = pl.program_id(1)
    b = pl.program_id(2)
    halves = TM_T // LANES
    n_c = HEAD_SIZE // CHAIN_B

    def matmul():
        z = lax.dot_general(wt_ref[...], xn_ref[...], _NT, preferred_element_type=F32)
        for half in range(halves):
            zt[tb % 2, b, half] = z[:, half * LANES:(half + 1) * LANES]

    def retile():
        store = _store_chain_tc if time_major else _store_chain_ct
        store(zt.at[1 - tb % 2], o_ref, halves, c_lo=b * n_c, n_c=n_c)

    @pl.when(tb == 0)
    def _():
        matmul()

    @pl.when(jnp.logical_and(tb > 0, tb < n_tb))
    def _():
        retile()
        matmul()

    @pl.when(tb == n_tb)
    def _():
        retile()


def _inproj_t(xn, wt, *, time_major, name):
    n_tb = SEQ // TM_T
    n = wt.shape[0] // RWKV_WIDTH

    def prev_tb(tb):
        return jnp.maximum(tb - 1, 0)

    if time_major:
        out_spec = pl.BlockSpec((None, TM_T * HEAD_SIZE, LANES), lambda j, tb, b: (j, prev_tb(tb), 0))
        out_shape = jax.ShapeDtypeStruct((n, SEQ * HEAD_SIZE, LANES), F32)
    else:
        out_spec = pl.BlockSpec((None, HEAD_SIZE, TM_T, LANES), lambda j, tb, b: (j, 0, prev_tb(tb), 0))
        out_shape = jax.ShapeDtypeStruct((n, HEAD_SIZE, SEQ, LANES), F32)
    return pl.pallas_call(
        functools.partial(_inproj_t_body, time_major=time_major, n_tb=n_tb),
        grid=(n, n_tb + 1, CHAIN_B),
        in_specs=[pl.BlockSpec((TM_T, D_MODEL), lambda j, tb, b: (b * n_tb + jnp.minimum(tb, n_tb - 1), 0)),
                  pl.BlockSpec((RWKV_WIDTH, D_MODEL), lambda j, tb, b: (j, 0), pipeline_mode=pl.Buffered(1))],
        out_specs=out_spec,
        out_shape=out_shape,
        scratch_shapes=[pltpu.VMEM((2, CHAIN_B, TM_T // LANES, RWKV_WIDTH, LANES), F32)],
        compiler_params=_cparams("arbitrary", "arbitrary", "arbitrary", vmem_mb=56),
        name=name,
    )(xn, wt)


def _conv_body(xn_ref, wc_ref, cw_ref, sc_ref, wco_ref, wgc_ref, o_ref, ulast_ref, us_ref, carry_ref, *,
               n_prompt_tiles, tiles_per_seq):
    i = pl.program_id(0)
    gate_b = jnp.dot(xn_ref[...], wc_ref[:, 0:CONV_WIDTH], preferred_element_type=F32)
    u = (jnp.dot(xn_ref[...], wc_ref[:, CONV_WIDTH:2 * CONV_WIDTH], preferred_element_type=F32)
         * jnp.dot(xn_ref[...], wc_ref[:, 2 * CONV_WIDTH:3 * CONV_WIDTH], preferred_element_type=F32))
    tm = u.shape[0]
    ulast_ref[...] = u[tm - SUBLANES:tm]
    w0 = cw_ref[0:1, :]
    w1 = cw_ref[1:2, :]
    w2 = cw_ref[2:3, :]

    def finish(p1, p2):
        conv = w0 * p2 + w1 * p1 + w2 * u
        y = (gate_b * conv).astype(BF16)
        g_conv = jnp.dot(xn_ref[...], wgc_ref[...], preferred_element_type=F32)
        o_ref[...] = _sigmoid(g_conv) * jnp.dot(y, wco_ref[...], preferred_element_type=F32)

    @pl.when(i < n_prompt_tiles)
    def _():
        @pl.when(i % tiles_per_seq == 0)
        def _():
            carry_ref[...] = jnp.zeros_like(carry_ref)

        row = lax.broadcasted_iota(jnp.int32, (tm, 1), 0)
        c1 = carry_ref[SUBLANES - 1:SUBLANES, :]
        c2 = carry_ref[SUBLANES - 2:SUBLANES - 1, :]
        p1 = jnp.where(row == 0, c1, pltpu.roll(u, 1, 0))
        p2 = jnp.where(row == 0, c2, jnp.where(row == 1, c1, pltpu.roll(u, 2, 0)))
        carry_ref[...] = u[tm - SUBLANES:tm]
        finish(p1, p2)

    @pl.when(i >= n_prompt_tiles)
    def _():
        p1 = jnp.concatenate([sc_ref[DEC_BATCH:2 * DEC_BATCH, :], u[:tm - DEC_BATCH]], axis=0)
        p2 = jnp.concatenate([sc_ref[...], u[:tm - 2 * DEC_BATCH]], axis=0)
        us_ref[...] = u[tm - 2 * DEC_BATCH:tm]
        finish(p1, p2)


def _conv_branch(xn, wc, conv_w, sc, wco, wgc):
    n_tiles = N_TOK // TM
    body = functools.partial(_conv_body, n_prompt_tiles=N_PROMPT // TM, tiles_per_seq=SEQ // TM)
    return pl.pallas_call(
        body,
        grid=(n_tiles,),
        in_specs=[pl.BlockSpec((TM, D_MODEL), lambda i: (i, 0)),
                  _resident((D_MODEL, 3 * CONV_WIDTH)),
                  pl.BlockSpec((3, CONV_WIDTH), lambda i: (0, 0)),
                  pl.BlockSpec((2 * DEC_BATCH, CONV_WIDTH), lambda i: (0, 0)),
                  _resident((CONV_WIDTH, D_MODEL)), _resident((D_MODEL, D_MODEL))],
        out_specs=[pl.BlockSpec((TM, D_MODEL), lambda i: (i, 0)),
                   pl.BlockSpec((SUBLANES, CONV_WIDTH), lambda i: (i, 0)),
                   pl.BlockSpec((2 * DEC_BATCH, CONV_WIDTH), lambda i: (0, 0))],
        out_shape=[jax.ShapeDtypeStruct((N_TOK, D_MODEL), F32),
                   jax.ShapeDtypeStruct((n_tiles * SUBLANES, CONV_WIDTH), F32),
                   jax.ShapeDtypeStruct((2 * DEC_BATCH, CONV_WIDTH), F32)],
        scratch_shapes=[pltpu.VMEM((SUBLANES, CONV_WIDTH), F32)],
        compiler_params=_cparams("arbitrary", vmem_mb=48),
        name="conv_branch",
    )(xn, wc, conv_w, sc, wco, wgc)


def _lora_body(xn_ref, wl_ref, mu_ref, st_ref, w2t_ref, a2t_ref, w2_ref, a2_ref, g2_ref,
               g_ref, wlc_ref, alc_ref, wls_ref, als_ref, last_ref, sraw_ref, zt, carry_p, carry_s, *,
               n_prompt_steps):
    s = pl.program_id(0)
    z = jnp.dot(xn_ref[...], wl_ref[...], preferred_element_type=F32)
    tm = z.shape[0]
    last_ref[...] = z[tm - SUBLANES:tm]

    def project(zl):
        tw = jnp.tanh(zl[:, 0:LANES]).astype(BF16)
        xa = zl[:, 0:2 * LANES].astype(BF16)
        g_ref[...] = jnp.dot(_sigmoid(zl[:, LANES:LORA_W]).astype(BF16), g2_ref[...], preferred_element_type=F32)
        return tw, xa

    @pl.when(s < n_prompt_steps)
    def _():
        b = s % CHAIN_B

        @pl.when(s < CHAIN_B)
        def _():
            carry_p[b] = jnp.zeros((SUBLANES, LORA_W), F32)

        row = lax.broadcasted_iota(jnp.int32, (tm, 1), 0)
        prev = jnp.where(row == 0, carry_p[b, SUBLANES - 1:SUBLANES, :], pltpu.roll(z, 1, 0))
        carry_p[b] = z[tm - SUBLANES:tm]
        tw, xa = project(z + mu_ref[...] * (prev - z))
        zt[0, b, 0] = lax.dot_general(w2t_ref[...], tw, _NT, preferred_element_type=F32)
        zt[1, b, 0] = lax.dot_general(a2t_ref[...], xa, _NT, preferred_element_type=F32)

        @pl.when(b == CHAIN_B - 1)
        def _():
            _store_chain_ct(zt.at[0], wlc_ref, 1)
            _store_chain_ct(zt.at[1], alc_ref, 1)

    @pl.when(s >= n_prompt_steps)
    def _():
        @pl.when(s == n_prompt_steps)
        def _():
            carry_s[...] = st_ref[...]

        prev = carry_s[...]
        carry_s[...] = z
        sraw_ref[...] = z
        tw, xa = project(z + mu_ref[...] * (prev - z))
        wls_ref[...] = jnp.dot(tw, w2_ref[...], preferred_element_type=F32)
        als_ref[...] = jnp.dot(xa, a2_ref[...], preferred_element_type=F32)


def _lora(xn, w_l, mu_l, st_l, w2t, a2t, w2p, a2p, g2b):
    n_tb = SEQ // TM_L
    nps = n_tb * CHAIN_B
    n_steps = N_TOK // TM_L

    def row_blk(s):
        return jnp.where(s < nps, (s % CHAIN_B) * n_tb + s // CHAIN_B, s)

    chain_spec = pl.BlockSpec((HEAD_SIZE, TM_L, LANES), lambda s: (0, jnp.minimum(s // CHAIN_B, n_tb - 1), 0))
    samp_spec = pl.BlockSpec((TM_L, RWKV_WIDTH), lambda s: (jnp.maximum(s - nps, 0), 0))
    chain_shape = jax.ShapeDtypeStruct((HEAD_SIZE, SEQ, LANES), F32)
    samp_shape = jax.ShapeDtypeStruct((N_SAMPLE, RWKV_WIDTH), F32)
    return pl.pallas_call(
        functools.partial(_lora_body, n_prompt_steps=nps),
        grid=(n_steps,),
        in_specs=[pl.BlockSpec((TM_L, D_MODEL), lambda s: (row_blk(s), 0)),
                  _resident(w_l.shape),
                  pl.BlockSpec((1, LORA_W), lambda s: (0, 0)),
                  pl.BlockSpec((DEC_BATCH, LORA_W), lambda s: (0, 0)),
                  _resident(w2t.shape), _resident(a2t.shape), _resident(w2p.shape), _resident(a2p.shape),
                  _resident(g2b.shape)],
        out_specs=[pl.BlockSpec((TM_L, RWKV_WIDTH), lambda s: (row_blk(s), 0)),
                   chain_spec, chain_spec, samp_spec, samp_spec,
                   pl.BlockSpec((SUBLANES, LORA_W), lambda s: (s, 0)),
                   pl.BlockSpec((DEC_BATCH, LORA_W), lambda s: (0, 0))],
        out_shape=[jax.ShapeDtypeStruct((N_TOK, RWKV_WIDTH), F32),
                   chain_shape, chain_shape, samp_shape, samp_shape,
                   jax.ShapeDtypeStruct((n_steps * SUBLANES, LORA_W), F32),
                   jax.ShapeDtypeStruct((DEC_BATCH, LORA_W), F32)],
        scratch_shapes=[pltpu.VMEM((2, CHAIN_B, 1, RWKV_WIDTH, LANES), F32),
                        pltpu.VMEM((CHAIN_B, SUBLANES, LORA_W), F32),
                        pltpu.VMEM((DEC_BATCH, LORA_W), F32)],
        compiler_params=_cparams("arbitrary", vmem_mb=48),
        name="lora",
    )(xn, w_l, mu_l, st_l, w2t, a2t, w2p, a2p, g2b)


def _scan_body(r_ref, k_ref, wl_ref, al_ref, v_ref, kk_ref, ka_ref, rk_ref, mur_ref, muk_ref, w0_ref, a0_ref,
               muv_ref, lw_ref, lb_ref, s0_ref,
               y_ref, s_ref, vec_ref, bon_ref, prevb_ref, prevv_ref, *, n_batches, steps):
    @pl.when(pl.program_id(1) == 0)
    def _():
        s_ref[...] = s0_ref[...]
        prevb_ref[...] = jnp.zeros_like(prevb_ref)
        prevv_ref[...] = jnp.zeros_like(prevv_ref)

    rows = HEAD_SIZE * SCAN_TB
    first_t = lax.broadcasted_iota(jnp.int32, (rows, 1), 0) % SCAN_TB == 0

    def batch(bi, carry):
        t0 = pl.multiple_of(bi * SCAN_TB, SCAN_TB)

        def load(ref):
            return ref[:, pl.ds(t0, SCAN_TB), :].reshape(rows, LANES)

        def cube(x):
            return x.reshape(HEAD_SIZE, SCAN_TB, LANES)

        def shifted(x, slot, mu_ref):
            prev = jnp.where(first_t, pltpu.roll(prevb_ref[slot], rows - (SCAN_TB - 1), 0), pltpu.roll(x, 1, 0))
            prevb_ref[slot] = x
            return x + mu_ref[...] * (prev - x)

        r = shifted(load(r_ref), 0, mur_ref)
        k = shifted(load(k_ref), 1, muk_ref)
        decay = jnp.exp(-EXP_M05 * _sigmoid(load(wl_ref) + w0_ref[...]))
        a = _sigmoid(load(al_ref) + a0_ref[...])
        kk = cube(k * kk_ref[...])
        nrm = jnp.sqrt(jnp.sum(kk * kk, axis=0))
        kk = kk * (1.0 / jnp.maximum(nrm, 1e-12))[None]
        kf = k * (1.0 + (a - 1.0) * ka_ref[...])
        vec_ref[0] = -kk
        vec_ref[1] = cube(decay)
        vec_ref[2] = kk * cube(a)
        vec_ref[3] = cube(kf)
        vec_ref[4] = cube(r)
        bon_ref[...] = jnp.sum(cube(r * kf * rk_ref[...]), axis=0)

        def step(tl, c):
            t = t0 + tl
            v_raw = v_ref[t]
            v = v_raw + muv_ref[...] * (prevv_ref[...] - v_raw)
            prevv_ref[...] = v_raw

            def row(j, kx):
                return vec_ref[j, kx, pl.ds(tl, HEAD_SIZE, stride=0), :]

            parts = [s_ref[kx] * row(0, kx) for kx in range(SCAN_ACC)]
            for kx in range(SCAN_ACC, HEAD_SIZE):
                parts[kx % SCAN_ACC] = parts[kx % SCAN_ACC] + s_ref[kx] * row(0, kx)
            sa = functools.reduce(lambda x, y: x + y, parts)

            parts = []
            for kx in range(HEAD_SIZE):
                sn = s_ref[kx] * row(1, kx) + sa * row(2, kx) + v * row(3, kx)
                s_ref[kx] = sn
                if kx < SCAN_ACC:
                    parts.append(sn * row(4, kx))
                else:
                    parts[kx % SCAN_ACC] = parts[kx % SCAN_ACC] + sn * row(4, kx)
            o = functools.reduce(lambda x, y: x + y, parts)

            mu = jnp.mean(o, axis=0, keepdims=True)
            dlt = o - mu
            var = jnp.mean(dlt * dlt, axis=0, keepdims=True)
            on = dlt * lax.rsqrt(var + GN_EPS) * lw_ref[...] + lb_ref[...]
            y_ref[t] = on + bon_ref[pl.ds(tl, HEAD_SIZE, stride=0), :] * v
            return c

        for tl in range(steps):
            step(tl, 0)
        return carry

    lax.fori_loop(0, n_batches, batch, 0)


def _scan(ct_seqs, v_seq, params8, params, s0, *, n_batches, steps, name):
    g = s0.shape[0]
    tt = n_batches * SCAN_TB
    ttv = v_seq[0].shape[1] if n_batches == 1 else n_batches * steps
    n_ti = ct_seqs[0][0].shape[2] // tt

    def spec(block, lead, tpos):
        def index(gi, ti):
            idx = [gi if lead is None else lead, 0, 0, 0]
            idx[tpos] = ti
            return tuple(idx)
        return pl.BlockSpec(block, index)

    ct_block = (None, HEAD_SIZE, tt, LANES)
    tv_block = (None, ttv, HEAD_SIZE, LANES)
    par8_spec = pl.BlockSpec((HEAD_SIZE * SCAN_TB, LANES), lambda gi, ti: (0, 0))
    par_spec = pl.BlockSpec((HEAD_SIZE, LANES), lambda gi, ti: (0, 0))
    st_spec = pl.BlockSpec((None, HEAD_SIZE, HEAD_SIZE, LANES), lambda gi, ti: (gi, 0, 0, 0))
    return pl.pallas_call(
        functools.partial(_scan_body, n_batches=n_batches, steps=steps),
        grid=(g, n_ti),
        in_specs=([spec(ct_block, lead, 2) for _, lead in ct_seqs] + [spec(tv_block, v_seq[1], 1)]
                  + [par8_spec] * len(params8) + [par_spec] * len(params) + [st_spec]),
        out_specs=[spec(tv_block, None, 1), st_spec],
        out_shape=[jax.ShapeDtypeStruct((g, n_ti * ttv, HEAD_SIZE, LANES), F32),
                   jax.ShapeDtypeStruct((g, HEAD_SIZE, HEAD_SIZE, LANES), F32)],
        scratch_shapes=[pltpu.VMEM((5, HEAD_SIZE, SCAN_TB, LANES), F32), pltpu.VMEM((SCAN_TB, LANES), F32),
                        pltpu.VMEM((2, HEAD_SIZE * SCAN_TB, LANES), F32), pltpu.VMEM((HEAD_SIZE, LANES), F32)],
        compiler_params=_cparams("arbitrary", "arbitrary", vmem_mb=48),
        name=name,
    )(*[a for a, _ in ct_seqs], v_seq[0], *params8, *params, s0)


def _sample_groups(x):
    return x.reshape(DEC_SEQ, DEC_BATCH // CHAIN_B, CHAIN_B, N_HEADS, HEAD_SIZE).transpose(1, 0, 2, 3, 4)


def _to_chain_sample_ct(x):
    x = _sample_groups(x).transpose(0, 4, 1, 2, 3).reshape(DEC_BATCH // CHAIN_B, HEAD_SIZE, DEC_SEQ, LANES)
    return jnp.pad(x, ((0, 0), (0, 0), (0, SCAN_TB - DEC_SEQ), (0, 0)))


def _to_chain_sample_tc(x):
    return _sample_groups(x).transpose(0, 1, 4, 2, 3).reshape(DEC_BATCH // CHAIN_B, DEC_SEQ, HEAD_SIZE, LANES)


def _from_chain_sample(y):
    ng = DEC_BATCH // CHAIN_B
    y = y.reshape(ng, DEC_SEQ, HEAD_SIZE, CHAIN_B, N_HEADS).transpose(1, 0, 3, 2, 4)
    return y.reshape(N_SAMPLE, RWKV_WIDTH)


def _param_chain(p):
    return jnp.tile(p.reshape(N_HEADS, HEAD_SIZE).T, (1, CHAIN_B))


def _param_chain8(p):
    return jnp.repeat(_param_chain(p), SCAN_TB, axis=0)


def _head_minor(w, axis):
    shape = w.shape
    w = w.reshape(shape[:axis] + (N_HEADS, HEAD_SIZE) + shape[axis + 1:])
    return jnp.swapaxes(w, axis, axis + 1).reshape(shape)


def _rwkv_out_body(yc_ref, ys_ref, g_ref, w_ref, o_ref, yt, *, n_prompt_steps):
    s = pl.program_id(0)

    def finish(y):
        o_ref[...] = jnp.dot((y * g_ref[...]).astype(BF16), w_ref[...], preferred_element_type=F32)

    @pl.when(s < n_prompt_steps)
    def _():
        b = s % CHAIN_B

        @pl.when(b == 0)
        def _():
            _load_chain_tc(yc_ref, yt, TM_T // LANES)

        finish(jnp.concatenate([yt[b, half].T for half in range(TM_T // LANES)], axis=0))

    @pl.when(s >= n_prompt_steps)
    def _():
        finish(ys_ref[...])


def _rwkv_out(y_chain, y_s, g, w):
    n_tb = SEQ // TM_T
    nps = n_tb * CHAIN_B

    def row_blk(s):
        return jnp.where(s < nps, (s % CHAIN_B) * n_tb + s // CHAIN_B, s)

    return pl.pallas_call(
        functools.partial(_rwkv_out_body, n_prompt_steps=nps),
        grid=(N_TOK // TM_T,),
        in_specs=[pl.BlockSpec((TM_T * HEAD_SIZE, LANES), lambda s: (jnp.minimum(s // CHAIN_B, n_tb - 1), 0)),
                  pl.BlockSpec((TM_T, RWKV_WIDTH), lambda s: (jnp.maximum(s - nps, 0), 0)),
                  pl.BlockSpec((TM_T, RWKV_WIDTH), lambda s: (row_blk(s), 0)),
                  _resident((RWKV_WIDTH, D_MODEL))],
        out_specs=pl.BlockSpec((TM_T, D_MODEL), lambda s: (row_blk(s), 0)),
        out_shape=jax.ShapeDtypeStruct((N_TOK, D_MODEL), F32),
        scratch_shapes=[pltpu.VMEM((CHAIN_B, TM_T // LANES, RWKV_WIDTH, LANES), F32)],
        compiler_params=_cparams("arbitrary", vmem_mb=56),
        name="rwkv_out",
    )(y_chain, y_s, g, w)


def _mix_body(co_ref, ro_ref, xn_ref, wgr_ref, xp_ref, xs_ref, wm_ref, nf_ref, wr_ref, br_ref,
              h_ref, hn_ref, ridx_ref, rw_ref, *, n_prompt_tiles):
    i = pl.program_id(0)
    g_rwkv = jnp.dot(xn_ref[...], wgr_ref[...], preferred_element_type=F32)
    mixed = co_ref[...] + _sigmoid(g_rwkv) * ro_ref[...]
    mo = jnp.dot(mixed.astype(BF16), wm_ref[...], preferred_element_type=F32)

    def finish(x):
        h = x + mo
        h_ref[...] = h
        ms = jnp.mean(h * h, axis=-1, keepdims=True)
        hn = h * lax.rsqrt(ms + RMS_EPS) * nf_ref[...]
        hnb = hn.astype(BF16)
        hn_ref[...] = hnb.reshape(hn.shape[0], D_MODEL // LANES, LANES)
        logits = jnp.dot(hnb, wr_ref[...], preferred_element_type=F32) + br_ref[...]
        tm = logits.shape[0]
        lane = lax.broadcasted_iota(jnp.int32, (tm, LANES), 1)
        neg = jnp.float32(-jnp.inf)
        gl = jnp.where(lane < N_GROUPS, logits, neg)
        gmax = jnp.max(gl, axis=-1, keepdims=True)
        g_idx = jnp.min(jnp.where(gl == gmax, lane, LANES), axis=-1, keepdims=True)
        g_w = 1.0 / jnp.sum(jnp.exp(gl - gmax), axis=-1, keepdims=True)
        lo = N_GROUPS + g_idx * EXPERTS_PER_GROUP
        el = jnp.where(jnp.logical_and(lane >= lo, lane < lo + EXPERTS_PER_GROUP), logits, neg)
        m1 = jnp.max(el, axis=-1, keepdims=True)
        i1 = jnp.min(jnp.where(el == m1, lane, LANES), axis=-1, keepdims=True)
        el2 = jnp.where(lane == i1, neg, el)
        m2 = jnp.max(el2, axis=-1, keepdims=True)
        i2 = jnp.min(jnp.where(el2 == m2, lane, LANES), axis=-1, keepdims=True)
        t2 = jnp.exp(m2 - m1)
        den = 1.0 + t2
        ridx_ref[...] = jnp.where(lane == 0, i1 - N_GROUPS, jnp.where(lane == 1, i2 - N_GROUPS, 0))
        rw_ref[...] = jnp.where(lane == 0, (1.0 / den) * g_w, jnp.where(lane == 1, (t2 / den) * g_w, 0.0))

    @pl.when(i < n_prompt_tiles)
    def _():
        finish(xp_ref[...])

    @pl.when(i >= n_prompt_tiles)
    def _():
        finish(xs_ref[...])


def _mix_route(conv_out, rwkv_out, xn, wgr, xp, xs, wm, nf, wr, br):
    npt = N_PROMPT // TM_S
    tok_spec = pl.BlockSpec((TM_S, D_MODEL), lambda i: (i, 0))
    small_spec = pl.BlockSpec((TM_S, LANES), lambda i: (i, 0))
    return pl.pallas_call(
        functools.partial(_mix_body, n_prompt_tiles=npt),
        grid=(N_TOK // TM_S,),
        in_specs=[tok_spec, tok_spec, tok_spec, _resident((D_MODEL, D_MODEL)),
                  pl.BlockSpec((TM_S, D_MODEL), lambda i: (jnp.minimum(i, npt - 1), 0)),
                  pl.BlockSpec((TM_S, D_MODEL), lambda i: (jnp.maximum(i - npt, 0), 0)),
                  _resident((D_MODEL, D_MODEL)),
                  pl.BlockSpec((1, D_MODEL), lambda i: (0, 0)),
                  _resident((D_MODEL, LANES)),
                  pl.BlockSpec((1, LANES), lambda i: (0, 0))],
        out_specs=[tok_spec, pl.BlockSpec((TM_S, D_MODEL // LANES, LANES), lambda i: (i, 0, 0)),
                   small_spec, small_spec],
        out_shape=[jax.ShapeDtypeStruct((N_TOK, D_MODEL), F32),
                   jax.ShapeDtypeStruct((N_TOK, D_MODEL // LANES, LANES), BF16),
                   jax.ShapeDtypeStruct((N_TOK, LANES), jnp.int32),
                   jax.ShapeDtypeStruct((N_TOK, LANES), F32)],
        compiler_params=_cparams("arbitrary", vmem_mb=56),
        name="mix_route",
    )(conv_out, rwkv_out, xn, wgr, xp, xs, wm, nf, wr, br)


def _expert_body(blk_e_ref, slot_tok_ref, nused_ref, first_ref, par_ref, next_e_ref,
                 hn_ref, wg_hbm, wu_hbm, wd_hbm, yb_ref,
                 xbuf, sem, wfg, wfu, wfd, wsem, wgb, wub, wdb):
    i = pl.program_id(0)
    nused = nused_ref[0]
    slot = i % N_XBUF

    def row_copy(blk, r, s):
        tok = slot_tok_ref[blk * MOE_BLOCK + r]
        return pltpu.make_async_copy(hn_ref.at[tok], xbuf.at[s, r], sem.at[s])

    def issue(blk, s):
        for r in range(MOE_BLOCK):
            row_copy(blk, r, s).start(priority=r % 2)

    def w_copies(e, s):
        return (pltpu.make_async_copy(wg_hbm.at[e], wfg.at[s], wsem.at[s]),
                pltpu.make_async_copy(wu_hbm.at[e], wfu.at[s], wsem.at[s]),
                pltpu.make_async_copy(wd_hbm.at[e], wfd.at[s], wsem.at[s]))

    @pl.when(jnp.logical_and(i == 0, nused > 0))
    def _():
        for c in w_copies(blk_e_ref[0], 0):
            c.start(priority=1)
        for a in range(GATHER_AHEAD):
            @pl.when(a < nused)
            def _():
                issue(a, a)

    @pl.when(i + GATHER_AHEAD < nused)
    def _():
        issue(i + GATHER_AHEAD, (i + GATHER_AHEAD) % N_XBUF)

    @pl.when(i < nused)
    def _():
        @pl.when(first_ref[i] == 1)
        def _():
            ws = par_ref[i]
            for c in w_copies(blk_e_ref[i], ws):
                c.wait()

            @pl.when(next_e_ref[i] >= 0)
            def _():
                for c in w_copies(next_e_ref[i], 1 - ws):
                    c.start(priority=1)

            wgb[...] = wfg[ws].astype(BF16)
            wub[...] = wfu[ws].astype(BF16)
            wdb[...] = wfd[ws].astype(BF16)

        for r in range(MOE_BLOCK):
            row_copy(i, r, slot).wait()
        xe = xbuf[slot].reshape(MOE_BLOCK, D_MODEL)
        gate = jnp.dot(xe, wgb[...], preferred_element_type=F32)
        up = jnp.dot(xe, wub[...], preferred_element_type=F32)
        hdn = (gate * _sigmoid(gate)) * up
        yb = jnp.dot(hdn.astype(BF16), wdb[...], preferred_element_type=F32)
        yb_ref[...] = yb.reshape(MOE_BLOCK, D_MODEL // LANES, LANES)

    @pl.when(i >= nused)
    def _():
        yb_ref[...] = jnp.zeros_like(yb_ref)


def _experts(plan, hn, wg, wu, wd):
    blk_e, slot_tok, nused, first, par, next_e = plan
    n_blocks = blk_e.shape[0]
    any_spec = pl.BlockSpec(memory_space=pl.ANY)
    return pl.pallas_call(
        _expert_body,
        grid_spec=pltpu.PrefetchScalarGridSpec(
            num_scalar_prefetch=6,
            grid=(n_blocks,),
            in_specs=[any_spec, any_spec, any_spec, any_spec],
            out_specs=pl.BlockSpec((MOE_BLOCK, D_MODEL // LANES, LANES), lambda i, *_: (i, 0, 0)),
            scratch_shapes=[pltpu.VMEM((N_XBUF, MOE_BLOCK, D_MODEL // LANES, LANES), BF16),
                            pltpu.SemaphoreType.DMA((N_XBUF,)),
                            pltpu.VMEM((2, D_MODEL, D_EXPERT), F32),
                            pltpu.VMEM((2, D_MODEL, D_EXPERT), F32),
                            pltpu.VMEM((2, D_EXPERT, D_MODEL), F32),
                            pltpu.SemaphoreType.DMA((2,)),
                            pltpu.VMEM((D_MODEL, D_EXPERT), BF16),
                            pltpu.VMEM((D_MODEL, D_EXPERT), BF16),
                            pltpu.VMEM((D_EXPERT, D_MODEL), BF16)]),
        out_shape=jax.ShapeDtypeStruct((n_blocks * MOE_BLOCK, D_MODEL // LANES, LANES), F32),
        compiler_params=_cparams("arbitrary", vmem_mb=48),
        name="experts",
    )(blk_e, slot_tok, nused, first, par, next_e, hn, wg, wu, wd)


def _combine_body(dest_ref, yb_ref, h_ref, rw_ref, p_ref, wpg_ref, wpp_ref, nf_ref, yp_ref, ys_ref,
                  ybuf, sem):
    i = pl.program_id(0)
    tm = h_ref.shape[0]
    slot = i % 2

    def row_copy(tile, r, s, sl):
        d = dest_ref[(tile * tm + r) * TOP_K + s]
        return pltpu.make_async_copy(yb_ref.at[d], ybuf.at[sl, s, r], sem.at[sl])

    def issue(tile, sl):
        for r in range(tm):
            row_copy(tile, r, 0, sl).start(priority=0)
            row_copy(tile, r, 1, sl).start(priority=1)

    @pl.when(i == 0)
    def _():
        issue(0, 0)

    @pl.when(i + 1 < pl.num_programs(0))
    def _():
        issue(i + 1, 1 - slot)

    for r in range(tm):
        row_copy(i, r, 0, slot).wait()
        row_copy(i, r, 1, slot).wait()

    rw = rw_ref[...]
    y0 = ybuf[slot, 0].reshape(tm, D_MODEL)
    y1 = ybuf[slot, 1].reshape(tm, D_MODEL)
    h2 = h_ref[...] + (y0 * rw[:, 0:1] + y1 * rw[:, 1:2])
    gate = _sigmoid(jnp.dot(h2.astype(BF16), wpg_ref[...], preferred_element_type=F32))
    pp = jnp.dot(p_ref[...].astype(BF16), wpp_ref[...], preferred_element_type=F32)
    h3 = h2 + gate * pp
    ms = jnp.mean(h3 * h3, axis=-1, keepdims=True)
    y = h3 * lax.rsqrt(ms + RMS_EPS) * nf_ref[...]

    @pl.when(i < N_PROMPT // TM_S)
    def _():
        yp_ref[...] = y

    @pl.when(i >= N_PROMPT // TM_S)
    def _():
        ys_ref[...] = y


def _combine(dest, yb, h, rw, p_all, wpg, wpp, nf):
    npt = N_PROMPT // TM_S
    return pl.pallas_call(
        _combine_body,
        grid_spec=pltpu.PrefetchScalarGridSpec(
            num_scalar_prefetch=1,
            grid=(N_TOK // TM_S,),
            in_specs=[pl.BlockSpec(memory_space=pl.ANY),
                      pl.BlockSpec((TM_S, D_MODEL), lambda i, d: (i, 0)),
                      pl.BlockSpec((TM_S, LANES), lambda i, d: (i, 0)),
                      pl.BlockSpec((TM_S, PLE_DIM), lambda i, d: (i, 0)),
                      pl.BlockSpec((D_MODEL, D_MODEL), lambda i, d: (0, 0), pipeline_mode=pl.Buffered(1)),
                      pl.BlockSpec((PLE_DIM, D_MODEL), lambda i, d: (0, 0), pipeline_mode=pl.Buffered(1)),
                      pl.BlockSpec((1, D_MODEL), lambda i, d: (0, 0))],
            out_specs=[pl.BlockSpec((TM_S, D_MODEL), lambda i, d: (jnp.minimum(i, npt - 1), 0)),
                       pl.BlockSpec((TM_S, D_MODEL), lambda i, d: (jnp.maximum(i - npt, 0), 0))],
            scratch_shapes=[pltpu.VMEM((2, TOP_K, TM_S, D_MODEL // LANES, LANES), F32),
                            pltpu.SemaphoreType.DMA((2,))]),
        out_shape=[jax.ShapeDtypeStruct((N_PROMPT, D_MODEL), F32),
                   jax.ShapeDtypeStruct((N_SAMPLE, D_MODEL), F32)],
        compiler_params=_cparams("arbitrary", vmem_mb=48),
        name="combine_ple",
    )(dest, yb, h, rw, p_all, wpg, wpp, nf)


def _dispatch_plan(eidx):
    n_assign = N_TOK * TOP_K
    e_flat = eidx.reshape(n_assign)
    onehot = (e_flat[:, None] == jnp.arange(N_EXPERTS, dtype=jnp.int32)[None, :]).astype(jnp.int32)
    csum = jnp.cumsum(onehot, axis=0)
    counts = csum[-1]
    rank = jnp.sum(csum * onehot, axis=1) - 1
    padded = (counts + MOE_BLOCK - 1) // MOE_BLOCK * MOE_BLOCK
    pad_end = jnp.cumsum(padded)
    pad_start = pad_end - padded
    dest = pad_start[e_flat] + rank
    n_blocks = -(-n_assign // MOE_BLOCK) + N_EXPERTS
    tok = jnp.arange(n_assign, dtype=jnp.int32) // TOP_K
    pad_tok = jnp.arange(n_blocks * MOE_BLOCK, dtype=jnp.int32) % N_TOK
    slot_tok = pad_tok.at[dest].set(tok)
    blk = jnp.arange(n_blocks, dtype=jnp.int32)
    blk_e = jnp.minimum(jnp.searchsorted(pad_end, blk * MOE_BLOCK, side="right"), N_EXPERTS - 1).astype(jnp.int32)
    nused = (pad_end[-1] // MOE_BLOCK).astype(jnp.int32)
    prev_e = jnp.concatenate([jnp.full((1,), -1, jnp.int32), blk_e[:-1]])
    first = jnp.logical_and(blk < nused, blk_e != prev_e)
    par = ((jnp.cumsum(first.astype(jnp.int32)) - 1) % 2).astype(jnp.int32)
    idx_first = jnp.where(first, blk, n_blocks)
    later = jnp.concatenate([lax.cummin(idx_first[::-1])[::-1][1:], jnp.full((1,), n_blocks, jnp.int32)])
    next_e = jnp.where(later < n_blocks, blk_e[jnp.minimum(later, n_blocks - 1)], -1).astype(jnp.int32)
    plan = (blk_e, slot_tok, nused.reshape(1), first.astype(jnp.int32), par, next_e)
    return dest.astype(jnp.int32), plan


def kernel(x_prompt, x_sample, state_conv, state_shift, state_wkv, p_prompt, p_sample, norm_mix, w_in, conv_w, w_conv_out, shift_mu, w0, w2, a0, a2, g2, k_k, k_a, r_k, lnx_w, lnx_b, w_rwkv_out, w_mix_out, norm_ffn, w_route_group, b_route_group, w_route_expert, b_route_expert, w_exp_gate, w_exp_up, w_exp_down, w_ple_proj, w_ple_gate, norm_final):
    c3 = 3 * CONV_WIDTH
    rw3 = 3 * RWKV_WIDTH
    xp = x_prompt.reshape(N_PROMPT, D_MODEL)
    xs = x_sample.transpose(1, 0, 2).reshape(N_SAMPLE, D_MODEL)
    win = w_in[0]
    mu = shift_mu[0]
    st = state_shift[0]

    xn = _norm_cast(xp, xs, norm_mix)
    rkv_s, _, sraw_rkv = _inproj(xn, win, mu[None, :rw3], st[:, :rw3],
                                 tn=1024, col_blk_off=c3 // 1024, n_col_blocks=rw3 // 1024,
                                 shift_lo=0, shift_hi=rw3 // 1024,
                                 tile_lo=N_PROMPT // TM, n_tiles=N_SAMPLE // TM, name="inproj_rkv_sample")
    n_lr = RWKV_PROJ - rw3
    mu_l = jnp.concatenate([mu[rw3:], jnp.zeros((LORA_W - n_lr,), F32)])[None, :]
    st_l = jnp.concatenate([st[:, rw3:], jnp.zeros((DEC_BATCH, LORA_W - n_lr), F32)], axis=1)
    w_l = win[:, c3 + rw3:c3 + rw3 + LORA_W].astype(BF16)
    w_gc = win[:, c3 + RWKV_PROJ:c3 + RWKV_PROJ + D_MODEL].astype(BF16)
    w_gr = win[:, c3 + RWKV_PROJ + D_MODEL:].astype(BF16)
    w_rk = _head_minor(win[:, c3:c3 + 2 * RWKV_WIDTH].reshape(D_MODEL, 2, RWKV_WIDTH), 2)
    rk_chain = _inproj_t(xn, w_rk.reshape(D_MODEL, 2 * RWKV_WIDTH).T.astype(BF16),
                         time_major=False, name="inproj_t_rk")
    v_chain = _inproj_t(xn, _head_minor(win[:, c3 + 2 * RWKV_WIDTH:c3 + rw3], 1).T.astype(BF16),
                        time_major=True, name="inproj_t_v")

    sc = state_conv[0].transpose(1, 0, 2).reshape(2 * DEC_BATCH, CONV_WIDTH)
    conv_out, ulast, us = _conv_branch(xn, win[:, :c3].astype(BF16), conv_w[0], sc,
                                       w_conv_out[0].astype(BF16), w_gc)

    def pad_rows(w, before, total):
        return jnp.pad(w, ((before, total - before - w.shape[0]), (0, 0))).astype(BF16)

    w2p = pad_rows(w2[0], 0, LANES)
    a2p = pad_rows(a2[0], DECAY_LORA, 2 * LANES)
    g2p = pad_rows(_head_minor(g2[0], 1), DECAY_LORA + A_LORA - LANES, LORA_W - LANES)
    g, wl_c, al_c, wl_s, al_s, last_l, sraw_l = _lora(xn, w_l, mu_l, st_l, _head_minor(w2p, 1).T,
                                                      _head_minor(a2p, 1).T, w2p, a2p, g2p)
    par8 = [_param_chain8(p) for p in (k_k[0], k_a[0], r_k[0].reshape(RWKV_WIDTH))]
    mu8 = [_param_chain8(mu[n * RWKV_WIDTH:(n + 1) * RWKV_WIDTH]) for n in range(2)]
    bias8 = [_param_chain8(w0[0]), _param_chain8(a0[0])]
    zero8 = jnp.zeros((HEAD_SIZE * SCAN_TB, LANES), F32)
    mu_v = _param_chain(mu[2 * RWKV_WIDTH:rw3])
    gn = [_param_chain(lnx_w[0]), _param_chain(lnx_b[0])]
    seqs_p = [(rk_chain, 0), (rk_chain, 1), (wl_c[None], None), (al_c[None], None)]
    v4 = v_chain.reshape(1, SEQ, HEAD_SIZE, LANES)
    s0_p = jnp.zeros((1, HEAD_SIZE, HEAD_SIZE, LANES), F32)
    y_p, sf_p = _scan(seqs_p, (v4, None), par8 + mu8 + bias8, [mu_v] + gn, s0_p,
                      n_batches=SCAN_TT // SCAN_TB, steps=SCAN_TB, name="wkv_scan_prompt")
    ng = DEC_BATCH // CHAIN_B
    s0_s = state_wkv[0].reshape(ng, CHAIN_B, N_HEADS, HEAD_SIZE, HEAD_SIZE).transpose(0, 4, 3, 1, 2)
    s0_s = s0_s.reshape(ng, HEAD_SIZE, HEAD_SIZE, LANES)
    seqs_s = [(_to_chain_sample_ct(rkv_s[:, n * RWKV_WIDTH:(n + 1) * RWKV_WIDTH]), None) for n in range(2)]
    seqs_s += [(_to_chain_sample_ct(wl_s), None), (_to_chain_sample_ct(al_s), None)]
    v_s = _to_chain_sample_tc(rkv_s[:, 2 * RWKV_WIDTH:rw3])
    y_s, sf_s = _scan(seqs_s, (v_s, None), par8 + [zero8, zero8] + bias8,
                      [jnp.zeros((HEAD_SIZE, LANES), F32)] + gn, s0_s,
                      n_batches=1, steps=DEC_SEQ, name="wkv_scan_sample")
    rwkv_out = _rwkv_out(y_p.reshape(SEQ * HEAD_SIZE, LANES), _from_chain_sample(y_s), g,
                         _head_minor(w_rwkv_out[0], 0).astype(BF16))

    wr = jnp.concatenate([w_route_group[0], w_route_expert[0],
                          jnp.zeros((D_MODEL, LANES - N_GROUPS - N_EXPERTS), F32)], axis=1)
    br = jnp.concatenate([b_route_group[0], b_route_expert[0],
                          jnp.zeros((LANES - N_GROUPS - N_EXPERTS,), F32)])[None, :]
    h, hn, ridx, rw = _mix_route(conv_out, rwkv_out, xn, w_gr, xp, xs, w_mix_out[0].astype(BF16),
                                 norm_ffn, wr.astype(BF16), br)

    dest, plan = _dispatch_plan(ridx[:, :TOP_K])
    yb = _experts(plan, hn, w_exp_gate[0], w_exp_up[0], w_exp_down[0])
    p_all = jnp.concatenate([p_prompt[0].reshape(N_PROMPT, PLE_DIM),
                             p_sample[0].transpose(1, 0, 2).reshape(N_SAMPLE, PLE_DIM)], axis=0)
    y_p2, y_s2 = _combine(dest, yb, h, rw, p_all, w_ple_gate[0].astype(BF16), w_ple_proj[0].astype(BF16),
                          norm_final[None, :])

    y_prompt = y_p2.reshape(BATCH, SEQ, D_MODEL)
    y_sample = y_s2.reshape(DEC_SEQ, DEC_BATCH, D_MODEL).transpose(1, 0, 2)
    tiles_per_seq = SEQ // TM
    seq_last = jnp.arange(BATCH) * tiles_per_seq + tiles_per_seq - 1

    conv_p = ulast.reshape(-1, SUBLANES, CONV_WIDTH)[seq_last, SUBLANES - 2:, :][None]
    conv_s = us.reshape(2, DEC_BATCH, CONV_WIDTH).transpose(1, 0, 2)[None]
    last_rkv = jnp.concatenate([rk_chain[:, :, SEQ - 1, :], v4[:, SEQ - 1]], axis=0)
    lm = last_rkv.reshape(3, HEAD_SIZE, BATCH, N_HEADS).transpose(2, 0, 3, 1).reshape(BATCH, rw3)
    lora_last = (SEQ // TM_L - 1) * CHAIN_B + jnp.arange(BATCH)
    lt = last_l.reshape(-1, SUBLANES, LORA_W)[lora_last, SUBLANES - 1, :n_lr]
    shift_p = jnp.concatenate([lm, lt], axis=1)[None]
    shift_s = jnp.concatenate([sraw_rkv, sraw_l[:, :n_lr]], axis=1)[None]
    wkv_p = sf_p.reshape(HEAD_SIZE, HEAD_SIZE, BATCH, N_HEADS).transpose(2, 3, 1, 0)[None]
    wkv_s = sf_s.reshape(ng, HEAD_SIZE, HEAD_SIZE, CHAIN_B, N_HEADS).transpose(0, 3, 4, 2, 1)
    wkv_s = wkv_s.reshape(DEC_BATCH, N_HEADS, HEAD_SIZE, HEAD_SIZE)[None]
    return (y_prompt, y_sample, conv_p, shift_p, wkv_p, conv_s, shift_s, wkv_s)
```

```python
import functools

import jax
import jax.numpy as jnp
from jax import lax
from jax.experimental import pallas as pl
from jax.experimental.pallas import tpu as pltpu

D_MODEL = 2048
BATCH = 4
SEQ = 2048
DEC_BATCH = 128
DEC_SEQ = 4
CONV_WIDTH = 1024
RWKV_WIDTH = 2048
HEAD_SIZE = 64
N_HEADS = RWKV_WIDTH // HEAD_SIZE
DECAY_LORA = 96
A_LORA = 96
GATE_LORA = 256
RWKV_PROJ = 3 * RWKV_WIDTH + DECAY_LORA + A_LORA + GATE_LORA
N_GROUPS = 8
EXPERTS_PER_GROUP = 8
N_EXPERTS = N_GROUPS * EXPERTS_PER_GROUP
TOP_K = 2
D_EXPERT = 512
MOE_BLOCK = 128
PLE_DIM = 256
RMS_EPS = 1e-6
GN_EPS = 64e-5

N_PROMPT = BATCH * SEQ
N_SAMPLE = DEC_BATCH * DEC_SEQ
N_TOK = N_PROMPT + N_SAMPLE
LANES = 128
SUBLANES = 8
TM = 512
TM_S = 256
TM_T = 256
TM_L = 128
CHAIN_B = LANES // N_HEADS
SCAN_TT = 64
SCAN_TB = SUBLANES
SCAN_ACC = 2
GATHER_AHEAD = 3
N_XBUF = GATHER_AHEAD + 1
LORA_W = 512
EXP_M05 = 0.6065306597126334
F32 = jnp.float32
BF16 = jnp.bfloat16
_NT = (((1,), (1,)), ((), ()))


def _sigmoid(x):
    return 0.5 * jnp.tanh(0.5 * x) + 0.5


def _cparams(*sem, vmem_mb=None):
    kw = dict(dimension_semantics=sem)
    if vmem_mb is not None:
        kw["vmem_limit_bytes"] = vmem_mb * 1024 * 1024
    return pltpu.CompilerParams(**kw)


def _resident(shape):
    nd = len(shape)
    return pl.BlockSpec(shape, lambda *_: (0,) * nd, pipeline_mode=pl.Buffered(1))


def _chain_tile(zt, half, c):
    start = c * N_HEADS
    rows = pl.ds(start if isinstance(c, int) else pl.multiple_of(start, N_HEADS), N_HEADS)
    return jnp.concatenate([zt[bb, half, rows, :] for bb in range(CHAIN_B)], axis=0)


def _store_chain_ct(zt, out_ref, halves, c_lo=0, n_c=HEAD_SIZE):
    for half in range(halves):
        for ci in range(n_c):
            out_ref[c_lo + ci, half * LANES:(half + 1) * LANES, :] = _chain_tile(zt, half, c_lo + ci).T


def _store_chain_tc(zt, out_ref, halves, c_lo=0, n_c=HEAD_SIZE):
    for half in range(halves):
        for ci in range(n_c):
            rows = pl.ds(half * LANES * HEAD_SIZE + c_lo + ci, LANES, stride=HEAD_SIZE)
            out_ref[rows, :] = _chain_tile(zt, half, c_lo + ci).T


def _load_chain_tc(y_ref, yt, halves):
    for half in range(halves):
        for v in range(HEAD_SIZE):
            mt = y_ref[pl.ds(half * LANES * HEAD_SIZE + v, LANES, stride=HEAD_SIZE), :].T
            for bb in range(CHAIN_B):
                yt[bb, half, v * N_HEADS:(v + 1) * N_HEADS, :] = mt[bb * N_HEADS:(bb + 1) * N_HEADS, :]


def _norm_body(xp_ref, xs_ref, g_ref, o_ref, *, n_prompt_tiles):
    i = pl.program_id(0)

    def f(x):
        ms = jnp.mean(x * x, axis=-1, keepdims=True)
        return (x * lax.rsqrt(ms + RMS_EPS) * g_ref[...]).astype(o_ref.dtype)

    @pl.when(i < n_prompt_tiles)
    def _():
        o_ref[...] = f(xp_ref[...])

    @pl.when(i >= n_prompt_tiles)
    def _():
        o_ref[...] = f(xs_ref[...])


def _norm_cast(xp, xs, g):
    npt = N_PROMPT // TM
    return pl.pallas_call(
        functools.partial(_norm_body, n_prompt_tiles=npt),
        grid=(N_TOK // TM,),
        in_specs=[pl.BlockSpec((TM, D_MODEL), lambda i: (jnp.minimum(i, npt - 1), 0)),
                  pl.BlockSpec((TM, D_MODEL), lambda i: (jnp.maximum(i - npt, 0), 0)),
                  pl.BlockSpec((1, D_MODEL), lambda i: (0, 0))],
        out_specs=pl.BlockSpec((TM, D_MODEL), lambda i: (i, 0)),
        out_shape=jax.ShapeDtypeStruct((N_TOK, D_MODEL), BF16),
        compiler_params=_cparams("arbitrary"),
        name="norm_cast",
    )(xp, xs, g)


def _inproj_body(xn_ref, w_ref, mu_ref, st_ref, z_ref, last_ref, sraw_ref, wb_ref, carry_ref, *,
                 shift_lo, shift_hi, tile_lo, n_prompt_tiles, tiles_per_seq):
    j = pl.program_id(0)
    i = pl.program_id(1) + tile_lo

    @pl.when(pl.program_id(1) == 0)
    def _():
        wb_ref[...] = w_ref[...].astype(BF16)

    z = jnp.dot(xn_ref[...], wb_ref[...], preferred_element_type=F32)
    tm = z.shape[0]
    last_ref[...] = z[tm - SUBLANES:tm]
    shifted = jnp.logical_and(j >= shift_lo, j < shift_hi)
    is_prompt = i < n_prompt_tiles

    @pl.when(jnp.logical_not(shifted))
    def _():
        z_ref[...] = z

    @pl.when(jnp.logical_and(shifted, is_prompt))
    def _():
        @pl.when(i % tiles_per_seq == 0)
        def _():
            carry_ref[...] = jnp.zeros_like(carry_ref)

        prev = pltpu.roll(z, 1, 0)
        row = lax.broadcasted_iota(jnp.int32, (tm, 1), 0)
        prev = jnp.where(row == 0, carry_ref[SUBLANES - 1:SUBLANES, :], prev)
        z_ref[...] = z + mu_ref[...] * (prev - z)
        carry_ref[...] = z[tm - SUBLANES:tm]

    @pl.when(jnp.logical_and(shifted, jnp.logical_not(is_prompt)))
    def _():
        prev = jnp.concatenate([st_ref[...], z[:tm - DEC_BATCH]], axis=0)
        z_ref[...] = z + mu_ref[...] * (prev - z)

    @pl.when(jnp.logical_not(is_prompt))
    def _():
        sraw_ref[...] = z[tm - DEC_BATCH:tm]


def _inproj(xn, w, mu, st, *, tn, col_blk_off, n_col_blocks, shift_lo, shift_hi, tile_lo, n_tiles, name):
    n_out = tn * n_col_blocks
    body = functools.partial(_inproj_body, shift_lo=shift_lo, shift_hi=shift_hi, tile_lo=tile_lo,
                             n_prompt_tiles=N_PROMPT // TM, tiles_per_seq=SEQ // TM)
    return pl.pallas_call(
        body,
        grid=(n_col_blocks, n_tiles),
        in_specs=[pl.BlockSpec((TM, D_MODEL), lambda j, i: (i + tile_lo, 0)),
                  pl.BlockSpec((D_MODEL, tn), lambda j, i: (0, j + col_blk_off)),
                  pl.BlockSpec((1, tn), lambda j, i: (0, j)),
                  pl.BlockSpec((DEC_BATCH, tn), lambda j, i: (0, j))],
        out_specs=[pl.BlockSpec((TM, tn), lambda j, i: (i, j)),
                   pl.BlockSpec((SUBLANES, tn), lambda j, i: (i, j)),
                   pl.BlockSpec((DEC_BATCH, tn), lambda j, i: (0, j))],
        out_shape=[jax.ShapeDtypeStruct((n_tiles * TM, n_out), F32),
                   jax.ShapeDtypeStruct((n_tiles * SUBLANES, n_out), F32),
                   jax.ShapeDtypeStruct((DEC_BATCH, n_out), F32)],
        scratch_shapes=[pltpu.VMEM((D_MODEL, tn), BF16), pltpu.VMEM((SUBLANES, tn), F32)],
        compiler_params=_cparams("arbitrary", "arbitrary", vmem_mb=48),
        name=name,
    )(xn, w, mu, st)


def _inproj_t_body(xn_ref, wt_ref, o_ref, zt, *, time_major, n_tb):
    tb = pl.program_id(1)
    b = pl.program_id(2)
    halves = TM_T // LANES
    n_c = HEAD_SIZE // CHAIN_B

    def matmul():
        z = lax.dot_general(wt_ref[...], xn_ref[...], _NT, preferred_element_type=F32)
        for half in range(halves):
            zt[tb % 2, b, half] = z[:, half * LANES:(half + 1) * LANES]

    def retile():
        store = _store_chain_tc if time_major else _store_chain_ct
        store(zt.at[1 - tb % 2], o_ref, halves, c_lo=b * n_c, n_c=n_c)

    @pl.when(tb == 0)
    def _():
        matmul()

    @pl.when(jnp.logical_and(tb > 0, tb < n_tb))
    def _():
        retile()
        matmul()

    @pl.when(tb == n_tb)
    def _():
        retile()


def _inproj_t(xn, wt, *, time_major, name):
    n_tb = SEQ // TM_T
    n = wt.shape[0] // RWKV_WIDTH

    def prev_tb(tb):
        return jnp.maximum(tb - 1, 0)

    if time_major:
        out_spec = pl.BlockSpec((None, TM_T * HEAD_SIZE, LANES), lambda j, tb, b: (j, prev_tb(tb), 0))
        out_shape = jax.ShapeDtypeStruct((n, SEQ * HEAD_SIZE, LANES), F32)
    else:
        out_spec = pl.BlockSpec((None, HEAD_SIZE, TM_T, LANES), lambda j, tb, b: (j, 0, prev_tb(tb), 0))
        out_shape = jax.ShapeDtypeStruct((n, HEAD_SIZE, SEQ, LANES), F32)
    return pl.pallas_call(
        functools.partial(_inproj_t_body, time_major=time_major, n_tb=n_tb),
        grid=(n, n_tb + 1, CHAIN_B),
        in_specs=[pl.BlockSpec((TM_T, D_MODEL), lambda j, tb, b: (b * n_tb + jnp.minimum(tb, n_tb - 1), 0)),
                  pl.BlockSpec((RWKV_WIDTH, D_MODEL), lambda j, tb, b: (j, 0), pipeline_mode=pl.Buffered(1))],
        out_specs=out_spec,
        out_shape=out_shape,
        scratch_shapes=[pltpu.VMEM((2, CHAIN_B, TM_T // LANES, RWKV_WIDTH, LANES), F32)],
        compiler_params=_cparams("arbitrary", "arbitrary", "arbitrary", vmem_mb=56),
        name=name,
    )(xn, wt)


def _conv_body(xn_ref, wc_ref, cw_ref, sc_ref, wco_ref, wgc_ref, o_ref, ulast_ref, us_ref, carry_ref, *,
               n_prompt_tiles, tiles_per_seq):
    i = pl.program_id(0)
    gate_b = jnp.dot(xn_ref[...], wc_ref[:, 0:CONV_WIDTH], preferred_element_type=F32)
    u = (jnp.dot(xn_ref[...], wc_ref[:, CONV_WIDTH:2 * CONV_WIDTH], preferred_element_type=F32)
         * jnp.dot(xn_ref[...], wc_ref[:, 2 * CONV_WIDTH:3 * CONV_WIDTH], preferred_element_type=F32))
    tm = u.shape[0]
    ulast_ref[...] = u[tm - SUBLANES:tm]
    w0 = cw_ref[0:1, :]
    w1 = cw_ref[1:2, :]
    w2 = cw_ref[2:3, :]

    def finish(p1, p2):
        conv = w0 * p2 + w1 * p1 + w2 * u
        y = (gate_b * conv).astype(BF16)
        g_conv = jnp.dot(xn_ref[...], wgc_ref[...], preferred_element_type=F32)
        o_ref[...] = _sigmoid(g_conv) * jnp.dot(y, wco_ref[...], preferred_element_type=F32)

    @pl.when(i < n_prompt_tiles)
    def _():
        @pl.when(i % tiles_per_seq == 0)
        def _():
            carry_ref[...] = jnp.zeros_like(carry_ref)

        row = lax.broadcasted_iota(jnp.int32, (tm, 1), 0)
        c1 = carry_ref[SUBLANES - 1:SUBLANES, :]
        c2 = carry_ref[SUBLANES - 2:SUBLANES - 1, :]
        p1 = jnp.where(row == 0, c1, pltpu.roll(u, 1, 0))
        p2 = jnp.where(row == 0, c2, jnp.where(row == 1, c1, pltpu.roll(u, 2, 0)))
        carry_ref[...] = u[tm - SUBLANES:tm]
        finish(p1, p2)

    @pl.when(i >= n_prompt_tiles)
    def _():
        p1 = jnp.concatenate([sc_ref[DEC_BATCH:2 * DEC_BATCH, :], u[:tm - DEC_BATCH]], axis=0)
        p2 = jnp.concatenate([sc_ref[...], u[:tm - 2 * DEC_BATCH]], axis=0)
        us_ref[...] = u[tm - 2 * DEC_BATCH:tm]
        finish(p1, p2)


def _conv_branch(xn, wc, conv_w, sc, wco, wgc):
    n_tiles = N_TOK // TM
    body = functools.partial(_conv_body, n_prompt_tiles=N_PROMPT // TM, tiles_per_seq=SEQ // TM)
    return pl.pallas_call(
        body,
        grid=(n_tiles,),
        in_specs=[pl.BlockSpec((TM, D_MODEL), lambda i: (i, 0)),
                  _resident((D_MODEL, 3 * CONV_WIDTH)),
                  pl.BlockSpec((3, CONV_WIDTH), lambda i: (0, 0)),
                  pl.BlockSpec((2 * DEC_BATCH, CONV_WIDTH), lambda i: (0, 0)),
                  _resident((CONV_WIDTH, D_MODEL)), _resident((D_MODEL, D_MODEL))],
        out_specs=[pl.BlockSpec((TM, D_MODEL), lambda i: (i, 0)),
                   pl.BlockSpec((SUBLANES, CONV_WIDTH), lambda i: (i, 0)),
                   pl.BlockSpec((2 * DEC_BATCH, CONV_WIDTH), lambda i: (0, 0))],
        out_shape=[jax.ShapeDtypeStruct((N_TOK, D_MODEL), F32),
                   jax.ShapeDtypeStruct((n_tiles * SUBLANES, CONV_WIDTH), F32),
                   jax.ShapeDtypeStruct((2 * DEC_BATCH, CONV_WIDTH), F32)],
        scratch_shapes=[pltpu.VMEM((SUBLANES, CONV_WIDTH), F32)],
        compiler_params=_cparams("arbitrary", vmem_mb=48),
        name="conv_branch",
    )(xn, wc, conv_w, sc, wco, wgc)


def _lora_body(xn_ref, wl_ref, mu_ref, st_ref, w2t_ref, a2t_ref, w2_ref, a2_ref, g2_ref,
               g_ref, wlc_ref, alc_ref, wls_ref, als_ref, last_ref, sraw_ref, zt, carry_p, carry_s, *,
               n_prompt_steps):
    s = pl.program_id(0)
    z = jnp.dot(xn_ref[...], wl_ref[...], preferred_element_type=F32)
    tm = z.shape[0]
    last_ref[...] = z[tm - SUBLANES:tm]

    def project(zl):
        tw = jnp.tanh(zl[:, 0:LANES]).astype(BF16)
        xa = zl[:, 0:2 * LANES].astype(BF16)
        g_ref[...] = jnp.dot(_sigmoid(zl[:, LANES:LORA_W]).astype(BF16), g2_ref[...], preferred_element_type=F32)
        return tw, xa

    @pl.when(s < n_prompt_steps)
    def _():
        b = s % CHAIN_B

        @pl.when(s < CHAIN_B)
        def _():
            carry_p[b] = jnp.zeros((SUBLANES, LORA_W), F32)

        row = lax.broadcasted_iota(jnp.int32, (tm, 1), 0)
        prev = jnp.where(row == 0, carry_p[b, SUBLANES - 1:SUBLANES, :], pltpu.roll(z, 1, 0))
        carry_p[b] = z[tm - SUBLANES:tm]
        tw, xa = project(z + mu_ref[...] * (prev - z))
        zt[0, b, 0] = lax.dot_general(w2t_ref[...], tw, _NT, preferred_element_type=F32)
        zt[1, b, 0] = lax.dot_general(a2t_ref[...], xa, _NT, preferred_element_type=F32)

        @pl.when(b == CHAIN_B - 1)
        def _():
            _store_chain_ct(zt.at[0], wlc_ref, 1)
            _store_chain_ct(zt.at[1], alc_ref, 1)

    @pl.when(s >= n_prompt_steps)
    def _():
        @pl.when(s == n_prompt_steps)
        def _():
            carry_s[...] = st_ref[...]

        prev = carry_s[...]
        carry_s[...] = z
        sraw_ref[...] = z
        tw, xa = project(z + mu_ref[...] * (prev - z))
        wls_ref[...] = jnp.dot(tw, w2_ref[...], preferred_element_type=F32)
        als_ref[...] = jnp.dot(xa, a2_ref[...], preferred_element_type=F32)


def _lora(xn, w_l, mu_l, st_l, w2t, a2t, w2p, a2p, g2b):
    n_tb = SEQ // TM_L
    nps = n_tb * CHAIN_B
    n_steps = N_TOK // TM_L

    def row_blk(s):
        return jnp.where(s < nps, (s % CHAIN_B) * n_tb + s // CHAIN_B, s)

    chain_spec = pl.BlockSpec((HEAD_SIZE, TM_L, LANES), lambda s: (0, jnp.minimum(s // CHAIN_B, n_tb - 1), 0))
    samp_spec = pl.BlockSpec((TM_L, RWKV_WIDTH), lambda s: (jnp.maximum(s - nps, 0), 0))
    chain_shape = jax.ShapeDtypeStruct((HEAD_SIZE, SEQ, LANES), F32)
    samp_shape = jax.ShapeDtypeStruct((N_SAMPLE, RWKV_WIDTH), F32)
    return pl.pallas_call(
        functools.partial(_lora_body, n_prompt_steps=nps),
        grid=(n_steps,),
        in_specs=[pl.BlockSpec((TM_L, D_MODEL), lambda s: (row_blk(s), 0)),
                  _resident(w_l.shape),
                  pl.BlockSpec((1, LORA_W), lambda s: (0, 0)),
                  pl.BlockSpec((DEC_BATCH, LORA_W), lambda s: (0, 0)),
                  _resident(w2t.shape), _resident(a2t.shape), _resident(w2p.shape), _resident(a2p.shape),
                  _resident(g2b.shape)],
        out_specs=[pl.BlockSpec((TM_L, RWKV_WIDTH), lambda s: (row_blk(s), 0)),
                   chain_spec, chain_spec, samp_spec, samp_spec,
                   pl.BlockSpec((SUBLANES, LORA_W), lambda s: (s, 0)),
                   pl.BlockSpec((DEC_BATCH, LORA_W), lambda s: (0, 0))],
        out_shape=[jax.ShapeDtypeStruct((N_TOK, RWKV_WIDTH), F32),
                   chain_shape, chain_shape, samp_shape, samp_shape,
                   jax.ShapeDtypeStruct((n_steps * SUBLANES, LORA_W), F32),
                   jax.ShapeDtypeStruct((DEC_BATCH, LORA_W), F32)],
        scratch_shapes=[pltpu.VMEM((2, CHAIN_B, 1, RWKV_WIDTH, LANES), F32),
                        pltpu.VMEM((CHAIN_B, SUBLANES, LORA_W), F32),
                        pltpu.VMEM((DEC_BATCH, LORA_W), F32)],
        compiler_params=_cparams("arbitrary", vmem_mb=48),
        name="lora",
    )(xn, w_l, mu_l, st_l, w2t, a2t, w2p, a2p, g2b)


def _scan_body(r_ref, k_ref, wl_ref, al_ref, v_ref, kk_ref, ka_ref, rk_ref, mur_ref, muk_ref, w0_ref, a0_ref,
               muv_ref, lw_ref, lb_ref, s0_ref,
               y_ref, s_ref, vec_ref, bon_ref, prevb_ref, prevv_ref, *, n_batches, steps):
    @pl.when(pl.program_id(1) == 0)
    def _():
        s_ref[...] = s0_ref[...]
        prevb_ref[...] = jnp.zeros_like(prevb_ref)
        prevv_ref[...] = jnp.zeros_like(prevv_ref)

    rows = HEAD_SIZE * SCAN_TB
    first_t = lax.broadcasted_iota(jnp.int32, (rows, 1), 0) % SCAN_TB == 0

    def batch(bi, carry):
        t0 = pl.multiple_of(bi * SCAN_TB, SCAN_TB)

        def load(ref):
            return ref[:, pl.ds(t0, SCAN_TB), :].reshape(rows, LANES)

        def cube(x):
            return x.reshape(HEAD_SIZE, SCAN_TB, LANES)

        def shifted(x, slot, mu_ref):
            prev = jnp.where(first_t, pltpu.roll(prevb_ref[slot], rows - (SCAN_TB - 1), 0), pltpu.roll(x, 1, 0))
            prevb_ref[slot] = x
            return x + mu_ref[...] * (prev - x)

        r = shifted(load(r_ref), 0, mur_ref)
        k = shifted(load(k_ref), 1, muk_ref)
        decay = jnp.exp(-EXP_M05 * _sigmoid(load(wl_ref) + w0_ref[...]))
        a = _sigmoid(load(al_ref) + a0_ref[...])
        kk = cube(k * kk_ref[...])
        nrm = jnp.sqrt(jnp.sum(kk * kk, axis=0))
        kk = kk * (1.0 / jnp.maximum(nrm, 1e-12))[None]
        kf = k * (1.0 + (a - 1.0) * ka_ref[...])
        vec_ref[0] = -kk
        vec_ref[1] = cube(decay)
        vec_ref[2] = kk * cube(a)
        vec_ref[3] = cube(kf)
        vec_ref[4] = cube(r)
        bon_ref[...] = jnp.sum(cube(r * kf * rk_ref[...]), axis=0)

        def step(tl, c):
            t = t0 + tl
            v_raw = v_ref[t]
            v = v_raw + muv_ref[...] * (prevv_ref[...] - v_raw)
            prevv_ref[...] = v_raw

            def row(j, kx):
                return vec_ref[j, kx, pl.ds(tl, HEAD_SIZE, stride=0), :]

            parts = [s_ref[kx] * row(0, kx) for kx in range(SCAN_ACC)]
            for kx in range(SCAN_ACC, HEAD_SIZE):
                parts[kx % SCAN_ACC] = parts[kx % SCAN_ACC] + s_ref[kx] * row(0, kx)
            sa = functools.reduce(lambda x, y: x + y, parts)

            parts = []
            for kx in range(HEAD_SIZE):
                sn = s_ref[kx] * row(1, kx) + sa * row(2, kx) + v * row(3, kx)
                s_ref[kx] = sn
                if kx < SCAN_ACC:
                    parts.append(sn * row(4, kx))
                else:
                    parts[kx % SCAN_ACC] = parts[kx % SCAN_ACC] + sn * row(4, kx)
            o = functools.reduce(lambda x, y: x + y, parts)

            mu = jnp.mean(o, axis=0, keepdims=True)
            dlt = o - mu
            var = jnp.mean(dlt * dlt, axis=0, keepdims=True)
            on = dlt * lax.rsqrt(var + GN_EPS) * lw_ref[...] + lb_ref[...]
            y_ref[t] = on + bon_ref[pl.ds(tl, HEAD_SIZE, stride=0), :] * v
            return c

        for tl in range(steps):
            step(tl, 0)
        return carry

    lax.fori_loop(0, n_batches, batch, 0)


def _scan(ct_seqs, v_seq, params8, params, s0, *, n_batches, steps, name):
    g = s0.shape[0]
    tt = n_batches * SCAN_TB
    ttv = v_seq[0].shape[1] if n_batches == 1 else n_batches * steps
    n_ti = ct_seqs[0][0].shape[2] // tt

    def spec(block, lead, tpos):
        def index(gi, ti):
            idx = [gi if lead is None else lead, 0, 0, 0]
            idx[tpos] = ti
            return tuple(idx)
        return pl.BlockSpec(block, index)

    ct_block = (None, HEAD_SIZE, tt, LANES)
    tv_block = (None, ttv, HEAD_SIZE, LANES)
    par8_spec = pl.BlockSpec((HEAD_SIZE * SCAN_TB, LANES), lambda gi, ti: (0, 0))
    par_spec = pl.BlockSpec((HEAD_SIZE, LANES), lambda gi, ti: (0, 0))
    st_spec = pl.BlockSpec((None, HEAD_SIZE, HEAD_SIZE, LANES), lambda gi, ti: (gi, 0, 0, 0))
    return pl.pallas_call(
        functools.partial(_scan_body, n_batches=n_batches, steps=steps),
        grid=(g, n_ti),
        in_specs=([spec(ct_block, lead, 2) for _, lead in ct_seqs] + [spec(tv_block, v_seq[1], 1)]
                  + [par8_spec] * len(params8) + [par_spec] * len(params) + [st_spec]),
        out_specs=[spec(tv_block, None, 1), st_spec],
        out_shape=[jax.ShapeDtypeStruct((g, n_ti * ttv, HEAD_SIZE, LANES), F32),
                   jax.ShapeDtypeStruct((g, HEAD_SIZE, HEAD_SIZE, LANES), F32)],
        scratch_shapes=[pltpu.VMEM((5, HEAD_SIZE, SCAN_TB, LANES), F32), pltpu.VMEM((SCAN_TB, LANES), F32),
                        pltpu.VMEM((2, HEAD_SIZE * SCAN_TB, LANES), F32), pltpu.VMEM((HEAD_SIZE, LANES), F32)],
        compiler_params=_cparams("arbitrary", "arbitrary", vmem_mb=48),
        name=name,
    )(*[a for a, _ in ct_seqs], v_seq[0], *params8, *params, s0)


def _sample_groups(x):
    return x.reshape(DEC_SEQ, DEC_BATCH // CHAIN_B, CHAIN_B, N_HEADS, HEAD_SIZE).transpose(1, 0, 2, 3, 4)


def _to_chain_sample_ct(x):
    x = _sample_groups(x).transpose(0, 4, 1, 2, 3).reshape(DEC_BATCH // CHAIN_B, HEAD_SIZE, DEC_SEQ, LANES)
    return jnp.pad(x, ((0, 0), (0, 0), (0, SCAN_TB - DEC_SEQ), (0, 0)))


def _to_chain_sample_tc(x):
    return _sample_groups(x).transpose(0, 1, 4, 2, 3).reshape(DEC_BATCH // CHAIN_B, DEC_SEQ, HEAD_SIZE, LANES)


def _from_chain_sample(y):
    ng = DEC_BATCH // CHAIN_B
    y = y.reshape(ng, DEC_SEQ, HEAD_SIZE, CHAIN_B, N_HEADS).transpose(1, 0, 3, 2, 4)
    return y.reshape(N_SAMPLE, RWKV_WIDTH)


def _param_chain(p):
    return jnp.tile(p.reshape(N_HEADS, HEAD_SIZE).T, (1, CHAIN_B))


def _param_chain8(p):
    return jnp.repeat(_param_chain(p), SCAN_TB, axis=0)


def _head_minor(w, axis):
    shape = w.shape
    w = w.reshape(shape[:axis] + (N_HEADS, HEAD_SIZE) + shape[axis + 1:])
    return jnp.swapaxes(w, axis, axis + 1).reshape(shape)


def _rwkv_out_body(yc_ref, ys_ref, g_ref, w_ref, o_ref, yt, *, n_prompt_steps):
    s = pl.program_id(0)

    def finish(y):
        o_ref[...] = jnp.dot((y * g_ref[...]).astype(BF16), w_ref[...], preferred_element_type=F32)

    @pl.when(s < n_prompt_steps)
    def _():
        b = s % CHAIN_B

        @pl.when(b == 0)
        def _():
            _load_chain_tc(yc_ref, yt, TM_T // LANES)

        finish(jnp.concatenate([yt[b, half].T for half in range(TM_T // LANES)], axis=0))

    @pl.when(s >= n_prompt_steps)
    def _():
        finish(ys_ref[...])


def _rwkv_out(y_chain, y_s, g, w):
    n_tb = SEQ // TM_T
    nps = n_tb * CHAIN_B

    def row_blk(s):
        return jnp.where(s < nps, (s % CHAIN_B) * n_tb + s // CHAIN_B, s)

    return pl.pallas_call(
        functools.partial(_rwkv_out_body, n_prompt_steps=nps),
        grid=(N_TOK // TM_T,),
        in_specs=[pl.BlockSpec((TM_T * HEAD_SIZE, LANES), lambda s: (jnp.minimum(s // CHAIN_B, n_tb - 1), 0)),
                  pl.BlockSpec((TM_T, RWKV_WIDTH), lambda s: (jnp.maximum(s - nps, 0), 0)),
                  pl.BlockSpec((TM_T, RWKV_WIDTH), lambda s: (row_blk(s), 0)),
                  _resident((RWKV_WIDTH, D_MODEL))],
        out_specs=pl.BlockSpec((TM_T, D_MODEL), lambda s: (row_blk(s), 0)),
        out_shape=jax.ShapeDtypeStruct((N_TOK, D_MODEL), F32),
        scratch_shapes=[pltpu.VMEM((CHAIN_B, TM_T // LANES, RWKV_WIDTH, LANES), F32)],
        compiler_params=_cparams("arbitrary", vmem_mb=56),
        name="rwkv_out",
    )(y_chain, y_s, g, w)


def _mix_body(co_ref, ro_ref, xn_ref, wgr_ref, xp_ref, xs_ref, wm_ref, nf_ref, wr_ref, br_ref,
              h_ref, hn_ref, ridx_ref, rw_ref, *, n_prompt_tiles):
    i = pl.program_id(0)
    g_rwkv = jnp.dot(xn_ref[...], wgr_ref[...], preferred_element_type=F32)
    mixed = co_ref[...] + _sigmoid(g_rwkv) * ro_ref[...]
    mo = jnp.dot(mixed.astype(BF16), wm_ref[...], preferred_element_type=F32)

    def finish(x):
        h = x + mo
        h_ref[...] = h
        ms = jnp.mean(h * h, axis=-1, keepdims=True)
        hn = h * lax.rsqrt(ms + RMS_EPS) * nf_ref[...]
        hnb = hn.astype(BF16)
        hn_ref[...] = hnb.reshape(hn.shape[0], D_MODEL // LANES, LANES)
        logits = jnp.dot(hnb, wr_ref[...], preferred_element_type=F32) + br_ref[...]
        tm = logits.shape[0]
        lane = lax.broadcasted_iota(jnp.int32, (tm, LANES), 1)
        neg = jnp.float32(-jnp.inf)
        gl = jnp.where(lane < N_GROUPS, logits, neg)
        gmax = jnp.max(gl, axis=-1, keepdims=True)
        g_idx = jnp.min(jnp.where(gl == gmax, lane, LANES), axis=-1, keepdims=True)
        g_w = 1.0 / jnp.sum(jnp.exp(gl - gmax), axis=-1, keepdims=True)
        lo = N_GROUPS + g_idx * EXPERTS_PER_GROUP
        el = jnp.where(jnp.logical_and(lane >= lo, lane < lo + EXPERTS_PER_GROUP), logits, neg)
        m1 = jnp.max(el, axis=-1, keepdims=True)
        i1 = jnp.min(jnp.where(el == m1, lane, LANES), axis=-1, keepdims=True)
        el2 = jnp.where(lane == i1, neg, el)
        m2 = jnp.max(el2, axis=-1, keepdims=True)
        i2 = jnp.min(jnp.where(el2 == m2, lane, LANES), axis=-1, keepdims=True)
        t2 = jnp.exp(m2 - m1)
        den = 1.0 + t2
        ridx_ref[...] = jnp.where(lane == 0, i1 - N_GROUPS, jnp.where(lane == 1, i2 - N_GROUPS, 0))
        rw_ref[...] = jnp.where(lane == 0, (1.0 / den) * g_w, jnp.where(lane == 1, (t2 / den) * g_w, 0.0))

    @pl.when(i < n_prompt_tiles)
    def _():
        finish(xp_ref[...])

    @pl.when(i >= n_prompt_tiles)
    def _():
        finish(xs_ref[...])


def _mix_route(conv_out, rwkv_out, xn, wgr, xp, xs, wm, nf, wr, br):
    npt = N_PROMPT // TM_S
    tok_spec = pl.BlockSpec((TM_S, D_MODEL), lambda i: (i, 0))
    small_spec = pl.BlockSpec((TM_S, LANES), lambda i: (i, 0))
    return pl.pallas_call(
        functools.partial(_mix_body, n_prompt_tiles=npt),
        grid=(N_TOK // TM_S,),
        in_specs=[tok_spec, tok_spec, tok_spec, _resident((D_MODEL, D_MODEL)),
                  pl.BlockSpec((TM_S, D_MODEL), lambda i: (jnp.minimum(i, npt - 1), 0)),
                  pl.BlockSpec((TM_S, D_MODEL), lambda i: (jnp.maximum(i - npt, 0), 0)),
                  _resident((D_MODEL, D_MODEL)),
                  pl.BlockSpec((1, D_MODEL), lambda i: (0, 0)),
                  _resident((D_MODEL, LANES)),
                  pl.BlockSpec((1, LANES), lambda i: (0, 0))],
        out_specs=[tok_spec, pl.BlockSpec((TM_S, D_MODEL // LANES, LANES), lambda i: (i, 0, 0)),
                   small_spec, small_spec],
        out_shape=[jax.ShapeDtypeStruct((N_TOK, D_MODEL), F32),
                   jax.ShapeDtypeStruct((N_TOK, D_MODEL // LANES, LANES), BF16),
                   jax.ShapeDtypeStruct((N_TOK, LANES), jnp.int32),
                   jax.ShapeDtypeStruct((N_TOK, LANES), F32)],
        compiler_params=_cparams("arbitrary", vmem_mb=56),
        name="mix_route",
    )(conv_out, rwkv_out, xn, wgr, xp, xs, wm, nf, wr, br)


def _expert_body(blk_e_ref, slot_tok_ref, nused_ref, first_ref, par_ref, next_e_ref,
                 hn_ref, wg_hbm, wu_hbm, wd_hbm, yb_ref,
                 xbuf, sem, wfg, wfu, wfd, wsem, wgb, wub, wdb):
    i = pl.program_id(0)
    nused = nused_ref[0]
    slot = i % N_XBUF

    def row_copy(blk, r, s):
        tok = slot_tok_ref[blk * MOE_BLOCK + r]
        return pltpu.make_async_copy(hn_ref.at[tok], xbuf.at[s, r], sem.at[s])

    def issue(blk, s):
        for r in range(MOE_BLOCK):
            row_copy(blk, r, s).start(priority=r % 2)

    def w_copies(e, s):
        return (pltpu.make_async_copy(wg_hbm.at[e], wfg.at[s], wsem.at[s]),
                pltpu.make_async_copy(wu_hbm.at[e], wfu.at[s], wsem.at[s]),
                pltpu.make_async_copy(wd_hbm.at[e], wfd.at[s], wsem.at[s]))

    @pl.when(jnp.logical_and(i == 0, nused > 0))
    def _():
        for c in w_copies(blk_e_ref[0], 0):
            c.start(priority=1)
        for a in range(GATHER_AHEAD):
            @pl.when(a < nused)
            def _():
                issue(a, a)

    @pl.when(i + GATHER_AHEAD < nused)
    def _():
        issue(i + GATHER_AHEAD, (i + GATHER_AHEAD) % N_XBUF)

    @pl.when(i < nused)
    def _():
        @pl.when(first_ref[i] == 1)
        def _():
            ws = par_ref[i]
            for c in w_copies(blk_e_ref[i], ws):
                c.wait()

            @pl.when(next_e_ref[i] >= 0)
            def _():
                for c in w_copies(next_e_ref[i], 1 - ws):
                    c.start(priority=1)

            wgb[...] = wfg[ws].astype(BF16)
            wub[...] = wfu[ws].astype(BF16)
            wdb[...] = wfd[ws].astype(BF16)

        for r in range(MOE_BLOCK):
            row_copy(i, r, slot).wait()
        xe = xbuf[slot].reshape(MOE_BLOCK, D_MODEL)
        gate = jnp.dot(xe, wgb[...], preferred_element_type=F32)
        up = jnp.dot(xe, wub[...], preferred_element_type=F32)
        hdn = (gate * _sigmoid(gate)) * up
        yb = jnp.dot(hdn.astype(BF16), wdb[...], preferred_element_type=F32)
        yb_ref[...] = yb.reshape(MOE_BLOCK, D_MODEL // LANES, LANES)

    @pl.when(i >= nused)
    def _():
        yb_ref[...] = jnp.zeros_like(yb_ref)


def _experts(plan, hn, wg, wu, wd):
    blk_e, slot_tok, nused, first, par, next_e = plan
    n_blocks = blk_e.shape[0]
    any_spec = pl.BlockSpec(memory_space=pl.ANY)
    return pl.pallas_call(
        _expert_body,
        grid_spec=pltpu.PrefetchScalarGridSpec(
            num_scalar_prefetch=6,
            grid=(n_blocks,),
            in_specs=[any_spec, any_spec, any_spec, any_spec],
            out_specs=pl.BlockSpec((MOE_BLOCK, D_MODEL // LANES, LANES), lambda i, *_: (i, 0, 0)),
            scratch_shapes=[pltpu.VMEM((N_XBUF, MOE_BLOCK, D_MODEL // LANES, LANES), BF16),
                            pltpu.SemaphoreType.DMA((N_XBUF,)),
                            pltpu.VMEM((2, D_MODEL, D_EXPERT), F32),
                            pltpu.VMEM((2, D_MODEL, D_EXPERT), F32),
                            pltpu.VMEM((2, D_EXPERT, D_MODEL), F32),
                            pltpu.SemaphoreType.DMA((2,)),
                            pltpu.VMEM((D_MODEL, D_EXPERT), BF16),
                            pltpu.VMEM((D_MODEL, D_EXPERT), BF16),
                            pltpu.VMEM((D_EXPERT, D_MODEL), BF16)]),
        out_shape=jax.ShapeDtypeStruct((n_blocks * MOE_BLOCK, D_MODEL // LANES, LANES), F32),
        compiler_params=_cparams("arbitrary", vmem_mb=48),
        name="experts",
    )(blk_e, slot_tok, nused, first, par, next_e, hn, wg, wu, wd)


def _combine_body(dest_ref, yb_ref, h_ref, rw_ref, p_ref, wpg_ref, wpp_ref, nf_ref, yp_ref, ys_ref,
                  ybuf, sem):
    i = pl.program_id(0)
    tm = h_ref.shape[0]
    slot = i % 2

    def row_copy(tile, r, s, sl):
        d = dest_ref[(tile * tm + r) * TOP_K + s]
        return pltpu.make_async_copy(yb_ref.at[d], ybuf.at[sl, s, r], sem.at[sl])

    def issue(tile, sl):
        for r in range(tm):
            row_copy(tile, r, 0, sl).start()
            row_copy(tile, r, 1, sl).start()

    @pl.when(i == 0)
    def _():
        issue(0, 0)

    @pl.when(i + 1 < pl.num_programs(0))
    def _():
        issue(i + 1, 1 - slot)

    for r in range(tm):
        row_copy(i, r, 0, slot).wait()
        row_copy(i, r, 1, slot).wait()

    rw = rw_ref[...]
    y0 = ybuf[slot, 0].reshape(tm, D_MODEL)
    y1 = ybuf[slot, 1].reshape(tm, D_MODEL)
    h2 = h_ref[...] + (y0 * rw[:, 0:1] + y1 * rw[:, 1:2])
    gate = _sigmoid(jnp.dot(h2.astype(BF16), wpg_ref[...], preferred_element_type=F32))
    pp = jnp.dot(p_ref[...].astype(BF16), wpp_ref[...], preferred_element_type=F32)
    h3 = h2 + gate * pp
    ms = jnp.mean(h3 * h3, axis=-1, keepdims=True)
    y = h3 * lax.rsqrt(ms + RMS_EPS) * nf_ref[...]

    @pl.when(i < N_PROMPT // TM_S)
    def _():
        yp_ref[...] = y

    @pl.when(i >= N_PROMPT // TM_S)
    def _():
        ys_ref[...] = y


def _combine(dest, yb, h, rw, p_all, wpg, wpp, nf):
    npt = N_PROMPT // TM_S
    return pl.pallas_call(
        _combine_body,
        grid_spec=pltpu.PrefetchScalarGridSpec(
            num_scalar_prefetch=1,
            grid=(N_TOK // TM_S,),
            in_specs=[pl.BlockSpec(memory_space=pl.ANY),
                      pl.BlockSpec((TM_S, D_MODEL), lambda i, d: (i, 0)),
                      pl.BlockSpec((TM_S, LANES), lambda i, d: (i, 0)),
                      pl.BlockSpec((TM_S, PLE_DIM), lambda i, d: (i, 0)),
                      pl.BlockSpec((D_MODEL, D_MODEL), lambda i, d: (0, 0), pipeline_mode=pl.Buffered(1)),
                      pl.BlockSpec((PLE_DIM, D_MODEL), lambda i, d: (0, 0), pipeline_mode=pl.Buffered(1)),
                      pl.BlockSpec((1, D_MODEL), lambda i, d: (0, 0))],
            out_specs=[pl.BlockSpec((TM_S, D_MODEL), lambda i, d: (jnp.minimum(i, npt - 1), 0)),
                       pl.BlockSpec((TM_S, D_MODEL), lambda i, d: (jnp.maximum(i - npt, 0), 0))],
            scratch_shapes=[pltpu.VMEM((2, TOP_K, TM_S, D_MODEL // LANES, LANES), F32),
                            pltpu.SemaphoreType.DMA((2,))]),
        out_shape=[jax.ShapeDtypeStruct((N_PROMPT, D_MODEL), F32),
                   jax.ShapeDtypeStruct((N_SAMPLE, D_MODEL), F32)],
        compiler_params=_cparams("arbitrary", vmem_mb=48),
        name="combine_ple",
    )(dest, yb, h, rw, p_all, wpg, wpp, nf)


def _dispatch_plan(eidx):
    n_assign = N_TOK * TOP_K
    e_flat = eidx.reshape(n_assign)
    onehot = (e_flat[:, None] == jnp.arange(N_EXPERTS, dtype=jnp.int32)[None, :]).astype(jnp.int32)
    csum = jnp.cumsum(onehot, axis=0)
    counts = csum[-1]
    rank = jnp.sum(csum * onehot, axis=1) - 1
    padded = (counts + MOE_BLOCK - 1) // MOE_BLOCK * MOE_BLOCK
    pad_end = jnp.cumsum(padded)
    pad_start = pad_end - padded
    dest = pad_start[e_flat] + rank
    n_blocks = -(-n_assign // MOE_BLOCK) + N_EXPERTS
    tok = jnp.arange(n_assign, dtype=jnp.int32) // TOP_K
    pad_tok = jnp.arange(n_blocks * MOE_BLOCK, dtype=jnp.int32) % N_TOK
    slot_tok = pad_tok.at[dest].set(tok)
    blk = jnp.arange(n_blocks, dtype=jnp.int32)
    blk_e = jnp.minimum(jnp.searchsorted(pad_end, blk * MOE_BLOCK, side="right"), N_EXPERTS - 1).astype(jnp.int32)
    nused = (pad_end[-1] // MOE_BLOCK).astype(jnp.int32)
    prev_e = jnp.concatenate([jnp.full((1,), -1, jnp.int32), blk_e[:-1]])
    first = jnp.logical_and(blk < nused, blk_e != prev_e)
    par = ((jnp.cumsum(first.astype(jnp.int32)) - 1) % 2).astype(jnp.int32)
    idx_first = jnp.where(first, blk, n_blocks)
    later = jnp.concatenate([lax.cummin(idx_first[::-1])[::-1][1:], jnp.full((1,), n_blocks, jnp.int32)])
    next_e = jnp.where(later < n_blocks, blk_e[jnp.minimum(later, n_blocks - 1)], -1).astype(jnp.int32)
    plan = (blk_e, slot_tok, nused.reshape(1), first.astype(jnp.int32), par, next_e)
    return dest.astype(jnp.int32), plan


def kernel(x_prompt, x_sample, state_conv, state_shift, state_wkv, p_prompt, p_sample, norm_mix, w_in, conv_w, w_conv_out, shift_mu, w0, w2, a0, a2, g2, k_k, k_a, r_k, lnx_w, lnx_b, w_rwkv_out, w_mix_out, norm_ffn, w_route_group, b_route_group, w_route_expert, b_route_expert, w_exp_gate, w_exp_up, w_exp_down, w_ple_proj, w_ple_gate, norm_final):
    c3 = 3 * CONV_WIDTH
    rw3 = 3 * RWKV_WIDTH
    xp = x_prompt.reshape(N_PROMPT, D_MODEL)
    xs = x_sample.transpose(1, 0, 2).reshape(N_SAMPLE, D_MODEL)
    win = w_in[0]
    mu = shift_mu[0]
    st = state_shift[0]

    xn = _norm_cast(xp, xs, norm_mix)
    rkv_s, _, sraw_rkv = _inproj(xn, win, mu[None, :rw3], st[:, :rw3],
                                 tn=1024, col_blk_off=c3 // 1024, n_col_blocks=rw3 // 1024,
                                 shift_lo=0, shift_hi=rw3 // 1024,
                                 tile_lo=N_PROMPT // TM, n_tiles=N_SAMPLE // TM, name="inproj_rkv_sample")
    n_lr = RWKV_PROJ - rw3
    mu_l = jnp.concatenate([mu[rw3:], jnp.zeros((LORA_W - n_lr,), F32)])[None, :]
    st_l = jnp.concatenate([st[:, rw3:], jnp.zeros((DEC_BATCH, LORA_W - n_lr), F32)], axis=1)
    w_l = win[:, c3 + rw3:c3 + rw3 + LORA_W].astype(BF16)
    w_gc = win[:, c3 + RWKV_PROJ:c3 + RWKV_PROJ + D_MODEL].astype(BF16)
    w_gr = win[:, c3 + RWKV_PROJ + D_MODEL:].astype(BF16)
    w_rk = _head_minor(win[:, c3:c3 + 2 * RWKV_WIDTH].reshape(D_MODEL, 2, RWKV_WIDTH), 2)
    rk_chain = _inproj_t(xn, w_rk.reshape(D_MODEL, 2 * RWKV_WIDTH).T.astype(BF16),
                         time_major=False, name="inproj_t_rk")
    v_chain = _inproj_t(xn, _head_minor(win[:, c3 + 2 * RWKV_WIDTH:c3 + rw3], 1).T.astype(BF16),
                        time_major=True, name="inproj_t_v")

    sc = state_conv[0].transpose(1, 0, 2).reshape(2 * DEC_BATCH, CONV_WIDTH)
    conv_out, ulast, us = _conv_branch(xn, win[:, :c3].astype(BF16), conv_w[0], sc,
                                       w_conv_out[0].astype(BF16), w_gc)

    def pad_rows(w, before, total):
        return jnp.pad(w, ((before, total - before - w.shape[0]), (0, 0))).astype(BF16)

    w2p = pad_rows(w2[0], 0, LANES)
    a2p = pad_rows(a2[0], DECAY_LORA, 2 * LANES)
    g2p = pad_rows(_head_minor(g2[0], 1), DECAY_LORA + A_LORA - LANES, LORA_W - LANES)
    g, wl_c, al_c, wl_s, al_s, last_l, sraw_l = _lora(xn, w_l, mu_l, st_l, _head_minor(w2p, 1).T,
                                                      _head_minor(a2p, 1).T, w2p, a2p, g2p)
    par8 = [_param_chain8(p) for p in (k_k[0], k_a[0], r_k[0].reshape(RWKV_WIDTH))]
    mu8 = [_param_chain8(mu[n * RWKV_WIDTH:(n + 1) * RWKV_WIDTH]) for n in range(2)]
    bias8 = [_param_chain8(w0[0]), _param_chain8(a0[0])]
    zero8 = jnp.zeros((HEAD_SIZE * SCAN_TB, LANES), F32)
    mu_v = _param_chain(mu[2 * RWKV_WIDTH:rw3])
    gn = [_param_chain(lnx_w[0]), _param_chain(lnx_b[0])]
    seqs_p = [(rk_chain, 0), (rk_chain, 1), (wl_c[None], None), (al_c[None], None)]
    v4 = v_chain.reshape(1, SEQ, HEAD_SIZE, LANES)
    s0_p = jnp.zeros((1, HEAD_SIZE, HEAD_SIZE, LANES), F32)
    y_p, sf_p = _scan(seqs_p, (v4, None), par8 + mu8 + bias8, [mu_v] + gn, s0_p,
                      n_batches=SCAN_TT // SCAN_TB, steps=SCAN_TB, name="wkv_scan_prompt")
    ng = DEC_BATCH // CHAIN_B
    s0_s = state_wkv[0].reshape(ng, CHAIN_B, N_HEADS, HEAD_SIZE, HEAD_SIZE).transpose(0, 4, 3, 1, 2)
    s0_s = s0_s.reshape(ng, HEAD_SIZE, HEAD_SIZE, LANES)
    seqs_s = [(_to_chain_sample_ct(rkv_s[:, n * RWKV_WIDTH:(n + 1) * RWKV_WIDTH]), None) for n in range(2)]
    seqs_s += [(_to_chain_sample_ct(wl_s), None), (_to_chain_sample_ct(al_s), None)]
    v_s = _to_chain_sample_tc(rkv_s[:, 2 * RWKV_WIDTH:rw3])
    y_s, sf_s = _scan(seqs_s, (v_s, None), par8 + [zero8, zero8] + bias8,
                      [jnp.zeros((HEAD_SIZE, LANES), F32)] + gn, s0_s,
                      n_batches=1, steps=DEC_SEQ, name="wkv_scan_sample")
    rwkv_out = _rwkv_out(y_p.reshape(SEQ * HEAD_SIZE, LANES), _from_chain_sample(y_s), g,
                         _head_minor(w_rwkv_out[0], 0).astype(BF16))

    wr = jnp.concatenate([w_route_group[0], w_route_expert[0],
                          jnp.zeros((D_MODEL, LANES - N_GROUPS - N_EXPERTS), F32)], axis=1)
    br = jnp.concatenate([b_route_group[0], b_route_expert[0],
                          jnp.zeros((LANES - N_GROUPS - N_EXPERTS,), F32)])[None, :]
    h, hn, ridx, rw = _mix_route(conv_out, rwkv_out, xn, w_gr, xp, xs, w_mix_out[0].astype(BF16),
                                 norm_ffn, wr.astype(BF16), br)

    dest, plan = _dispatch_plan(ridx[:, :TOP_K])
    yb = _experts(plan, hn, w_exp_gate[0], w_exp_up[0], w_exp_down[0])
    p_all = jnp.concatenate([p_prompt[0].reshape(N_PROMPT, PLE_DIM),
                             p_sample[0].transpose(1, 0, 2).reshape(N_SAMPLE, PLE_DIM)], axis=0)
    y_p2, y_s2 = _combine(dest, yb, h, rw, p_all, w_ple_gate[0].astype(BF16), w_ple_proj[0].astype(BF16),
                          norm_final[None, :])

    y_prompt = y_p2.reshape(BATCH, SEQ, D_MODEL)
    y_sample = y_s2.reshape(DEC_SEQ, DEC_BATCH, D_MODEL).transpose(1, 0, 2)
    tiles_per_seq = SEQ // TM
    seq_last = jnp.arange(BATCH) * tiles_per_seq + tiles_per_seq - 1

    conv_p = ulast.reshape(-1, SUBLANES, CONV_WIDTH)[seq_last, SUBLANES - 2:, :][None]
    conv_s = us.reshape(2, DEC_BATCH, CONV_WIDTH).transpose(1, 0, 2)[None]
    last_rkv = jnp.concatenate([rk_chain[:, :, SEQ - 1, :], v4[:, SEQ - 1]], axis=0)
    lm = last_rkv.reshape(3, HEAD_SIZE, BATCH, N_HEADS).transpose(2, 0, 3, 1).reshape(BATCH, rw3)
    lora_last = (SEQ // TM_L - 1) * CHAIN_B + jnp.arange(BATCH)
    lt = last_l.reshape(-1, SUBLANES, LORA_W)[lora_last, SUBLANES - 1, :n_lr]
    shift_p = jnp.concatenate([lm, lt], axis=1)[None]
    shift_s = jnp.concatenate([sraw_rkv, sraw_l[:, :n_lr]], axis=1)[None]
    wkv_p = sf_p.reshape(HEAD_SIZE, HEAD_SIZE, BATCH, N_HEADS).transpose(2, 3, 1, 0)[None]
    wkv_s = sf_s.reshape(ng, HEAD_SIZE, HEAD_SIZE, CHAIN_B, N_HEADS).transpose(0, 3, 4, 2, 1)
    wkv_s = wkv_s.reshape(DEC_BATCH, N_HEADS, HEAD_SIZE, HEAD_SIZE)[None]
    return (y_prompt, y_sample, conv_p, shift_p, wkv_p, conv_s, shift_s, wkv_s)
```

```python
import functools

import jax
import jax.numpy as jnp
from jax import lax
from jax.experimental import pallas as pl
from jax.experimental.pallas import tpu as pltpu

D_MODEL = 2048
BATCH = 4
SEQ = 2048
DEC_BATCH = 128
DEC_SEQ = 4
CONV_WIDTH = 1024
RWKV_WIDTH = 2048
HEAD_SIZE = 64
N_HEADS = RWKV_WIDTH // HEAD_SIZE
DECAY_LORA = 96
A_LORA = 96
GATE_LORA = 256
RWKV_PROJ = 3 * RWKV_WIDTH + DECAY_LORA + A_LORA + GATE_LORA
N_GROUPS = 8
EXPERTS_PER_GROUP = 8
N_EXPERTS = N_GROUPS * EXPERTS_PER_GROUP
TOP_K = 2
D_EXPERT = 512
MOE_BLOCK = 128
PLE_DIM = 256
RMS_EPS = 1e-6
GN_EPS = 64e-5

N_PROMPT = BATCH * SEQ
N_SAMPLE = DEC_BATCH * DEC_SEQ
N_TOK = N_PROMPT + N_SAMPLE
LANES = 128
SUBLANES = 8
TM = 512
TM_S = 256
TM_T = 256
TM_L = 128
CHAIN_B = LANES // N_HEADS
SCAN_TT = 64
SCAN_TB = SUBLANES
SCAN_ACC = 2
GATHER_AHEAD = 3
N_XBUF = GATHER_AHEAD + 1
LORA_W = 512
EXP_M05 = 0.6065306597126334
F32 = jnp.float32
BF16 = jnp.bfloat16
_NT = (((1,), (1,)), ((), ()))


def _sigmoid(x):
    return 1.0 / (1.0 + jnp.exp(-x))


def _cparams(*sem, vmem_mb=None):
    kw = dict(dimension_semantics=sem)
    if vmem_mb is not None:
        kw["vmem_limit_bytes"] = vmem_mb * 1024 * 1024
    return pltpu.CompilerParams(**kw)


def _resident(shape):
    nd = len(shape)
    return pl.BlockSpec(shape, lambda *_: (0,) * nd, pipeline_mode=pl.Buffered(1))


def _chain_tile(zt, half, c):
    start = c * N_HEADS
    rows = pl.ds(start if isinstance(c, int) else pl.multiple_of(start, N_HEADS), N_HEADS)
    return jnp.concatenate([zt[bb, half, rows, :] for bb in range(CHAIN_B)], axis=0)


def _store_chain_ct(zt, out_ref, halves, c_lo=0, n_c=HEAD_SIZE):
    for half in range(halves):
        for ci in range(n_c):
            out_ref[c_lo + ci, half * LANES:(half + 1) * LANES, :] = _chain_tile(zt, half, c_lo + ci).T


def _store_chain_tc(zt, out_ref, halves, c_lo=0, n_c=HEAD_SIZE):
    for half in range(halves):
        for ci in range(n_c):
            rows = pl.ds(half * LANES * HEAD_SIZE + c_lo + ci, LANES, stride=HEAD_SIZE)
            out_ref[rows, :] = _chain_tile(zt, half, c_lo + ci).T


def _load_chain_tc(y_ref, yt, halves):
    for half in range(halves):
        for v in range(HEAD_SIZE):
            mt = y_ref[pl.ds(half * LANES * HEAD_SIZE + v, LANES, stride=HEAD_SIZE), :].T
            for bb in range(CHAIN_B):
                yt[bb, half, v * N_HEADS:(v + 1) * N_HEADS, :] = mt[bb * N_HEADS:(bb + 1) * N_HEADS, :]


def _norm_body(xp_ref, xs_ref, g_ref, o_ref, *, n_prompt_tiles):
    i = pl.program_id(0)

    def f(x):
        ms = jnp.mean(x * x, axis=-1, keepdims=True)
        return (x * lax.rsqrt(ms + RMS_EPS) * g_ref[...]).astype(o_ref.dtype)

    @pl.when(i < n_prompt_tiles)
    def _():
        o_ref[...] = f(xp_ref[...])

    @pl.when(i >= n_prompt_tiles)
    def _():
        o_ref[...] = f(xs_ref[...])


def _norm_cast(xp, xs, g):
    npt = N_PROMPT // TM
    return pl.pallas_call(
        functools.partial(_norm_body, n_prompt_tiles=npt),
        grid=(N_TOK // TM,),
        in_specs=[pl.BlockSpec((TM, D_MODEL), lambda i: (jnp.minimum(i, npt - 1), 0)),
                  pl.BlockSpec((TM, D_MODEL), lambda i: (jnp.maximum(i - npt, 0), 0)),
                  pl.BlockSpec((1, D_MODEL), lambda i: (0, 0))],
        out_specs=pl.BlockSpec((TM, D_MODEL), lambda i: (i, 0)),
        out_shape=jax.ShapeDtypeStruct((N_TOK, D_MODEL), BF16),
        compiler_params=_cparams("arbitrary"),
        name="norm_cast",
    )(xp, xs, g)


def _inproj_body(xn_ref, w_ref, mu_ref, st_ref, z_ref, last_ref, sraw_ref, wb_ref, carry_ref, *,
                 shift_lo, shift_hi, tile_lo, n_prompt_tiles, tiles_per_seq):
    j = pl.program_id(0)
    i = pl.program_id(1) + tile_lo

    @pl.when(pl.program_id(1) == 0)
    def _():
        wb_ref[...] = w_ref[...].astype(BF16)

    z = jnp.dot(xn_ref[...], wb_ref[...], preferred_element_type=F32)
    tm = z.shape[0]
    last_ref[...] = z[tm - SUBLANES:tm]
    shifted = jnp.logical_and(j >= shift_lo, j < shift_hi)
    is_prompt = i < n_prompt_tiles

    @pl.when(jnp.logical_not(shifted))
    def _():
        z_ref[...] = z

    @pl.when(jnp.logical_and(shifted, is_prompt))
    def _():
        @pl.when(i % tiles_per_seq == 0)
        def _():
            carry_ref[...] = jnp.zeros_like(carry_ref)

        prev = pltpu.roll(z, 1, 0)
        row = lax.broadcasted_iota(jnp.int32, (tm, 1), 0)
        prev = jnp.where(row == 0, carry_ref[SUBLANES - 1:SUBLANES, :], prev)
        z_ref[...] = z + mu_ref[...] * (prev - z)
        carry_ref[...] = z[tm - SUBLANES:tm]

    @pl.when(jnp.logical_and(shifted, jnp.logical_not(is_prompt)))
    def _():
        prev = jnp.concatenate([st_ref[...], z[:tm - DEC_BATCH]], axis=0)
        z_ref[...] = z + mu_ref[...] * (prev - z)

    @pl.when(jnp.logical_not(is_prompt))
    def _():
        sraw_ref[...] = z[tm - DEC_BATCH:tm]


def _inproj(xn, w, mu, st, *, tn, col_blk_off, n_col_blocks, shift_lo, shift_hi, tile_lo, n_tiles, name):
    n_out = tn * n_col_blocks
    body = functools.partial(_inproj_body, shift_lo=shift_lo, shift_hi=shift_hi, tile_lo=tile_lo,
                             n_prompt_tiles=N_PROMPT // TM, tiles_per_seq=SEQ // TM)
    return pl.pallas_call(
        body,
        grid=(n_col_blocks, n_tiles),
        in_specs=[pl.BlockSpec((TM, D_MODEL), lambda j, i: (i + tile_lo, 0)),
                  pl.BlockSpec((D_MODEL, tn), lambda j, i: (0, j + col_blk_off)),
                  pl.BlockSpec((1, tn), lambda j, i: (0, j)),
                  pl.BlockSpec((DEC_BATCH, tn), lambda j, i: (0, j))],
        out_specs=[pl.BlockSpec((TM, tn), lambda j, i: (i, j)),
                   pl.BlockSpec((SUBLANES, tn), lambda j, i: (i, j)),
                   pl.BlockSpec((DEC_BATCH, tn), lambda j, i: (0, j))],
        out_shape=[jax.ShapeDtypeStruct((n_tiles * TM, n_out), F32),
                   jax.ShapeDtypeStruct((n_tiles * SUBLANES, n_out), F32),
                   jax.ShapeDtypeStruct((DEC_BATCH, n_out), F32)],
        scratch_shapes=[pltpu.VMEM((D_MODEL, tn), BF16), pltpu.VMEM((SUBLANES, tn), F32)],
        compiler_params=_cparams("arbitrary", "arbitrary", vmem_mb=48),
        name=name,
    )(xn, w, mu, st)


def _inproj_t_body(xn_ref, wt_ref, o_ref, zt, *, time_major, n_tb):
    tb = pl.program_id(1)
    b = pl.program_id(2)
    halves = TM_T // LANES
    n_c = HEAD_SIZE // CHAIN_B

    def matmul():
        z = lax.dot_general(wt_ref[...], xn_ref[...], _NT, preferred_element_type=F32)
        for half in range(halves):
            zt[tb % 2, b, half] = z[:, half * LANES:(half + 1) * LANES]

    def retile():
        store = _store_chain_tc if time_major else _store_chain_ct
        store(zt.at[1 - tb % 2], o_ref, halves, c_lo=b * n_c, n_c=n_c)

    @pl.when(tb == 0)
    def _():
        matmul()

    @pl.when(jnp.logical_and(tb > 0, tb < n_tb))
    def _():
        retile()
        matmul()

    @pl.when(tb == n_tb)
    def _():
        retile()


def _inproj_t(xn, wt, *, time_major, name):
    n_tb = SEQ // TM_T
    n = wt.shape[0] // RWKV_WIDTH

    def prev_tb(tb):
        return jnp.maximum(tb - 1, 0)

    if time_major:
        out_spec = pl.BlockSpec((None, TM_T * HEAD_SIZE, LANES), lambda j, tb, b: (j, prev_tb(tb), 0))
        out_shape = jax.ShapeDtypeStruct((n, SEQ * HEAD_SIZE, LANES), F32)
    else:
        out_spec = pl.BlockSpec((None, HEAD_SIZE, TM_T, LANES), lambda j, tb, b: (j, 0, prev_tb(tb), 0))
        out_shape = jax.ShapeDtypeStruct((n, HEAD_SIZE, SEQ, LANES), F32)
    return pl.pallas_call(
        functools.partial(_inproj_t_body, time_major=time_major, n_tb=n_tb),
        grid=(n, n_tb + 1, CHAIN_B),
        in_specs=[pl.BlockSpec((TM_T, D_MODEL), lambda j, tb, b: (b * n_tb + jnp.minimum(tb, n_tb - 1), 0)),
                  pl.BlockSpec((RWKV_WIDTH, D_MODEL), lambda j, tb, b: (j, 0), pipeline_mode=pl.Buffered(1))],
        out_specs=out_spec,
        out_shape=out_shape,
        scratch_shapes=[pltpu.VMEM((2, CHAIN_B, TM_T // LANES, RWKV_WIDTH, LANES), F32)],
        compiler_params=_cparams("arbitrary", "arbitrary", "arbitrary", vmem_mb=56),
        name=name,
    )(xn, wt)


def _conv_body(xn_ref, wc_ref, cw_ref, sc_ref, wco_ref, wgc_ref, o_ref, ulast_ref, us_ref, carry_ref, *,
               n_prompt_tiles, tiles_per_seq):
    i = pl.program_id(0)
    gate_b = jnp.dot(xn_ref[...], wc_ref[:, 0:CONV_WIDTH], preferred_element_type=F32)
    u = (jnp.dot(xn_ref[...], wc_ref[:, CONV_WIDTH:2 * CONV_WIDTH], preferred_element_type=F32)
         * jnp.dot(xn_ref[...], wc_ref[:, 2 * CONV_WIDTH:3 * CONV_WIDTH], preferred_element_type=F32))
    tm = u.shape[0]
    ulast_ref[...] = u[tm - SUBLANES:tm]
    w0 = cw_ref[0:1, :]
    w1 = cw_ref[1:2, :]
    w2 = cw_ref[2:3, :]

    def finish(p1, p2):
        conv = w0 * p2 + w1 * p1 + w2 * u
        y = (gate_b * conv).astype(BF16)
        g_conv = jnp.dot(xn_ref[...], wgc_ref[...], preferred_element_type=F32)
        o_ref[...] = _sigmoid(g_conv) * jnp.dot(y, wco_ref[...], preferred_element_type=F32)

    @pl.when(i < n_prompt_tiles)
    def _():
        @pl.when(i % tiles_per_seq == 0)
        def _():
            carry_ref[...] = jnp.zeros_like(carry_ref)

        row = lax.broadcasted_iota(jnp.int32, (tm, 1), 0)
        c1 = carry_ref[SUBLANES - 1:SUBLANES, :]
        c2 = carry_ref[SUBLANES - 2:SUBLANES - 1, :]
        p1 = jnp.where(row == 0, c1, pltpu.roll(u, 1, 0))
        p2 = jnp.where(row == 0, c2, jnp.where(row == 1, c1, pltpu.roll(u, 2, 0)))
        carry_ref[...] = u[tm - SUBLANES:tm]
        finish(p1, p2)

    @pl.when(i >= n_prompt_tiles)
    def _():
        p1 = jnp.concatenate([sc_ref[DEC_BATCH:2 * DEC_BATCH, :], u[:tm - DEC_BATCH]], axis=0)
        p2 = jnp.concatenate([sc_ref[...], u[:tm - 2 * DEC_BATCH]], axis=0)
        us_ref[...] = u[tm - 2 * DEC_BATCH:tm]
        finish(p1, p2)


def _conv_branch(xn, wc, conv_w, sc, wco, wgc):
    n_tiles = N_TOK // TM
    body = functools.partial(_conv_body, n_prompt_tiles=N_PROMPT // TM, tiles_per_seq=SEQ // TM)
    return pl.pallas_call(
        body,
        grid=(n_tiles,),
        in_specs=[pl.BlockSpec((TM, D_MODEL), lambda i: (i, 0)),
                  _resident((D_MODEL, 3 * CONV_WIDTH)),
                  pl.BlockSpec((3, CONV_WIDTH), lambda i: (0, 0)),
                  pl.BlockSpec((2 * DEC_BATCH, CONV_WIDTH), lambda i: (0, 0)),
                  _resident((CONV_WIDTH, D_MODEL)), _resident((D_MODEL, D_MODEL))],
        out_specs=[pl.BlockSpec((TM, D_MODEL), lambda i: (i, 0)),
                   pl.BlockSpec((SUBLANES, CONV_WIDTH), lambda i: (i, 0)),
                   pl.BlockSpec((2 * DEC_BATCH, CONV_WIDTH), lambda i: (0, 0))],
        out_shape=[jax.ShapeDtypeStruct((N_TOK, D_MODEL), F32),
                   jax.ShapeDtypeStruct((n_tiles * SUBLANES, CONV_WIDTH), F32),
                   jax.ShapeDtypeStruct((2 * DEC_BATCH, CONV_WIDTH), F32)],
        scratch_shapes=[pltpu.VMEM((SUBLANES, CONV_WIDTH), F32)],
        compiler_params=_cparams("arbitrary", vmem_mb=48),
        name="conv_branch",
    )(xn, wc, conv_w, sc, wco, wgc)


def _lora_body(xn_ref, wl_ref, mu_ref, st_ref, w2t_ref, a2t_ref, w2_ref, a2_ref, g2_ref,
               g_ref, wlc_ref, alc_ref, wls_ref, als_ref, last_ref, sraw_ref, zt, carry_p, carry_s, *,
               n_prompt_steps):
    s = pl.program_id(0)
    z = jnp.dot(xn_ref[...], wl_ref[...], preferred_element_type=F32)
    tm = z.shape[0]
    last_ref[...] = z[tm - SUBLANES:tm]

    def project(zl):
        tw = jnp.tanh(zl[:, 0:LANES]).astype(BF16)
        xa = zl[:, 0:2 * LANES].astype(BF16)
        g_ref[...] = jnp.dot(_sigmoid(zl[:, LANES:LORA_W]).astype(BF16), g2_ref[...], preferred_element_type=F32)
        return tw, xa

    @pl.when(s < n_prompt_steps)
    def _():
        b = s % CHAIN_B

        @pl.when(s < CHAIN_B)
        def _():
            carry_p[b] = jnp.zeros((SUBLANES, LORA_W), F32)

        row = lax.broadcasted_iota(jnp.int32, (tm, 1), 0)
        prev = jnp.where(row == 0, carry_p[b, SUBLANES - 1:SUBLANES, :], pltpu.roll(z, 1, 0))
        carry_p[b] = z[tm - SUBLANES:tm]
        tw, xa = project(z + mu_ref[...] * (prev - z))
        zt[0, b, 0] = lax.dot_general(w2t_ref[...], tw, _NT, preferred_element_type=F32)
        zt[1, b, 0] = lax.dot_general(a2t_ref[...], xa, _NT, preferred_element_type=F32)

        @pl.when(b == CHAIN_B - 1)
        def _():
            _store_chain_ct(zt.at[0], wlc_ref, 1)
            _store_chain_ct(zt.at[1], alc_ref, 1)

    @pl.when(s >= n_prompt_steps)
    def _():
        @pl.when(s == n_prompt_steps)
        def _():
            carry_s[...] = st_ref[...]

        prev = carry_s[...]
        carry_s[...] = z
        sraw_ref[...] = z
        tw, xa = project(z + mu_ref[...] * (prev - z))
        wls_ref[...] = jnp.dot(tw, w2_ref[...], preferred_element_type=F32)
        als_ref[...] = jnp.dot(xa, a2_ref[...], preferred_element_type=F32)


def _lora(xn, w_l, mu_l, st_l, w2t, a2t, w2p, a2p, g2b):
    n_tb = SEQ // TM_L
    nps = n_tb * CHAIN_B
    n_steps = N_TOK // TM_L

    def row_blk(s):
        return jnp.where(s < nps, (s % CHAIN_B) * n_tb + s // CHAIN_B, s)

    chain_spec = pl.BlockSpec((HEAD_SIZE, TM_L, LANES), lambda s: (0, jnp.minimum(s // CHAIN_B, n_tb - 1), 0))
    samp_spec = pl.BlockSpec((TM_L, RWKV_WIDTH), lambda s: (jnp.maximum(s - nps, 0), 0))
    chain_shape = jax.ShapeDtypeStruct((HEAD_SIZE, SEQ, LANES), F32)
    samp_shape = jax.ShapeDtypeStruct((N_SAMPLE, RWKV_WIDTH), F32)
    return pl.pallas_call(
        functools.partial(_lora_body, n_prompt_steps=nps),
        grid=(n_steps,),
        in_specs=[pl.BlockSpec((TM_L, D_MODEL), lambda s: (row_blk(s), 0)),
                  _resident(w_l.shape),
                  pl.BlockSpec((1, LORA_W), lambda s: (0, 0)),
                  pl.BlockSpec((DEC_BATCH, LORA_W), lambda s: (0, 0)),
                  _resident(w2t.shape), _resident(a2t.shape), _resident(w2p.shape), _resident(a2p.shape),
                  _resident(g2b.shape)],
        out_specs=[pl.BlockSpec((TM_L, RWKV_WIDTH), lambda s: (row_blk(s), 0)),
                   chain_spec, chain_spec, samp_spec, samp_spec,
                   pl.BlockSpec((SUBLANES, LORA_W), lambda s: (s, 0)),
                   pl.BlockSpec((DEC_BATCH, LORA_W), lambda s: (0, 0))],
        out_shape=[jax.ShapeDtypeStruct((N_TOK, RWKV_WIDTH), F32),
                   chain_shape, chain_shape, samp_shape, samp_shape,
                   jax.ShapeDtypeStruct((n_steps * SUBLANES, LORA_W), F32),
                   jax.ShapeDtypeStruct((DEC_BATCH, LORA_W), F32)],
        scratch_shapes=[pltpu.VMEM((2, CHAIN_B, 1, RWKV_WIDTH, LANES), F32),
                        pltpu.VMEM((CHAIN_B, SUBLANES, LORA_W), F32),
                        pltpu.VMEM((DEC_BATCH, LORA_W), F32)],
        compiler_params=_cparams("arbitrary", vmem_mb=48),
        name="lora",
    )(xn, w_l, mu_l, st_l, w2t, a2t, w2p, a2p, g2b)


def _scan_body(r_ref, k_ref, wl_ref, al_ref, v_ref, kk_ref, ka_ref, rk_ref, mur_ref, muk_ref, w0_ref, a0_ref,
               muv_ref, lw_ref, lb_ref, s0_ref,
               y_ref, s_ref, vec_ref, bon_ref, prevb_ref, prevv_ref, *, n_batches, steps):
    @pl.when(pl.program_id(1) == 0)
    def _():
        s_ref[...] = s0_ref[...]
        prevb_ref[...] = jnp.zeros_like(prevb_ref)
        prevv_ref[...] = jnp.zeros_like(prevv_ref)

    rows = HEAD_SIZE * SCAN_TB
    first_t = lax.broadcasted_iota(jnp.int32, (rows, 1), 0) % SCAN_TB == 0

    def batch(bi, carry):
        t0 = pl.multiple_of(bi * SCAN_TB, SCAN_TB)

        def load(ref):
            return ref[:, pl.ds(t0, SCAN_TB), :].reshape(rows, LANES)

        def cube(x):
            return x.reshape(HEAD_SIZE, SCAN_TB, LANES)

        def shifted(x, slot, mu_ref):
            prev = jnp.where(first_t, pltpu.roll(prevb_ref[slot], rows - (SCAN_TB - 1), 0), pltpu.roll(x, 1, 0))
            prevb_ref[slot] = x
            return x + mu_ref[...] * (prev - x)

        r = shifted(load(r_ref), 0, mur_ref)
        k = shifted(load(k_ref), 1, muk_ref)
        decay = jnp.exp(-EXP_M05 * _sigmoid(load(wl_ref) + w0_ref[...]))
        a = _sigmoid(load(al_ref) + a0_ref[...])
        kk = cube(k * kk_ref[...])
        nrm = jnp.sqrt(jnp.sum(kk * kk, axis=0))
        kk = kk * (1.0 / jnp.maximum(nrm, 1e-12))[None]
        kf = k * (1.0 + (a - 1.0) * ka_ref[...])
        vec_ref[0] = -kk
        vec_ref[1] = cube(decay)
        vec_ref[2] = kk * cube(a)
        vec_ref[3] = cube(kf)
        vec_ref[4] = cube(r)
        bon_ref[...] = jnp.sum(cube(r * kf * rk_ref[...]), axis=0)

        def step(tl, c):
            t = t0 + tl
            v_raw = v_ref[t]
            v = v_raw + muv_ref[...] * (prevv_ref[...] - v_raw)
            prevv_ref[...] = v_raw

            def row(j, kx):
                return vec_ref[j, kx, pl.ds(tl, HEAD_SIZE, stride=0), :]

            parts = [s_ref[kx] * row(0, kx) for kx in range(SCAN_ACC)]
            for kx in range(SCAN_ACC, HEAD_SIZE):
                parts[kx % SCAN_ACC] = parts[kx % SCAN_ACC] + s_ref[kx] * row(0, kx)
            sa = functools.reduce(lambda x, y: x + y, parts)

            parts = []
            for kx in range(HEAD_SIZE):
                sn = s_ref[kx] * row(1, kx) + sa * row(2, kx) + v * row(3, kx)
                s_ref[kx] = sn
                if kx < SCAN_ACC:
                    parts.append(sn * row(4, kx))
                else:
                    parts[kx % SCAN_ACC] = parts[kx % SCAN_ACC] + sn * row(4, kx)
            o = functools.reduce(lambda x, y: x + y, parts)

            mu = jnp.mean(o, axis=0, keepdims=True)
            dlt = o - mu
            var = jnp.mean(dlt * dlt, axis=0, keepdims=True)
            on = dlt * lax.rsqrt(var + GN_EPS) * lw_ref[...] + lb_ref[...]
            y_ref[t] = on + bon_ref[pl.ds(tl, HEAD_SIZE, stride=0), :] * v
            return c

        for tl in range(steps):
            step(tl, 0)
        return carry

    lax.fori_loop(0, n_batches, batch, 0)


def _scan(ct_seqs, v_seq, params8, params, s0, *, n_batches, steps, name):
    g = s0.shape[0]
    tt = n_batches * SCAN_TB
    ttv = v_seq[0].shape[1] if n_batches == 1 else n_batches * steps
    n_ti = ct_seqs[0][0].shape[2] // tt

    def spec(block, lead, tpos):
        def index(gi, ti):
            idx = [gi if lead is None else lead, 0, 0, 0]
            idx[tpos] = ti
            return tuple(idx)
        return pl.BlockSpec(block, index)

    ct_block = (None, HEAD_SIZE, tt, LANES)
    tv_block = (None, ttv, HEAD_SIZE, LANES)
    par8_spec = pl.BlockSpec((HEAD_SIZE * SCAN_TB, LANES), lambda gi, ti: (0, 0))
    par_spec = pl.BlockSpec((HEAD_SIZE, LANES), lambda gi, ti: (0, 0))
    st_spec = pl.BlockSpec((None, HEAD_SIZE, HEAD_SIZE, LANES), lambda gi, ti: (gi, 0, 0, 0))
    return pl.pallas_call(
        functools.partial(_scan_body, n_batches=n_batches, steps=steps),
        grid=(g, n_ti),
        in_specs=([spec(ct_block, lead, 2) for _, lead in ct_seqs] + [spec(tv_block, v_seq[1], 1)]
                  + [par8_spec] * len(params8) + [par_spec] * len(params) + [st_spec]),
        out_specs=[spec(tv_block, None, 1), st_spec],
        out_shape=[jax.ShapeDtypeStruct((g, n_ti * ttv, HEAD_SIZE, LANES), F32),
                   jax.ShapeDtypeStruct((g, HEAD_SIZE, HEAD_SIZE, LANES), F32)],
        scratch_shapes=[pltpu.VMEM((5, HEAD_SIZE, SCAN_TB, LANES), F32), pltpu.VMEM((SCAN_TB, LANES), F32),
                        pltpu.VMEM((2, HEAD_SIZE * SCAN_TB, LANES), F32), pltpu.VMEM((HEAD_SIZE, LANES), F32)],
        compiler_params=_cparams("arbitrary", "arbitrary", vmem_mb=48),
        name=name,
    )(*[a for a, _ in ct_seqs], v_seq[0], *params8, *params, s0)


def _sample_groups(x):
    return x.reshape(DEC_SEQ, DEC_BATCH // CHAIN_B, CHAIN_B, N_HEADS, HEAD_SIZE).transpose(1, 0, 2, 3, 4)


def _to_chain_sample_ct(x):
    x = _sample_groups(x).transpose(0, 4, 1, 2, 3).reshape(DEC_BATCH // CHAIN_B, HEAD_SIZE, DEC_SEQ, LANES)
    return jnp.pad(x, ((0, 0), (0, 0), (0, SCAN_TB - DEC_SEQ), (0, 0)))


def _to_chain_sample_tc(x):
    return _sample_groups(x).transpose(0, 1, 4, 2, 3).reshape(DEC_BATCH // CHAIN_B, DEC_SEQ, HEAD_SIZE, LANES)


def _from_chain_sample(y):
    ng = DEC_BATCH // CHAIN_B
    y = y.reshape(ng, DEC_SEQ, HEAD_SIZE, CHAIN_B, N_HEADS).transpose(1, 0, 3, 2, 4)
    return y.reshape(N_SAMPLE, RWKV_WIDTH)


def _param_chain(p):
    return jnp.tile(p.reshape(N_HEADS, HEAD_SIZE).T, (1, CHAIN_B))


def _param_chain8(p):
    return jnp.repeat(_param_chain(p), SCAN_TB, axis=0)


def _head_minor(w, axis):
    shape = w.shape
    w = w.reshape(shape[:axis] + (N_HEADS, HEAD_SIZE) + shape[axis + 1:])
    return jnp.swapaxes(w, axis, axis + 1).reshape(shape)


def _rwkv_out_body(yc_ref, ys_ref, g_ref, w_ref, o_ref, yt, *, n_prompt_steps):
    s = pl.program_id(0)

    def finish(y):
        o_ref[...] = jnp.dot((y * g_ref[...]).astype(BF16), w_ref[...], preferred_element_type=F32)

    @pl.when(s < n_prompt_steps)
    def _():
        b = s % CHAIN_B

        @pl.when(b == 0)
        def _():
            _load_chain_tc(yc_ref, yt, TM_T // LANES)

        finish(jnp.concatenate([yt[b, half].T for half in range(TM_T // LANES)], axis=0))

    @pl.when(s >= n_prompt_steps)
    def _():
        finish(ys_ref[...])


def _rwkv_out(y_chain, y_s, g, w):
    n_tb = SEQ // TM_T
    nps = n_tb * CHAIN_B

    def row_blk(s):
        return jnp.where(s < nps, (s % CHAIN_B) * n_tb + s // CHAIN_B, s)

    return pl.pallas_call(
        functools.partial(_rwkv_out_body, n_prompt_steps=nps),
        grid=(N_TOK // TM_T,),
        in_specs=[pl.BlockSpec((TM_T * HEAD_SIZE, LANES), lambda s: (jnp.minimum(s // CHAIN_B, n_tb - 1), 0)),
                  pl.BlockSpec((TM_T, RWKV_WIDTH), lambda s: (jnp.maximum(s - nps, 0), 0)),
                  pl.BlockSpec((TM_T, RWKV_WIDTH), lambda s: (row_blk(s), 0)),
                  _resident((RWKV_WIDTH, D_MODEL))],
        out_specs=pl.BlockSpec((TM_T, D_MODEL), lambda s: (row_blk(s), 0)),
        out_shape=jax.ShapeDtypeStruct((N_TOK, D_MODEL), F32),
        scratch_shapes=[pltpu.VMEM((CHAIN_B, TM_T // LANES, RWKV_WIDTH, LANES), F32)],
        compiler_params=_cparams("arbitrary", vmem_mb=56),
        name="rwkv_out",
    )(y_chain, y_s, g, w)


def _mix_body(co_ref, ro_ref, xn_ref, wgr_ref, xp_ref, xs_ref, wm_ref, nf_ref, wr_ref, br_ref,
              h_ref, hn_ref, ridx_ref, rw_ref, *, n_prompt_tiles):
    i = pl.program_id(0)
    cw = D_MODEL // 4
    mo = None
    for c in range(4):
        cols = slice(c * cw, (c + 1) * cw)
        g_rwkv = jnp.dot(xn_ref[...], wgr_ref[:, cols], preferred_element_type=F32)
        mixed = co_ref[:, cols] + _sigmoid(g_rwkv) * ro_ref[:, cols]
        part = jnp.dot(mixed.astype(BF16), wm_ref[cols, :], preferred_element_type=F32)
        mo = part if mo is None else mo + part

    def finish(x):
        h = x + mo
        h_ref[...] = h
        ms = jnp.mean(h * h, axis=-1, keepdims=True)
        hn = h * lax.rsqrt(ms + RMS_EPS) * nf_ref[...]
        hnb = hn.astype(BF16)
        hn_ref[...] = hnb.reshape(hn.shape[0], D_MODEL // LANES, LANES)
        logits = jnp.dot(hnb, wr_ref[...], preferred_element_type=F32) + br_ref[...]
        tm = logits.shape[0]
        lane = lax.broadcasted_iota(jnp.int32, (tm, LANES), 1)
        neg = jnp.float32(-jnp.inf)
        gl = jnp.where(lane < N_GROUPS, logits, neg)
        gmax = jnp.max(gl, axis=-1, keepdims=True)
        g_idx = jnp.min(jnp.where(gl == gmax, lane, LANES), axis=-1, keepdims=True)
        g_w = 1.0 / jnp.sum(jnp.exp(gl - gmax), axis=-1, keepdims=True)
        lo = N_GROUPS + g_idx * EXPERTS_PER_GROUP
        el = jnp.where(jnp.logical_and(lane >= lo, lane < lo + EXPERTS_PER_GROUP), logits, neg)
        m1 = jnp.max(el, axis=-1, keepdims=True)
        i1 = jnp.min(jnp.where(el == m1, lane, LANES), axis=-1, keepdims=True)
        el2 = jnp.where(lane == i1, neg, el)
        m2 = jnp.max(el2, axis=-1, keepdims=True)
        i2 = jnp.min(jnp.where(el2 == m2, lane, LANES), axis=-1, keepdims=True)
        t2 = jnp.exp(m2 - m1)
        den = 1.0 + t2
        ridx_ref[...] = jnp.where(lane == 0, i1 - N_GROUPS, jnp.where(lane == 1, i2 - N_GROUPS, 0))
        rw_ref[...] = jnp.where(lane == 0, (1.0 / den) * g_w, jnp.where(lane == 1, (t2 / den) * g_w, 0.0))

    @pl.when(i < n_prompt_tiles)
    def _():
        finish(xp_ref[...])

    @pl.when(i >= n_prompt_tiles)
    def _():
        finish(xs_ref[...])


def _mix_route(conv_out, rwkv_out, xn, wgr, xp, xs, wm, nf, wr, br):
    npt = N_PROMPT // TM_S
    tok_spec = pl.BlockSpec((TM_S, D_MODEL), lambda i: (i, 0))
    small_spec = pl.BlockSpec((TM_S, LANES), lambda i: (i, 0))
    return pl.pallas_call(
        functools.partial(_mix_body, n_prompt_tiles=npt),
        grid=(N_TOK // TM_S,),
        in_specs=[tok_spec, tok_spec, tok_spec, _resident((D_MODEL, D_MODEL)),
                  pl.BlockSpec((TM_S, D_MODEL), lambda i: (jnp.minimum(i, npt - 1), 0)),
                  pl.BlockSpec((TM_S, D_MODEL), lambda i: (jnp.maximum(i - npt, 0), 0)),
                  _resident((D_MODEL, D_MODEL)),
                  pl.BlockSpec((1, D_MODEL), lambda i: (0, 0)),
                  _resident((D_MODEL, LANES)),
                  pl.BlockSpec((1, LANES), lambda i: (0, 0))],
        out_specs=[tok_spec, pl.BlockSpec((TM_S, D_MODEL // LANES, LANES), lambda i: (i, 0, 0)),
                   small_spec, small_spec],
        out_shape=[jax.ShapeDtypeStruct((N_TOK, D_MODEL), F32),
                   jax.ShapeDtypeStruct((N_TOK, D_MODEL // LANES, LANES), BF16),
                   jax.ShapeDtypeStruct((N_TOK, LANES), jnp.int32),
                   jax.ShapeDtypeStruct((N_TOK, LANES), F32)],
        compiler_params=_cparams("arbitrary", vmem_mb=56),
        name="mix_route",
    )(conv_out, rwkv_out, xn, wgr, xp, xs, wm, nf, wr, br)


def _expert_body(blk_e_ref, slot_tok_ref, nused_ref, first_ref, par_ref, next_e_ref,
                 hn_ref, wg_hbm, wu_hbm, wd_hbm, yb_ref,
                 xbuf, sem, wfg, wfu, wfd, wsem, wgb, wub, wdb):
    i = pl.program_id(0)
    nused = nused_ref[0]
    slot = i % N_XBUF

    def row_copy(blk, r, s):
        tok = slot_tok_ref[blk * MOE_BLOCK + r]
        return pltpu.make_async_copy(hn_ref.at[tok], xbuf.at[s, r], sem.at[s])

    def issue(blk, s):
        for r in range(MOE_BLOCK):
            row_copy(blk, r, s).start(priority=r % 2)

    def w_copies(e, s):
        return (pltpu.make_async_copy(wg_hbm.at[e], wfg.at[s], wsem.at[s]),
                pltpu.make_async_copy(wu_hbm.at[e], wfu.at[s], wsem.at[s]),
                pltpu.make_async_copy(wd_hbm.at[e], wfd.at[s], wsem.at[s]))

    @pl.when(jnp.logical_and(i == 0, nused > 0))
    def _():
        for c in w_copies(blk_e_ref[0], 0):
            c.start(priority=1)
        for a in range(GATHER_AHEAD):
            @pl.when(a < nused)
            def _():
                issue(a, a)

    @pl.when(i + GATHER_AHEAD < nused)
    def _():
        issue(i + GATHER_AHEAD, (i + GATHER_AHEAD) % N_XBUF)

    @pl.when(i < nused)
    def _():
        @pl.when(first_ref[i] == 1)
        def _():
            ws = par_ref[i]
            for c in w_copies(blk_e_ref[i], ws):
                c.wait()

            @pl.when(next_e_ref[i] >= 0)
            def _():
                for c in w_copies(next_e_ref[i], 1 - ws):
                    c.start(priority=1)

            wgb[...] = wfg[ws].astype(BF16)
            wub[...] = wfu[ws].astype(BF16)
            wdb[...] = wfd[ws].astype(BF16)

        for r in range(MOE_BLOCK):
            row_copy(i, r, slot).wait()
        xe = xbuf[slot].reshape(MOE_BLOCK, D_MODEL)
        gate = jnp.dot(xe, wgb[...], preferred_element_type=F32)
        up = jnp.dot(xe, wub[...], preferred_element_type=F32)
        hdn = (gate * _sigmoid(gate)) * up
        yb = jnp.dot(hdn.astype(BF16), wdb[...], preferred_element_type=F32)
        yb_ref[...] = yb.reshape(MOE_BLOCK, D_MODEL // LANES, LANES)

    @pl.when(i >= nused)
    def _():
        yb_ref[...] = jnp.zeros_like(yb_ref)


def _experts(plan, hn, wg, wu, wd):
    blk_e, slot_tok, nused, first, par, next_e = plan
    n_blocks = blk_e.shape[0]
    any_spec = pl.BlockSpec(memory_space=pl.ANY)
    return pl.pallas_call(
        _expert_body,
        grid_spec=pltpu.PrefetchScalarGridSpec(
            num_scalar_prefetch=6,
            grid=(n_blocks,),
            in_specs=[any_spec, any_spec, any_spec, any_spec],
            out_specs=pl.BlockSpec((MOE_BLOCK, D_MODEL // LANES, LANES), lambda i, *_: (i, 0, 0)),
            scratch_shapes=[pltpu.VMEM((N_XBUF, MOE_BLOCK, D_MODEL // LANES, LANES), BF16),
                            pltpu.SemaphoreType.DMA((N_XBUF,)),
                            pltpu.VMEM((2, D_MODEL, D_EXPERT), F32),
                            pltpu.VMEM((2, D_MODEL, D_EXPERT), F32),
                            pltpu.VMEM((2, D_EXPERT, D_MODEL), F32),
                            pltpu.SemaphoreType.DMA((2,)),
                            pltpu.VMEM((D_MODEL, D_EXPERT), BF16),
                            pltpu.VMEM((D_MODEL, D_EXPERT), BF16),
                            pltpu.VMEM((D_EXPERT, D_MODEL), BF16)]),
        out_shape=jax.ShapeDtypeStruct((n_blocks * MOE_BLOCK, D_MODEL // LANES, LANES), F32),
        compiler_params=_cparams("arbitrary", vmem_mb=48),
        name="experts",
    )(blk_e, slot_tok, nused, first, par, next_e, hn, wg, wu, wd)


def _combine_body(dest_ref, yb_ref, h_ref, rw_ref, p_ref, wpg_ref, wpp_ref, nf_ref, yp_ref, ys_ref,
                  ybuf, sem):
    i = pl.program_id(0)
    tm = h_ref.shape[0]
    slot = i % 2

    def row_copy(tile, r, s, sl):
        d = dest_ref[(tile * tm + r) * TOP_K + s]
        return pltpu.make_async_copy(yb_ref.at[d], ybuf.at[sl, s, r], sem.at[sl])

    def issue(tile, sl):
        for r in range(tm):
            row_copy(tile, r, 0, sl).start()
            row_copy(tile, r, 1, sl).start()

    @pl.when(i == 0)
    def _():
        issue(0, 0)

    @pl.when(i + 1 < pl.num_programs(0))
    def _():
        issue(i + 1, 1 - slot)

    for r in range(tm):
        row_copy(i, r, 0, slot).wait()
        row_copy(i, r, 1, slot).wait()

    rw = rw_ref[...]
    y0 = ybuf[slot, 0].reshape(tm, D_MODEL)
    y1 = ybuf[slot, 1].reshape(tm, D_MODEL)
    h2 = h_ref[...] + (y0 * rw[:, 0:1] + y1 * rw[:, 1:2])
    gate = _sigmoid(jnp.dot(h2.astype(BF16), wpg_ref[...], preferred_element_type=F32))
    pp = jnp.dot(p_ref[...].astype(BF16), wpp_ref[...], preferred_element_type=F32)
    h3 = h2 + gate * pp
    ms = jnp.mean(h3 * h3, axis=-1, keepdims=True)
    y = h3 * lax.rsqrt(ms + RMS_EPS) * nf_ref[...]

    @pl.when(i < N_PROMPT // TM_S)
    def _():
        yp_ref[...] = y

    @pl.when(i >= N_PROMPT // TM_S)
    def _():
        ys_ref[...] = y


def _combine(dest, yb, h, rw, p_all, wpg, wpp, nf):
    npt = N_PROMPT // TM_S
    return pl.pallas_call(
        _combine_body,
        grid_spec=pltpu.PrefetchScalarGridSpec(
            num_scalar_prefetch=1,
            grid=(N_TOK // TM_S,),
            in_specs=[pl.BlockSpec(memory_space=pl.ANY),
                      pl.BlockSpec((TM_S, D_MODEL), lambda i, d: (i, 0)),
                      pl.BlockSpec((TM_S, LANES), lambda i, d: (i, 0)),
                      pl.BlockSpec((TM_S, PLE_DIM), lambda i, d: (i, 0)),
                      pl.BlockSpec((D_MODEL, D_MODEL), lambda i, d: (0, 0), pipeline_mode=pl.Buffered(1)),
                      pl.BlockSpec((PLE_DIM, D_MODEL), lambda i, d: (0, 0), pipeline_mode=pl.Buffered(1)),
                      pl.BlockSpec((1, D_MODEL), lambda i, d: (0, 0))],
            out_specs=[pl.BlockSpec((TM_S, D_MODEL), lambda i, d: (jnp.minimum(i, npt - 1), 0)),
                       pl.BlockSpec((TM_S, D_MODEL), lambda i, d: (jnp.maximum(i - npt, 0), 0))],
            scratch_shapes=[pltpu.VMEM((2, TOP_K, TM_S, D_MODEL // LANES, LANES), F32),
                            pltpu.SemaphoreType.DMA((2,))]),
        out_shape=[jax.ShapeDtypeStruct((N_PROMPT, D_MODEL), F32),
                   jax.ShapeDtypeStruct((N_SAMPLE, D_MODEL), F32)],
        compiler_params=_cparams("arbitrary", vmem_mb=48),
        name="combine_ple",
    )(dest, yb, h, rw, p_all, wpg, wpp, nf)


def _dispatch_plan(eidx):
    n_assign = N_TOK * TOP_K
    e_flat = eidx.reshape(n_assign)
    onehot = (e_flat[:, None] == jnp.arange(N_EXPERTS, dtype=jnp.int32)[None, :]).astype(jnp.int32)
    csum = jnp.cumsum(onehot, axis=0)
    counts = csum[-1]
    rank = jnp.sum(csum * onehot, axis=1) - 1
    padded = (counts + MOE_BLOCK - 1) // MOE_BLOCK * MOE_BLOCK
    pad_end = jnp.cumsum(padded)
    pad_start = pad_end - padded
    dest = pad_start[e_flat] + rank
    n_blocks = -(-n_assign // MOE_BLOCK) + N_EXPERTS
    tok = jnp.arange(n_assign, dtype=jnp.int32) // TOP_K
    pad_tok = jnp.arange(n_blocks * MOE_BLOCK, dtype=jnp.int32) % N_TOK
    slot_tok = pad_tok.at[dest].set(tok)
    blk = jnp.arange(n_blocks, dtype=jnp.int32)
    blk_e = jnp.minimum(jnp.searchsorted(pad_end, blk * MOE_BLOCK, side="right"), N_EXPERTS - 1).astype(jnp.int32)
    nused = (pad_end[-1] // MOE_BLOCK).astype(jnp.int32)
    prev_e = jnp.concatenate([jnp.full((1,), -1, jnp.int32), blk_e[:-1]])
    first = jnp.logical_and(blk < nused, blk_e != prev_e)
    par = ((jnp.cumsum(first.astype(jnp.int32)) - 1) % 2).astype(jnp.int32)
    idx_first = jnp.where(first, blk, n_blocks)
    later = jnp.concatenate([lax.cummin(idx_first[::-1])[::-1][1:], jnp.full((1,), n_blocks, jnp.int32)])
    next_e = jnp.where(later < n_blocks, blk_e[jnp.minimum(later, n_blocks - 1)], -1).astype(jnp.int32)
    plan = (blk_e, slot_tok, nused.reshape(1), first.astype(jnp.int32), par, next_e)
    return dest.astype(jnp.int32), plan


def kernel(x_prompt, x_sample, state_conv, state_shift, state_wkv, p_prompt, p_sample, norm_mix, w_in, conv_w, w_conv_out, shift_mu, w0, w2, a0, a2, g2, k_k, k_a, r_k, lnx_w, lnx_b, w_rwkv_out, w_mix_out, norm_ffn, w_route_group, b_route_group, w_route_expert, b_route_expert, w_exp_gate, w_exp_up, w_exp_down, w_ple_proj, w_ple_gate, norm_final):
    c3 = 3 * CONV_WIDTH
    rw3 = 3 * RWKV_WIDTH
    xp = x_prompt.reshape(N_PROMPT, D_MODEL)
    xs = x_sample.transpose(1, 0, 2).reshape(N_SAMPLE, D_MODEL)
    win = w_in[0]
    mu = shift_mu[0]
    st = state_shift[0]

    xn = _norm_cast(xp, xs, norm_mix)
    rkv_s, _, sraw_rkv = _inproj(xn, win, mu[None, :rw3], st[:, :rw3],
                                 tn=1024, col_blk_off=c3 // 1024, n_col_blocks=rw3 // 1024,
                                 shift_lo=0, shift_hi=rw3 // 1024,
                                 tile_lo=N_PROMPT // TM, n_tiles=N_SAMPLE // TM, name="inproj_rkv_sample")
    n_lr = RWKV_PROJ - rw3
    mu_l = jnp.concatenate([mu[rw3:], jnp.zeros((LORA_W - n_lr,), F32)])[None, :]
    st_l = jnp.concatenate([st[:, rw3:], jnp.zeros((DEC_BATCH, LORA_W - n_lr), F32)], axis=1)
    w_l = win[:, c3 + rw3:c3 + rw3 + LORA_W].astype(BF16)
    w_gc = win[:, c3 + RWKV_PROJ:c3 + RWKV_PROJ + D_MODEL].astype(BF16)
    w_gr = win[:, c3 + RWKV_PROJ + D_MODEL:].astype(BF16)
    w_rk = _head_minor(win[:, c3:c3 + 2 * RWKV_WIDTH].reshape(D_MODEL, 2, RWKV_WIDTH), 2)
    rk_chain = _inproj_t(xn, w_rk.reshape(D_MODEL, 2 * RWKV_WIDTH).T.astype(BF16),
                         time_major=False, name="inproj_t_rk")
    v_chain = _inproj_t(xn, _head_minor(win[:, c3 + 2 * RWKV_WIDTH:c3 + rw3], 1).T.astype(BF16),
                        time_major=True, name="inproj_t_v")

    sc = state_conv[0].transpose(1, 0, 2).reshape(2 * DEC_BATCH, CONV_WIDTH)
    conv_out, ulast, us = _conv_branch(xn, win[:, :c3].astype(BF16), conv_w[0], sc,
                                       w_conv_out[0].astype(BF16), w_gc)

    def pad_rows(w, before, total):
        return jnp.pad(w, ((before, total - before - w.shape[0]), (0, 0))).astype(BF16)

    w2p = pad_rows(w2[0], 0, LANES)
    a2p = pad_rows(a2[0], DECAY_LORA, 2 * LANES)
    g2p = pad_rows(_head_minor(g2[0], 1), DECAY_LORA + A_LORA - LANES, LORA_W - LANES)
    g, wl_c, al_c, wl_s, al_s, last_l, sraw_l = _lora(xn, w_l, mu_l, st_l, _head_minor(w2p, 1).T,
                                                      _head_minor(a2p, 1).T, w2p, a2p, g2p)
    par8 = [_param_chain8(p) for p in (k_k[0], k_a[0], r_k[0].reshape(RWKV_WIDTH))]
    mu8 = [_param_chain8(mu[n * RWKV_WIDTH:(n + 1) * RWKV_WIDTH]) for n in range(2)]
    bias8 = [_param_chain8(w0[0]), _param_chain8(a0[0])]
    zero8 = jnp.zeros((HEAD_SIZE * SCAN_TB, LANES), F32)
    mu_v = _param_chain(mu[2 * RWKV_WIDTH:rw3])
    gn = [_param_chain(lnx_w[0]), _param_chain(lnx_b[0])]
    seqs_p = [(rk_chain, 0), (rk_chain, 1), (wl_c[None], None), (al_c[None], None)]
    v4 = v_chain.reshape(1, SEQ, HEAD_SIZE, LANES)
    s0_p = jnp.zeros((1, HEAD_SIZE, HEAD_SIZE, LANES), F32)
    y_p, sf_p = _scan(seqs_p, (v4, None), par8 + mu8 + bias8, [mu_v] + gn, s0_p,
                      n_batches=SCAN_TT // SCAN_TB, steps=SCAN_TB, name="wkv_scan_prompt")
    ng = DEC_BATCH // CHAIN_B
    s0_s = state_wkv[0].reshape(ng, CHAIN_B, N_HEADS, HEAD_SIZE, HEAD_SIZE).transpose(0, 4, 3, 1, 2)
    s0_s = s0_s.reshape(ng, HEAD_SIZE, HEAD_SIZE, LANES)
    seqs_s = [(_to_chain_sample_ct(rkv_s[:, n * RWKV_WIDTH:(n + 1) * RWKV_WIDTH]), None) for n in range(2)]
    seqs_s += [(_to_chain_sample_ct(wl_s), None), (_to_chain_sample_ct(al_s), None)]
    v_s = _to_chain_sample_tc(rkv_s[:, 2 * RWKV_WIDTH:rw3])
    y_s, sf_s = _scan(seqs_s, (v_s, None), par8 + [zero8, zero8] + bias8,
                      [jnp.zeros((HEAD_SIZE, LANES), F32)] + gn, s0_s,
                      n_batches=1, steps=DEC_SEQ, name="wkv_scan_sample")
    rwkv_out = _rwkv_out(y_p.reshape(SEQ * HEAD_SIZE, LANES), _from_chain_sample(y_s), g,
                         _head_minor(w_rwkv_out[0], 0).astype(BF16))

    wr = jnp.concatenate([w_route_group[0], w_route_expert[0],
                          jnp.zeros((D_MODEL, LANES - N_GROUPS - N_EXPERTS), F32)], axis=1)
    br = jnp.concatenate([b_route_group[0], b_route_expert[0],
                          jnp.zeros((LANES - N_GROUPS - N_EXPERTS,), F32)])[None, :]
    h, hn, ridx, rw = _mix_route(conv_out, rwkv_out, xn, w_gr, xp, xs, w_mix_out[0].astype(BF16),
                                 norm_ffn, wr.astype(BF16), br)

    dest, plan = _dispatch_plan(ridx[:, :TOP_K])
    yb = _experts(plan, hn, w_exp_gate[0], w_exp_up[0], w_exp_down[0])
    p_all = jnp.concatenate([p_prompt[0].reshape(N_PROMPT, PLE_DIM),
                             p_sample[0].transpose(1, 0, 2).reshape(N_SAMPLE, PLE_DIM)], axis=0)
    y_p2, y_s2 = _combine(dest, yb, h, rw, p_all, w_ple_gate[0].astype(BF16), w_ple_proj[0].astype(BF16),
                          norm_final[None, :])

    y_prompt = y_p2.reshape(BATCH, SEQ, D_MODEL)
    y_sample = y_s2.reshape(DEC_SEQ, DEC_BATCH, D_MODEL).transpose(1, 0, 2)
    tiles_per_seq = SEQ // TM
    seq_last = jnp.arange(BATCH) * tiles_per_seq + tiles_per_seq - 1

    conv_p = ulast.reshape(-1, SUBLANES, CONV_WIDTH)[seq_last, SUBLANES - 2:, :][None]
    conv_s = us.reshape(2, DEC_BATCH, CONV_WIDTH).transpose(1, 0, 2)[None]
    last_rkv = jnp.concatenate([rk_chain[:, :, SEQ - 1, :], v4[:, SEQ - 1]], axis=0)
    lm = last_rkv.reshape(3, HEAD_SIZE, BATCH, N_HEADS).transpose(2, 0, 3, 1).reshape(BATCH, rw3)
    lora_last = (SEQ // TM_L - 1) * CHAIN_B + jnp.arange(BATCH)
    lt = last_l.reshape(-1, SUBLANES, LORA_W)[lora_last, SUBLANES - 1, :n_lr]
    shift_p = jnp.concatenate([lm, lt], axis=1)[None]
    shift_s = jnp.concatenate([sraw_rkv, sraw_l[:, :n_lr]], axis=1)[None]
    wkv_p = sf_p.reshape(HEAD_SIZE, HEAD_SIZE, BATCH, N_HEADS).transpose(2, 3, 1, 0)[None]
    wkv_s = sf_s.reshape(ng, HEAD_SIZE, HEAD_SIZE, CHAIN_B, N_HEADS).transpose(0, 3, 4, 2, 1)
    wkv_s = wkv_s.reshape(DEC_BATCH, N_HEADS, HEAD_SIZE, HEAD_SIZE)[None]
    return (y_prompt, y_sample, conv_p, shift_p, wkv_p, conv_s, shift_s, wkv_s)
```
